```python
import math
import jax, jax.numpy as jnp
from jax import lax
import numpy as np

D_MODEL = 1024
BATCH = 4
SEQ = 4096
DEPTH = 2
DEC_BATCH = 128
DEC_SEQ = 8
PAST_LEN = 2048
PAGE_SIZE = 128

N_A = DEPTH // 2
N_B = DEPTH - N_A
D_INNER = D_MODEL
CONV_W = 31
N_HEADS = 16
HEAD_DIM = 64
KV_HEADS = 4
GROUP = N_HEADS // KV_HEADS
KV_DIM = KV_HEADS * HEAD_DIM
N_BRANCH = 3
L_CMP = 32
CMP_STRIDE = 16
CMP_HID = 256
L_SLC = 64
N_SEL = 16
WINDOW = 512
Q_BLK = 128
NUM_BUCKETS = 32
MAX_DISTANCE = 128
MAX_EXACT = NUM_BUCKETS // 2
Q_WIDTH = N_HEADS * HEAD_DIM
B_IN = Q_WIDTH + N_BRANCH * Q_WIDTH + N_BRANCH * N_HEADS
EPS = 1e-6
NEG = -1e30
BIG = 1e9

kernel_name = 'yoco_conformer_nsa_decode_step'


def rms_norm(x, g):
    xf = x.astype(jnp.float32)
    y = xf * lax.rsqrt(jnp.mean(xf * xf, axis=-1, keepdims=True) + EPS)
    return (y * g.astype(jnp.float32)).astype(x.dtype)


def layer_norm(x, g, b):
    xf = x.astype(jnp.float32)
    mu = jnp.mean(xf, axis=-1, keepdims=True)
    var = jnp.mean(jnp.square(xf - mu), axis=-1, keepdims=True)
    y = (xf - mu) * lax.rsqrt(var + EPS)
    return (y * g.astype(jnp.float32) + b.astype(jnp.float32)).astype(x.dtype)


def t5_bucket(dist):
    n = jnp.maximum(dist, 0)
    nf = jnp.maximum(n, 1).astype(jnp.float32)
    large = MAX_EXACT + (jnp.log(nf / MAX_EXACT) / math.log(MAX_DISTANCE / MAX_EXACT)
                         * (NUM_BUCKETS - MAX_EXACT)).astype(jnp.int32)
    large = jnp.minimum(large, NUM_BUCKETS - 1)
    return jnp.where(n < MAX_EXACT, n, large)


def masked_softmax(logits, mask, axis):
    l = jnp.where(mask, logits.astype(jnp.float32), NEG)
    m = jnp.max(l, axis=axis, keepdims=True)
    p = jnp.where(mask, jnp.exp(l - m), 0.0)
    d = jnp.sum(p, axis=axis, keepdims=True)
    return p / jnp.maximum(d, 1e-30)


def conformer_layer(x, conv_prev, g_pre, w_in, conv_w, conv_b, ln_g, ln_b, w_out, g_post):
    h = rms_norm(x, g_pre)
    a, b, z = jnp.split(h @ w_in, 3, axis=-1)
    glu = a * jax.nn.sigmoid(b)
    full = jnp.concatenate([conv_prev.astype(glu.dtype), glu], axis=1)
    c = lax.conv_general_dilated(full, conv_w[:, None, :].astype(glu.dtype), (1,), 'VALID',
                                 dimension_numbers=('NWC', 'WIO', 'NWC'),
                                 feature_group_count=D_INNER) + conv_b
    y = jax.nn.silu(layer_norm(c, ln_g, ln_b)) * jax.nn.silu(z)
    return x + rms_norm(y @ w_out, g_post), full[:, -(CONV_W - 1):]


def kv_rows(x, g_kv, w_kv):
    h = rms_norm(x, g_kv)
    kv = (h @ w_kv).reshape(x.shape[0], x.shape[1], N_BRANCH, 2, KV_HEADS, HEAD_DIM)
    return kv[:, :, 0], kv[:, :, 1], kv[:, :, 2]


def compress(rows, pe, w1, w2):
    b, L = rows.shape[:2]
    r_parts = L_CMP // CMP_STRIDE
    n_chunk = L // CMP_STRIDE
    nc = n_chunk - r_parts + 1
    chunks = rows[:, :n_chunk * CMP_STRIDE].reshape(b, n_chunk, CMP_STRIDE, KV_HEADS, HEAD_DIM)
    w1r = w1.reshape(r_parts, CMP_STRIDE, HEAD_DIM, CMP_HID)
    pe_r = pe.reshape(r_parts, CMP_STRIDE, HEAD_DIM)
    pre = jnp.einsum('rsd,rsdh->h', pe_r, w1r)
    for r in range(r_parts):
        pre = pre + jnp.einsum('bnskd,sdh->bnkh', chunks[:, r:r + nc], w1r[r])
    return jax.nn.silu(pre) @ w2


def sparse_keys(cmp_rows, slc_rows, cmp_pe, cmp_w1, cmp_w2):
    kc = compress(cmp_rows[:, :, 0], cmp_pe[0], cmp_w1[0], cmp_w2[0])
    vc = compress(cmp_rows[:, :, 1], cmp_pe[1], cmp_w1[1], cmp_w2[1])
    b, L = slc_rows.shape[:2]
    nb = -(-L // L_SLC)
    s = jnp.pad(slc_rows, ((0, 0), (0, nb * L_SLC - L), (0, 0), (0, 0), (0, 0)))
    s = s.reshape(b, nb, L_SLC, 2, KV_HEADS, HEAD_DIM).transpose(3, 0, 4, 1, 2, 5)
    return kc, vc, s[0], s[1]


def cmp_to_slc(nc, nb):
    cs = jnp.arange(nc)[:, None] * CMP_STRIDE
    js = jnp.arange(nb)[None, :] * L_SLC
    return ((cs <= js + L_SLC - 1) & (cs + L_CMP - 1 >= js)).astype(jnp.float32)


def nsa_block(q, qpos, kc, vc, ks, vs, kw, vw, kwpos, rel_bias):
    f32 = jnp.float32
    scale = HEAD_DIM ** -0.5
    bias_kvg = rel_bias.reshape(NUM_BUCKETS, KV_HEADS, GROUP).astype(f32)
    nc = kc.shape[1]
    nb = ks.shape[2]
    c_end = jnp.arange(nc) * CMP_STRIDE + (L_CMP - 1)
    dist_c = qpos[:, None] - c_end[None, :]
    bias_c = jnp.transpose(bias_kvg[t5_bucket(dist_c)], (2, 3, 0, 1))
    lc = jnp.einsum('btkgd,bckd->bkgtc', q, kc).astype(f32) * scale + bias_c
    pc = masked_softmax(lc, dist_c >= 0, -1)
    o_c = jnp.einsum('bkgtc,bckd->btkgd', pc, vc.astype(f32))
    imp = jnp.einsum('bkgtc,cj->bktj', pc, cmp_to_slc(nc, nb))
    j = jnp.arange(nb)[None, :]
    tb = (qpos // L_SLC)[:, None]
    valid = j * L_SLC <= qpos[:, None]
    forced = (j == 0) | (j == tb) | (j == tb - 1)
    score = jnp.where(valid, jnp.where(forced, BIG, imp), -BIG)
    _, idx = lax.top_k(score, min(N_SEL, nb))
    bi = jnp.arange(q.shape[0])[:, None, None, None]
    ki = jnp.arange(KV_HEADS)[None, :, None, None]
    kg = ks[bi, ki, idx]
    vg = vs[bi, ki, idx]
    s_pos = idx[..., None] * L_SLC + jnp.arange(L_SLC)
    dist_s = qpos[None, None, :, None, None] - s_pos
    bias_s = jnp.moveaxis(bias_kvg[t5_bucket(dist_s), ki[..., None]], -1, 3)
    ls = jnp.einsum('btkgd,bktnld->bktgnl', q, kg).astype(f32) * scale + bias_s
    ps = masked_softmax(ls, (dist_s >= 0)[:, :, :, None], (-2, -1))
    o_s = jnp.einsum('bktgnl,bktnld->btkgd', ps, vg.astype(f32))
    dist_w = qpos[:, None] - kwpos[None, :]
    mask_w = (dist_w >= 0) & (dist_w <= WINDOW) & (kwpos[None, :] >= 0)
    bias_w = jnp.transpose(bias_kvg[t5_bucket(dist_w)], (2, 3, 0, 1))
    lw = jnp.einsum('btkgd,bskd->bkgts', q, kw).astype(f32) * scale + bias_w
    pw = masked_softmax(lw, mask_w, -1)
    o_w = jnp.einsum('bkgts,bskd->btkgd', pw, vw.astype(f32))
    return jnp.stack([o_c, o_s, o_w], axis=2)


def nsa_query_side(x, g_pre, w_in):
    b, t = x.shape[:2]
    u = rms_norm(x, g_pre) @ w_in
    q = u[..., :Q_WIDTH].reshape(b, t, KV_HEADS, GROUP, HEAD_DIM)
    z = u[..., Q_WIDTH:Q_WIDTH * (1 + N_BRANCH)].reshape(b, t, N_BRANCH, N_HEADS, HEAD_DIM)
    gate = jax.nn.sigmoid(u[..., Q_WIDTH * (1 + N_BRANCH):].astype(jnp.float32)).reshape(b, t, N_BRANCH, N_HEADS)
    return q, z, gate


def nsa_output(x, o, z, gate, w_out, g_post):
    b, t = x.shape[:2]
    y = jnp.sum(gate[..., None] * o * jax.nn.silu(z.astype(jnp.float32)), axis=2)
    y = y.reshape(b, t, Q_WIDTH).astype(x.dtype)
    return x + rms_norm(y @ w_out, g_post)


def nsa_attend_prompt(q, kc, vc, ks, vs, k_win, v_win, rel_bias):
    b, t = q.shape[:2]
    nq = t // Q_BLK
    qb = q.reshape(b * nq, Q_BLK, KV_HEADS, GROUP, HEAD_DIM)
    bidx = jnp.repeat(jnp.arange(b), nq)
    t0s = jnp.tile(jnp.arange(nq) * Q_BLK, b)
    pad = ((0, 0), (WINDOW, 0), (0, 0), (0, 0))
    kw_pad = jnp.pad(k_win, pad)
    vw_pad = jnp.pad(v_win, pad)

    def step(args):
        qi, bb, t0 = args
        qpos = t0 + jnp.arange(Q_BLK)
        kw = lax.dynamic_slice_in_dim(kw_pad[bb], t0, WINDOW + Q_BLK, axis=0)[None]
        vw = lax.dynamic_slice_in_dim(vw_pad[bb], t0, WINDOW + Q_BLK, axis=0)[None]
        kwpos = t0 - WINDOW + jnp.arange(WINDOW + Q_BLK)
        return nsa_block(qi[None], qpos, kc[bb][None], vc[bb][None], ks[bb][None], vs[bb][None],
                         kw, vw, kwpos, rel_bias)[0]

    o = lax.map(step, (qb, bidx, t0s))
    return o.reshape(b, t, N_BRANCH, N_HEADS, HEAD_DIM)


def nsa_attend_sample(q, kc, vc, ks, vs, kw, vw, past_len, rel_bias):
    b, tq = q.shape[:2]
    qpos = past_len + jnp.arange(tq)
    kwpos = past_len - (kw.shape[1] - tq) + jnp.arange(kw.shape[1])

    def step(args):
        qi, kci, vci, ksi, vsi, kwi, vwi = args
        return nsa_block(qi[None], qpos, kci[None], vci[None], ksi[None], vsi[None],
                         kwi[None], vwi[None], kwpos, rel_bias)[0]

    o = lax.map(step, (q, kc, vc, ks, vs, kw, vw))
    return o.reshape(b, tq, N_BRANCH, N_HEADS, HEAD_DIM)


def setup_inputs(seed: int = 0) -> dict:
    key = jax.random.key(seed)
    k = jax.random.split(key, 26)
    n_pages = PAST_LEN // PAGE_SIZE
    n_pool = (DEC_BATCH * n_pages * 5) // 4
    w_keep = min(WINDOW, PAST_LEN)

    def nrm(kk, shape, s):
        return jax.random.normal(kk, shape, jnp.float32) * s

    def gain(kk, shape):
        return 1.0 + nrm(kk, shape, 0.05)

    page_table = jax.random.permutation(k[0], n_pool)[:DEC_BATCH * n_pages]
    page_table = page_table.reshape(DEC_BATCH, n_pages).astype(jnp.int32)
    return {
        'x_prompt': nrm(k[1], (BATCH, SEQ, D_MODEL), 1.0),
        'x_sample': nrm(k[2], (DEC_BATCH, DEC_SEQ, D_MODEL), 1.0),
        'state_conv': nrm(k[3], (N_A, DEC_BATCH, CONV_W - 1, D_INNER), 0.5),
        'cache_cmp': nrm(k[4], (n_pool, PAGE_SIZE, 2, KV_HEADS, HEAD_DIM), 1.0),
        'cache_slc': nrm(k[5], (n_pool, PAGE_SIZE, 2, KV_HEADS, HEAD_DIM), 1.0),
        'state_win': nrm(k[6], (DEC_BATCH, w_keep, 2, KV_HEADS, HEAD_DIM), 1.0),
        'page_table': page_table,
        'rel_bias': nrm(k[7], (NUM_BUCKETS, N_HEADS), 0.5),
        'a_norm_pre': gain(k[8], (N_A, D_MODEL)),
        'a_w_in': nrm(k[9], (N_A, D_MODEL, 3 * D_INNER), D_MODEL ** -0.5),
        'a_conv_w': nrm(k[10], (N_A, CONV_W, D_INNER), CONV_W ** -0.5),
        'a_conv_b': nrm(k[11], (N_A, D_INNER), 0.02),
        'a_ln_g': gain(k[12], (N_A, D_INNER)),
        'a_ln_b': nrm(k[13], (N_A, D_INNER), 0.02),
        'a_w_out': nrm(k[14], (N_A, D_INNER, D_MODEL), D_INNER ** -0.5),
        'a_norm_post': gain(k[15], (N_A, D_MODEL)),
        'kv_norm': gain(k[16], (D_MODEL,)),
        'w_kv': nrm(k[17], (D_MODEL, 2 * N_BRANCH * KV_DIM), D_MODEL ** -0.5),
        'cmp_pe': nrm(k[18], (2, L_CMP, HEAD_DIM), 0.1),
        'cmp_w1': nrm(k[19], (2, L_CMP * HEAD_DIM, CMP_HID), (L_CMP * HEAD_DIM) ** -0.5),
        'cmp_w2': nrm(k[20], (2, CMP_HID, HEAD_DIM), CMP_HID ** -0.5),
        'b_norm_pre': gain(k[21], (N_B, D_MODEL)),
        'b_w_in': nrm(k[22], (N_B, D_MODEL, B_IN), D_MODEL ** -0.5),
        'b_w_out': nrm(k[23], (N_B, Q_WIDTH, D_MODEL), Q_WIDTH ** -0.5),
        'b_norm_post': gain(k[24], (N_B, D_MODEL)),
    }


def reference(x_prompt, x_sample, state_conv, cache_cmp, cache_slc, state_win, page_table, rel_bias,
              a_norm_pre, a_w_in, a_conv_w, a_conv_b, a_ln_g, a_ln_b, a_w_out, a_norm_post,
              kv_norm, w_kv, cmp_pe, cmp_w1, cmp_w2,
              b_norm_pre, b_w_in, b_w_out, b_norm_post):
    n_pages = page_table.shape[1]
    past_len = n_pages * PAGE_SIZE
    bp, t_prompt = x_prompt.shape[:2]
    bd = x_sample.shape[0]
    xp, xs = x_prompt, x_sample
    conv_p, conv_s = [], []
    for l in range(N_A):
        zeros = jnp.zeros((bp, CONV_W - 1, D_INNER), xp.dtype)
        xp, cp = conformer_layer(xp, zeros, a_norm_pre[l], a_w_in[l], a_conv_w[l], a_conv_b[l],
                                 a_ln_g[l], a_ln_b[l], a_w_out[l], a_norm_post[l])
        xs, cs = conformer_layer(xs, state_conv[l], a_norm_pre[l], a_w_in[l], a_conv_w[l], a_conv_b[l],
                                 a_ln_g[l], a_ln_b[l], a_w_out[l], a_norm_post[l])
        conv_p.append(cp)
        conv_s.append(cs)
    cmp_p, slc_p, win_p = kv_rows(xp, kv_norm, w_kv)
    cmp_s, slc_s, win_s = kv_rows(xs, kv_norm, w_kv)
    kc_p, vc_p, ks_p, vs_p = sparse_keys(cmp_p, slc_p, cmp_pe, cmp_w1, cmp_w2)
    past_cmp = cache_cmp[page_table].reshape(bd, past_len, 2, KV_HEADS, HEAD_DIM).astype(cmp_s.dtype)
    past_slc = cache_slc[page_table].reshape(bd, past_len, 2, KV_HEADS, HEAD_DIM).astype(slc_s.dtype)
    kc_s, vc_s, ks_s, vs_s = sparse_keys(jnp.concatenate([past_cmp, cmp_s], axis=1),
                                         jnp.concatenate([past_slc, slc_s], axis=1),
                                         cmp_pe, cmp_w1, cmp_w2)
    win_all = jnp.concatenate([state_win.astype(win_s.dtype), win_s], axis=1)
    for l in range(N_B):
        qp, zp, gp = nsa_query_side(xp, b_norm_pre[l], b_w_in[l])
        op = nsa_attend_prompt(qp, kc_p, vc_p, ks_p, vs_p, win_p[:, :, 0], win_p[:, :, 1], rel_bias)
        xp = nsa_output(xp, op, zp, gp, b_w_out[l], b_norm_post[l])
        qs, zs, gs = nsa_query_side(xs, b_norm_pre[l], b_w_in[l])
        os_ = nsa_attend_sample(qs, kc_s, vc_s, ks_s, vs_s, win_all[:, :, 0], win_all[:, :, 1],
                                past_len, rel_bias)
        xs = nsa_output(xs, os_, zs, gs, b_w_out[l], b_norm_post[l])
    new_win_p = win_p[:, -min(WINDOW, t_prompt):]
    new_win_s = win_all[:, -min(WINDOW, win_all.shape[1]):]
    return (xp, xs, jnp.stack(conv_p), jnp.stack(conv_s), cmp_p, cmp_s, slc_p, slc_s, new_win_p, new_win_s)
```

```python
import functools
import math

import jax
import jax.numpy as jnp
from jax import lax
from jax.experimental import pallas as pl
from jax.experimental.pallas import tpu as pltpu

F32 = jnp.float32
BF16 = jnp.bfloat16

EPS = 1e-6
NEG = -1e30
BIG = 1e9
M_FLOOR = -1e20

HEAD_DIM = 64
KV_HEADS = 4
N_HEADS = 16
GROUP = N_HEADS // KV_HEADS
N_BRANCH = 3
CONV_W = 31
L_CMP = 32
CMP_STRIDE = 16
L_SLC = 64
N_SEL = 16
WINDOW = 512
Q_BLK = 128
NUM_BUCKETS = 32
MAX_DISTANCE = 128
MAX_EXACT = NUM_BUCKETS // 2
PAGE_SIZE = 128
LANES = 128
HALO = 32

VMEM_LIMIT = 56 * 1024 * 1024


def _cparams(*sem):
    return pltpu.CompilerParams(dimension_semantics=sem, vmem_limit_bytes=VMEM_LIMIT)


def _sigmoid(x):
    return 1.0 / (1.0 + jnp.exp(-x))


def _rms(x, g):
    return x * lax.rsqrt(jnp.mean(x * x, axis=-1, keepdims=True) + EPS) * g


def _lane(shape):
    return lax.broadcasted_iota(jnp.int32, shape, len(shape) - 1)


def _swap_halves(x):
    return pltpu.roll(x, HEAD_DIM, axis=x.ndim - 1)


def _glu_proj_body(x_ref, g_ref, w_ref, glu_ref, sz_ref):
    di = glu_ref.shape[-1]
    h = _rms(x_ref[...], g_ref[...])
    u = jnp.dot(h.astype(BF16), w_ref[...], preferred_element_type=F32)
    z = u[:, 2 * di:]
    glu_ref[...] = u[:, :di] * _sigmoid(u[:, di:2 * di])
    sz_ref[...] = z * _sigmoid(z)


def _glu_proj(x, g, w_bf, tm):
    rows, d = x.shape
    di = w_bf.shape[1] // 3
    return pl.pallas_call(
        _glu_proj_body,
        grid=(rows // tm,),
        in_specs=[pl.BlockSpec((tm, d), lambda i: (i, 0)),
                  pl.BlockSpec((1, d), lambda i: (0, 0)),
                  pl.BlockSpec((d, 3 * di), lambda i: (0, 0))],
        out_specs=[pl.BlockSpec((tm, di), lambda i: (i, 0)),
                   pl.BlockSpec((tm, di), lambda i: (i, 0))],
        out_shape=[jax.ShapeDtypeStruct((rows, di), F32)] * 2,
        compiler_params=_cparams("parallel"),
        name="glu_proj",
    )(x, g.reshape(1, d), w_bf)


def _conv_tail(c, sz, x, lg_ref, lb_ref, wo_ref, gp_ref):
    mu = jnp.mean(c, axis=-1, keepdims=True)
    cc = c - mu
    var = jnp.mean(cc * cc, axis=-1, keepdims=True)
    y = cc * lax.rsqrt(var + EPS) * lg_ref[...] + lb_ref[...]
    y = y * _sigmoid(y) * sz
    o = jnp.dot(y.astype(BF16), wo_ref[...], preferred_element_type=F32)
    return x + _rms(o, gp_ref[...])


def _conv_prompt_body(glu_ref, prev_ref, sz_ref, x_ref, cw_ref, cb_ref, lg_ref, lb_ref, wo_ref, gp_ref,
                      out_ref, full_ref, c_ref, *, tm):
    t = pl.program_id(1)
    full_ref[0:HALO, :] = jnp.where(t > 0, prev_ref[0], 0.0)
    full_ref[HALO:HALO + tm, :] = glu_ref[0]
    d = c_ref.shape[-1]
    first = HALO - (CONV_W - 1)
    n_phase = 8
    n_grp = -(-CONV_W // n_phase)
    for lc in range(d // LANES):
        ln = slice(lc * LANES, (lc + 1) * LANES)
        acc = jnp.zeros((tm, LANES), F32)
        for b in range(n_phase):
            sb = full_ref[first + b:first + b + tm + 8 * (n_grp - 1), ln]
            for a in range(n_grp):
                j = 8 * a + b
                if j < CONV_W:
                    acc = acc + cw_ref[j:j + 1, ln] * sb[8 * a:8 * a + tm]
        c_ref[:, ln] = acc + cb_ref[:, ln]
    out_ref[0] = _conv_tail(c_ref[...], sz_ref[0], x_ref[0], lg_ref, lb_ref, wo_ref, gp_ref)


def _conv_prompt(glu, sz, x, cw, cb, lg, lb, wo_bf, gp, tm):
    b, t, d = x.shape
    di = glu.shape[-1]
    per = tm // HALO
    vec = lambda n: pl.BlockSpec((1, n), lambda i, j: (0, 0))
    return pl.pallas_call(
        functools.partial(_conv_prompt_body, tm=tm),
        grid=(b, t // tm),
        in_specs=[pl.BlockSpec((1, tm, di), lambda i, j: (i, j, 0)),
                  pl.BlockSpec((1, HALO, di), lambda i, j: (i, jnp.maximum(j * per - 1, 0), 0)),
                  pl.BlockSpec((1, tm, di), lambda i, j: (i, j, 0)),
                  pl.BlockSpec((1, tm, d), lambda i, j: (i, j, 0)),
                  pl.BlockSpec((CONV_W, di), lambda i, j: (0, 0)),
                  vec(di), vec(di), vec(di),
                  pl.BlockSpec((di, d), lambda i, j: (0, 0)),
                  vec(d)],
        out_specs=pl.BlockSpec((1, tm, d), lambda i, j: (i, j, 0)),
        out_shape=jax.ShapeDtypeStruct((b, t, d), F32),
        scratch_shapes=[pltpu.VMEM((HALO + tm, di), F32), pltpu.VMEM((tm, di), F32)],
        compiler_params=_cparams("parallel", "arbitrary"),
        name="conv_prompt",
    )(glu, glu, sz, x, cw, cb.reshape(1, di), lg.reshape(1, di), lb.reshape(1, di), wo_bf, gp.reshape(1, d))


def _conv_sample_body(glu_ref, st_ref, sz_ref, x_ref, cw_ref, cb_ref, lg_ref, lb_ref, wo_ref, gp_ref,
                      out_ref, full_ref, c_ref, *, nb, tq):
    d = c_ref.shape[-1]
    first = HALO - (CONV_W - 1)
    full_ref[:, first:HALO, :] = st_ref[...]
    full_ref[:, HALO:HALO + tq, :] = glu_ref[...].reshape(nb, tq, d)
    for lc in range(d // LANES):
        ln = slice(lc * LANES, (lc + 1) * LANES)
        acc = jnp.zeros((nb, tq, LANES), F32)
        for j in range(CONV_W):
            acc = acc + cw_ref[j:j + 1, ln] * full_ref[:, first + j:first + j + tq, ln]
        c_ref[:, ln] = (acc + cb_ref[:, ln]).reshape(nb * tq, LANES)
    out_ref[...] = _conv_tail(c_ref[...], sz_ref[...], x_ref[...], lg_ref, lb_ref, wo_ref, gp_ref)


def _conv_sample(glu, state, sz, x, cw, cb, lg, lb, wo_bf, gp, nb):
    n_seq = state.shape[0]
    rows, d = x.shape
    di = glu.shape[-1]
    tq = rows // n_seq
    vec = lambda n: pl.BlockSpec((1, n), lambda i: (0, 0))
    return pl.pallas_call(
        functools.partial(_conv_sample_body, nb=nb, tq=tq),
        grid=(n_seq // nb,),
        in_specs=[pl.BlockSpec((nb * tq, di), lambda i: (i, 0)),
                  pl.BlockSpec((nb, CONV_W - 1, di), lambda i: (i, 0, 0)),
                  pl.BlockSpec((nb * tq, di), lambda i: (i, 0)),
                  pl.BlockSpec((nb * tq, d), lambda i: (i, 0)),
                  pl.BlockSpec((CONV_W, di), lambda i: (0, 0)),
                  vec(di), vec(di), vec(di),
                  pl.BlockSpec((di, d), lambda i: (0, 0)),
                  vec(d)],
        out_specs=pl.BlockSpec((nb * tq, d), lambda i: (i, 0)),
        out_shape=jax.ShapeDtypeStruct((rows, d), F32),
        scratch_shapes=[pltpu.VMEM((nb, HALO + tq, di), F32), pltpu.VMEM((nb * tq, di), F32)],
        compiler_params=_cparams("parallel"),
        name="conv_sample",
    )(glu, state, sz, x, cw, cb.reshape(1, di), lg.reshape(1, di), lb.reshape(1, di), wo_bf, gp.reshape(1, d))


def _head_major(col_pair, head):
    return col_pair if head % 2 == 0 else _swap_halves(col_pair)


def _nsa_proj_body(x_ref, gkv_ref, gq_ref, wkv_ref, wq_ref, wz_ref, wg_ref,
                   cmp_ref, slc_ref, win_ref, qa_ref, sz_ref, gate_ref, *aug_refs, seq_len, tm):
    x = x_ref[...]
    xn = x * lax.rsqrt(jnp.mean(x * x, axis=-1, keepdims=True) + EPS)
    hkv = (xn * gkv_ref[...]).astype(BF16)
    hq = (xn * gq_ref[...]).astype(BF16)
    kv = jnp.dot(hkv, wkv_ref[...], preferred_element_type=F32)
    width = 2 * KV_HEADS * HEAD_DIM
    cmp_ref[...] = kv[:, :width]
    slc_ref[...] = kv[:, width:2 * width]
    win_ref[...] = kv[:, 2 * width:]
    lane = _lane((tm, LANES))
    low = lane < HEAD_DIM
    uq = jnp.dot(hq, wq_ref[...], preferred_element_type=F32)
    scale = HEAD_DIM ** -0.5
    for h in range(N_HEADS):
        qh = _head_major(uq[:, (h // 2) * LANES:(h // 2 + 1) * LANES], h)
        qa_ref[:, h * LANES:(h + 1) * LANES] = jnp.where(low, qh * scale, 0.0).astype(BF16)
    z = jnp.dot(hq, wz_ref[...], preferred_element_type=F32)
    sz_ref[...] = z * _sigmoid(z)
    gate_ref[...] = _sigmoid(jnp.dot(hq, wg_ref[...], preferred_element_type=F32))
    if aug_refs:
        ska_ref, sv_ref, wka_ref, wv_ref = aug_refs
        row = pl.program_id(0) * tm + lax.broadcasted_iota(jnp.int32, (tm, LANES), 0)
        blk = (row % seq_len) // L_SLC
        onehot = (lane - HEAD_DIM == blk).astype(F32)
        kw = KV_HEADS * HEAD_DIM
        for h in range(KV_HEADS):
            ks = _head_major(kv[:, width + (h // 2) * LANES:width + (h // 2 + 1) * LANES], h)
            ska_ref[:, h * LANES:(h + 1) * LANES] = jnp.where(low, ks, onehot).astype(BF16)
            kwn = _head_major(kv[:, 2 * width + (h // 2) * LANES:2 * width + (h // 2 + 1) * LANES], h)
            wka_ref[:, h * LANES:(h + 1) * LANES] = jnp.where(low, kwn, 0.0).astype(BF16)
        sv_ref[...] = kv[:, width + kw:2 * width].astype(BF16)
        wv_ref[...] = kv[:, 2 * width + kw:].astype(BF16)


def _nsa_proj(x, g_kv, g_q, wkv_bf, wq_bf, wz_bf, wg_bf, tm, seq_len=None):
    rows, d = x.shape
    width = 2 * KV_HEADS * HEAD_DIM
    row = lambda n: pl.BlockSpec((tm, n), lambda i: (i, 0))
    full = lambda a: pl.BlockSpec(a.shape, lambda i: (0, 0))
    out_specs = [row(width), row(width), row(width), row(N_HEADS * LANES), row(wz_bf.shape[1]), row(LANES)]
    out_shape = [jax.ShapeDtypeStruct((rows, width), F32)] * 3 + [
        jax.ShapeDtypeStruct((rows, N_HEADS * LANES), BF16),
        jax.ShapeDtypeStruct((rows, wz_bf.shape[1]), F32),
        jax.ShapeDtypeStruct((rows, LANES), F32)]
    if seq_len is not None:
        out_specs += [row(KV_HEADS * LANES), row(width // 2), row(KV_HEADS * LANES), row(width // 2)]
        out_shape += [jax.ShapeDtypeStruct((rows, KV_HEADS * LANES), BF16),
                      jax.ShapeDtypeStruct((rows, width // 2), BF16)] * 2
    gkv = g_kv.reshape(1, d)
    gq = g_q.reshape(1, d)
    return pl.pallas_call(
        functools.partial(_nsa_proj_body, seq_len=seq_len, tm=tm),
        grid=(rows // tm,),
        in_specs=[row(d), full(gkv), full(gq), full(wkv_bf), full(wq_bf), full(wz_bf), full(wg_bf)],
        out_specs=out_specs,
        out_shape=out_shape,
        compiler_params=_cparams("parallel"),
        name="nsa_proj",
    )(x, gkv, gq, wkv_bf, wq_bf, wz_bf, wg_bf)


def _compress_body(pt_ref, *refs, n_pages, n_col):
    del pt_ref
    page_refs = refs[:n_pages * n_col]
    w1_ref, pe_ref, w2_ref, out_ref, a_ref = refs[n_pages * n_col:]
    cpp = page_refs[0].shape[1] // CMP_STRIDE
    n_chunk = n_pages * cpp
    hid = w2_ref.shape[2]
    low = _lane((cpp, LANES)) < HEAD_DIM
    for p in range(n_pages):
        for c in range(n_col):
            pr = page_refs[p * n_col + c]
            for s in range(0, CMP_STRIDE, 2):
                b0 = pr[0, pl.ds(s, cpp, stride=CMP_STRIDE), :]
                b1 = pr[0, pl.ds(s + 1, cpp, stride=CMP_STRIDE), :]
                dst = (slice(p * cpp, (p + 1) * cpp), slice((s // 2) * LANES, (s // 2 + 1) * LANES))
                a_ref[(2 * c,) + dst] = jnp.where(low, b0, _swap_halves(b1))
                a_ref[(2 * c + 1,) + dst] = jnp.where(low, _swap_halves(b0), b1)
    for kv in range(2):
        w1 = w1_ref[kv]
        pe = jnp.dot(pe_ref[kv].astype(BF16), w1, preferred_element_type=F32)
        pe_term = pe[0:1, :hid] + pe[1:2, hid:]
        for hp in range(KV_HEADS // 2):
            pair = jnp.zeros((n_chunk, LANES), F32)
            for par in range(2):
                part = jnp.dot(a_ref[kv * KV_HEADS + 2 * hp + par].astype(BF16), w1, preferred_element_type=F32)
                pre = part[:, :hid] + pltpu.roll(part[:, hid:], n_chunk - 1, axis=0) + pe_term
                mid = pre * _sigmoid(pre)
                pair = pair + jnp.dot(mid.astype(BF16), w2_ref[kv, par], preferred_element_type=F32)
            col = kv * (KV_HEADS // 2) + hp
            out_ref[0, :, col * LANES:(col + 1) * LANES] = pair


def _compress(pages, page_table, w1cat_bf, pe8, w2_bf, n_pages):
    n_seq = page_table.shape[0] // n_pages
    page_rows, width = pages.shape[1:]
    n_chunk = n_pages * page_rows // CMP_STRIDE
    n_col = width // LANES
    page_spec = lambda j, c: pl.BlockSpec((1, page_rows, LANES), lambda i, pt: (pt[i * n_pages + j], 0, c))
    full = lambda a: pl.BlockSpec(a.shape, lambda i, pt: (0,) * a.ndim)
    grid_spec = pltpu.PrefetchScalarGridSpec(
        num_scalar_prefetch=1,
        grid=(n_seq,),
        in_specs=[page_spec(j, c) for j in range(n_pages) for c in range(n_col)]
        + [full(w1cat_bf), full(pe8), full(w2_bf)],
        out_specs=pl.BlockSpec((1, n_chunk, width), lambda i, pt: (i, 0, 0)),
        scratch_shapes=[pltpu.VMEM((2 * KV_HEADS, n_chunk, CMP_STRIDE * HEAD_DIM), F32)],
    )
    return pl.pallas_call(
        functools.partial(_compress_body, n_pages=n_pages, n_col=n_col),
        grid_spec=grid_spec,
        out_shape=jax.ShapeDtypeStruct((n_seq, n_chunk, width), F32),
        compiler_params=_cparams("parallel"),
        name="compress",
    )(page_table, *([pages] * (n_pages * n_col)), w1cat_bf, pe8, w2_bf)


def _bias_tile_body(rb_ref, out_ref, *, lane_step, offset, hi):
    h = pl.program_id(0)
    shape = out_ref.shape[1:]
    d = lax.broadcasted_iota(jnp.int32, shape, 0) + lane_step * _lane(shape) + offset
    n = jnp.maximum(d, 0)
    nf = jnp.maximum(n, 1).astype(F32)
    large = MAX_EXACT + (jnp.log(nf / MAX_EXACT) / math.log(MAX_DISTANCE / MAX_EXACT)
                         * (NUM_BUCKETS - MAX_EXACT)).astype(jnp.int32)
    large = jnp.minimum(large, NUM_BUCKETS - 1)
    bucket = jnp.where(n < MAX_EXACT, n, large)
    far = rb_ref[NUM_BUCKETS - 1, h]
    val = jnp.zeros(shape, F32)
    for k in range(NUM_BUCKETS - 1):
        val = jnp.where(bucket == k, rb_ref[k, h] - far, val)
    out_ref[0] = jnp.where((d >= 0) & (d <= hi), val, NEG)


def _bias_tile(rel_bias, rows, width, lane_step, offset, hi=1 << 30):
    return pl.pallas_call(
        functools.partial(_bias_tile_body, lane_step=lane_step, offset=offset, hi=hi),
        grid=(N_HEADS,),
        in_specs=[pl.BlockSpec(memory_space=pltpu.SMEM)],
        out_specs=pl.BlockSpec((1, rows, width), lambda h: (h, 0, 0)),
        out_shape=jax.ShapeDtypeStruct((N_HEADS, rows, width), F32),
        compiler_params=_cparams("parallel"),
        name="bias_tile",
    )(rel_bias)


_NT = (((1,), (1,)), ((), ()))


def _softmax_rows(s):
    m = jnp.maximum(jnp.max(s, axis=-1, keepdims=True), M_FLOOR)
    p = jnp.exp(s - m)
    return p, jnp.sum(p, axis=-1, keepdims=True)


def _gate_col(gate, col):
    return jnp.sum(jnp.where(_lane(gate.shape) == col, gate, 0.0), axis=-1, keepdims=True)


def _pair_columns(o_even, o_odd, valid_half):
    lo = o_even if valid_half == 0 else _swap_halves(o_even)
    hi = o_odd if valid_half == 1 else _swap_halves(o_odd)
    return jnp.where(_lane(lo.shape) < HEAD_DIM, lo, hi)


def _select_blocks(imp, tb, n_sel):
    lane = _lane(imp.shape)
    lane_f = lane.astype(F32)
    j = lane - HEAD_DIM
    valid = (j >= 0) & (j <= tb)
    forced = (j == 0) | (j == tb) | (j == tb - 1)
    score = jnp.where(valid, jnp.where(forced, BIG, imp), -BIG)
    score = jnp.where(j >= 0, score, -jnp.inf)
    sel = jnp.zeros(imp.shape, jnp.bool_)
    for _ in range(n_sel):
        best = jnp.max(score, axis=-1, keepdims=True)
        first = jnp.min(jnp.where(score == best, lane_f, 4.0 * LANES), axis=-1, keepdims=True)
        pick = lane_f == first
        sel = sel | pick
        score = jnp.where(pick, -jnp.inf, score)
    return jnp.where(sel & valid, 0.0, jnp.where(j >= 0, NEG, 0.0))


def _cmp_prompt_body(qa_ref, kc_ref, vc_ref, m_ref, cb_ref, gate_ref, oc_ref, qsel_ref, *, n_key):
    qt = pl.program_id(1)
    start = pl.multiple_of(8 * qt + 8, 8)
    kwin = kc_ref[0, pl.ds(start, n_key), :]
    vwin = vc_ref[0, pl.ds(start, n_key), :]
    mwin = m_ref[pl.ds(start, n_key), :]
    lane = _lane((Q_BLK, LANES))
    low = lane < HEAD_DIM
    pos = qt * Q_BLK + lax.broadcasted_iota(jnp.int32, (Q_BLK, 1), 0)
    tb = pos // L_SLC
    gate = gate_ref[...]
    for kvh in range(KV_HEADS):
        ka = kwin[:, kvh * LANES:(kvh + 1) * LANES].astype(BF16)
        vp = vwin[:, (kvh // 2) * LANES:(kvh // 2 + 1) * LANES].astype(BF16)
        pc_sum = jnp.zeros((Q_BLK, n_key), F32)
        outs = []
        qs = []
        for g in range(GROUP):
            h = kvh * GROUP + g
            q = qa_ref[0, :, h * LANES:(h + 1) * LANES].astype(F32)
            qs.append(q)
            qa = jnp.where(lane == HEAD_DIM, NEG, q).astype(BF16)
            s = lax.dot_general(qa, ka, _NT, preferred_element_type=F32) + cb_ref[h]
            p, l = _softmax_rows(s)
            inv = 1.0 / jnp.maximum(l, 1e-30)
            pc_sum = pc_sum + p * inv
            o = jnp.dot(p.astype(BF16), vp, preferred_element_type=F32)
            outs.append(o * (inv * _gate_col(gate, h)))
        for gp in range(GROUP // 2):
            col = kvh * (GROUP // 2) + gp
            oc_ref[0, :, col * LANES:(col + 1) * LANES] = _pair_columns(outs[2 * gp], outs[2 * gp + 1], kvh % 2)
        imp = jnp.dot(pc_sum, mwin, preferred_element_type=F32, precision=lax.Precision.HIGHEST)
        selneg = _select_blocks(imp, tb, N_SEL)
        for g in range(GROUP):
            qsel_ref[0, kvh, g * Q_BLK:(g + 1) * Q_BLK, :] = jnp.where(low, qs[g], selneg).astype(BF16)


def _cmp_prompt(qa, kc_pad, vc_pad, m_pad, cb, gate, batch, seq):
    nq = seq // Q_BLK
    n_key = kc_pad.shape[1] // 2
    d_out = N_HEADS * HEAD_DIM
    return pl.pallas_call(
        functools.partial(_cmp_prompt_body, n_key=n_key),
        grid=(batch, nq),
        in_specs=[pl.BlockSpec((1, Q_BLK, N_HEADS * LANES), lambda b, t: (b, t, 0)),
                  pl.BlockSpec((1,) + kc_pad.shape[1:], lambda b, t: (b, 0, 0)),
                  pl.BlockSpec((1,) + vc_pad.shape[1:], lambda b, t: (b, 0, 0)),
                  pl.BlockSpec(m_pad.shape, lambda b, t: (0, 0)),
                  pl.BlockSpec(cb.shape, lambda b, t: (0, 0, 0)),
                  pl.BlockSpec((Q_BLK, LANES), lambda b, t: (b * nq + t, 0))],
        out_specs=[pl.BlockSpec((1, Q_BLK, d_out), lambda b, t: (b, t, 0)),
                   pl.BlockSpec((1, KV_HEADS, GROUP * Q_BLK, LANES), lambda b, t: (b * nq + t, 0, 0, 0))],
        out_shape=[jax.ShapeDtypeStruct((batch, seq, d_out), F32),
                   jax.ShapeDtypeStruct((batch * nq, KV_HEADS, GROUP * Q_BLK, LANES), BF16)],
        compiler_params=_cparams("parallel", "parallel"),
        name="cmp_prompt",
    )(qa.reshape(batch, seq, -1), kc_pad, vc_pad, m_pad, cb, gate)


KEY_TILE = 256


def _slc_prompt_body(q_ref, k_ref, v_ref, sb_ref, gate_ref, out_ref, m_ref, l_ref, acc_ref):
    kvh = pl.program_id(1)
    qt = pl.program_id(2)
    rows = GROUP * Q_BLK
    q = q_ref[0, 0]
    m_ref[...] = jnp.full((rows, 1), M_FLOOR, F32)
    l_ref[...] = jnp.zeros((rows, 1), F32)
    acc_ref[...] = jnp.zeros((rows, LANES), F32)
    last = (qt * Q_BLK) // KEY_TILE

    def step(kt, carry):
        base = pl.multiple_of(kt * KEY_TILE, KEY_TILE)
        ks = k_ref[0, pl.ds(base, KEY_TILE), :]
        vs = v_ref[0, pl.ds(base, KEY_TILE), :]
        case = jnp.minimum(qt % 2 + 2 * (last - kt), 3)
        s = lax.dot_general(q, ks, _NT, preferred_element_type=F32)
        s = (s.reshape(GROUP, Q_BLK, KEY_TILE) + sb_ref[case, 0]).reshape(rows, KEY_TILE)
        m_old = m_ref[...]
        m_new = jnp.maximum(m_old, jnp.max(s, axis=-1, keepdims=True))
        alpha = jnp.exp(m_old - m_new)
        p = jnp.exp(s - m_new)
        l_ref[...] = alpha * l_ref[...] + jnp.sum(p, axis=-1, keepdims=True)
        acc_ref[...] = alpha * acc_ref[...] + jnp.dot(p.astype(BF16), vs, preferred_element_type=F32)
        m_ref[...] = m_new
        return carry

    lax.fori_loop(0, last + 1, step, 0)
    o = acc_ref[...] * (1.0 / jnp.maximum(l_ref[...], 1e-30))
    gate = gate_ref[...]
    for parity in range(2):
        @pl.when(kvh % 2 == parity)
        def _():
            outs = []
            for g in range(GROUP):
                gcol = _gate_col(gate, N_HEADS + kvh * GROUP + g)
                outs.append(o[g * Q_BLK:(g + 1) * Q_BLK] * gcol)
            for gp in range(GROUP // 2):
                out_ref[0, :, gp * LANES:(gp + 1) * LANES] = _pair_columns(outs[2 * gp], outs[2 * gp + 1], parity)


def _slc_prompt(qsel, ska, sv, sb, gate, batch, seq):
    nq = seq // Q_BLK
    rows = GROUP * Q_BLK
    d_out = N_HEADS * HEAD_DIM
    return pl.pallas_call(
        _slc_prompt_body,
        grid=(batch, KV_HEADS, nq),
        in_specs=[pl.BlockSpec((1, 1, rows, LANES), lambda b, h, t: (b * nq + t, h, 0, 0)),
                  pl.BlockSpec((1, seq, LANES), lambda b, h, t: (b, 0, h)),
                  pl.BlockSpec((1, seq, LANES), lambda b, h, t: (b, 0, h // 2)),
                  pl.BlockSpec((4, 1, GROUP, Q_BLK, KEY_TILE), lambda b, h, t: (0, h, 0, 0, 0)),
                  pl.BlockSpec((Q_BLK, LANES), lambda b, h, t: (b * nq + t, 0))],
        out_specs=pl.BlockSpec((1, Q_BLK, GROUP * HEAD_DIM), lambda b, h, t: (b, t, h)),
        out_shape=jax.ShapeDtypeStruct((batch, seq, d_out), F32),
        scratch_shapes=[pltpu.VMEM((rows, 1), F32), pltpu.VMEM((rows, 1), F32), pltpu.VMEM((rows, LANES), F32)],
        compiler_params=_cparams("parallel", "parallel", "arbitrary"),
        name="slc_prompt",
    )(qsel, ska.reshape(batch, seq, -1), sv.reshape(batch, seq, -1), sb, gate)


def _win_prompt_body(qa_ref, k_ref, v_ref, wb_ref, gate_ref, out_ref, *, n_key):
    kvh = pl.program_id(1)
    qt = pl.program_id(2)
    base = pl.multiple_of(qt * Q_BLK, Q_BLK)
    ks = k_ref[0, pl.ds(base, n_key), :]
    vs = v_ref[0, pl.ds(base, n_key), :]
    lane = _lane((Q_BLK, LANES))
    gate = gate_ref[...]
    outs = []
    for g in range(GROUP):
        q = qa_ref[0, :, g * LANES:(g + 1) * LANES]
        qa = jnp.where(lane == HEAD_DIM, jnp.asarray(NEG, BF16), q)
        s = lax.dot_general(qa, ks, _NT, preferred_element_type=F32) + wb_ref[g]
        p, l = _softmax_rows(s)
        o = jnp.dot(p.astype(BF16), vs, preferred_element_type=F32)
        gcol = _gate_col(gate, 2 * N_HEADS + kvh * GROUP + g)
        outs.append(o * (gcol / jnp.maximum(l, 1e-30)))
    for parity in range(2):
        @pl.when(kvh % 2 == parity)
        def _():
            for gp in range(GROUP // 2):
                out_ref[0, :, gp * LANES:(gp + 1) * LANES] = _pair_columns(outs[2 * gp], outs[2 * gp + 1], parity)


def _win_prompt(qa, wka_pad, wv_pad, wb, gate, batch, seq):
    nq = seq // Q_BLK
    n_key = WINDOW + Q_BLK
    d_out = N_HEADS * HEAD_DIM
    padded = wka_pad.shape[1]
    return pl.pallas_call(
        functools.partial(_win_prompt_body, n_key=n_key),
        grid=(batch, KV_HEADS, nq),
        in_specs=[pl.BlockSpec((1, Q_BLK, GROUP * LANES), lambda b, h, t: (b, t, h)),
                  pl.BlockSpec((1, padded, LANES), lambda b, h, t: (b, 0, h)),
                  pl.BlockSpec((1, padded, LANES), lambda b, h, t: (b, 0, h // 2)),
                  pl.BlockSpec((GROUP, Q_BLK, n_key), lambda b, h, t: (h, 0, 0)),
                  pl.BlockSpec((Q_BLK, LANES), lambda b, h, t: (b * nq + t, 0))],
        out_specs=pl.BlockSpec((1, Q_BLK, GROUP * HEAD_DIM), lambda b, h, t: (b, t, h)),
        out_shape=jax.ShapeDtypeStruct((batch, seq, d_out), F32),
        compiler_params=_cparams("parallel", "parallel", "parallel"),
        name="win_prompt",
    )(qa.reshape(batch, seq, -1), wka_pad, wv_pad, wb, gate)


def _nsa_out_body(oc_ref, os_ref, ow_ref, sz_ref, x_ref, wo_ref, gp_ref, out_ref):
    d = oc_ref.shape[-1]
    y = oc_ref[...] * sz_ref[:, :d] + os_ref[...] * sz_ref[:, d:2 * d] + ow_ref[...] * sz_ref[:, 2 * d:]
    o = jnp.dot(y.astype(BF16), wo_ref[...], preferred_element_type=F32)
    out_ref[...] = x_ref[...] + _rms(o, gp_ref[...])


def _nsa_out(oc, os_, ow, sz, x, wo_bf, gp, tm):
    rows, d = x.shape
    dq = oc.shape[-1]
    row = lambda n: pl.BlockSpec((tm, n), lambda i: (i, 0))
    return pl.pallas_call(
        _nsa_out_body,
        grid=(rows // tm,),
        in_specs=[row(dq), row(dq), row(dq), row(N_BRANCH * dq), row(d),
                  pl.BlockSpec((dq, d), lambda i: (0, 0)), pl.BlockSpec((1, d), lambda i: (0, 0))],
        out_specs=row(d),
        out_shape=jax.ShapeDtypeStruct((rows, d), F32),
        compiler_params=_cparams("parallel"),
        name="nsa_out",
    )(oc, os_, ow, sz, x, wo_bf, gp.reshape(1, d))


def _sample_attn_body(pt_ref, *refs, n_pages, tq, w_keep):
    del pt_ref
    page_refs = refs[:n_pages]
    (qa_ref, kvc_ref, snew_ref, wst_ref, wnew_ref, m_ref, e_ref, cbs_ref, sbs_ref, wbs_ref, gate_ref,
     oc_ref, os_ref, ow_ref, ksel_ref, kwin_ref) = refs[n_pages:]
    n_past = n_pages * PAGE_SIZE
    n_sk = ksel_ref.shape[0]
    n_wk = kwin_ref.shape[0]
    width = ksel_ref.shape[1]
    kw = width // 2
    for p in range(n_pages):
        ksel_ref[p * PAGE_SIZE:(p + 1) * PAGE_SIZE, :] = page_refs[p][0]
    ksel_ref[n_past:n_past + tq, :] = snew_ref[...]
    ksel_ref[n_past + tq:, :] = jnp.zeros((n_sk - n_past - tq, width), F32)
    kwin_ref[0:w_keep, :] = wst_ref[0]
    kwin_ref[w_keep:w_keep + tq, :] = wnew_ref[...]
    kwin_ref[w_keep + tq:, :] = jnp.zeros((n_wk - w_keep - tq, width), F32)
    pos = n_past + lax.broadcasted_iota(jnp.int32, (tq, 1), 0)
    tb = pos // L_SLC
    gate = gate_ref[...]
    rows = GROUP * tq

    def attend(q, k, v, bias, extra=None):
        s = lax.dot_general(q, k, _NT, preferred_element_type=F32)
        if extra is not None:
            s = s + extra
        s = (s.reshape(GROUP, tq, s.shape[-1]) + bias).reshape(rows, s.shape[-1])
        p, l = _softmax_rows(s)
        inv = 1.0 / jnp.maximum(l, 1e-30)
        return p, inv, jnp.dot(p.astype(BF16), v, preferred_element_type=F32)

    def store(out_ref, o, kvh, branch):
        outs = [o[g * tq:(g + 1) * tq] * _gate_col(gate, branch * N_HEADS + kvh * GROUP + g) for g in range(GROUP)]
        for gp in range(GROUP // 2):
            col = kvh * (GROUP // 2) + gp
            out_ref[:, col * LANES:(col + 1) * LANES] = _pair_columns(outs[2 * gp], outs[2 * gp + 1], kvh % 2)

    for pr in range(KV_HEADS // 2):
        kcol = slice(pr * LANES, (pr + 1) * LANES)
        vcol = slice(kw + pr * LANES, kw + (pr + 1) * LANES)
        kc = kvc_ref[0, :, kcol].astype(BF16)
        vc = kvc_ref[0, :, vcol].astype(BF16)
        ksl = ksel_ref[:, kcol].astype(BF16)
        vsl = ksel_ref[:, vcol].astype(BF16)
        kwn = kwin_ref[:, kcol].astype(BF16)
        vwn = kwin_ref[:, vcol].astype(BF16)
        for half in range(2):
            kvh = 2 * pr + half
            heads = slice(kvh * GROUP, (kvh + 1) * GROUP)
            q = jnp.concatenate([qa_ref[:, h * LANES:(h + 1) * LANES]
                                 for h in range(kvh * GROUP, (kvh + 1) * GROUP)], axis=0)
            q = (q if half == 0 else _swap_halves(q)).astype(BF16)
            p, inv, o = attend(q, kc, vc, cbs_ref[heads])
            store(oc_ref, o * inv, kvh, 0)
            pc_sum = jnp.sum((p * inv).reshape(GROUP, tq, p.shape[-1]), axis=0)
            imp = jnp.dot(pc_sum, m_ref[...], preferred_element_type=F32, precision=lax.Precision.HIGHEST)
            selneg = _select_blocks(imp, tb, N_SEL)
            qm = jnp.concatenate([selneg] * GROUP, axis=0).astype(BF16)
            block_mask = jnp.dot(qm, e_ref[...], preferred_element_type=F32)
            _, inv, o = attend(q, ksl, vsl, sbs_ref[heads], block_mask)
            store(os_ref, o * inv, kvh, 1)
            _, inv, o = attend(q, kwn, vwn, wbs_ref[heads])
            store(ow_ref, o * inv, kvh, 2)


def _sample_attn(qa, kvc, pages, page_table, slc_new, win_state, win_new, m_s, e_s, cbs, sbs, wbs, gate,
                 n_seq, tq, n_pages):
    width = pages.shape[-1]
    n_sk = sbs.shape[-1]
    n_wk = wbs.shape[-1]
    w_keep = win_state.shape[1]
    d_out = N_HEADS * HEAD_DIM
    page_spec = lambda j: pl.BlockSpec((1, PAGE_SIZE, width), lambda i, pt: (pt[i * n_pages + j], 0, 0))
    full = lambda a: pl.BlockSpec(a.shape, lambda i, pt: (0,) * a.ndim)
    row = lambda n: pl.BlockSpec((tq, n), lambda i, pt: (i, 0))
    grid_spec = pltpu.PrefetchScalarGridSpec(
        num_scalar_prefetch=1,
        grid=(n_seq,),
        in_specs=[page_spec(j) for j in range(n_pages)] + [
            row(N_HEADS * LANES),
            pl.BlockSpec((1,) + kvc.shape[1:], lambda i, pt: (i, 0, 0)),
            row(width),
            pl.BlockSpec((1, w_keep, width), lambda i, pt: (i, 0, 0)),
            row(width),
            full(m_s), full(e_s), full(cbs), full(sbs), full(wbs),
            row(LANES)],
        out_specs=[row(d_out)] * 3,
        scratch_shapes=[pltpu.VMEM((n_sk, width), F32), pltpu.VMEM((n_wk, width), F32)],
    )
    return pl.pallas_call(
        functools.partial(_sample_attn_body, n_pages=n_pages, tq=tq, w_keep=w_keep),
        grid_spec=grid_spec,
        out_shape=[jax.ShapeDtypeStruct((n_seq * tq, d_out), F32)] * 3,
        compiler_params=_cparams("parallel"),
        name="sample_attn",
    )(page_table, *([pages] * n_pages), qa, kvc, slc_new, win_state, win_new, m_s, e_s, cbs, sbs, wbs, gate)


def _overlap_matrix(n_rows, row0, n_cmp, n_blk):
    import numpy as np
    m = np.zeros((n_rows, LANES), np.float32)
    cs = np.arange(n_cmp)[:, None] * CMP_STRIDE
    js = np.arange(n_blk)[None, :] * L_SLC
    m[row0:row0 + n_cmp, HEAD_DIM:HEAD_DIM + n_blk] = (cs <= js + L_SLC - 1) & (cs + L_CMP - 1 >= js)
    return jnp.asarray(m)


def kernel(x_prompt, x_sample, state_conv, cache_cmp, cache_slc, state_win, page_table, rel_bias, a_norm_pre, a_w_in, a_conv_w, a_conv_b, a_ln_g, a_ln_b, a_w_out, a_norm_post, kv_norm, w_kv, cmp_pe, cmp_w1, cmp_w2, b_norm_pre, b_w_in, b_w_out, b_norm_post):
    import numpy as np
    bp, tp, d = x_prompt.shape
    bd, tq, _ = x_sample.shape
    n_pages = page_table.shape[1]
    past = n_pages * PAGE_SIZE
    w_keep = state_win.shape[1]
    width = 2 * KV_HEADS * HEAD_DIM
    kw = KV_HEADS * HEAD_DIM
    dq = N_HEADS * HEAD_DIM
    assert b_w_in.shape[0] == 1 and tp % (2 * KEY_TILE) == 0 and tp // L_SLC <= HEAD_DIM
    tm = 256

    xp = x_prompt
    xs = x_sample.reshape(bd * tq, d)
    conv_p, conv_s = [], []
    for l in range(a_w_in.shape[0]):
        w_in = a_w_in[l].astype(BF16)
        w_out = a_w_out[l].astype(BF16)
        di = a_w_out.shape[1]
        tail = (a_conv_w[l], a_conv_b[l], a_ln_g[l], a_ln_b[l], w_out, a_norm_post[l])
        glu, sz = _glu_proj(xp.reshape(bp * tp, d), a_norm_pre[l], w_in, tm)
        glu = glu.reshape(bp, tp, di)
        xp = _conv_prompt(glu, sz.reshape(bp, tp, di), xp, *tail, tm)
        conv_p.append(glu[:, -(CONV_W - 1):])
        glu, sz = _glu_proj(xs, a_norm_pre[l], w_in, tm)
        xs = _conv_sample(glu, state_conv[l], sz, xs, *tail, 16)
        conv_s.append(jnp.concatenate([state_conv[l], glu.reshape(bd, tq, di)], axis=1)[:, -(CONV_W - 1):])

    bw = b_w_in[0]
    n_gate = N_BRANCH * N_HEADS
    wq = bw[:, :dq].astype(BF16)
    wz = bw[:, dq:dq * (1 + N_BRANCH)].astype(BF16)
    wg = jnp.pad(bw[:, dq * (1 + N_BRANCH):], ((0, 0), (0, LANES - n_gate))).astype(BF16)
    wkv = w_kv.astype(BF16)
    xp2 = xp.reshape(bp * tp, d)
    (cmp_p, slc_p, win_p, qa_p, sz_p, gate_p, ska, sv, wka, wv) = _nsa_proj(
        xp2, kv_norm, b_norm_pre[0], wkv, wq, wz, wg, tm, seq_len=tp)
    cmp_s, slc_s, win_s, qa_s, sz_s, gate_s = _nsa_proj(xs, kv_norm, b_norm_pre[0], wkv, wq, wz, wg, tm)
    qa_s = qa_s.astype(F32)

    w1cat = jnp.concatenate([cmp_w1[:, :CMP_STRIDE * HEAD_DIM], cmp_w1[:, CMP_STRIDE * HEAD_DIM:]], axis=2).astype(BF16)
    pe8 = jnp.pad(cmp_pe.reshape(2, L_CMP // CMP_STRIDE, CMP_STRIDE * HEAD_DIM), ((0, 0), (0, 6), (0, 0)))
    zero = jnp.zeros_like(cmp_w2)
    w2h = jnp.stack([jnp.concatenate([cmp_w2, zero], axis=2), jnp.concatenate([zero, cmp_w2], axis=2)], axis=1).astype(BF16)
    prompt_page = 1024
    pp = tp // prompt_page
    kvc_p = _compress(cmp_p.reshape(bp * pp, prompt_page, width), jnp.arange(bp * pp, dtype=jnp.int32), w1cat, pe8, w2h, pp)
    pt_flat = page_table.reshape(-1).astype(jnp.int32)
    kvc_s = _compress(cache_cmp.reshape(-1, PAGE_SIZE, width), pt_flat, w1cat, pe8, w2h, n_pages)

    n_chunk_p = kvc_p.shape[1]
    nc_p = n_chunk_p - L_CMP // CMP_STRIDE + 1
    real = (jnp.arange(n_chunk_p) < nc_p)[None, :, None]
    flag = jnp.zeros((LANES - HEAD_DIM,), F32).at[0].set(1.0)
    kc4 = jnp.where(real, kvc_p[:, :, :kw], 0.0).reshape(bp, n_chunk_p, KV_HEADS, HEAD_DIM)
    aug = jnp.where(real[..., None], 0.0, flag) * jnp.ones((bp, n_chunk_p, KV_HEADS, 1), F32)
    kc_real = jnp.concatenate([kc4, aug], axis=-1).reshape(bp, n_chunk_p, KV_HEADS * LANES)
    pad_row = jnp.concatenate([jnp.zeros((HEAD_DIM,), F32), flag])
    kc_front = jnp.broadcast_to(jnp.tile(pad_row, KV_HEADS), (bp, n_chunk_p, KV_HEADS * LANES))
    kc_pad = jnp.concatenate([kc_front, kc_real], axis=1)
    vc_pad = jnp.concatenate([jnp.zeros((bp, n_chunk_p, kw), F32), jnp.where(real, kvc_p[:, :, kw:], 0.0)], axis=1)
    m_pad = _overlap_matrix(2 * n_chunk_p, n_chunk_p, nc_p, tp // L_SLC)
    cb = _bias_tile(rel_bias, Q_BLK, n_chunk_p, -CMP_STRIDE, CMP_STRIDE * (n_chunk_p - Q_BLK // CMP_STRIDE) - (L_CMP - 1))
    oc_p, qsel = _cmp_prompt(qa_p, kc_pad, vc_pad, m_pad, cb, gate_p, bp, tp)
    sb = jnp.stack([_bias_tile(rel_bias, Q_BLK, KEY_TILE, -1, off) for off in (0, Q_BLK, 2 * Q_BLK, 3 * Q_BLK)])
    sb = sb.reshape(4, KV_HEADS, GROUP, Q_BLK, KEY_TILE)
    os_p = _slc_prompt(qsel, ska, sv, sb, gate_p, bp, tp)
    wb = _bias_tile(rel_bias, Q_BLK, WINDOW + Q_BLK, -1, WINDOW, WINDOW)
    wka_front = jnp.broadcast_to(jnp.tile(pad_row, KV_HEADS).astype(BF16), (bp, WINDOW, KV_HEADS * LANES))
    wka_pad = jnp.concatenate([wka_front, wka.reshape(bp, tp, -1)], axis=1)
    wv_pad = jnp.concatenate([jnp.zeros((bp, WINDOW, kw), BF16), wv.reshape(bp, tp, -1)], axis=1)
    ow_p = _win_prompt(qa_p, wka_pad, wv_pad, wb, gate_p, bp, tp)
    wo = b_w_out[0].astype(BF16)
    y_p = _nsa_out(oc_p.reshape(bp * tp, dq), os_p.reshape(bp * tp, dq), ow_p.reshape(bp * tp, dq), sz_p, xp2,
                   wo, b_norm_post[0], tm)

    n_chunk_s = kvc_s.shape[1]
    nc_s = n_chunk_s - L_CMP // CMP_STRIDE + 1
    nb_s = -(-(past + tq) // L_SLC)
    n_sk = -(-(nb_s * L_SLC) // LANES) * LANES
    n_wk = -(-(w_keep + tq) // LANES) * LANES
    m_s = _overlap_matrix(n_chunk_s, 0, nc_s, nb_s)
    e_np = np.zeros((LANES, n_sk), np.float32)
    e_np[HEAD_DIM + np.arange(n_sk) // L_SLC, np.arange(n_sk)] = 1.0
    e_s = jnp.asarray(e_np, BF16)
    cbs = _bias_tile(rel_bias, tq, n_chunk_s, -CMP_STRIDE, past - (L_CMP - 1))
    sbs = _bias_tile(rel_bias, tq, n_sk, -1, past)
    wbs = _bias_tile(rel_bias, tq, n_wk, -1, w_keep, WINDOW)
    oc_s, os_s, ow_s = _sample_attn(qa_s, kvc_s, cache_slc.reshape(-1, PAGE_SIZE, width), pt_flat, slc_s,
                                    state_win.reshape(bd, w_keep, width), win_s, m_s, e_s, cbs, sbs, wbs, gate_s,
                                    bd, tq, n_pages)
    y_s = _nsa_out(oc_s, os_s, ow_s, sz_s, xs, wo, b_norm_post[0], tm)

    kv5 = lambda a, b, t: a.reshape(b, t, 2, KV_HEADS, HEAD_DIM)
    win_p5 = kv5(win_p, bp, tp)
    win_s5 = kv5(win_s, bd, tq)
    win_all = jnp.concatenate([state_win, win_s5], axis=1)
    return (y_p.reshape(bp, tp, d), y_s.reshape(bd, tq, d), jnp.stack(conv_p), jnp.stack(conv_s),
            kv5(cmp_p, bp, tp), kv5(cmp_s, bd, tq), kv5(slc_p, bp, tp), kv5(slc_s, bd, tq),
            win_p5[:, -min(WINDOW, tp):], win_all[:, -min(WINDOW, win_all.shape[1]):])
```

```python
import functools
import math

import jax
import jax.numpy as jnp
from jax import lax
from jax.experimental import pallas as pl
from jax.experimental.pallas import tpu as pltpu

F32 = jnp.float32
BF16 = jnp.bfloat16

EPS = 1e-6
NEG = -1e30
BIG = 1e9
M_FLOOR = -1e20

HEAD_DIM = 64
KV_HEADS = 4
N_HEADS = 16
GROUP = N_HEADS // KV_HEADS
N_BRANCH = 3
CONV_W = 31
L_CMP = 32
CMP_STRIDE = 16
L_SLC = 64
N_SEL = 16
WINDOW = 512
Q_BLK = 128
NUM_BUCKETS = 32
MAX_DISTANCE = 128
MAX_EXACT = NUM_BUCKETS // 2
PAGE_SIZE = 128
LANES = 128
HALO = 32

VMEM_LIMIT = 56 * 1024 * 1024


def _cparams(*sem):
    return pltpu.CompilerParams(dimension_semantics=sem, vmem_limit_bytes=VMEM_LIMIT)


def _sigmoid(x):
    return 1.0 / (1.0 + jnp.exp(-x))


def _rms(x, g):
    return x * lax.rsqrt(jnp.mean(x * x, axis=-1, keepdims=True) + EPS) * g


def _lane(shape):
    return lax.broadcasted_iota(jnp.int32, shape, len(shape) - 1)


def _swap_halves(x):
    return pltpu.roll(x, HEAD_DIM, axis=x.ndim - 1)


def _glu_proj_body(x_ref, g_ref, w_ref, glu_ref, sz_ref):
    di = glu_ref.shape[-1]
    h = _rms(x_ref[...], g_ref[...])
    u = jnp.dot(h.astype(BF16), w_ref[...], preferred_element_type=F32)
    z = u[:, 2 * di:]
    glu_ref[...] = u[:, :di] * _sigmoid(u[:, di:2 * di])
    sz_ref[...] = z * _sigmoid(z)


def _glu_proj(x, g, w_bf, tm):
    rows, d = x.shape
    di = w_bf.shape[1] // 3
    return pl.pallas_call(
        _glu_proj_body,
        grid=(rows // tm,),
        in_specs=[pl.BlockSpec((tm, d), lambda i: (i, 0)),
                  pl.BlockSpec((1, d), lambda i: (0, 0)),
                  pl.BlockSpec((d, 3 * di), lambda i: (0, 0))],
        out_specs=[pl.BlockSpec((tm, di), lambda i: (i, 0)),
                   pl.BlockSpec((tm, di), lambda i: (i, 0))],
        out_shape=[jax.ShapeDtypeStruct((rows, di), F32)] * 2,
        compiler_params=_cparams("parallel"),
        name="glu_proj",
    )(x, g.reshape(1, d), w_bf)


def _conv_tail(c, sz, x, lg_ref, lb_ref, wo_ref, gp_ref):
    mu = jnp.mean(c, axis=-1, keepdims=True)
    cc = c - mu
    var = jnp.mean(cc * cc, axis=-1, keepdims=True)
    y = cc * lax.rsqrt(var + EPS) * lg_ref[...] + lb_ref[...]
    y = y * _sigmoid(y) * sz
    o = jnp.dot(y.astype(BF16), wo_ref[...], preferred_element_type=F32)
    return x + _rms(o, gp_ref[...])


def _conv_prompt_body(glu_ref, prev_ref, sz_ref, x_ref, cw_ref, cb_ref, lg_ref, lb_ref, wo_ref, gp_ref,
                      out_ref, full_ref, c_ref, *, tm):
    t = pl.program_id(1)
    full_ref[0:HALO, :] = jnp.where(t > 0, prev_ref[0], 0.0)
    full_ref[HALO:HALO + tm, :] = glu_ref[0]
    d = c_ref.shape[-1]
    first = HALO - (CONV_W - 1)
    n_phase = 8
    n_grp = -(-CONV_W // n_phase)
    for lc in range(d // LANES):
        ln = slice(lc * LANES, (lc + 1) * LANES)
        acc = jnp.zeros((tm, LANES), F32)
        for b in range(n_phase):
            sb = full_ref[first + b:first + b + tm + 8 * (n_grp - 1), ln]
            for a in range(n_grp):
                j = 8 * a + b
                if j < CONV_W:
                    acc = acc + cw_ref[j:j + 1, ln] * sb[8 * a:8 * a + tm]
        c_ref[:, ln] = acc + cb_ref[:, ln]
    out_ref[0] = _conv_tail(c_ref[...], sz_ref[0], x_ref[0], lg_ref, lb_ref, wo_ref, gp_ref)


def _conv_prompt(glu, sz, x, cw, cb, lg, lb, wo_bf, gp, tm):
    b, t, d = x.shape
    di = glu.shape[-1]
    per = tm // HALO
    vec = lambda n: pl.BlockSpec((1, n), lambda i, j: (0, 0))
    return pl.pallas_call(
        functools.partial(_conv_prompt_body, tm=tm),
        grid=(b, t // tm),
        in_specs=[pl.BlockSpec((1, tm, di), lambda i, j: (i, j, 0)),
                  pl.BlockSpec((1, HALO, di), lambda i, j: (i, jnp.maximum(j * per - 1, 0), 0)),
                  pl.BlockSpec((1, tm, di), lambda i, j: (i, j, 0)),
                  pl.BlockSpec((1, tm, d), lambda i, j: (i, j, 0)),
                  pl.BlockSpec((CONV_W, di), lambda i, j: (0, 0)),
                  vec(di), vec(di), vec(di),
                  pl.BlockSpec((di, d), lambda i, j: (0, 0)),
                  vec(d)],
        out_specs=pl.BlockSpec((1, tm, d), lambda i, j: (i, j, 0)),
        out_shape=jax.ShapeDtypeStruct((b, t, d), F32),
        scratch_shapes=[pltpu.VMEM((HALO + tm, di), F32), pltpu.VMEM((tm, di), F32)],
        compiler_params=_cparams("parallel", "arbitrary"),
        name="conv_prompt",
    )(glu, glu, sz, x, cw, cb.reshape(1, di), lg.reshape(1, di), lb.reshape(1, di), wo_bf, gp.reshape(1, d))


def _conv_sample_body(glu_ref, st_ref, sz_ref, x_ref, cw_ref, cb_ref, lg_ref, lb_ref, wo_ref, gp_ref,
                      out_ref, full_ref, c_ref, *, nb, tq):
    d = c_ref.shape[-1]
    first = HALO - (CONV_W - 1)
    full_ref[:, first:HALO, :] = st_ref[...]
    full_ref[:, HALO:HALO + tq, :] = glu_ref[...].reshape(nb, tq, d)
    for lc in range(d // LANES):
        ln = slice(lc * LANES, (lc + 1) * LANES)
        acc = jnp.zeros((nb, tq, LANES), F32)
        for j in range(CONV_W):
            acc = acc + cw_ref[j:j + 1, ln] * full_ref[:, first + j:first + j + tq, ln]
        c_ref[:, ln] = (acc + cb_ref[:, ln]).reshape(nb * tq, LANES)
    out_ref[...] = _conv_tail(c_ref[...], sz_ref[...], x_ref[...], lg_ref, lb_ref, wo_ref, gp_ref)


def _conv_sample(glu, state, sz, x, cw, cb, lg, lb, wo_bf, gp, nb):
    n_seq = state.shape[0]
    rows, d = x.shape
    di = glu.shape[-1]
    tq = rows // n_seq
    vec = lambda n: pl.BlockSpec((1, n), lambda i: (0, 0))
    return pl.pallas_call(
        functools.partial(_conv_sample_body, nb=nb, tq=tq),
        grid=(n_seq // nb,),
        in_specs=[pl.BlockSpec((nb * tq, di), lambda i: (i, 0)),
                  pl.BlockSpec((nb, CONV_W - 1, di), lambda i: (i, 0, 0)),
                  pl.BlockSpec((nb * tq, di), lambda i: (i, 0)),
                  pl.BlockSpec((nb * tq, d), lambda i: (i, 0)),
                  pl.BlockSpec((CONV_W, di), lambda i: (0, 0)),
                  vec(di), vec(di), vec(di),
                  pl.BlockSpec((di, d), lambda i: (0, 0)),
                  vec(d)],
        out_specs=pl.BlockSpec((nb * tq, d), lambda i: (i, 0)),
        out_shape=jax.ShapeDtypeStruct((rows, d), F32),
        scratch_shapes=[pltpu.VMEM((nb, HALO + tq, di), F32), pltpu.VMEM((nb * tq, di), F32)],
        compiler_params=_cparams("parallel"),
        name="conv_sample",
    )(glu, state, sz, x, cw, cb.reshape(1, di), lg.reshape(1, di), lb.reshape(1, di), wo_bf, gp.reshape(1, d))


def _head_major(col_pair, head):
    return col_pair if head % 2 == 0 else _swap_halves(col_pair)


def _nsa_proj_body(x_ref, gkv_ref, gq_ref, wkv_ref, wq_ref, wz_ref, wg_ref,
                   cmp_ref, slc_ref, win_ref, qa_ref, sz_ref, gate_ref, *aug_refs, seq_len, tm):
    x = x_ref[...]
    xn = x * lax.rsqrt(jnp.mean(x * x, axis=-1, keepdims=True) + EPS)
    hkv = (xn * gkv_ref[...]).astype(BF16)
    hq = (xn * gq_ref[...]).astype(BF16)
    kv = jnp.dot(hkv, wkv_ref[...], preferred_element_type=F32)
    width = 2 * KV_HEADS * HEAD_DIM
    cmp_ref[...] = kv[:, :width]
    slc_ref[...] = kv[:, width:2 * width]
    win_ref[...] = kv[:, 2 * width:]
    lane = _lane((tm, LANES))
    low = lane < HEAD_DIM
    uq = jnp.dot(hq, wq_ref[...], preferred_element_type=F32)
    scale = HEAD_DIM ** -0.5
    for h in range(N_HEADS):
        qh = _head_major(uq[:, (h // 2) * LANES:(h // 2 + 1) * LANES], h)
        qa_ref[:, h * LANES:(h + 1) * LANES] = jnp.where(low, qh * scale, 0.0).astype(BF16)
    z = jnp.dot(hq, wz_ref[...], preferred_element_type=F32)
    sz_ref[...] = z * _sigmoid(z)
    gate_ref[...] = _sigmoid(jnp.dot(hq, wg_ref[...], preferred_element_type=F32))
    if aug_refs:
        ska_ref, sv_ref, wka_ref, wv_ref = aug_refs
        row = pl.program_id(0) * tm + lax.broadcasted_iota(jnp.int32, (tm, LANES), 0)
        blk = (row % seq_len) // L_SLC
        onehot = (lane - HEAD_DIM == blk).astype(F32)
        kw = KV_HEADS * HEAD_DIM
        for h in range(KV_HEADS):
            ks = _head_major(kv[:, width + (h // 2) * LANES:width + (h // 2 + 1) * LANES], h)
            ska_ref[:, h * LANES:(h + 1) * LANES] = jnp.where(low, ks, onehot).astype(BF16)
            kwn = _head_major(kv[:, 2 * width + (h // 2) * LANES:2 * width + (h // 2 + 1) * LANES], h)
            wka_ref[:, h * LANES:(h + 1) * LANES] = jnp.where(low, kwn, 0.0).astype(BF16)
        sv_ref[...] = kv[:, width + kw:2 * width].T.astype(BF16)
        wv_ref[...] = kv[:, 2 * width + kw:].astype(BF16)


def _nsa_proj(x, g_kv, g_q, wkv_bf, wq_bf, wz_bf, wg_bf, tm, seq_len=None):
    rows, d = x.shape
    width = 2 * KV_HEADS * HEAD_DIM
    row = lambda n: pl.BlockSpec((tm, n), lambda i: (i, 0))
    full = lambda a: pl.BlockSpec(a.shape, lambda i: (0, 0))
    out_specs = [row(width), row(width), row(width), row(N_HEADS * LANES), row(wz_bf.shape[1]), row(LANES)]
    out_shape = [jax.ShapeDtypeStruct((rows, width), F32)] * 3 + [
        jax.ShapeDtypeStruct((rows, N_HEADS * LANES), BF16),
        jax.ShapeDtypeStruct((rows, wz_bf.shape[1]), F32),
        jax.ShapeDtypeStruct((rows, LANES), F32)]
    if seq_len is not None:
        out_specs += [row(KV_HEADS * LANES), pl.BlockSpec((width // 2, tm), lambda i: (0, i)),
                      row(KV_HEADS * LANES), row(width // 2)]
        out_shape += [jax.ShapeDtypeStruct((rows, KV_HEADS * LANES), BF16),
                      jax.ShapeDtypeStruct((width // 2, rows), BF16),
                      jax.ShapeDtypeStruct((rows, KV_HEADS * LANES), BF16),
                      jax.ShapeDtypeStruct((rows, width // 2), BF16)]
    gkv = g_kv.reshape(1, d)
    gq = g_q.reshape(1, d)
    return pl.pallas_call(
        functools.partial(_nsa_proj_body, seq_len=seq_len, tm=tm),
        grid=(rows // tm,),
        in_specs=[row(d), full(gkv), full(gq), full(wkv_bf), full(wq_bf), full(wz_bf), full(wg_bf)],
        out_specs=out_specs,
        out_shape=out_shape,
        compiler_params=_cparams("parallel"),
        name="nsa_proj",
    )(x, gkv, gq, wkv_bf, wq_bf, wz_bf, wg_bf)


def _compress_body(pt_ref, *refs, n_pages, n_col):
    del pt_ref
    page_refs = refs[:n_pages * n_col]
    w1_ref, pe_ref, w2_ref, out_ref, a_ref = refs[n_pages * n_col:]
    cpp = page_refs[0].shape[1] // CMP_STRIDE
    n_chunk = n_pages * cpp
    hid = w2_ref.shape[2]
    low = _lane((cpp, LANES)) < HEAD_DIM
    for p in range(n_pages):
        for c in range(n_col):
            pr = page_refs[p * n_col + c]
            for s in range(0, CMP_STRIDE, 2):
                b0 = pr[0, pl.ds(s, cpp, stride=CMP_STRIDE), :]
                b1 = pr[0, pl.ds(s + 1, cpp, stride=CMP_STRIDE), :]
                dst = (slice(p * cpp, (p + 1) * cpp), slice((s // 2) * LANES, (s // 2 + 1) * LANES))
                a_ref[(2 * c,) + dst] = jnp.where(low, b0, _swap_halves(b1))
                a_ref[(2 * c + 1,) + dst] = jnp.where(low, _swap_halves(b0), b1)
    for kv in range(2):
        w1 = w1_ref[kv]
        pe = jnp.dot(pe_ref[kv].astype(BF16), w1, preferred_element_type=F32)
        pe_term = pe[0:1, :hid] + pe[1:2, hid:]
        for hp in range(KV_HEADS // 2):
            pair = jnp.zeros((n_chunk, LANES), F32)
            for par in range(2):
                part = jnp.dot(a_ref[kv * KV_HEADS + 2 * hp + par].astype(BF16), w1, preferred_element_type=F32)
                pre = part[:, :hid] + pltpu.roll(part[:, hid:], n_chunk - 1, axis=0) + pe_term
                mid = pre * _sigmoid(pre)
                pair = pair + jnp.dot(mid.astype(BF16), w2_ref[kv, par], preferred_element_type=F32)
            col = kv * (KV_HEADS // 2) + hp
            out_ref[0, :, col * LANES:(col + 1) * LANES] = pair


def _compress(pages, page_table, w1cat_bf, pe8, w2_bf, n_pages):
    n_seq = page_table.shape[0] // n_pages
    page_rows, width = pages.shape[1:]
    n_chunk = n_pages * page_rows // CMP_STRIDE
    n_col = width // LANES
    page_spec = lambda j, c: pl.BlockSpec((1, page_rows, LANES), lambda i, pt: (pt[i * n_pages + j], 0, c))
    full = lambda a: pl.BlockSpec(a.shape, lambda i, pt: (0,) * a.ndim)
    grid_spec = pltpu.PrefetchScalarGridSpec(
        num_scalar_prefetch=1,
        grid=(n_seq,),
        in_specs=[page_spec(j, c) for j in range(n_pages) for c in range(n_col)]
        + [full(w1cat_bf), full(pe8), full(w2_bf)],
        out_specs=pl.BlockSpec((1, n_chunk, width), lambda i, pt: (i, 0, 0)),
        scratch_shapes=[pltpu.VMEM((2 * KV_HEADS, n_chunk, CMP_STRIDE * HEAD_DIM), F32)],
    )
    return pl.pallas_call(
        functools.partial(_compress_body, n_pages=n_pages, n_col=n_col),
        grid_spec=grid_spec,
        out_shape=jax.ShapeDtypeStruct((n_seq, n_chunk, width), F32),
        compiler_params=_cparams("parallel"),
        name="compress",
    )(page_table, *([pages] * (n_pages * n_col)), w1cat_bf, pe8, w2_bf)


def _bias_tile_body(rb_ref, out_ref, *, row_step, lane_step, offset, hi):
    h = pl.program_id(0)
    shape = out_ref.shape[1:]
    d = row_step * lax.broadcasted_iota(jnp.int32, shape, 0) + lane_step * _lane(shape) + offset
    n = jnp.maximum(d, 0)
    nf = jnp.maximum(n, 1).astype(F32)
    large = MAX_EXACT + (jnp.log(nf / MAX_EXACT) / math.log(MAX_DISTANCE / MAX_EXACT)
                         * (NUM_BUCKETS - MAX_EXACT)).astype(jnp.int32)
    large = jnp.minimum(large, NUM_BUCKETS - 1)
    bucket = jnp.where(n < MAX_EXACT, n, large)
    far = rb_ref[NUM_BUCKETS - 1, h]
    val = jnp.zeros(shape, F32)
    for k in range(NUM_BUCKETS - 1):
        val = jnp.where(bucket == k, rb_ref[k, h] - far, val)
    out_ref[0] = jnp.where((d >= 0) & (d <= hi), val, NEG)


def _bias_tile(rel_bias, rows, width, lane_step, offset, hi=1 << 30, row_step=1):
    return pl.pallas_call(
        functools.partial(_bias_tile_body, row_step=row_step, lane_step=lane_step, offset=offset, hi=hi),
        grid=(N_HEADS,),
        in_specs=[pl.BlockSpec(memory_space=pltpu.SMEM)],
        out_specs=pl.BlockSpec((1, rows, width), lambda h: (h, 0, 0)),
        out_shape=jax.ShapeDtypeStruct((N_HEADS, rows, width), F32),
        compiler_params=_cparams("parallel"),
        name="bias_tile",
    )(rel_bias)


_NT = (((1,), (1,)), ((), ()))


def _softmax_rows(s):
    m = jnp.maximum(jnp.max(s, axis=-1, keepdims=True), M_FLOOR)
    p = jnp.exp(s - m)
    return p, jnp.sum(p, axis=-1, keepdims=True)


def _gate_col(gate, col):
    return jnp.sum(jnp.where(_lane(gate.shape) == col, gate, 0.0), axis=-1, keepdims=True)


def _pair_columns(o_even, o_odd, valid_half):
    lo = o_even if valid_half == 0 else _swap_halves(o_even)
    hi = o_odd if valid_half == 1 else _swap_halves(o_odd)
    return jnp.where(_lane(lo.shape) < HEAD_DIM, lo, hi)


def _select_blocks(imp, tb, n_sel):
    lane = _lane(imp.shape)
    lane_f = lane.astype(F32)
    j = lane - HEAD_DIM
    valid = (j >= 0) & (j <= tb)
    forced = (j == 0) | (j == tb) | (j == tb - 1)
    score = jnp.where(valid, jnp.where(forced, BIG, imp), -BIG)
    score = jnp.where(j >= 0, score, -jnp.inf)
    sel = jnp.zeros(imp.shape, jnp.bool_)
    for _ in range(n_sel):
        best = jnp.max(score, axis=-1, keepdims=True)
        first = jnp.min(jnp.where(score == best, lane_f, 4.0 * LANES), axis=-1, keepdims=True)
        pick = lane_f == first
        sel = sel | pick
        score = jnp.where(pick, -jnp.inf, score)
    return jnp.where(sel & valid, 0.0, jnp.where(j >= 0, NEG, 0.0))


def _cmp_prompt_body(qa_ref, kc_ref, vc_ref, m_ref, cb_ref, gate_ref, oc_ref, qsel_ref, *, n_key):
    qt = pl.program_id(1)
    start = pl.multiple_of(8 * qt + 8, 8)
    kwin = kc_ref[0, pl.ds(start, n_key), :]
    vwin = vc_ref[0, pl.ds(start, n_key), :]
    mwin = m_ref[pl.ds(start, n_key), :]
    lane = _lane((Q_BLK, LANES))
    low = lane < HEAD_DIM
    pos = qt * Q_BLK + lax.broadcasted_iota(jnp.int32, (Q_BLK, 1), 0)
    tb = pos // L_SLC
    gate = gate_ref[...]
    for kvh in range(KV_HEADS):
        ka = kwin[:, kvh * LANES:(kvh + 1) * LANES].astype(BF16)
        vp = vwin[:, (kvh // 2) * LANES:(kvh // 2 + 1) * LANES].astype(BF16)
        pc_sum = jnp.zeros((Q_BLK, n_key), F32)
        outs = []
        qs = []
        for g in range(GROUP):
            h = kvh * GROUP + g
            q = qa_ref[0, :, h * LANES:(h + 1) * LANES].astype(F32)
            qs.append(q)
            qa = jnp.where(lane == HEAD_DIM, NEG, q).astype(BF16)
            s = lax.dot_general(qa, ka, _NT, preferred_element_type=F32) + cb_ref[h]
            p, l = _softmax_rows(s)
            inv = 1.0 / jnp.maximum(l, 1e-30)
            pc_sum = pc_sum + p * inv
            o = jnp.dot(p.astype(BF16), vp, preferred_element_type=F32)
            outs.append(o * (inv * _gate_col(gate, h)))
        for gp in range(GROUP // 2):
            col = kvh * (GROUP // 2) + gp
            oc_ref[0, :, col * LANES:(col + 1) * LANES] = _pair_columns(outs[2 * gp], outs[2 * gp + 1], kvh % 2)
        imp = jnp.dot(pc_sum, mwin, preferred_element_type=F32, precision=lax.Precision.HIGHEST)
        selneg = _select_blocks(imp, tb, N_SEL)
        for g in range(GROUP):
            qsel_ref[0, kvh, :, g * Q_BLK:(g + 1) * Q_BLK] = jnp.where(low, qs[g], selneg).T.astype(BF16)


def _cmp_prompt(qa, kc_pad, vc_pad, m_pad, cb, gate, batch, seq):
    nq = seq // Q_BLK
    n_key = kc_pad.shape[1] // 2
    d_out = N_HEADS * HEAD_DIM
    return pl.pallas_call(
        functools.partial(_cmp_prompt_body, n_key=n_key),
        grid=(batch, nq),
        in_specs=[pl.BlockSpec((1, Q_BLK, N_HEADS * LANES), lambda b, t: (b, t, 0)),
                  pl.BlockSpec((1,) + kc_pad.shape[1:], lambda b, t: (b, 0, 0)),
                  pl.BlockSpec((1,) + vc_pad.shape[1:], lambda b, t: (b, 0, 0)),
                  pl.BlockSpec(m_pad.shape, lambda b, t: (0, 0)),
                  pl.BlockSpec(cb.shape, lambda b, t: (0, 0, 0)),
                  pl.BlockSpec((Q_BLK, LANES), lambda b, t: (b * nq + t, 0))],
        out_specs=[pl.BlockSpec((1, Q_BLK, d_out), lambda b, t: (b, t, 0)),
                   pl.BlockSpec((1, KV_HEADS, LANES, GROUP * Q_BLK), lambda b, t: (b * nq + t, 0, 0, 0))],
        out_shape=[jax.ShapeDtypeStruct((batch, seq, d_out), F32),
                   jax.ShapeDtypeStruct((batch * nq, KV_HEADS, LANES, GROUP * Q_BLK), BF16)],
        compiler_params=_cparams("parallel", "parallel"),
        name="cmp_prompt",
    )(qa.reshape(batch, seq, -1), kc_pad, vc_pad, m_pad, cb, gate)


KEY_TILE = 256
COL_CHAIN = 256


def _slc_prompt_body(q_ref, k_ref, v_ref, sb_ref, gate_ref, out_ref, m_ref, l_ref, acc_ref):
    kvh = pl.program_id(1)
    qt = pl.program_id(2)
    cols = GROUP * Q_BLK
    n_chain = cols // COL_CHAIN
    m_ref[...] = jnp.full(m_ref.shape, M_FLOOR, F32)
    l_ref[...] = jnp.zeros(l_ref.shape, F32)
    acc_ref[...] = jnp.zeros(acc_ref.shape, F32)
    last = (qt * Q_BLK) // KEY_TILE

    def make_step(with_bias):
        def step(kt, carry):
            base = pl.multiple_of(kt * KEY_TILE, KEY_TILE)
            ks = k_ref[0, pl.ds(base, KEY_TILE), :]
            vt = v_ref[:, pl.ds(base, KEY_TILE)]
            case = qt % 2 + 2 * (last - kt)
            chains = [slice(c * COL_CHAIN, (c + 1) * COL_CHAIN) for c in range(n_chain)]
            ss = [jnp.dot(ks, q_ref[0, 0, :, cs], preferred_element_type=F32) for cs in chains]
            if with_bias:
                ss = [s + sb_ref[case, 0, :, cs] for s, cs in zip(ss, chains)]
            ps, alphas = [], []
            for c, s in enumerate(ss):
                m_old = m_ref[c]
                m_new = jnp.maximum(m_old, jnp.max(s, axis=0, keepdims=True))
                alpha = jnp.exp(m_old - m_new)
                p = jnp.exp(s - m_new)
                l_ref[c] = alpha * l_ref[c] + jnp.sum(p, axis=0, keepdims=True)
                m_ref[c] = m_new
                ps.append(p.astype(BF16))
                alphas.append(alpha)
            for c in range(n_chain):
                acc_ref[c] = alphas[c] * acc_ref[c] + jnp.dot(vt, ps[c], preferred_element_type=F32)
            return carry
        return step

    n_far = jnp.maximum(last - 1, 0)
    lax.fori_loop(0, n_far, make_step(False), 0)
    lax.fori_loop(n_far, last + 1, make_step(True), 0)
    gate = gate_ref[...]
    outs = []
    for g in range(GROUP):
        c, gl = divmod(g * Q_BLK, COL_CHAIN)
        inv = 1.0 / jnp.maximum(l_ref[c][:, gl:gl + Q_BLK], 1e-30)
        o = (acc_ref[c][:, gl:gl + Q_BLK] * inv).T
        outs.append(o * _gate_col(gate, N_HEADS + kvh * GROUP + g))
    for parity in range(2):
        @pl.when(kvh % 2 == parity)
        def _():
            for gp in range(GROUP // 2):
                out_ref[0, :, gp * LANES:(gp + 1) * LANES] = _pair_columns(outs[2 * gp], outs[2 * gp + 1], parity)


def _slc_prompt(qsel_t, ska, sv_t, sb_t, gate, batch, seq):
    nq = seq // Q_BLK
    cols = GROUP * Q_BLK
    d_out = N_HEADS * HEAD_DIM
    n_chain = cols // COL_CHAIN
    return pl.pallas_call(
        _slc_prompt_body,
        grid=(batch, KV_HEADS, nq),
        in_specs=[pl.BlockSpec((1, 1, LANES, cols), lambda b, h, t: (b * nq + t, h, 0, 0)),
                  pl.BlockSpec((1, seq, LANES), lambda b, h, t: (b, 0, h)),
                  pl.BlockSpec((LANES, seq), lambda b, h, t: (h // 2, b)),
                  pl.BlockSpec((4, 1, KEY_TILE, cols), lambda b, h, t: (0, h, 0, 0)),
                  pl.BlockSpec((Q_BLK, LANES), lambda b, h, t: (b * nq + t, 0))],
        out_specs=pl.BlockSpec((1, Q_BLK, GROUP * HEAD_DIM), lambda b, h, t: (b, t, h)),
        out_shape=jax.ShapeDtypeStruct((batch, seq, d_out), F32),
        scratch_shapes=[pltpu.VMEM((n_chain, 1, COL_CHAIN), F32), pltpu.VMEM((n_chain, 1, COL_CHAIN), F32),
                        pltpu.VMEM((n_chain, LANES, COL_CHAIN), F32)],
        compiler_params=_cparams("parallel", "parallel", "arbitrary"),
        name="slc_prompt",
    )(qsel_t, ska.reshape(batch, seq, -1), sv_t, sb_t, gate)


def _win_prompt_body(qa_ref, k_ref, v_ref, wb_ref, gate_ref, out_ref, *, n_key):
    kvh = pl.program_id(1)
    qt = pl.program_id(2)
    base = pl.multiple_of(qt * Q_BLK, Q_BLK)
    ks = k_ref[0, pl.ds(base, n_key), :]
    vs = v_ref[0, pl.ds(base, n_key), :]
    lane = _lane((Q_BLK, LANES))
    gate = gate_ref[...]
    outs = []
    for g in range(GROUP):
        q = qa_ref[0, :, g * LANES:(g + 1) * LANES]
        qa = jnp.where(lane == HEAD_DIM, jnp.asarray(NEG, BF16), q)
        s = lax.dot_general(qa, ks, _NT, preferred_element_type=F32) + wb_ref[g]
        p, l = _softmax_rows(s)
        o = jnp.dot(p.astype(BF16), vs, preferred_element_type=F32)
        gcol = _gate_col(gate, 2 * N_HEADS + kvh * GROUP + g)
        outs.append(o * (gcol / jnp.maximum(l, 1e-30)))
    for parity in range(2):
        @pl.when(kvh % 2 == parity)
        def _():
            for gp in range(GROUP // 2):
                out_ref[0, :, gp * LANES:(gp + 1) * LANES] = _pair_columns(outs[2 * gp], outs[2 * gp + 1], parity)


def _win_prompt(qa, wka_pad, wv_pad, wb, gate, batch, seq):
    nq = seq // Q_BLK
    n_key = WINDOW + Q_BLK
    d_out = N_HEADS * HEAD_DIM
    padded = wka_pad.shape[1]
    return pl.pallas_call(
        functools.partial(_win_prompt_body, n_key=n_key),
        grid=(batch, KV_HEADS, nq),
        in_specs=[pl.BlockSpec((1, Q_BLK, GROUP * LANES), lambda b, h, t: (b, t, h)),
                  pl.BlockSpec((1, padded, LANES), lambda b, h, t: (b, 0, h)),
                  pl.BlockSpec((1, padded, LANES), lambda b, h, t: (b, 0, h // 2)),
                  pl.BlockSpec((GROUP, Q_BLK, n_key), lambda b, h, t: (h, 0, 0)),
                  pl.BlockSpec((Q_BLK, LANES), lambda b, h, t: (b * nq + t, 0))],
        out_specs=pl.BlockSpec((1, Q_BLK, GROUP * HEAD_DIM), lambda b, h, t: (b, t, h)),
        out_shape=jax.ShapeDtypeStruct((batch, seq, d_out), F32),
        compiler_params=_cparams("parallel", "parallel", "parallel"),
        name="win_prompt",
    )(qa.reshape(batch, seq, -1), wka_pad, wv_pad, wb, gate)


def _nsa_out_body(oc_ref, os_ref, ow_ref, sz_ref, x_ref, wo_ref, gp_ref, out_ref):
    d = oc_ref.shape[-1]
    y = oc_ref[...] * sz_ref[:, :d] + os_ref[...] * sz_ref[:, d:2 * d] + ow_ref[...] * sz_ref[:, 2 * d:]
    o = jnp.dot(y.astype(BF16), wo_ref[...], preferred_element_type=F32)
    out_ref[...] = x_ref[...] + _rms(o, gp_ref[...])


def _nsa_out(oc, os_, ow, sz, x, wo_bf, gp, tm):
    rows, d = x.shape
    dq = oc.shape[-1]
    row = lambda n: pl.BlockSpec((tm, n), lambda i: (i, 0))
    return pl.pallas_call(
        _nsa_out_body,
        grid=(rows // tm,),
        in_specs=[row(dq), row(dq), row(dq), row(N_BRANCH * dq), row(d),
                  pl.BlockSpec((dq, d), lambda i: (0, 0)), pl.BlockSpec((1, d), lambda i: (0, 0))],
        out_specs=row(d),
        out_shape=jax.ShapeDtypeStruct((rows, d), F32),
        compiler_params=_cparams("parallel"),
        name="nsa_out",
    )(oc, os_, ow, sz, x, wo_bf, gp.reshape(1, d))


def _sample_attn_body(pt_ref, *refs, n_pages, tq, w_keep):
    del pt_ref
    page_refs = refs[:n_pages]
    (qa_ref, kvc_ref, snew_ref, wst_ref, wnew_ref, m_ref, e_ref, cbs_ref, sbs_ref, wbs_ref, gate_ref,
     oc_ref, os_ref, ow_ref, ksel_ref, kwin_ref) = refs[n_pages:]
    n_past = n_pages * PAGE_SIZE
    n_sk = ksel_ref.shape[0]
    n_wk = kwin_ref.shape[0]
    width = ksel_ref.shape[1]
    kw = width // 2
    for p in range(n_pages):
        ksel_ref[p * PAGE_SIZE:(p + 1) * PAGE_SIZE, :] = page_refs[p][0]
    ksel_ref[n_past:n_past + tq, :] = snew_ref[...]
    ksel_ref[n_past + tq:, :] = jnp.zeros((n_sk - n_past - tq, width), F32)
    kwin_ref[0:w_keep, :] = wst_ref[0]
    kwin_ref[w_keep:w_keep + tq, :] = wnew_ref[...]
    kwin_ref[w_keep + tq:, :] = jnp.zeros((n_wk - w_keep - tq, width), F32)
    pos = n_past + lax.broadcasted_iota(jnp.int32, (tq, 1), 0)
    tb = pos // L_SLC
    gate = gate_ref[...]
    rows = GROUP * tq

    def attend(q, k, v, bias, extra=None):
        s = lax.dot_general(q, k, _NT, preferred_element_type=F32)
        if extra is not None:
            s = s + extra
        s = (s.reshape(GROUP, tq, s.shape[-1]) + bias).reshape(rows, s.shape[-1])
        p, l = _softmax_rows(s)
        inv = 1.0 / jnp.maximum(l, 1e-30)
        return p, inv, jnp.dot(p.astype(BF16), v, preferred_element_type=F32)

    def store(out_ref, o, kvh, branch):
        outs = [o[g * tq:(g + 1) * tq] * _gate_col(gate, branch * N_HEADS + kvh * GROUP + g) for g in range(GROUP)]
        for gp in range(GROUP // 2):
            col = kvh * (GROUP // 2) + gp
            out_ref[:, col * LANES:(col + 1) * LANES] = _pair_columns(outs[2 * gp], outs[2 * gp + 1], kvh % 2)

    for pr in range(KV_HEADS // 2):
        kcol = slice(pr * LANES, (pr + 1) * LANES)
        vcol = slice(kw + pr * LANES, kw + (pr + 1) * LANES)
        kc = kvc_ref[0, :, kcol].astype(BF16)
        vc = kvc_ref[0, :, vcol].astype(BF16)
        ksl = ksel_ref[:, kcol].astype(BF16)
        vsl = ksel_ref[:, vcol].astype(BF16)
        kwn = kwin_ref[:, kcol].astype(BF16)
        vwn = kwin_ref[:, vcol].astype(BF16)
        for half in range(2):
            kvh = 2 * pr + half
            heads = slice(kvh * GROUP, (kvh + 1) * GROUP)
            q = jnp.concatenate([qa_ref[:, h * LANES:(h + 1) * LANES]
                                 for h in range(kvh * GROUP, (kvh + 1) * GROUP)], axis=0)
            q = (q if half == 0 else _swap_halves(q)).astype(BF16)
            p, inv, o = attend(q, kc, vc, cbs_ref[heads])
            store(oc_ref, o * inv, kvh, 0)
            pc_sum = jnp.sum((p * inv).reshape(GROUP, tq, p.shape[-1]), axis=0)
            imp = jnp.dot(pc_sum, m_ref[...], preferred_element_type=F32, precision=lax.Precision.HIGHEST)
            selneg = _select_blocks(imp, tb, N_SEL)
            qm = jnp.concatenate([selneg] * GROUP, axis=0).astype(BF16)
            block_mask = jnp.dot(qm, e_ref[...], preferred_element_type=F32)
            _, inv, o = attend(q, ksl, vsl, sbs_ref[heads], block_mask)
            store(os_ref, o * inv, kvh, 1)
            _, inv, o = attend(q, kwn, vwn, wbs_ref[heads])
            store(ow_ref, o * inv, kvh, 2)


def _sample_attn(qa, kvc, pages, page_table, slc_new, win_state, win_new, m_s, e_s, cbs, sbs, wbs, gate,
                 n_seq, tq, n_pages):
    width = pages.shape[-1]
    n_sk = sbs.shape[-1]
    n_wk = wbs.shape[-1]
    w_keep = win_state.shape[1]
    d_out = N_HEADS * HEAD_DIM
    page_spec = lambda j: pl.BlockSpec((1, PAGE_SIZE, width), lambda i, pt: (pt[i * n_pages + j], 0, 0))
    full = lambda a: pl.BlockSpec(a.shape, lambda i, pt: (0,) * a.ndim)
    row = lambda n: pl.BlockSpec((tq, n), lambda i, pt: (i, 0))
    grid_spec = pltpu.PrefetchScalarGridSpec(
        num_scalar_prefetch=1,
        grid=(n_seq,),
        in_specs=[page_spec(j) for j in range(n_pages)] + [
            row(N_HEADS * LANES),
            pl.BlockSpec((1,) + kvc.shape[1:], lambda i, pt: (i, 0, 0)),
            row(width),
            pl.BlockSpec((1, w_keep, width), lambda i, pt: (i, 0, 0)),
            row(width),
            full(m_s), full(e_s), full(cbs), full(sbs), full(wbs),
            row(LANES)],
        out_specs=[row(d_out)] * 3,
        scratch_shapes=[pltpu.VMEM((n_sk, width), F32), pltpu.VMEM((n_wk, width), F32)],
    )
    return pl.pallas_call(
        functools.partial(_sample_attn_body, n_pages=n_pages, tq=tq, w_keep=w_keep),
        grid_spec=grid_spec,
        out_shape=[jax.ShapeDtypeStruct((n_seq * tq, d_out), F32)] * 3,
        compiler_params=_cparams("parallel"),
        name="sample_attn",
    )(page_table, *([pages] * n_pages), qa, kvc, slc_new, win_state, win_new, m_s, e_s, cbs, sbs, wbs, gate)


def _overlap_matrix(n_rows, row0, n_cmp, n_blk):
    import numpy as np
    m = np.zeros((n_rows, LANES), np.float32)
    cs = np.arange(n_cmp)[:, None] * CMP_STRIDE
    js = np.arange(n_blk)[None, :] * L_SLC
    m[row0:row0 + n_cmp, HEAD_DIM:HEAD_DIM + n_blk] = (cs <= js + L_SLC - 1) & (cs + L_CMP - 1 >= js)
    return jnp.asarray(m)


def kernel(x_prompt, x_sample, state_conv, cache_cmp, cache_slc, state_win, page_table, rel_bias, a_norm_pre, a_w_in, a_conv_w, a_conv_b, a_ln_g, a_ln_b, a_w_out, a_norm_post, kv_norm, w_kv, cmp_pe, cmp_w1, cmp_w2, b_norm_pre, b_w_in, b_w_out, b_norm_post):
    import numpy as np
    bp, tp, d = x_prompt.shape
    bd, tq, _ = x_sample.shape
    n_pages = page_table.shape[1]
    past = n_pages * PAGE_SIZE
    w_keep = state_win.shape[1]
    width = 2 * KV_HEADS * HEAD_DIM
    kw = KV_HEADS * HEAD_DIM
    dq = N_HEADS * HEAD_DIM
    assert b_w_in.shape[0] == 1 and tp % (2 * KEY_TILE) == 0 and tp // L_SLC <= HEAD_DIM
    tm = 256

    xp = x_prompt
    xs = x_sample.reshape(bd * tq, d)
    conv_p, conv_s = [], []
    for l in range(a_w_in.shape[0]):
        w_in = a_w_in[l].astype(BF16)
        w_out = a_w_out[l].astype(BF16)
        di = a_w_out.shape[1]
        tail = (a_conv_w[l], a_conv_b[l], a_ln_g[l], a_ln_b[l], w_out, a_norm_post[l])
        glu, sz = _glu_proj(xp.reshape(bp * tp, d), a_norm_pre[l], w_in, tm)
        glu = glu.reshape(bp, tp, di)
        xp = _conv_prompt(glu, sz.reshape(bp, tp, di), xp, *tail, tm)
        conv_p.append(glu[:, -(CONV_W - 1):])
        glu, sz = _glu_proj(xs, a_norm_pre[l], w_in, tm)
        xs = _conv_sample(glu, state_conv[l], sz, xs, *tail, 16)
        conv_s.append(jnp.concatenate([state_conv[l], glu.reshape(bd, tq, di)], axis=1)[:, -(CONV_W - 1):])

    bw = b_w_in[0]
    n_gate = N_BRANCH * N_HEADS
    wq = bw[:, :dq].astype(BF16)
    wz = bw[:, dq:dq * (1 + N_BRANCH)].astype(BF16)
    wg = jnp.pad(bw[:, dq * (1 + N_BRANCH):], ((0, 0), (0, LANES - n_gate))).astype(BF16)
    wkv = w_kv.astype(BF16)
    xp2 = xp.reshape(bp * tp, d)
    (cmp_p, slc_p, win_p, qa_p, sz_p, gate_p, ska, sv, wka, wv) = _nsa_proj(
        xp2, kv_norm, b_norm_pre[0], wkv, wq, wz, wg, tm, seq_len=tp)
    cmp_s, slc_s, win_s, qa_s, sz_s, gate_s = _nsa_proj(xs, kv_norm, b_norm_pre[0], wkv, wq, wz, wg, tm)
    qa_s = qa_s.astype(F32)

    w1cat = jnp.concatenate([cmp_w1[:, :CMP_STRIDE * HEAD_DIM], cmp_w1[:, CMP_STRIDE * HEAD_DIM:]], axis=2).astype(BF16)
    pe8 = jnp.pad(cmp_pe.reshape(2, L_CMP // CMP_STRIDE, CMP_STRIDE * HEAD_DIM), ((0, 0), (0, 6), (0, 0)))
    zero = jnp.zeros_like(cmp_w2)
    w2h = jnp.stack([jnp.concatenate([cmp_w2, zero], axis=2), jnp.concatenate([zero, cmp_w2], axis=2)], axis=1).astype(BF16)
    prompt_page = 1024
    pp = tp // prompt_page
    kvc_p = _compress(cmp_p.reshape(bp * pp, prompt_page, width), jnp.arange(bp * pp, dtype=jnp.int32), w1cat, pe8, w2h, pp)
    pt_flat = page_table.reshape(-1).astype(jnp.int32)
    kvc_s = _compress(cache_cmp.reshape(-1, PAGE_SIZE, width), pt_flat, w1cat, pe8, w2h, n_pages)

    n_chunk_p = kvc_p.shape[1]
    nc_p = n_chunk_p - L_CMP // CMP_STRIDE + 1
    real = (jnp.arange(n_chunk_p) < nc_p)[None, :, None]
    flag = jnp.zeros((LANES - HEAD_DIM,), F32).at[0].set(1.0)
    kc4 = jnp.where(real, kvc_p[:, :, :kw], 0.0).reshape(bp, n_chunk_p, KV_HEADS, HEAD_DIM)
    aug = jnp.where(real[..., None], 0.0, flag) * jnp.ones((bp, n_chunk_p, KV_HEADS, 1), F32)
    kc_real = jnp.concatenate([kc4, aug], axis=-1).reshape(bp, n_chunk_p, KV_HEADS * LANES)
    pad_row = jnp.concatenate([jnp.zeros((HEAD_DIM,), F32), flag])
    kc_front = jnp.broadcast_to(jnp.tile(pad_row, KV_HEADS), (bp, n_chunk_p, KV_HEADS * LANES))
    kc_pad = jnp.concatenate([kc_front, kc_real], axis=1)
    vc_pad = jnp.concatenate([jnp.zeros((bp, n_chunk_p, kw), F32), jnp.where(real, kvc_p[:, :, kw:], 0.0)], axis=1)
    m_pad = _overlap_matrix(2 * n_chunk_p, n_chunk_p, nc_p, tp // L_SLC)
    cb = _bias_tile(rel_bias, Q_BLK, n_chunk_p, -CMP_STRIDE, CMP_STRIDE * (n_chunk_p - Q_BLK // CMP_STRIDE) - (L_CMP - 1))
    oc_p, qsel = _cmp_prompt(qa_p, kc_pad, vc_pad, m_pad, cb, gate_p, bp, tp)
    sb = jnp.stack([_bias_tile(rel_bias, KEY_TILE, Q_BLK, 1, off, row_step=-1)
                    for off in (0, Q_BLK, 2 * Q_BLK, 3 * Q_BLK)])
    sb = sb.reshape(4, KV_HEADS, GROUP, KEY_TILE, Q_BLK).transpose(0, 1, 3, 2, 4).reshape(4, KV_HEADS, KEY_TILE, GROUP * Q_BLK)
    os_p = _slc_prompt(qsel, ska, sv, sb, gate_p, bp, tp)
    wb = _bias_tile(rel_bias, Q_BLK, WINDOW + Q_BLK, -1, WINDOW, WINDOW)
    wka_front = jnp.broadcast_to(jnp.tile(pad_row, KV_HEADS).astype(BF16), (bp, WINDOW, KV_HEADS * LANES))
    wka_pad = jnp.concatenate([wka_front, wka.reshape(bp, tp, -1)], axis=1)
    wv_pad = jnp.concatenate([jnp.zeros((bp, WINDOW, kw), BF16), wv.reshape(bp, tp, -1)], axis=1)
    ow_p = _win_prompt(qa_p, wka_pad, wv_pad, wb, gate_p, bp, tp)
    wo = b_w_out[0].astype(BF16)
    y_p = _nsa_out(oc_p.reshape(bp * tp, dq), os_p.reshape(bp * tp, dq), ow_p.reshape(bp * tp, dq), sz_p, xp2,
                   wo, b_norm_post[0], tm)

    n_chunk_s = kvc_s.shape[1]
    nc_s = n_chunk_s - L_CMP // CMP_STRIDE + 1
    nb_s = -(-(past + tq) // L_SLC)
    n_sk = -(-(nb_s * L_SLC) // LANES) * LANES
    n_wk = -(-(w_keep + tq) // LANES) * LANES
    m_s = _overlap_matrix(n_chunk_s, 0, nc_s, nb_s)
    e_np = np.zeros((LANES, n_sk), np.float32)
    e_np[HEAD_DIM + np.arange(n_sk) // L_SLC, np.arange(n_sk)] = 1.0
    e_s = jnp.asarray(e_np, BF16)
    cbs = _bias_tile(rel_bias, tq, n_chunk_s, -CMP_STRIDE, past - (L_CMP - 1))
    sbs = _bias_tile(rel_bias, tq, n_sk, -1, past)
    wbs = _bias_tile(rel_bias, tq, n_wk, -1, w_keep, WINDOW)
    oc_s, os_s, ow_s = _sample_attn(qa_s, kvc_s, cache_slc.reshape(-1, PAGE_SIZE, width), pt_flat, slc_s,
                                    state_win.reshape(bd, w_keep, width), win_s, m_s, e_s, cbs, sbs, wbs, gate_s,
                                    bd, tq, n_pages)
    y_s = _nsa_out(oc_s, os_s, ow_s, sz_s, xs, wo, b_norm_post[0], tm)

    kv5 = lambda a, b, t: a.reshape(b, t, 2, KV_HEADS, HEAD_DIM)
    win_p5 = kv5(win_p, bp, tp)
    win_s5 = kv5(win_s, bd, tq)
    win_all = jnp.concatenate([state_win, win_s5], axis=1)
    return (y_p.reshape(bp, tp, d), y_s.reshape(bd, tq, d), jnp.stack(conv_p), jnp.stack(conv_s),
            kv5(cmp_p, bp, tp), kv5(cmp_s, bd, tq), kv5(slc_p, bp, tp), kv5(slc_s, bd, tq),
            win_p5[:, -min(WINDOW, tp):], win_all[:, -min(WINDOW, win_all.shape[1]):])
```

```python
import functools
import math

import jax
import jax.numpy as jnp
from jax import lax
from jax.experimental import pallas as pl
from jax.experimental.pallas import tpu as pltpu

F32 = jnp.float32
BF16 = jnp.bfloat16

EPS = 1e-6
NEG = -1e30
BIG = 1e9
M_FLOOR = -1e20

HEAD_DIM = 64
KV_HEADS = 4
N_HEADS = 16
GROUP = N_HEADS // KV_HEADS
N_BRANCH = 3
CONV_W = 31
L_CMP = 32
CMP_STRIDE = 16
L_SLC = 64
N_SEL = 16
WINDOW = 512
Q_BLK = 128
NUM_BUCKETS = 32
MAX_DISTANCE = 128
MAX_EXACT = NUM_BUCKETS // 2
PAGE_SIZE = 128
LANES = 128
HALO = 32

VMEM_LIMIT = 56 * 1024 * 1024


def _cparams(*sem):
    return pltpu.CompilerParams(dimension_semantics=sem, vmem_limit_bytes=VMEM_LIMIT)


def _sigmoid(x):
    return 1.0 / (1.0 + jnp.exp(-x))


def _rms(x, g):
    return x * lax.rsqrt(jnp.mean(x * x, axis=-1, keepdims=True) + EPS) * g


def _lane(shape):
    return lax.broadcasted_iota(jnp.int32, shape, len(shape) - 1)


def _swap_halves(x):
    return pltpu.roll(x, HEAD_DIM, axis=x.ndim - 1)


def _glu_proj_body(x_ref, g_ref, w_ref, glu_ref, sz_ref):
    di = glu_ref.shape[-1]
    h = _rms(x_ref[...], g_ref[...])
    u = jnp.dot(h.astype(BF16), w_ref[...], preferred_element_type=F32)
    z = u[:, 2 * di:]
    glu_ref[...] = u[:, :di] * _sigmoid(u[:, di:2 * di])
    sz_ref[...] = z * _sigmoid(z)


def _glu_proj(x, g, w_bf, tm):
    rows, d = x.shape
    di = w_bf.shape[1] // 3
    return pl.pallas_call(
        _glu_proj_body,
        grid=(rows // tm,),
        in_specs=[pl.BlockSpec((tm, d), lambda i: (i, 0)),
                  pl.BlockSpec((1, d), lambda i: (0, 0)),
                  pl.BlockSpec((d, 3 * di), lambda i: (0, 0))],
        out_specs=[pl.BlockSpec((tm, di), lambda i: (i, 0)),
                   pl.BlockSpec((tm, di), lambda i: (i, 0))],
        out_shape=[jax.ShapeDtypeStruct((rows, di), F32)] * 2,
        compiler_params=_cparams("parallel"),
        name="glu_proj",
    )(x, g.reshape(1, d), w_bf)


def _conv_tail(c, sz, x, lg_ref, lb_ref, wo_ref, gp_ref):
    mu = jnp.mean(c, axis=-1, keepdims=True)
    cc = c - mu
    var = jnp.mean(cc * cc, axis=-1, keepdims=True)
    y = cc * lax.rsqrt(var + EPS) * lg_ref[...] + lb_ref[...]
    y = y * _sigmoid(y) * sz
    o = jnp.dot(y.astype(BF16), wo_ref[...], preferred_element_type=F32)
    return x + _rms(o, gp_ref[...])


def _conv_prompt_body(glu_ref, prev_ref, sz_ref, x_ref, cw_ref, cb_ref, lg_ref, lb_ref, wo_ref, gp_ref,
                      out_ref, full_ref, c_ref, *, tm):
    t = pl.program_id(1)
    full_ref[0:HALO, :] = jnp.where(t > 0, prev_ref[0], 0.0)
    full_ref[HALO:HALO + tm, :] = glu_ref[0]
    d = c_ref.shape[-1]
    first = HALO - (CONV_W - 1)
    n_phase = 8
    n_grp = -(-CONV_W // n_phase)
    for lc in range(d // LANES):
        ln = slice(lc * LANES, (lc + 1) * LANES)
        acc = jnp.zeros((tm, LANES), F32)
        for b in range(n_phase):
            sb = full_ref[first + b:first + b + tm + 8 * (n_grp - 1), ln]
            for a in range(n_grp):
                j = 8 * a + b
                if j < CONV_W:
                    acc = acc + cw_ref[j:j + 1, ln] * sb[8 * a:8 * a + tm]
        c_ref[:, ln] = acc + cb_ref[:, ln]
    out_ref[0] = _conv_tail(c_ref[...], sz_ref[0], x_ref[0], lg_ref, lb_ref, wo_ref, gp_ref)


def _conv_prompt(glu, sz, x, cw, cb, lg, lb, wo_bf, gp, tm):
    b, t, d = x.shape
    di = glu.shape[-1]
    per = tm // HALO
    vec = lambda n: pl.BlockSpec((1, n), lambda i, j: (0, 0))
    return pl.pallas_call(
        functools.partial(_conv_prompt_body, tm=tm),
        grid=(b, t // tm),
        in_specs=[pl.BlockSpec((1, tm, di), lambda i, j: (i, j, 0)),
                  pl.BlockSpec((1, HALO, di), lambda i, j: (i, jnp.maximum(j * per - 1, 0), 0)),
                  pl.BlockSpec((1, tm, di), lambda i, j: (i, j, 0)),
                  pl.BlockSpec((1, tm, d), lambda i, j: (i, j, 0)),
                  pl.BlockSpec((CONV_W, di), lambda i, j: (0, 0)),
                  vec(di), vec(di), vec(di),
                  pl.BlockSpec((di, d), lambda i, j: (0, 0)),
                  vec(d)],
        out_specs=pl.BlockSpec((1, tm, d), lambda i, j: (i, j, 0)),
        out_shape=jax.ShapeDtypeStruct((b, t, d), F32),
        scratch_shapes=[pltpu.VMEM((HALO + tm, di), F32), pltpu.VMEM((tm, di), F32)],
        compiler_params=_cparams("parallel", "arbitrary"),
        name="conv_prompt",
    )(glu, glu, sz, x, cw, cb.reshape(1, di), lg.reshape(1, di), lb.reshape(1, di), wo_bf, gp.reshape(1, d))


def _conv_sample_body(glu_ref, st_ref, sz_ref, x_ref, cw_ref, cb_ref, lg_ref, lb_ref, wo_ref, gp_ref,
                      out_ref, full_ref, c_ref, *, nb, tq):
    d = c_ref.shape[-1]
    first = HALO - (CONV_W - 1)
    full_ref[:, first:HALO, :] = st_ref[...]
    full_ref[:, HALO:HALO + tq, :] = glu_ref[...].reshape(nb, tq, d)
    for lc in range(d // LANES):
        ln = slice(lc * LANES, (lc + 1) * LANES)
        acc = jnp.zeros((nb, tq, LANES), F32)
        for j in range(CONV_W):
            acc = acc + cw_ref[j:j + 1, ln] * full_ref[:, first + j:first + j + tq, ln]
        c_ref[:, ln] = (acc + cb_ref[:, ln]).reshape(nb * tq, LANES)
    out_ref[...] = _conv_tail(c_ref[...], sz_ref[...], x_ref[...], lg_ref, lb_ref, wo_ref, gp_ref)


def _conv_sample(glu, state, sz, x, cw, cb, lg, lb, wo_bf, gp, nb):
    n_seq = state.shape[0]
    rows, d = x.shape
    di = glu.shape[-1]
    tq = rows // n_seq
    vec = lambda n: pl.BlockSpec((1, n), lambda i: (0, 0))
    return pl.pallas_call(
        functools.partial(_conv_sample_body, nb=nb, tq=tq),
        grid=(n_seq // nb,),
        in_specs=[pl.BlockSpec((nb * tq, di), lambda i: (i, 0)),
                  pl.BlockSpec((nb, CONV_W - 1, di), lambda i: (i, 0, 0)),
                  pl.BlockSpec((nb * tq, di), lambda i: (i, 0)),
                  pl.BlockSpec((nb * tq, d), lambda i: (i, 0)),
                  pl.BlockSpec((CONV_W, di), lambda i: (0, 0)),
                  vec(di), vec(di), vec(di),
                  pl.BlockSpec((di, d), lambda i: (0, 0)),
                  vec(d)],
        out_specs=pl.BlockSpec((nb * tq, d), lambda i: (i, 0)),
        out_shape=jax.ShapeDtypeStruct((rows, d), F32),
        scratch_shapes=[pltpu.VMEM((nb, HALO + tq, di), F32), pltpu.VMEM((nb * tq, di), F32)],
        compiler_params=_cparams("parallel"),
        name="conv_sample",
    )(glu, state, sz, x, cw, cb.reshape(1, di), lg.reshape(1, di), lb.reshape(1, di), wo_bf, gp.reshape(1, d))


def _head_major(col_pair, head):
    return col_pair if head % 2 == 0 else _swap_halves(col_pair)


def _nsa_proj_body(x_ref, gkv_ref, gq_ref, wkv_ref, wq_ref, wz_ref, wg_ref,
                   cmp_ref, slc_ref, win_ref, qa_ref, sz_ref, gate_ref, *aug_refs, seq_len, tm):
    x = x_ref[...]
    xn = x * lax.rsqrt(jnp.mean(x * x, axis=-1, keepdims=True) + EPS)
    hkv = (xn * gkv_ref[...]).astype(BF16)
    hq = (xn * gq_ref[...]).astype(BF16)
    kv = jnp.dot(hkv, wkv_ref[...], preferred_element_type=F32)
    width = 2 * KV_HEADS * HEAD_DIM
    cmp_ref[...] = kv[:, :width]
    slc_ref[...] = kv[:, width:2 * width]
    win_ref[...] = kv[:, 2 * width:]
    lane = _lane((tm, LANES))
    low = lane < HEAD_DIM
    uq = jnp.dot(hq, wq_ref[...], preferred_element_type=F32)
    scale = HEAD_DIM ** -0.5
    for h in range(N_HEADS):
        qh = _head_major(uq[:, (h // 2) * LANES:(h // 2 + 1) * LANES], h)
        qa_ref[:, h * LANES:(h + 1) * LANES] = jnp.where(low, qh * scale, 0.0).astype(BF16)
    z = jnp.dot(hq, wz_ref[...], preferred_element_type=F32)
    sz_ref[...] = z * _sigmoid(z)
    gate_ref[...] = _sigmoid(jnp.dot(hq, wg_ref[...], preferred_element_type=F32))
    if aug_refs:
        ska_ref, sv_ref, wka_ref, wv_ref = aug_refs
        row = pl.program_id(0) * tm + lax.broadcasted_iota(jnp.int32, (tm, LANES), 0)
        blk = (row % seq_len) // L_SLC
        onehot = (lane - HEAD_DIM == blk).astype(F32)
        kw = KV_HEADS * HEAD_DIM
        for h in range(KV_HEADS):
            ks = _head_major(kv[:, width + (h // 2) * LANES:width + (h // 2 + 1) * LANES], h)
            ska_ref[:, h * LANES:(h + 1) * LANES] = jnp.where(low, ks, onehot).astype(BF16)
            kwn = _head_major(kv[:, 2 * width + (h // 2) * LANES:2 * width + (h // 2 + 1) * LANES], h)
            wka_ref[:, h * LANES:(h + 1) * LANES] = jnp.where(low, kwn, 0.0).astype(BF16)
        sv_ref[...] = kv[:, width + kw:2 * width].T.astype(BF16)
        wv_ref[...] = kv[:, 2 * width + kw:].astype(BF16)


def _nsa_proj(x, g_kv, g_q, wkv_bf, wq_bf, wz_bf, wg_bf, tm, seq_len=None):
    rows, d = x.shape
    width = 2 * KV_HEADS * HEAD_DIM
    row = lambda n: pl.BlockSpec((tm, n), lambda i: (i, 0))
    full = lambda a: pl.BlockSpec(a.shape, lambda i: (0, 0))
    out_specs = [row(width), row(width), row(width), row(N_HEADS * LANES), row(wz_bf.shape[1]), row(LANES)]
    out_shape = [jax.ShapeDtypeStruct((rows, width), F32)] * 3 + [
        jax.ShapeDtypeStruct((rows, N_HEADS * LANES), BF16),
        jax.ShapeDtypeStruct((rows, wz_bf.shape[1]), F32),
        jax.ShapeDtypeStruct((rows, LANES), F32)]
    if seq_len is not None:
        out_specs += [row(KV_HEADS * LANES), pl.BlockSpec((width // 2, tm), lambda i: (0, i)),
                      row(KV_HEADS * LANES), row(width // 2)]
        out_shape += [jax.ShapeDtypeStruct((rows, KV_HEADS * LANES), BF16),
                      jax.ShapeDtypeStruct((width // 2, rows), BF16),
                      jax.ShapeDtypeStruct((rows, KV_HEADS * LANES), BF16),
                      jax.ShapeDtypeStruct((rows, width // 2), BF16)]
    gkv = g_kv.reshape(1, d)
    gq = g_q.reshape(1, d)
    return pl.pallas_call(
        functools.partial(_nsa_proj_body, seq_len=seq_len, tm=tm),
        grid=(rows // tm,),
        in_specs=[row(d), full(gkv), full(gq), full(wkv_bf), full(wq_bf), full(wz_bf), full(wg_bf)],
        out_specs=out_specs,
        out_shape=out_shape,
        compiler_params=_cparams("parallel"),
        name="nsa_proj",
    )(x, gkv, gq, wkv_bf, wq_bf, wz_bf, wg_bf)


def _compress_body(pt_ref, *refs, n_pages, n_col):
    del pt_ref
    page_refs = refs[:n_pages * n_col]
    w1_ref, pe_ref, w2_ref, out_ref, a_ref = refs[n_pages * n_col:]
    cpp = page_refs[0].shape[1] // CMP_STRIDE
    n_chunk = n_pages * cpp
    hid = w2_ref.shape[2]
    low = _lane((cpp, LANES)) < HEAD_DIM
    for p in range(n_pages):
        for c in range(n_col):
            pr = page_refs[p * n_col + c]
            for s in range(0, CMP_STRIDE, 2):
                b0 = pr[0, pl.ds(s, cpp, stride=CMP_STRIDE), :]
                b1 = pr[0, pl.ds(s + 1, cpp, stride=CMP_STRIDE), :]
                dst = (slice(p * cpp, (p + 1) * cpp), slice((s // 2) * LANES, (s // 2 + 1) * LANES))
                a_ref[(2 * c,) + dst] = jnp.where(low, b0, _swap_halves(b1))
                a_ref[(2 * c + 1,) + dst] = jnp.where(low, _swap_halves(b0), b1)
    for kv in range(2):
        w1 = w1_ref[kv]
        pe = jnp.dot(pe_ref[kv].astype(BF16), w1, preferred_element_type=F32)
        pe_term = pe[0:1, :hid] + pe[1:2, hid:]
        for hp in range(KV_HEADS // 2):
            pair = jnp.zeros((n_chunk, LANES), F32)
            for par in range(2):
                part = jnp.dot(a_ref[kv * KV_HEADS + 2 * hp + par].astype(BF16), w1, preferred_element_type=F32)
                pre = part[:, :hid] + pltpu.roll(part[:, hid:], n_chunk - 1, axis=0) + pe_term
                mid = pre * _sigmoid(pre)
                pair = pair + jnp.dot(mid.astype(BF16), w2_ref[kv, par], preferred_element_type=F32)
            col = kv * (KV_HEADS // 2) + hp
            out_ref[0, :, col * LANES:(col + 1) * LANES] = pair


def _compress(pages, page_table, w1cat_bf, pe8, w2_bf, n_pages):
    n_seq = page_table.shape[0] // n_pages
    page_rows, width = pages.shape[1:]
    n_chunk = n_pages * page_rows // CMP_STRIDE
    n_col = width // LANES
    page_spec = lambda j, c: pl.BlockSpec((1, page_rows, LANES), lambda i, pt: (pt[i * n_pages + j], 0, c))
    full = lambda a: pl.BlockSpec(a.shape, lambda i, pt: (0,) * a.ndim)
    grid_spec = pltpu.PrefetchScalarGridSpec(
        num_scalar_prefetch=1,
        grid=(n_seq,),
        in_specs=[page_spec(j, c) for j in range(n_pages) for c in range(n_col)]
        + [full(w1cat_bf), full(pe8), full(w2_bf)],
        out_specs=pl.BlockSpec((1, n_chunk, width), lambda i, pt: (i, 0, 0)),
        scratch_shapes=[pltpu.VMEM((2 * KV_HEADS, n_chunk, CMP_STRIDE * HEAD_DIM), F32)],
    )
    return pl.pallas_call(
        functools.partial(_compress_body, n_pages=n_pages, n_col=n_col),
        grid_spec=grid_spec,
        out_shape=jax.ShapeDtypeStruct((n_seq, n_chunk, width), F32),
        compiler_params=_cparams("parallel"),
        name="compress",
    )(page_table, *([pages] * (n_pages * n_col)), w1cat_bf, pe8, w2_bf)


def _bias_tile_body(rb_ref, out_ref, *, row_step, lane_step, offset, hi):
    h = pl.program_id(0)
    shape = out_ref.shape[1:]
    d = row_step * lax.broadcasted_iota(jnp.int32, shape, 0) + lane_step * _lane(shape) + offset
    n = jnp.maximum(d, 0)
    nf = jnp.maximum(n, 1).astype(F32)
    large = MAX_EXACT + (jnp.log(nf / MAX_EXACT) / math.log(MAX_DISTANCE / MAX_EXACT)
                         * (NUM_BUCKETS - MAX_EXACT)).astype(jnp.int32)
    large = jnp.minimum(large, NUM_BUCKETS - 1)
    bucket = jnp.where(n < MAX_EXACT, n, large)
    far = rb_ref[NUM_BUCKETS - 1, h]
    val = jnp.zeros(shape, F32)
    for k in range(NUM_BUCKETS - 1):
        val = jnp.where(bucket == k, rb_ref[k, h] - far, val)
    out_ref[0] = jnp.where((d >= 0) & (d <= hi), val, NEG)


def _bias_tile(rel_bias, rows, width, lane_step, offset, hi=1 << 30, row_step=1):
    return pl.pallas_call(
        functools.partial(_bias_tile_body, row_step=row_step, lane_step=lane_step, offset=offset, hi=hi),
        grid=(N_HEADS,),
        in_specs=[pl.BlockSpec(memory_space=pltpu.SMEM)],
        out_specs=pl.BlockSpec((1, rows, width), lambda h: (h, 0, 0)),
        out_shape=jax.ShapeDtypeStruct((N_HEADS, rows, width), F32),
        compiler_params=_cparams("parallel"),
        name="bias_tile",
    )(rel_bias)


_NT = (((1,), (1,)), ((), ()))


def _softmax_rows(s):
    m = jnp.maximum(jnp.max(s, axis=-1, keepdims=True), M_FLOOR)
    p = jnp.exp(s - m)
    return p, jnp.sum(p, axis=-1, keepdims=True)


def _gate_col(gate, col):
    return jnp.sum(jnp.where(_lane(gate.shape) == col, gate, 0.0), axis=-1, keepdims=True)


def _pair_columns(o_even, o_odd, valid_half):
    lo = o_even if valid_half == 0 else _swap_halves(o_even)
    hi = o_odd if valid_half == 1 else _swap_halves(o_odd)
    return jnp.where(_lane(lo.shape) < HEAD_DIM, lo, hi)


def _select_blocks(imp, tb, n_sel):
    lane = _lane(imp.shape)
    lane_f = lane.astype(F32)
    j = lane - HEAD_DIM
    valid = (j >= 0) & (j <= tb)
    forced = (j == 0) | (j == tb) | (j == tb - 1)
    score = jnp.where(valid, jnp.where(forced, BIG, imp), -BIG)
    score = jnp.where(j >= 0, score, -jnp.inf)
    sel = jnp.zeros(imp.shape, jnp.bool_)
    for _ in range(n_sel):
        best = jnp.max(score, axis=-1, keepdims=True)
        first = jnp.min(jnp.where(score == best, lane_f, 4.0 * LANES), axis=-1, keepdims=True)
        pick = lane_f == first
        sel = sel | pick
        score = jnp.where(pick, -jnp.inf, score)
    return jnp.where(sel & valid, 0.0, jnp.where(j >= 0, NEG, 0.0))


def _cmp_prompt_body(qa_ref, kc_ref, vc_ref, m_ref, cb_ref, gate_ref, oc_ref, qsel_ref, *, n_key):
    qt = pl.program_id(1)
    start = pl.multiple_of(8 * qt + 8, 8)
    kwin = kc_ref[0, pl.ds(start, n_key), :]
    vwin = vc_ref[0, pl.ds(start, n_key), :]
    mwin = m_ref[pl.ds(start, n_key), :]
    lane = _lane((Q_BLK, LANES))
    low = lane < HEAD_DIM
    pos = qt * Q_BLK + lax.broadcasted_iota(jnp.int32, (Q_BLK, 1), 0)
    tb = pos // L_SLC
    gate = gate_ref[...]
    for kvh in range(KV_HEADS):
        ka = kwin[:, kvh * LANES:(kvh + 1) * LANES].astype(BF16)
        vp = vwin[:, (kvh // 2) * LANES:(kvh // 2 + 1) * LANES].astype(BF16)
        pc_sum = jnp.zeros((Q_BLK, n_key), F32)
        outs = []
        qs = []
        for g in range(GROUP):
            h = kvh * GROUP + g
            q = qa_ref[0, :, h * LANES:(h + 1) * LANES].astype(F32)
            qs.append(q)
            qa = jnp.where(lane == HEAD_DIM, NEG, q).astype(BF16)
            s = lax.dot_general(qa, ka, _NT, preferred_element_type=F32) + cb_ref[h]
            p, l = _softmax_rows(s)
            inv = 1.0 / jnp.maximum(l, 1e-30)
            pc_sum = pc_sum + p * inv
            o = jnp.dot(p.astype(BF16), vp, preferred_element_type=F32)
            outs.append(o * (inv * _gate_col(gate, h)))
        for gp in range(GROUP // 2):
            col = kvh * (GROUP // 2) + gp
            oc_ref[0, :, col * LANES:(col + 1) * LANES] = _pair_columns(outs[2 * gp], outs[2 * gp + 1], kvh % 2)
        imp = jnp.dot(pc_sum, mwin, preferred_element_type=F32, precision=lax.Precision.HIGHEST)
        selneg = _select_blocks(imp, tb, N_SEL)
        for g in range(GROUP):
            qsel_ref[0, kvh, :, g * Q_BLK:(g + 1) * Q_BLK] = jnp.where(low, qs[g], selneg).T.astype(BF16)


def _cmp_prompt(qa, kc_pad, vc_pad, m_pad, cb, gate, batch, seq):
    nq = seq // Q_BLK
    n_key = kc_pad.shape[1] // 2
    d_out = N_HEADS * HEAD_DIM
    return pl.pallas_call(
        functools.partial(_cmp_prompt_body, n_key=n_key),
        grid=(batch, nq),
        in_specs=[pl.BlockSpec((1, Q_BLK, N_HEADS * LANES), lambda b, t: (b, t, 0)),
                  pl.BlockSpec((1,) + kc_pad.shape[1:], lambda b, t: (b, 0, 0)),
                  pl.BlockSpec((1,) + vc_pad.shape[1:], lambda b, t: (b, 0, 0)),
                  pl.BlockSpec(m_pad.shape, lambda b, t: (0, 0)),
                  pl.BlockSpec(cb.shape, lambda b, t: (0, 0, 0)),
                  pl.BlockSpec((Q_BLK, LANES), lambda b, t: (b * nq + t, 0))],
        out_specs=[pl.BlockSpec((1, Q_BLK, d_out), lambda b, t: (b, t, 0)),
                   pl.BlockSpec((1, KV_HEADS, LANES, GROUP * Q_BLK), lambda b, t: (b * nq + t, 0, 0, 0))],
        out_shape=[jax.ShapeDtypeStruct((batch, seq, d_out), F32),
                   jax.ShapeDtypeStruct((batch * nq, KV_HEADS, LANES, GROUP * Q_BLK), BF16)],
        compiler_params=_cparams("parallel", "parallel"),
        name="cmp_prompt",
    )(qa.reshape(batch, seq, -1), kc_pad, vc_pad, m_pad, cb, gate)


KEY_TILE = 256
COL_CHAIN = 256


def _slc_prompt_body(q_ref, k_ref, v_ref, sb_ref, gate_ref, out_ref, m_ref, l_ref, acc_ref):
    kvh = pl.program_id(1)
    qt = pl.program_id(2)
    cols = GROUP * Q_BLK
    n_chain = cols // COL_CHAIN
    m_ref[...] = jnp.full(m_ref.shape, M_FLOOR, F32)
    l_ref[...] = jnp.zeros(l_ref.shape, F32)
    acc_ref[...] = jnp.zeros(acc_ref.shape, F32)
    last = (qt * Q_BLK) // KEY_TILE

    def make_step(with_bias):
        def step(kt, carry):
            base = pl.multiple_of(kt * KEY_TILE, KEY_TILE)
            ks = k_ref[0, pl.ds(base, KEY_TILE), :]
            vt = v_ref[:, pl.ds(base, KEY_TILE)]
            case = qt % 2 + 2 * (last - kt)
            chains = [slice(c * COL_CHAIN, (c + 1) * COL_CHAIN) for c in range(n_chain)]
            ss = [jnp.dot(ks, q_ref[0, 0, :, cs], preferred_element_type=F32) for cs in chains]
            if with_bias:
                ss = [s + sb_ref[case, 0, :, cs] for s, cs in zip(ss, chains)]
            ps, alphas = [], []
            for c, s in enumerate(ss):
                m_old = m_ref[c]
                m_new = jnp.maximum(m_old, jnp.max(s, axis=0, keepdims=True))
                alpha = jnp.exp(m_old - m_new)
                p = jnp.exp(s - m_new)
                l_ref[c] = alpha * l_ref[c] + jnp.sum(p, axis=0, keepdims=True)
                m_ref[c] = m_new
                ps.append(p.astype(BF16))
                alphas.append(alpha)
            for c in range(n_chain):
                acc_ref[c] = alphas[c] * acc_ref[c] + jnp.dot(vt, ps[c], preferred_element_type=F32)
            return carry
        return step

    n_far = jnp.maximum(last - 1, 0)
    lax.fori_loop(0, n_far, make_step(False), 0)
    lax.fori_loop(n_far, last + 1, make_step(True), 0)
    gate = gate_ref[...]
    outs = []
    for g in range(GROUP):
        c, gl = divmod(g * Q_BLK, COL_CHAIN)
        inv = 1.0 / jnp.maximum(l_ref[c][:, gl:gl + Q_BLK], 1e-30)
        o = (acc_ref[c][:, gl:gl + Q_BLK] * inv).T
        outs.append(o * _gate_col(gate, N_HEADS + kvh * GROUP + g))
    for parity in range(2):
        @pl.when(kvh % 2 == parity)
        def _():
            for gp in range(GROUP // 2):
                out_ref[0, :, gp * LANES:(gp + 1) * LANES] = _pair_columns(outs[2 * gp], outs[2 * gp + 1], parity)


def _slc_prompt(qsel_t, ska, sv_t, sb_t, gate, batch, seq):
    nq = seq // Q_BLK
    cols = GROUP * Q_BLK
    d_out = N_HEADS * HEAD_DIM
    n_chain = cols // COL_CHAIN
    return pl.pallas_call(
        _slc_prompt_body,
        grid=(batch, KV_HEADS, nq),
        in_specs=[pl.BlockSpec((1, 1, LANES, cols), lambda b, h, t: (b * nq + t, h, 0, 0)),
                  pl.BlockSpec((1, seq, LANES), lambda b, h, t: (b, 0, h)),
                  pl.BlockSpec((LANES, seq), lambda b, h, t: (h // 2, b)),
                  pl.BlockSpec((4, 1, KEY_TILE, cols), lambda b, h, t: (0, h, 0, 0)),
                  pl.BlockSpec((Q_BLK, LANES), lambda b, h, t: (b * nq + t, 0))],
        out_specs=pl.BlockSpec((1, Q_BLK, GROUP * HEAD_DIM), lambda b, h, t: (b, t, h)),
        out_shape=jax.ShapeDtypeStruct((batch, seq, d_out), F32),
        scratch_shapes=[pltpu.VMEM((n_chain, 1, COL_CHAIN), F32), pltpu.VMEM((n_chain, 1, COL_CHAIN), F32),
                        pltpu.VMEM((n_chain, LANES, COL_CHAIN), F32)],
        compiler_params=_cparams("parallel", "parallel", "arbitrary"),
        name="slc_prompt",
    )(qsel_t, ska.reshape(batch, seq, -1), sv_t, sb_t, gate)


def _win_prompt_body(qa_ref, k_ref, v_ref, wb_ref, gate_ref, out_ref, *, n_key):
    kvh = pl.program_id(1)
    qt = pl.program_id(2)
    base = pl.multiple_of(qt * Q_BLK, Q_BLK)
    ks = k_ref[0, pl.ds(base, n_key), :]
    vs = v_ref[0, pl.ds(base, n_key), :]
    lane = _lane((Q_BLK, LANES))
    gate = gate_ref[...]
    outs = []
    for g in range(GROUP):
        q = qa_ref[0, :, g * LANES:(g + 1) * LANES]
        qa = jnp.where(lane == HEAD_DIM, jnp.asarray(NEG, BF16), q)
        s = lax.dot_general(qa, ks, _NT, preferred_element_type=F32) + wb_ref[g]
        p, l = _softmax_rows(s)
        o = jnp.dot(p.astype(BF16), vs, preferred_element_type=F32)
        gcol = _gate_col(gate, 2 * N_HEADS + kvh * GROUP + g)
        outs.append(o * (gcol / jnp.maximum(l, 1e-30)))
    for parity in range(2):
        @pl.when(kvh % 2 == parity)
        def _():
            for gp in range(GROUP // 2):
                out_ref[0, :, gp * LANES:(gp + 1) * LANES] = _pair_columns(outs[2 * gp], outs[2 * gp + 1], parity)


def _win_prompt(qa, wka_pad, wv_pad, wb, gate, batch, seq):
    nq = seq // Q_BLK
    n_key = WINDOW + Q_BLK
    d_out = N_HEADS * HEAD_DIM
    padded = wka_pad.shape[1]
    return pl.pallas_call(
        functools.partial(_win_prompt_body, n_key=n_key),
        grid=(batch, KV_HEADS, nq),
        in_specs=[pl.BlockSpec((1, Q_BLK, GROUP * LANES), lambda b, h, t: (b, t, h)),
                  pl.BlockSpec((1, padded, LANES), lambda b, h, t: (b, 0, h)),
                  pl.BlockSpec((1, padded, LANES), lambda b, h, t: (b, 0, h // 2)),
                  pl.BlockSpec((GROUP, Q_BLK, n_key), lambda b, h, t: (h, 0, 0)),
                  pl.BlockSpec((Q_BLK, LANES), lambda b, h, t: (b * nq + t, 0))],
        out_specs=pl.BlockSpec((1, Q_BLK, GROUP * HEAD_DIM), lambda b, h, t: (b, t, h)),
        out_shape=jax.ShapeDtypeStruct((batch, seq, d_out), F32),
        compiler_params=_cparams("parallel", "parallel", "parallel"),
        name="win_prompt",
    )(qa.reshape(batch, seq, -1), wka_pad, wv_pad, wb, gate)


def _nsa_out_body(oc_ref, os_ref, ow_ref, sz_ref, x_ref, wo_ref, gp_ref, out_ref):
    d = oc_ref.shape[-1]
    y = oc_ref[...] * sz_ref[:, :d] + os_ref[...] * sz_ref[:, d:2 * d] + ow_ref[...] * sz_ref[:, 2 * d:]
    o = jnp.dot(y.astype(BF16), wo_ref[...], preferred_element_type=F32)
    out_ref[...] = x_ref[...] + _rms(o, gp_ref[...])


def _nsa_out(oc, os_, ow, sz, x, wo_bf, gp, tm):
    rows, d = x.shape
    dq = oc.shape[-1]
    row = lambda n: pl.BlockSpec((tm, n), lambda i: (i, 0))
    return pl.pallas_call(
        _nsa_out_body,
        grid=(rows // tm,),
        in_specs=[row(dq), row(dq), row(dq), row(N_BRANCH * dq), row(d),
                  pl.BlockSpec((dq, d), lambda i: (0, 0)), pl.BlockSpec((1, d), lambda i: (0, 0))],
        out_specs=row(d),
        out_shape=jax.ShapeDtypeStruct((rows, d), F32),
        compiler_params=_cparams("parallel"),
        name="nsa_out",
    )(oc, os_, ow, sz, x, wo_bf, gp.reshape(1, d))


def _sample_attn_body(pt_ref, *refs, n_pages, tq, w_keep):
    del pt_ref
    page_refs = refs[:n_pages]
    (qa_ref, kvc_ref, snew_ref, wst_ref, wnew_ref, m_ref, e_ref, cbs_ref, sbs_ref, wbs_ref, gate_ref,
     oc_ref, os_ref, ow_ref, ksel_ref, kwin_ref) = refs[n_pages:]
    n_past = n_pages * PAGE_SIZE
    n_sk = ksel_ref.shape[0]
    n_wk = kwin_ref.shape[0]
    width = ksel_ref.shape[1]
    kw = width // 2
    for p in range(n_pages):
        ksel_ref[p * PAGE_SIZE:(p + 1) * PAGE_SIZE, :] = page_refs[p][0]
    ksel_ref[n_past:n_past + tq, :] = snew_ref[...]
    ksel_ref[n_past + tq:, :] = jnp.zeros((n_sk - n_past - tq, width), F32)
    kwin_ref[0:w_keep, :] = wst_ref[0]
    kwin_ref[w_keep:w_keep + tq, :] = wnew_ref[...]
    kwin_ref[w_keep + tq:, :] = jnp.zeros((n_wk - w_keep - tq, width), F32)
    pos = n_past + lax.broadcasted_iota(jnp.int32, (tq, 1), 0)
    tb = pos // L_SLC
    gate = gate_ref[...]
    rows = GROUP * tq

    def attend(q, k, v, bias, extra=None):
        s = lax.dot_general(q, k, _NT, preferred_element_type=F32)
        if extra is not None:
            s = s + extra
        s = (s.reshape(GROUP, tq, s.shape[-1]) + bias).reshape(rows, s.shape[-1])
        p, l = _softmax_rows(s)
        inv = 1.0 / jnp.maximum(l, 1e-30)
        return p, inv, jnp.dot(p.astype(BF16), v, preferred_element_type=F32)

    def store(out_ref, o, kvh, branch):
        outs = [o[g * tq:(g + 1) * tq] * _gate_col(gate, branch * N_HEADS + kvh * GROUP + g) for g in range(GROUP)]
        for gp in range(GROUP // 2):
            col = kvh * (GROUP // 2) + gp
            out_ref[:, col * LANES:(col + 1) * LANES] = _pair_columns(outs[2 * gp], outs[2 * gp + 1], kvh % 2)

    def pair_cols(ref, pr):
        kcol = slice(pr * LANES, (pr + 1) * LANES)
        vcol = slice(kw + pr * LANES, kw + (pr + 1) * LANES)
        if len(ref.shape) == 3:
            return ref[0, :, kcol].astype(BF16), ref[0, :, vcol].astype(BF16)
        return ref[:, kcol].astype(BF16), ref[:, vcol].astype(BF16)

    qs, imps = [], []
    for kvh in range(KV_HEADS):
        q = jnp.concatenate([qa_ref[:, h * LANES:(h + 1) * LANES]
                             for h in range(kvh * GROUP, (kvh + 1) * GROUP)], axis=0)
        qs.append((q if kvh % 2 == 0 else _swap_halves(q)).astype(BF16))
    for pr in range(KV_HEADS // 2):
        kc, vc = pair_cols(kvc_ref, pr)
        for kvh in (2 * pr, 2 * pr + 1):
            p, inv, o = attend(qs[kvh], kc, vc, cbs_ref[kvh * GROUP:(kvh + 1) * GROUP])
            store(oc_ref, o * inv, kvh, 0)
            pc_sum = jnp.sum((p * inv).reshape(GROUP, tq, p.shape[-1]), axis=0)
            imps.append(jnp.dot(pc_sum, m_ref[...], preferred_element_type=F32, precision=lax.Precision.HIGHEST))
    selneg = _select_blocks(jnp.concatenate(imps, axis=0), jnp.concatenate([tb] * KV_HEADS, axis=0), N_SEL)
    for pr in range(KV_HEADS // 2):
        ksl, vsl = pair_cols(ksel_ref, pr)
        kwn, vwn = pair_cols(kwin_ref, pr)
        for kvh in (2 * pr, 2 * pr + 1):
            heads = slice(kvh * GROUP, (kvh + 1) * GROUP)
            qm = jnp.concatenate([selneg[kvh * tq:(kvh + 1) * tq]] * GROUP, axis=0).astype(BF16)
            block_mask = jnp.dot(qm, e_ref[...], preferred_element_type=F32)
            _, inv, o = attend(qs[kvh], ksl, vsl, sbs_ref[heads], block_mask)
            store(os_ref, o * inv, kvh, 1)
            _, inv, o = attend(qs[kvh], kwn, vwn, wbs_ref[heads])
            store(ow_ref, o * inv, kvh, 2)


def _sample_attn(qa, kvc, pages, page_table, slc_new, win_state, win_new, m_s, e_s, cbs, sbs, wbs, gate,
                 n_seq, tq, n_pages):
    width = pages.shape[-1]
    n_sk = sbs.shape[-1]
    n_wk = wbs.shape[-1]
    w_keep = win_state.shape[1]
    d_out = N_HEADS * HEAD_DIM
    page_spec = lambda j: pl.BlockSpec((1, PAGE_SIZE, width), lambda i, pt: (pt[i * n_pages + j], 0, 0))
    full = lambda a: pl.BlockSpec(a.shape, lambda i, pt: (0,) * a.ndim)
    row = lambda n: pl.BlockSpec((tq, n), lambda i, pt: (i, 0))
    grid_spec = pltpu.PrefetchScalarGridSpec(
        num_scalar_prefetch=1,
        grid=(n_seq,),
        in_specs=[page_spec(j) for j in range(n_pages)] + [
            row(N_HEADS * LANES),
            pl.BlockSpec((1,) + kvc.shape[1:], lambda i, pt: (i, 0, 0)),
            row(width),
            pl.BlockSpec((1, w_keep, width), lambda i, pt: (i, 0, 0)),
            row(width),
            full(m_s), full(e_s), full(cbs), full(sbs), full(wbs),
            row(LANES)],
        out_specs=[row(d_out)] * 3,
        scratch_shapes=[pltpu.VMEM((n_sk, width), F32), pltpu.VMEM((n_wk, width), F32)],
    )
    return pl.pallas_call(
        functools.partial(_sample_attn_body, n_pages=n_pages, tq=tq, w_keep=w_keep),
        grid_spec=grid_spec,
        out_shape=[jax.ShapeDtypeStruct((n_seq * tq, d_out), F32)] * 3,
        compiler_params=_cparams("parallel"),
        name="sample_attn",
    )(page_table, *([pages] * n_pages), qa, kvc, slc_new, win_state, win_new, m_s, e_s, cbs, sbs, wbs, gate)


def _overlap_matrix(n_rows, row0, n_cmp, n_blk):
    import numpy as np
    m = np.zeros((n_rows, LANES), np.float32)
    cs = np.arange(n_cmp)[:, None] * CMP_STRIDE
    js = np.arange(n_blk)[None, :] * L_SLC
    m[row0:row0 + n_cmp, HEAD_DIM:HEAD_DIM + n_blk] = (cs <= js + L_SLC - 1) & (cs + L_CMP - 1 >= js)
    return jnp.asarray(m)


def kernel(x_prompt, x_sample, state_conv, cache_cmp, cache_slc, state_win, page_table, rel_bias, a_norm_pre, a_w_in, a_conv_w, a_conv_b, a_ln_g, a_ln_b, a_w_out, a_norm_post, kv_norm, w_kv, cmp_pe, cmp_w1, cmp_w2, b_norm_pre, b_w_in, b_w_out, b_norm_post):
    import numpy as np
    bp, tp, d = x_prompt.shape
    bd, tq, _ = x_sample.shape
    n_pages = page_table.shape[1]
    past = n_pages * PAGE_SIZE
    w_keep = state_win.shape[1]
    width = 2 * KV_HEADS * HEAD_DIM
    kw = KV_HEADS * HEAD_DIM
    dq = N_HEADS * HEAD_DIM
    assert b_w_in.shape[0] == 1 and tp % (2 * KEY_TILE) == 0 and tp // L_SLC <= HEAD_DIM
    tm = 256

    xp = x_prompt
    xs = x_sample.reshape(bd * tq, d)
    conv_p, conv_s = [], []
    for l in range(a_w_in.shape[0]):
        w_in = a_w_in[l].astype(BF16)
        w_out = a_w_out[l].astype(BF16)
        di = a_w_out.shape[1]
        tail = (a_conv_w[l], a_conv_b[l], a_ln_g[l], a_ln_b[l], w_out, a_norm_post[l])
        glu, sz = _glu_proj(xp.reshape(bp * tp, d), a_norm_pre[l], w_in, tm)
        glu = glu.reshape(bp, tp, di)
        xp = _conv_prompt(glu, sz.reshape(bp, tp, di), xp, *tail, tm)
        conv_p.append(glu[:, -(CONV_W - 1):])
        glu, sz = _glu_proj(xs, a_norm_pre[l], w_in, tm)
        xs = _conv_sample(glu, state_conv[l], sz, xs, *tail, 16)
        conv_s.append(jnp.concatenate([state_conv[l], glu.reshape(bd, tq, di)], axis=1)[:, -(CONV_W - 1):])

    bw = b_w_in[0]
    n_gate = N_BRANCH * N_HEADS
    wq = bw[:, :dq].astype(BF16)
    wz = bw[:, dq:dq * (1 + N_BRANCH)].astype(BF16)
    wg = jnp.pad(bw[:, dq * (1 + N_BRANCH):], ((0, 0), (0, LANES - n_gate))).astype(BF16)
    wkv = w_kv.astype(BF16)
    xp2 = xp.reshape(bp * tp, d)
    (cmp_p, slc_p, win_p, qa_p, sz_p, gate_p, ska, sv, wka, wv) = _nsa_proj(
        xp2, kv_norm, b_norm_pre[0], wkv, wq, wz, wg, tm, seq_len=tp)
    cmp_s, slc_s, win_s, qa_s, sz_s, gate_s = _nsa_proj(xs, kv_norm, b_norm_pre[0], wkv, wq, wz, wg, tm)
    qa_s = qa_s.astype(F32)

    w1cat = jnp.concatenate([cmp_w1[:, :CMP_STRIDE * HEAD_DIM], cmp_w1[:, CMP_STRIDE * HEAD_DIM:]], axis=2).astype(BF16)
    pe8 = jnp.pad(cmp_pe.reshape(2, L_CMP // CMP_STRIDE, CMP_STRIDE * HEAD_DIM), ((0, 0), (0, 6), (0, 0)))
    zero = jnp.zeros_like(cmp_w2)
    w2h = jnp.stack([jnp.concatenate([cmp_w2, zero], axis=2), jnp.concatenate([zero, cmp_w2], axis=2)], axis=1).astype(BF16)
    prompt_page = 1024
    pp = tp // prompt_page
    kvc_p = _compress(cmp_p.reshape(bp * pp, prompt_page, width), jnp.arange(bp * pp, dtype=jnp.int32), w1cat, pe8, w2h, pp)
    pt_flat = page_table.reshape(-1).astype(jnp.int32)
    kvc_s = _compress(cache_cmp.reshape(-1, PAGE_SIZE, width), pt_flat, w1cat, pe8, w2h, n_pages)

    n_chunk_p = kvc_p.shape[1]
    nc_p = n_chunk_p - L_CMP // CMP_STRIDE + 1
    real = (jnp.arange(n_chunk_p) < nc_p)[None, :, None]
    flag = jnp.zeros((LANES - HEAD_DIM,), F32).at[0].set(1.0)
    kc4 = jnp.where(real, kvc_p[:, :, :kw], 0.0).reshape(bp, n_chunk_p, KV_HEADS, HEAD_DIM)
    aug = jnp.where(real[..., None], 0.0, flag) * jnp.ones((bp, n_chunk_p, KV_HEADS, 1), F32)
    kc_real = jnp.concatenate([kc4, aug], axis=-1).reshape(bp, n_chunk_p, KV_HEADS * LANES)
    pad_row = jnp.concatenate([jnp.zeros((HEAD_DIM,), F32), flag])
    kc_front = jnp.broadcast_to(jnp.tile(pad_row, KV_HEADS), (bp, n_chunk_p, KV_HEADS * LANES))
    kc_pad = jnp.concatenate([kc_front, kc_real], axis=1)
    vc_pad = jnp.concatenate([jnp.zeros((bp, n_chunk_p, kw), F32), jnp.where(real, kvc_p[:, :, kw:], 0.0)], axis=1)
    m_pad = _overlap_matrix(2 * n_chunk_p, n_chunk_p, nc_p, tp // L_SLC)
    cb = _bias_tile(rel_bias, Q_BLK, n_chunk_p, -CMP_STRIDE, CMP_STRIDE * (n_chunk_p - Q_BLK // CMP_STRIDE) - (L_CMP - 1))
    oc_p, qsel = _cmp_prompt(qa_p, kc_pad, vc_pad, m_pad, cb, gate_p, bp, tp)
    sb = jnp.stack([_bias_tile(rel_bias, KEY_TILE, Q_BLK, 1, off, row_step=-1)
                    for off in (0, Q_BLK, 2 * Q_BLK, 3 * Q_BLK)])
    sb = sb.reshape(4, KV_HEADS, GROUP, KEY_TILE, Q_BLK).transpose(0, 1, 3, 2, 4).reshape(4, KV_HEADS, KEY_TILE, GROUP * Q_BLK)
    os_p = _slc_prompt(qsel, ska, sv, sb, gate_p, bp, tp)
    wb = _bias_tile(rel_bias, Q_BLK, WINDOW + Q_BLK, -1, WINDOW, WINDOW)
    wka_front = jnp.broadcast_to(jnp.tile(pad_row, KV_HEADS).astype(BF16), (bp, WINDOW, KV_HEADS * LANES))
    wka_pad = jnp.concatenate([wka_front, wka.reshape(bp, tp, -1)], axis=1)
    wv_pad = jnp.concatenate([jnp.zeros((bp, WINDOW, kw), BF16), wv.reshape(bp, tp, -1)], axis=1)
    ow_p = _win_prompt(qa_p, wka_pad, wv_pad, wb, gate_p, bp, tp)
    wo = b_w_out[0].astype(BF16)
    y_p = _nsa_out(oc_p.reshape(bp * tp, dq), os_p.reshape(bp * tp, dq), ow_p.reshape(bp * tp, dq), sz_p, xp2,
                   wo, b_norm_post[0], tm)

    n_chunk_s = kvc_s.shape[1]
    nc_s = n_chunk_s - L_CMP // CMP_STRIDE + 1
    nb_s = -(-(past + tq) // L_SLC)
    n_sk = -(-(nb_s * L_SLC) // LANES) * LANES
    n_wk = -(-(w_keep + tq) // LANES) * LANES
    m_s = _overlap_matrix(n_chunk_s, 0, nc_s, nb_s)
    e_np = np.zeros((LANES, n_sk), np.float32)
    e_np[HEAD_DIM + np.arange(n_sk) // L_SLC, np.arange(n_sk)] = 1.0
    e_s = jnp.asarray(e_np, BF16)
    cbs = _bias_tile(rel_bias, tq, n_chunk_s, -CMP_STRIDE, past - (L_CMP - 1))
    sbs = _bias_tile(rel_bias, tq, n_sk, -1, past)
    wbs = _bias_tile(rel_bias, tq, n_wk, -1, w_keep, WINDOW)
    oc_s, os_s, ow_s = _sample_attn(qa_s, kvc_s, cache_slc.reshape(-1, PAGE_SIZE, width), pt_flat, slc_s,
                                    state_win.reshape(bd, w_keep, width), win_s, m_s, e_s, cbs, sbs, wbs, gate_s,
                                    bd, tq, n_pages)
    y_s = _nsa_out(oc_s, os_s, ow_s, sz_s, xs, wo, b_norm_post[0], tm)

    kv5 = lambda a, b, t: a.reshape(b, t, 2, KV_HEADS, HEAD_DIM)
    win_p5 = kv5(win_p, bp, tp)
    win_s5 = kv5(win_s, bd, tq)
    win_all = jnp.concatenate([state_win, win_s5], axis=1)
    return (y_p.reshape(bp, tp, d), y_s.reshape(bd, tq, d), jnp.stack(conv_p), jnp.stack(conv_s),
            kv5(cmp_p, bp, tp), kv5(cmp_s, bd, tq), kv5(slc_p, bp, tp), kv5(slc_s, bd, tq),
            win_p5[:, -min(WINDOW, tp):], win_all[:, -min(WINDOW, win_all.shape[1]):])
```

```python
import functools
import math

import jax
import jax.numpy as jnp
from jax import lax
from jax.experimental import pallas as pl
from jax.experimental.pallas import tpu as pltpu

F32 = jnp.float32
BF16 = jnp.bfloat16

EPS = 1e-6
NEG = -1e30
BIG = 1e9
M_FLOOR = -1e20

HEAD_DIM = 64
KV_HEADS = 4
N_HEADS = 16
GROUP = N_HEADS // KV_HEADS
N_BRANCH = 3
CONV_W = 31
L_CMP = 32
CMP_STRIDE = 16
L_SLC = 64
N_SEL = 16
WINDOW = 512
Q_BLK = 128
NUM_BUCKETS = 32
MAX_DISTANCE = 128
MAX_EXACT = NUM_BUCKETS // 2
PAGE_SIZE = 128
LANES = 128
HALO = 32

VMEM_LIMIT = 56 * 1024 * 1024


def _cparams(*sem):
    return pltpu.CompilerParams(dimension_semantics=sem, vmem_limit_bytes=VMEM_LIMIT)


def _sigmoid(x):
    return 1.0 / (1.0 + jnp.exp(-x))


def _rms(x, g):
    return x * lax.rsqrt(jnp.mean(x * x, axis=-1, keepdims=True) + EPS) * g


def _lane(shape):
    return lax.broadcasted_iota(jnp.int32, shape, len(shape) - 1)


def _swap_halves(x):
    return pltpu.roll(x, HEAD_DIM, axis=x.ndim - 1)


def _glu_proj_body(x_ref, g_ref, w_ref, glu_ref, sz_ref):
    di = glu_ref.shape[-1]
    h = _rms(x_ref[...], g_ref[...])
    u = jnp.dot(h.astype(BF16), w_ref[...], preferred_element_type=F32)
    z = u[:, 2 * di:]
    glu_ref[...] = u[:, :di] * _sigmoid(u[:, di:2 * di])
    sz_ref[...] = z * _sigmoid(z)


def _glu_proj(x, g, w_bf, tm):
    rows, d = x.shape
    di = w_bf.shape[1] // 3
    return pl.pallas_call(
        _glu_proj_body,
        grid=(rows // tm,),
        in_specs=[pl.BlockSpec((tm, d), lambda i: (i, 0)),
                  pl.BlockSpec((1, d), lambda i: (0, 0)),
                  pl.BlockSpec((d, 3 * di), lambda i: (0, 0))],
        out_specs=[pl.BlockSpec((tm, di), lambda i: (i, 0)),
                   pl.BlockSpec((tm, di), lambda i: (i, 0))],
        out_shape=[jax.ShapeDtypeStruct((rows, di), F32)] * 2,
        compiler_params=_cparams("parallel"),
        name="glu_proj",
    )(x, g.reshape(1, d), w_bf)


def _conv_tail(c, sz, x, lg_ref, lb_ref, wo_ref, gp_ref):
    mu = jnp.mean(c, axis=-1, keepdims=True)
    cc = c - mu
    var = jnp.mean(cc * cc, axis=-1, keepdims=True)
    y = cc * lax.rsqrt(var + EPS) * lg_ref[...] + lb_ref[...]
    y = y * _sigmoid(y) * sz
    o = jnp.dot(y.astype(BF16), wo_ref[...], preferred_element_type=F32)
    return x + _rms(o, gp_ref[...])


def _conv_prompt_body(glu_ref, prev_ref, sz_ref, x_ref, cw_ref, cb_ref, lg_ref, lb_ref, wo_ref, gp_ref,
                      out_ref, full_ref, c_ref, *, tm):
    t = pl.program_id(1)
    full_ref[0:HALO, :] = jnp.where(t > 0, prev_ref[0], 0.0)
    full_ref[HALO:HALO + tm, :] = glu_ref[0]
    d = c_ref.shape[-1]
    first = HALO - (CONV_W - 1)
    n_phase = 8
    n_grp = -(-CONV_W // n_phase)
    for lc in range(d // LANES):
        ln = slice(lc * LANES, (lc + 1) * LANES)
        acc = jnp.zeros((tm, LANES), F32)
        for b in range(n_phase):
            sb = full_ref[first + b:first + b + tm + 8 * (n_grp - 1), ln]
            for a in range(n_grp):
                j = 8 * a + b
                if j < CONV_W:
                    acc = acc + cw_ref[j:j + 1, ln] * sb[8 * a:8 * a + tm]
        c_ref[:, ln] = acc + cb_ref[:, ln]
    out_ref[0] = _conv_tail(c_ref[...], sz_ref[0], x_ref[0], lg_ref, lb_ref, wo_ref, gp_ref)


def _conv_prompt(glu, sz, x, cw, cb, lg, lb, wo_bf, gp, tm):
    b, t, d = x.shape
    di = glu.shape[-1]
    per = tm // HALO
    vec = lambda n: pl.BlockSpec((1, n), lambda i, j: (0, 0))
    return pl.pallas_call(
        functools.partial(_conv_prompt_body, tm=tm),
        grid=(b, t // tm),
        in_specs=[pl.BlockSpec((1, tm, di), lambda i, j: (i, j, 0)),
                  pl.BlockSpec((1, HALO, di), lambda i, j: (i, jnp.maximum(j * per - 1, 0), 0)),
                  pl.BlockSpec((1, tm, di), lambda i, j: (i, j, 0)),
                  pl.BlockSpec((1, tm, d), lambda i, j: (i, j, 0)),
                  pl.BlockSpec((CONV_W, di), lambda i, j: (0, 0)),
                  vec(di), vec(di), vec(di),
                  pl.BlockSpec((di, d), lambda i, j: (0, 0)),
                  vec(d)],
        out_specs=pl.BlockSpec((1, tm, d), lambda i, j: (i, j, 0)),
        out_shape=jax.ShapeDtypeStruct((b, t, d), F32),
        scratch_shapes=[pltpu.VMEM((HALO + tm, di), F32), pltpu.VMEM((tm, di), F32)],
        compiler_params=_cparams("parallel", "arbitrary"),
        name="conv_prompt",
    )(glu, glu, sz, x, cw, cb.reshape(1, di), lg.reshape(1, di), lb.reshape(1, di), wo_bf, gp.reshape(1, d))


def _conv_sample_body(glu_ref, st_ref, sz_ref, x_ref, cw_ref, cb_ref, lg_ref, lb_ref, wo_ref, gp_ref,
                      out_ref, full_ref, c_ref, *, nb, tq):
    d = c_ref.shape[-1]
    first = HALO - (CONV_W - 1)
    full_ref[:, first:HALO, :] = st_ref[...]
    full_ref[:, HALO:HALO + tq, :] = glu_ref[...].reshape(nb, tq, d)
    for lc in range(d // LANES):
        ln = slice(lc * LANES, (lc + 1) * LANES)
        acc = jnp.zeros((nb, tq, LANES), F32)
        for j in range(CONV_W):
            acc = acc + cw_ref[j:j + 1, ln] * full_ref[:, first + j:first + j + tq, ln]
        c_ref[:, ln] = (acc + cb_ref[:, ln]).reshape(nb * tq, LANES)
    out_ref[...] = _conv_tail(c_ref[...], sz_ref[...], x_ref[...], lg_ref, lb_ref, wo_ref, gp_ref)


def _conv_sample(glu, state, sz, x, cw, cb, lg, lb, wo_bf, gp, nb):
    n_seq = state.shape[0]
    rows, d = x.shape
    di = glu.shape[-1]
    tq = rows // n_seq
    vec = lambda n: pl.BlockSpec((1, n), lambda i: (0, 0))
    return pl.pallas_call(
        functools.partial(_conv_sample_body, nb=nb, tq=tq),
        grid=(n_seq // nb,),
        in_specs=[pl.BlockSpec((nb * tq, di), lambda i: (i, 0)),
                  pl.BlockSpec((nb, CONV_W - 1, di), lambda i: (i, 0, 0)),
                  pl.BlockSpec((nb * tq, di), lambda i: (i, 0)),
                  pl.BlockSpec((nb * tq, d), lambda i: (i, 0)),
                  pl.BlockSpec((CONV_W, di), lambda i: (0, 0)),
                  vec(di), vec(di), vec(di),
                  pl.BlockSpec((di, d), lambda i: (0, 0)),
                  vec(d)],
        out_specs=pl.BlockSpec((nb * tq, d), lambda i: (i, 0)),
        out_shape=jax.ShapeDtypeStruct((rows, d), F32),
        scratch_shapes=[pltpu.VMEM((nb, HALO + tq, di), F32), pltpu.VMEM((nb * tq, di), F32)],
        compiler_params=_cparams("parallel"),
        name="conv_sample",
    )(glu, state, sz, x, cw, cb.reshape(1, di), lg.reshape(1, di), lb.reshape(1, di), wo_bf, gp.reshape(1, d))


def _head_major(col_pair, head):
    return col_pair if head % 2 == 0 else _swap_halves(col_pair)


def _nsa_proj_body(x_ref, gkv_ref, gq_ref, wkv_ref, wq_ref, wz_ref, wg_ref,
                   cmp_ref, slc_ref, win_ref, qa_ref, sz_ref, gate_ref, *aug_refs, seq_len, tm):
    x = x_ref[...]
    xn = x * lax.rsqrt(jnp.mean(x * x, axis=-1, keepdims=True) + EPS)
    hkv = (xn * gkv_ref[...]).astype(BF16)
    hq = (xn * gq_ref[...]).astype(BF16)
    kv = jnp.dot(hkv, wkv_ref[...], preferred_element_type=F32)
    width = 2 * KV_HEADS * HEAD_DIM
    cmp_ref[...] = kv[:, :width]
    slc_ref[...] = kv[:, width:2 * width]
    win_ref[...] = kv[:, 2 * width:]
    lane = _lane((tm, LANES))
    low = lane < HEAD_DIM
    uq = jnp.dot(hq, wq_ref[...], preferred_element_type=F32)
    scale = HEAD_DIM ** -0.5
    for h in range(N_HEADS):
        qh = _head_major(uq[:, (h // 2) * LANES:(h // 2 + 1) * LANES], h)
        qa_ref[:, h * LANES:(h + 1) * LANES] = jnp.where(low, qh * scale, 0.0).astype(BF16)
    z = jnp.dot(hq, wz_ref[...], preferred_element_type=F32)
    sz_ref[...] = z * _sigmoid(z)
    gate_ref[...] = _sigmoid(jnp.dot(hq, wg_ref[...], preferred_element_type=F32))
    if aug_refs:
        ska_ref, sv_ref, wka_ref, wv_ref = aug_refs
        row = pl.program_id(0) * tm + lax.broadcasted_iota(jnp.int32, (tm, LANES), 0)
        blk = (row % seq_len) // L_SLC
        onehot = (lane - HEAD_DIM == blk).astype(F32)
        kw = KV_HEADS * HEAD_DIM
        for h in range(KV_HEADS):
            ks = _head_major(kv[:, width + (h // 2) * LANES:width + (h // 2 + 1) * LANES], h)
            ska_ref[:, h * LANES:(h + 1) * LANES] = jnp.where(low, ks, onehot).astype(BF16)
            kwn = _head_major(kv[:, 2 * width + (h // 2) * LANES:2 * width + (h // 2 + 1) * LANES], h)
            wka_ref[:, h * LANES:(h + 1) * LANES] = jnp.where(low, kwn, 0.0).astype(BF16)
        sv_ref[...] = kv[:, width + kw:2 * width].T.astype(BF16)
        wv_ref[...] = kv[:, 2 * width + kw:].T.astype(BF16)


def _nsa_proj(x, g_kv, g_q, wkv_bf, wq_bf, wz_bf, wg_bf, tm, seq_len=None):
    rows, d = x.shape
    width = 2 * KV_HEADS * HEAD_DIM
    row = lambda n: pl.BlockSpec((tm, n), lambda i: (i, 0))
    full = lambda a: pl.BlockSpec(a.shape, lambda i: (0, 0))
    out_specs = [row(width), row(width), row(width), row(N_HEADS * LANES), row(wz_bf.shape[1]), row(LANES)]
    out_shape = [jax.ShapeDtypeStruct((rows, width), F32)] * 3 + [
        jax.ShapeDtypeStruct((rows, N_HEADS * LANES), BF16),
        jax.ShapeDtypeStruct((rows, wz_bf.shape[1]), F32),
        jax.ShapeDtypeStruct((rows, LANES), F32)]
    if seq_len is not None:
        out_specs += [row(KV_HEADS * LANES), pl.BlockSpec((width // 2, tm), lambda i: (0, i))] * 2
        out_shape += [jax.ShapeDtypeStruct((rows, KV_HEADS * LANES), BF16),
                      jax.ShapeDtypeStruct((width // 2, rows), BF16)] * 2
    gkv = g_kv.reshape(1, d)
    gq = g_q.reshape(1, d)
    return pl.pallas_call(
        functools.partial(_nsa_proj_body, seq_len=seq_len, tm=tm),
        grid=(rows // tm,),
        in_specs=[row(d), full(gkv), full(gq), full(wkv_bf), full(wq_bf), full(wz_bf), full(wg_bf)],
        out_specs=out_specs,
        out_shape=out_shape,
        compiler_params=_cparams("parallel"),
        name="nsa_proj",
    )(x, gkv, gq, wkv_bf, wq_bf, wz_bf, wg_bf)


def _compress_body(pt_ref, *refs, n_pages, n_col):
    del pt_ref
    page_refs = refs[:n_pages * n_col]
    w1_ref, pe_ref, w2_ref, out_ref, a_ref = refs[n_pages * n_col:]
    cpp = page_refs[0].shape[1] // CMP_STRIDE
    n_chunk = n_pages * cpp
    hid = w2_ref.shape[2]
    low = _lane((cpp, LANES)) < HEAD_DIM
    for p in range(n_pages):
        for c in range(n_col):
            pr = page_refs[p * n_col + c]
            for s in range(0, CMP_STRIDE, 2):
                b0 = pr[0, pl.ds(s, cpp, stride=CMP_STRIDE), :]
                b1 = pr[0, pl.ds(s + 1, cpp, stride=CMP_STRIDE), :]
                dst = (slice(p * cpp, (p + 1) * cpp), slice((s // 2) * LANES, (s // 2 + 1) * LANES))
                a_ref[(2 * c,) + dst] = jnp.where(low, b0, _swap_halves(b1))
                a_ref[(2 * c + 1,) + dst] = jnp.where(low, _swap_halves(b0), b1)
    for kv in range(2):
        w1 = w1_ref[kv]
        pe = jnp.dot(pe_ref[kv].astype(BF16), w1, preferred_element_type=F32)
        pe_term = pe[0:1, :hid] + pe[1:2, hid:]
        for hp in range(KV_HEADS // 2):
            pair = jnp.zeros((n_chunk, LANES), F32)
            for par in range(2):
                part = jnp.dot(a_ref[kv * KV_HEADS + 2 * hp + par].astype(BF16), w1, preferred_element_type=F32)
                pre = part[:, :hid] + pltpu.roll(part[:, hid:], n_chunk - 1, axis=0) + pe_term
                mid = pre * _sigmoid(pre)
                pair = pair + jnp.dot(mid.astype(BF16), w2_ref[kv, par], preferred_element_type=F32)
            col = kv * (KV_HEADS // 2) + hp
            out_ref[0, :, col * LANES:(col + 1) * LANES] = pair


def _compress(pages, page_table, w1cat_bf, pe8, w2_bf, n_pages):
    n_seq = page_table.shape[0] // n_pages
    page_rows, width = pages.shape[1:]
    n_chunk = n_pages * page_rows // CMP_STRIDE
    n_col = width // LANES
    page_spec = lambda j, c: pl.BlockSpec((1, page_rows, LANES), lambda i, pt: (pt[i * n_pages + j], 0, c))
    full = lambda a: pl.BlockSpec(a.shape, lambda i, pt: (0,) * a.ndim)
    grid_spec = pltpu.PrefetchScalarGridSpec(
        num_scalar_prefetch=1,
        grid=(n_seq,),
        in_specs=[page_spec(j, c) for j in range(n_pages) for c in range(n_col)]
        + [full(w1cat_bf), full(pe8), full(w2_bf)],
        out_specs=pl.BlockSpec((1, n_chunk, width), lambda i, pt: (i, 0, 0)),
        scratch_shapes=[pltpu.VMEM((2 * KV_HEADS, n_chunk, CMP_STRIDE * HEAD_DIM), F32)],
    )
    return pl.pallas_call(
        functools.partial(_compress_body, n_pages=n_pages, n_col=n_col),
        grid_spec=grid_spec,
        out_shape=jax.ShapeDtypeStruct((n_seq, n_chunk, width), F32),
        compiler_params=_cparams("parallel"),
        name="compress",
    )(page_table, *([pages] * (n_pages * n_col)), w1cat_bf, pe8, w2_bf)


def _bias_tile_body(rb_ref, out_ref, *, row_step, lane_step, offset, hi):
    h = pl.program_id(0)
    shape = out_ref.shape[1:]
    d = row_step * lax.broadcasted_iota(jnp.int32, shape, 0) + lane_step * _lane(shape) + offset
    n = jnp.maximum(d, 0)
    nf = jnp.maximum(n, 1).astype(F32)
    large = MAX_EXACT + (jnp.log(nf / MAX_EXACT) / math.log(MAX_DISTANCE / MAX_EXACT)
                         * (NUM_BUCKETS - MAX_EXACT)).astype(jnp.int32)
    large = jnp.minimum(large, NUM_BUCKETS - 1)
    bucket = jnp.where(n < MAX_EXACT, n, large)
    far = rb_ref[NUM_BUCKETS - 1, h]
    val = jnp.zeros(shape, F32)
    for k in range(NUM_BUCKETS - 1):
        val = jnp.where(bucket == k, rb_ref[k, h] - far, val)
    out_ref[0] = jnp.where((d >= 0) & (d <= hi), val, NEG)


def _bias_tile(rel_bias, rows, width, lane_step, offset, hi=1 << 30, row_step=1):
    return pl.pallas_call(
        functools.partial(_bias_tile_body, row_step=row_step, lane_step=lane_step, offset=offset, hi=hi),
        grid=(N_HEADS,),
        in_specs=[pl.BlockSpec(memory_space=pltpu.SMEM)],
        out_specs=pl.BlockSpec((1, rows, width), lambda h: (h, 0, 0)),
        out_shape=jax.ShapeDtypeStruct((N_HEADS, rows, width), F32),
        compiler_params=_cparams("parallel"),
        name="bias_tile",
    )(rel_bias)


_NT = (((1,), (1,)), ((), ()))


def _softmax_rows(s):
    m = jnp.maximum(jnp.max(s, axis=-1, keepdims=True), M_FLOOR)
    p = jnp.exp(s - m)
    return p, jnp.sum(p, axis=-1, keepdims=True)


def _gate_col(gate, col):
    return jnp.sum(jnp.where(_lane(gate.shape) == col, gate, 0.0), axis=-1, keepdims=True)


def _pair_columns(o_even, o_odd, valid_half):
    lo = o_even if valid_half == 0 else _swap_halves(o_even)
    hi = o_odd if valid_half == 1 else _swap_halves(o_odd)
    return jnp.where(_lane(lo.shape) < HEAD_DIM, lo, hi)


def _select_blocks_t(imp_t, tb, n_sel):
    nb, nt = imp_t.shape
    sub = 8
    j = lax.broadcasted_iota(jnp.int32, (nb, nt), 0)
    valid = j <= tb
    forced = (j == 0) | (j == tb) | (j == tb - 1)
    score = jnp.where(valid, jnp.where(forced, BIG, imp_t), -BIG)
    groups = [score[lo:lo + sub] for lo in range(0, nb, sub)]
    jr = lax.broadcasted_iota(jnp.int32, (sub, nt), 0)
    counts = [jnp.zeros((sub, nt), F32) for _ in groups]
    for i in range(nb):
        row = score[i:i + 1]
        for r, grp in enumerate(groups):
            lo = r * sub
            if lo > i:
                one = jnp.where(row >= grp, 1.0, 0.0)
            elif lo + sub - 1 <= i:
                one = jnp.where(row > grp, 1.0, 0.0)
            else:
                one = jnp.where(jr + lo > i, jnp.where(row >= grp, 1.0, 0.0), jnp.where(row > grp, 1.0, 0.0))
            counts[r] = counts[r] + one
    rank = jnp.concatenate(counts, axis=0)
    return jnp.where((rank < n_sel) & valid, 0.0, NEG)


def _select_blocks(imp, tb, n_sel):
    lane = _lane(imp.shape)
    lane_f = lane.astype(F32)
    j = lane - HEAD_DIM
    valid = (j >= 0) & (j <= tb)
    forced = (j == 0) | (j == tb) | (j == tb - 1)
    score = jnp.where(valid, jnp.where(forced, BIG, imp), -BIG)
    score = jnp.where(j >= 0, score, -jnp.inf)
    sel = jnp.zeros(imp.shape, jnp.bool_)
    for _ in range(n_sel):
        best = jnp.max(score, axis=-1, keepdims=True)
        first = jnp.min(jnp.where(score == best, lane_f, 4.0 * LANES), axis=-1, keepdims=True)
        pick = lane_f == first
        sel = sel | pick
        score = jnp.where(pick, -jnp.inf, score)
    return jnp.where(sel & valid, 0.0, jnp.where(j >= 0, NEG, 0.0))


def _cmp_prompt_body(qa_ref, kc_ref, vc_ref, m_ref, cb_ref, gate_ref, oc_ref, qsel_ref, qflag_ref, *, n_key):
    qt = pl.program_id(1)
    start = pl.multiple_of(8 * qt + 8, 8)
    kwin = kc_ref[0, pl.ds(start, n_key), :]
    vwin = vc_ref[0, pl.ds(start, n_key), :]
    mwin_t = m_ref[pl.ds(start, n_key), :].T
    feat = lax.broadcasted_iota(jnp.int32, (LANES, Q_BLK), 0)
    pos = qt * Q_BLK + lax.broadcasted_iota(jnp.int32, (1, Q_BLK), 1)
    tb = pos // L_SLC
    gate = gate_ref[...]
    n_pair = GROUP // 2
    for kvh in range(KV_HEADS):
        ka = kwin[:, kvh * LANES:(kvh + 1) * LANES].astype(BF16)
        vpt = vwin[:, (kvh // 2) * LANES:(kvh // 2 + 1) * LANES].T.astype(BF16)
        qts = [qa_ref[0, :, h * LANES:(h + 1) * LANES].astype(F32).T for h in range(kvh * GROUP, (kvh + 1) * GROUP)]
        pc_sum = jnp.zeros((n_key, Q_BLK), F32)
        outs = []
        for gp in range(n_pair):
            qa = jnp.concatenate([jnp.where(feat == HEAD_DIM, NEG, qts[2 * gp + e]) for e in range(2)], axis=1)
            qa = qa.astype(BF16)
            qflag_ref[0, kvh, :, 2 * gp * Q_BLK:(2 * gp + 2) * Q_BLK] = qa
            s = jnp.dot(ka, qa, preferred_element_type=F32)
            s = s + cb_ref[kvh, :, 2 * gp * Q_BLK:(2 * gp + 2) * Q_BLK]
            m = jnp.maximum(jnp.max(s, axis=0, keepdims=True), M_FLOOR)
            p = jnp.exp(s - m)
            inv = 1.0 / jnp.maximum(jnp.sum(p, axis=0, keepdims=True), 1e-30)
            pc = p * inv
            pc_sum = pc_sum + pc[:, :Q_BLK] + pc[:, Q_BLK:]
            o = jnp.dot(vpt, p.astype(BF16), preferred_element_type=F32) * inv
            for e in range(2):
                h = kvh * GROUP + 2 * gp + e
                outs.append(o[:, e * Q_BLK:(e + 1) * Q_BLK].T * _gate_col(gate, h))
        for gp in range(n_pair):
            col = kvh * n_pair + gp
            oc_ref[0, :, col * LANES:(col + 1) * LANES] = _pair_columns(outs[2 * gp], outs[2 * gp + 1], kvh % 2)
        imp_t = jnp.dot(mwin_t, pc_sum, preferred_element_type=F32, precision=lax.Precision.HIGHEST)
        selneg_t = _select_blocks_t(imp_t[HEAD_DIM:], tb, N_SEL)
        for g in range(GROUP):
            qsel_ref[0, kvh, :, g * Q_BLK:(g + 1) * Q_BLK] = jnp.concatenate(
                [qts[g][:HEAD_DIM], selneg_t], axis=0).astype(BF16)


def _cmp_prompt(qa, kc_pad, vc_pad, m_pad, cb, gate, batch, seq):
    nq = seq // Q_BLK
    n_key = kc_pad.shape[1] // 2
    d_out = N_HEADS * HEAD_DIM
    return pl.pallas_call(
        functools.partial(_cmp_prompt_body, n_key=n_key),
        grid=(batch, nq),
        in_specs=[pl.BlockSpec((1, Q_BLK, N_HEADS * LANES), lambda b, t: (b, t, 0)),
                  pl.BlockSpec((1,) + kc_pad.shape[1:], lambda b, t: (b, 0, 0)),
                  pl.BlockSpec((1,) + vc_pad.shape[1:], lambda b, t: (b, 0, 0)),
                  pl.BlockSpec(m_pad.shape, lambda b, t: (0, 0)),
                  pl.BlockSpec(cb.shape, lambda b, t: (0, 0, 0)),
                  pl.BlockSpec((Q_BLK, LANES), lambda b, t: (b * nq + t, 0))],
        out_specs=[pl.BlockSpec((1, Q_BLK, d_out), lambda b, t: (b, t, 0))]
        + [pl.BlockSpec((1, KV_HEADS, LANES, GROUP * Q_BLK), lambda b, t: (b * nq + t, 0, 0, 0))] * 2,
        out_shape=[jax.ShapeDtypeStruct((batch, seq, d_out), F32)]
        + [jax.ShapeDtypeStruct((batch * nq, KV_HEADS, LANES, GROUP * Q_BLK), BF16)] * 2,
        compiler_params=_cparams("parallel", "parallel"),
        name="cmp_prompt",
    )(qa.reshape(batch, seq, -1), kc_pad, vc_pad, m_pad, cb, gate)


KEY_TILE = 256
COL_CHAIN = 256


def _store_chain(out_ref, o_t, gate, gate_col0, lane_col):
    pair = jnp.concatenate([o_t[:, :Q_BLK], o_t[:, Q_BLK:]], axis=0).T
    g = jnp.where(_lane(pair.shape) < HEAD_DIM, _gate_col(gate, gate_col0), _gate_col(gate, gate_col0 + 1))
    out_ref[0, :, lane_col * LANES:(lane_col + 1) * LANES] = pair * g


_PAIR_CHAINS = [(e, c) for e in range(2) for c in range(GROUP * Q_BLK // COL_CHAIN)]


def _slc_prompt_body(q_ref, k_ref, v_ref, sb_ref, gate_ref, out_ref, m_ref, l_ref, acc_ref):
    pr = pl.program_id(1)
    qt = pl.program_id(2)
    m_ref[...] = jnp.full(m_ref.shape, M_FLOOR, F32)
    l_ref[...] = jnp.zeros(l_ref.shape, F32)
    acc_ref[...] = jnp.zeros(acc_ref.shape, F32)
    last = (qt * Q_BLK) // KEY_TILE
    cs = lambda c: slice(c * COL_CHAIN, (c + 1) * COL_CHAIN)

    def make_step(with_bias):
        def step(kt, carry):
            base = pl.multiple_of(kt * KEY_TILE, KEY_TILE)
            case = qt % 2 + 2 * (last - kt)
            ks = [k_ref[0, pl.ds(base, KEY_TILE), e * LANES:(e + 1) * LANES] for e in range(2)]
            ss = [jnp.dot(ks[e], q_ref[0, e, :, cs(c)], preferred_element_type=F32) for e, c in _PAIR_CHAINS]
            if with_bias:
                ss = [s + sb_ref[case, e, :, cs(c)] for s, (e, c) in zip(ss, _PAIR_CHAINS)]
            ps, alphas = [], []
            for i, s in enumerate(ss):
                m_old = m_ref[i]
                m_new = jnp.maximum(m_old, jnp.max(s, axis=0, keepdims=True))
                alpha = jnp.exp(m_old - m_new)
                p = jnp.exp(s - m_new)
                l_ref[i] = alpha * l_ref[i] + jnp.sum(p, axis=0, keepdims=True)
                m_ref[i] = m_new
                ps.append(p.astype(BF16))
                alphas.append(alpha)
            for i, (e, c) in enumerate(_PAIR_CHAINS):
                vt = v_ref[e * HEAD_DIM:(e + 1) * HEAD_DIM, pl.ds(base, KEY_TILE)]
                acc_ref[i] = alphas[i] * acc_ref[i] + jnp.dot(vt, ps[i], preferred_element_type=F32)
            return carry
        return step

    n_far = jnp.maximum(last - 1, 0)
    lax.fori_loop(0, n_far, make_step(False), 0)
    lax.fori_loop(n_far, last + 1, make_step(True), 0)
    gate = gate_ref[...]
    for i, (e, c) in enumerate(_PAIR_CHAINS):
        o = acc_ref[i] * (1.0 / jnp.maximum(l_ref[i], 1e-30))
        _store_chain(out_ref, o, gate, N_HEADS + (2 * pr + e) * GROUP + 2 * c, i)


def _slc_prompt(qsel_t, ska, sv_t, sb_t, gate, batch, seq):
    nq = seq // Q_BLK
    cols = GROUP * Q_BLK
    d_out = N_HEADS * HEAD_DIM
    n_chain = len(_PAIR_CHAINS)
    return pl.pallas_call(
        _slc_prompt_body,
        grid=(batch, KV_HEADS // 2, nq),
        in_specs=[pl.BlockSpec((1, 2, LANES, cols), lambda b, p, t: (b * nq + t, p, 0, 0)),
                  pl.BlockSpec((1, seq, 2 * LANES), lambda b, p, t: (b, 0, p)),
                  pl.BlockSpec((2 * HEAD_DIM, seq), lambda b, p, t: (p, b)),
                  pl.BlockSpec((4, 2, KEY_TILE, cols), lambda b, p, t: (0, p, 0, 0)),
                  pl.BlockSpec((Q_BLK, LANES), lambda b, p, t: (b * nq + t, 0))],
        out_specs=pl.BlockSpec((1, Q_BLK, 2 * GROUP * HEAD_DIM), lambda b, p, t: (b, t, p)),
        out_shape=jax.ShapeDtypeStruct((batch, seq, d_out), F32),
        scratch_shapes=[pltpu.VMEM((n_chain, 1, COL_CHAIN), F32), pltpu.VMEM((n_chain, 1, COL_CHAIN), F32),
                        pltpu.VMEM((n_chain, HEAD_DIM, COL_CHAIN), F32)],
        compiler_params=_cparams("parallel", "parallel", "arbitrary"),
        name="slc_prompt",
    )(qsel_t, ska.reshape(batch, seq, -1), sv_t, sb_t, gate)


def _win_prompt_body(q_ref, k_ref, v_ref, wb_ref, gate_ref, out_ref, *, n_key):
    pr = pl.program_id(1)
    qt = pl.program_id(2)
    base = pl.multiple_of(qt * Q_BLK, Q_BLK)
    cs = lambda c: slice(c * COL_CHAIN, (c + 1) * COL_CHAIN)
    ks = [k_ref[0, pl.ds(base, n_key), e * LANES:(e + 1) * LANES] for e in range(2)]
    ss = [jnp.dot(ks[e], q_ref[0, e, :, cs(c)], preferred_element_type=F32) + wb_ref[e, :, cs(c)]
          for e, c in _PAIR_CHAINS]
    ps, invs = [], []
    for s in ss:
        m = jnp.maximum(jnp.max(s, axis=0, keepdims=True), M_FLOOR)
        p = jnp.exp(s - m)
        invs.append(1.0 / jnp.maximum(jnp.sum(p, axis=0, keepdims=True), 1e-30))
        ps.append(p.astype(BF16))
    gate = gate_ref[...]
    for i, (e, c) in enumerate(_PAIR_CHAINS):
        vt = v_ref[e * HEAD_DIM:(e + 1) * HEAD_DIM, pl.ds(base, n_key)]
        o = jnp.dot(vt, ps[i], preferred_element_type=F32) * invs[i]
        _store_chain(out_ref, o, gate, 2 * N_HEADS + (2 * pr + e) * GROUP + 2 * c, i)


def _win_prompt(q_t, wka_pad, wv_t_pad, wb_t, gate, batch, seq):
    nq = seq // Q_BLK
    n_key = WINDOW + Q_BLK
    cols = GROUP * Q_BLK
    d_out = N_HEADS * HEAD_DIM
    padded = wka_pad.shape[1]
    return pl.pallas_call(
        functools.partial(_win_prompt_body, n_key=n_key),
        grid=(batch, KV_HEADS // 2, nq),
        in_specs=[pl.BlockSpec((1, 2, LANES, cols), lambda b, p, t: (b * nq + t, p, 0, 0)),
                  pl.BlockSpec((1, padded, 2 * LANES), lambda b, p, t: (b, 0, p)),
                  pl.BlockSpec((2 * HEAD_DIM, padded), lambda b, p, t: (p, b)),
                  pl.BlockSpec((2, n_key, cols), lambda b, p, t: (p, 0, 0)),
                  pl.BlockSpec((Q_BLK, LANES), lambda b, p, t: (b * nq + t, 0))],
        out_specs=pl.BlockSpec((1, Q_BLK, 2 * GROUP * HEAD_DIM), lambda b, p, t: (b, t, p)),
        out_shape=jax.ShapeDtypeStruct((batch, seq, d_out), F32),
        compiler_params=_cparams("parallel", "parallel", "parallel"),
        name="win_prompt",
    )(q_t, wka_pad, wv_t_pad, wb_t, gate)


def _nsa_out_body(oc_ref, os_ref, ow_ref, sz_ref, x_ref, wo_ref, gp_ref, out_ref):
    d = oc_ref.shape[-1]
    y = oc_ref[...] * sz_ref[:, :d] + os_ref[...] * sz_ref[:, d:2 * d] + ow_ref[...] * sz_ref[:, 2 * d:]
    o = jnp.dot(y.astype(BF16), wo_ref[...], preferred_element_type=F32)
    out_ref[...] = x_ref[...] + _rms(o, gp_ref[...])


def _nsa_out(oc, os_, ow, sz, x, wo_bf, gp, tm):
    rows, d = x.shape
    dq = oc.shape[-1]
    row = lambda n: pl.BlockSpec((tm, n), lambda i: (i, 0))
    return pl.pallas_call(
        _nsa_out_body,
        grid=(rows // tm,),
        in_specs=[row(dq), row(dq), row(dq), row(N_BRANCH * dq), row(d),
                  pl.BlockSpec((dq, d), lambda i: (0, 0)), pl.BlockSpec((1, d), lambda i: (0, 0))],
        out_specs=row(d),
        out_shape=jax.ShapeDtypeStruct((rows, d), F32),
        compiler_params=_cparams("parallel"),
        name="nsa_out",
    )(oc, os_, ow, sz, x, wo_bf, gp.reshape(1, d))


def _sample_attn_body(pt_ref, *refs, n_pages, tq, w_keep):
    del pt_ref
    page_refs = refs[:n_pages]
    (qa_ref, kvc_ref, snew_ref, wst_ref, wnew_ref, m_ref, e_ref, cbs_ref, sbs_ref, wbs_ref, gate_ref,
     oc_ref, os_ref, ow_ref, ksel_ref, kwin_ref) = refs[n_pages:]
    n_past = n_pages * PAGE_SIZE
    n_sk = ksel_ref.shape[0]
    n_wk = kwin_ref.shape[0]
    width = ksel_ref.shape[1]
    kw = width // 2
    for p in range(n_pages):
        ksel_ref[p * PAGE_SIZE:(p + 1) * PAGE_SIZE, :] = page_refs[p][0]
    ksel_ref[n_past:n_past + tq, :] = snew_ref[...]
    ksel_ref[n_past + tq:, :] = jnp.zeros((n_sk - n_past - tq, width), F32)
    kwin_ref[0:w_keep, :] = wst_ref[0]
    kwin_ref[w_keep:w_keep + tq, :] = wnew_ref[...]
    kwin_ref[w_keep + tq:, :] = jnp.zeros((n_wk - w_keep - tq, width), F32)
    pos = n_past + lax.broadcasted_iota(jnp.int32, (tq, 1), 0)
    tb = pos // L_SLC
    gate = gate_ref[...]
    rows = GROUP * tq

    def attend(q, k, v, bias, extra=None):
        s = lax.dot_general(q, k, _NT, preferred_element_type=F32)
        if extra is not None:
            s = s + extra
        s = (s.reshape(GROUP, tq, s.shape[-1]) + bias).reshape(rows, s.shape[-1])
        p, l = _softmax_rows(s)
        inv = 1.0 / jnp.maximum(l, 1e-30)
        return p, inv, jnp.dot(p.astype(BF16), v, preferred_element_type=F32)

    def store(out_ref, o, kvh, branch):
        outs = [o[g * tq:(g + 1) * tq] * _gate_col(gate, branch * N_HEADS + kvh * GROUP + g) for g in range(GROUP)]
        for gp in range(GROUP // 2):
            col = kvh * (GROUP // 2) + gp
            out_ref[:, col * LANES:(col + 1) * LANES] = _pair_columns(outs[2 * gp], outs[2 * gp + 1], kvh % 2)

    def pair_cols(ref, pr):
        kcol = slice(pr * LANES, (pr + 1) * LANES)
        vcol = slice(kw + pr * LANES, kw + (pr + 1) * LANES)
        if len(ref.shape) == 3:
            return ref[0, :, kcol].astype(BF16), ref[0, :, vcol].astype(BF16)
        return ref[:, kcol].astype(BF16), ref[:, vcol].astype(BF16)

    qs, imps = [], []
    for kvh in range(KV_HEADS):
        q = jnp.concatenate([qa_ref[:, h * LANES:(h + 1) * LANES]
                             for h in range(kvh * GROUP, (kvh + 1) * GROUP)], axis=0)
        qs.append((q if kvh % 2 == 0 else _swap_halves(q)).astype(BF16))
    for pr in range(KV_HEADS // 2):
        kc, vc = pair_cols(kvc_ref, pr)
        for kvh in (2 * pr, 2 * pr + 1):
            p, inv, o = attend(qs[kvh], kc, vc, cbs_ref[kvh * GROUP:(kvh + 1) * GROUP])
            store(oc_ref, o * inv, kvh, 0)
            pc_sum = jnp.sum((p * inv).reshape(GROUP, tq, p.shape[-1]), axis=0)
            imps.append(jnp.dot(pc_sum, m_ref[...], preferred_element_type=F32, precision=lax.Precision.HIGHEST))
    selneg = _select_blocks(jnp.concatenate(imps, axis=0), jnp.concatenate([tb] * KV_HEADS, axis=0), N_SEL)
    for pr in range(KV_HEADS // 2):
        ksl, vsl = pair_cols(ksel_ref, pr)
        kwn, vwn = pair_cols(kwin_ref, pr)
        for kvh in (2 * pr, 2 * pr + 1):
            heads = slice(kvh * GROUP, (kvh + 1) * GROUP)
            qm = jnp.concatenate([selneg[kvh * tq:(kvh + 1) * tq]] * GROUP, axis=0).astype(BF16)
            block_mask = jnp.dot(qm, e_ref[...], preferred_element_type=F32)
            _, inv, o = attend(qs[kvh], ksl, vsl, sbs_ref[heads], block_mask)
            store(os_ref, o * inv, kvh, 1)
            _, inv, o = attend(qs[kvh], kwn, vwn, wbs_ref[heads])
            store(ow_ref, o * inv, kvh, 2)


def _sample_attn(qa, kvc, pages, page_table, slc_new, win_state, win_new, m_s, e_s, cbs, sbs, wbs, gate,
                 n_seq, tq, n_pages):
    width = pages.shape[-1]
    n_sk = sbs.shape[-1]
    n_wk = wbs.shape[-1]
    w_keep = win_state.shape[1]
    d_out = N_HEADS * HEAD_DIM
    page_spec = lambda j: pl.BlockSpec((1, PAGE_SIZE, width), lambda i, pt: (pt[i * n_pages + j], 0, 0))
    full = lambda a: pl.BlockSpec(a.shape, lambda i, pt: (0,) * a.ndim)
    row = lambda n: pl.BlockSpec((tq, n), lambda i, pt: (i, 0))
    grid_spec = pltpu.PrefetchScalarGridSpec(
        num_scalar_prefetch=1,
        grid=(n_seq,),
        in_specs=[page_spec(j) for j in range(n_pages)] + [
            row(N_HEADS * LANES),
            pl.BlockSpec((1,) + kvc.shape[1:], lambda i, pt: (i, 0, 0)),
            row(width),
            pl.BlockSpec((1, w_keep, width), lambda i, pt: (i, 0, 0)),
            row(width),
            full(m_s), full(e_s), full(cbs), full(sbs), full(wbs),
            row(LANES)],
        out_specs=[row(d_out)] * 3,
        scratch_shapes=[pltpu.VMEM((n_sk, width), F32), pltpu.VMEM((n_wk, width), F32)],
    )
    return pl.pallas_call(
        functools.partial(_sample_attn_body, n_pages=n_pages, tq=tq, w_keep=w_keep),
        grid_spec=grid_spec,
        out_shape=[jax.ShapeDtypeStruct((n_seq * tq, d_out), F32)] * 3,
        compiler_params=_cparams("parallel"),
        name="sample_attn",
    )(page_table, *([pages] * n_pages), qa, kvc, slc_new, win_state, win_new, m_s, e_s, cbs, sbs, wbs, gate)


def _overlap_matrix(n_rows, row0, n_cmp, n_blk):
    import numpy as np
    m = np.zeros((n_rows, LANES), np.float32)
    cs = np.arange(n_cmp)[:, None] * CMP_STRIDE
    js = np.arange(n_blk)[None, :] * L_SLC
    m[row0:row0 + n_cmp, HEAD_DIM:HEAD_DIM + n_blk] = (cs <= js + L_SLC - 1) & (cs + L_CMP - 1 >= js)
    return jnp.asarray(m)


def kernel(x_prompt, x_sample, state_conv, cache_cmp, cache_slc, state_win, page_table, rel_bias, a_norm_pre, a_w_in, a_conv_w, a_conv_b, a_ln_g, a_ln_b, a_w_out, a_norm_post, kv_norm, w_kv, cmp_pe, cmp_w1, cmp_w2, b_norm_pre, b_w_in, b_w_out, b_norm_post):
    import numpy as np
    bp, tp, d = x_prompt.shape
    bd, tq, _ = x_sample.shape
    n_pages = page_table.shape[1]
    past = n_pages * PAGE_SIZE
    w_keep = state_win.shape[1]
    width = 2 * KV_HEADS * HEAD_DIM
    kw = KV_HEADS * HEAD_DIM
    dq = N_HEADS * HEAD_DIM
    assert b_w_in.shape[0] == 1 and tp % (2 * KEY_TILE) == 0 and tp // L_SLC <= HEAD_DIM
    tm = 256
    tm_s = min(tm, bd * tq)

    xp = x_prompt
    xs = x_sample.reshape(bd * tq, d)
    conv_p, conv_s = [], []
    for l in range(a_w_in.shape[0]):
        w_in = a_w_in[l].astype(BF16)
        w_out = a_w_out[l].astype(BF16)
        di = a_w_out.shape[1]
        tail = (a_conv_w[l], a_conv_b[l], a_ln_g[l], a_ln_b[l], w_out, a_norm_post[l])
        glu, sz = _glu_proj(xp.reshape(bp * tp, d), a_norm_pre[l], w_in, tm)
        glu = glu.reshape(bp, tp, di)
        xp = _conv_prompt(glu, sz.reshape(bp, tp, di), xp, *tail, tm)
        conv_p.append(glu[:, -(CONV_W - 1):])
        glu, sz = _glu_proj(xs, a_norm_pre[l], w_in, tm_s)
        xs = _conv_sample(glu, state_conv[l], sz, xs, *tail, 16)
        conv_s.append(jnp.concatenate([state_conv[l], glu.reshape(bd, tq, di)], axis=1)[:, -(CONV_W - 1):])

    bw = b_w_in[0]
    n_gate = N_BRANCH * N_HEADS
    wq = bw[:, :dq].astype(BF16)
    wz = bw[:, dq:dq * (1 + N_BRANCH)].astype(BF16)
    wg = jnp.pad(bw[:, dq * (1 + N_BRANCH):], ((0, 0), (0, LANES - n_gate))).astype(BF16)
    wkv = w_kv.astype(BF16)
    xp2 = xp.reshape(bp * tp, d)
    (cmp_p, slc_p, win_p, qa_p, sz_p, gate_p, ska, sv, wka, wv) = _nsa_proj(
        xp2, kv_norm, b_norm_pre[0], wkv, wq, wz, wg, tm, seq_len=tp)
    cmp_s, slc_s, win_s, qa_s, sz_s, gate_s = _nsa_proj(xs, kv_norm, b_norm_pre[0], wkv, wq, wz, wg, tm_s)
    qa_s = qa_s.astype(F32)

    w1cat = jnp.concatenate([cmp_w1[:, :CMP_STRIDE * HEAD_DIM], cmp_w1[:, CMP_STRIDE * HEAD_DIM:]], axis=2).astype(BF16)
    pe8 = jnp.pad(cmp_pe.reshape(2, L_CMP // CMP_STRIDE, CMP_STRIDE * HEAD_DIM), ((0, 0), (0, 6), (0, 0)))
    zero = jnp.zeros_like(cmp_w2)
    w2h = jnp.stack([jnp.concatenate([cmp_w2, zero], axis=2), jnp.concatenate([zero, cmp_w2], axis=2)], axis=1).astype(BF16)
    prompt_page = min(1024, tp)
    pp = tp // prompt_page
    kvc_p = _compress(cmp_p.reshape(bp * pp, prompt_page, width), jnp.arange(bp * pp, dtype=jnp.int32), w1cat, pe8, w2h, pp)
    pt_flat = page_table.reshape(-1).astype(jnp.int32)
    kvc_s = _compress(cache_cmp.reshape(-1, PAGE_SIZE, width), pt_flat, w1cat, pe8, w2h, n_pages)

    n_chunk_p = kvc_p.shape[1]
    nc_p = n_chunk_p - L_CMP // CMP_STRIDE + 1
    real = (jnp.arange(n_chunk_p) < nc_p)[None, :, None]
    flag = jnp.zeros((LANES - HEAD_DIM,), F32).at[0].set(1.0)
    kc4 = jnp.where(real, kvc_p[:, :, :kw], 0.0).reshape(bp, n_chunk_p, KV_HEADS, HEAD_DIM)
    aug = jnp.where(real[..., None], 0.0, flag) * jnp.ones((bp, n_chunk_p, KV_HEADS, 1), F32)
    kc_real = jnp.concatenate([kc4, aug], axis=-1).reshape(bp, n_chunk_p, KV_HEADS * LANES)
    pad_row = jnp.concatenate([jnp.zeros((HEAD_DIM,), F32), flag])
    kc_front = jnp.broadcast_to(jnp.tile(pad_row, KV_HEADS), (bp, n_chunk_p, KV_HEADS * LANES))
    kc_pad = jnp.concatenate([kc_front, kc_real], axis=1)
    vc_pad = jnp.concatenate([jnp.zeros((bp, n_chunk_p, kw), F32), jnp.where(real, kvc_p[:, :, kw:], 0.0)], axis=1)
    m_pad = _overlap_matrix(2 * n_chunk_p, n_chunk_p, nc_p, tp // L_SLC)
    cb = _bias_tile(rel_bias, n_chunk_p, Q_BLK, 1, CMP_STRIDE * (n_chunk_p - Q_BLK // CMP_STRIDE) - (L_CMP - 1),
                    row_step=-CMP_STRIDE)
    cb = cb.reshape(KV_HEADS, GROUP, n_chunk_p, Q_BLK).transpose(0, 2, 1, 3).reshape(KV_HEADS, n_chunk_p, GROUP * Q_BLK)
    oc_p, qsel, qflag = _cmp_prompt(qa_p, kc_pad, vc_pad, m_pad, cb, gate_p, bp, tp)
    sb = jnp.stack([_bias_tile(rel_bias, KEY_TILE, Q_BLK, 1, off, row_step=-1)
                    for off in (0, Q_BLK, 2 * Q_BLK, 3 * Q_BLK)])
    sb = sb.reshape(4, KV_HEADS, GROUP, KEY_TILE, Q_BLK).transpose(0, 1, 3, 2, 4).reshape(4, KV_HEADS, KEY_TILE, GROUP * Q_BLK)
    os_p = _slc_prompt(qsel, ska, sv, sb, gate_p, bp, tp)
    n_wkey = WINDOW + Q_BLK
    wb = _bias_tile(rel_bias, n_wkey, Q_BLK, 1, WINDOW, WINDOW, row_step=-1)
    wb = wb.reshape(KV_HEADS, GROUP, n_wkey, Q_BLK).transpose(0, 2, 1, 3).reshape(KV_HEADS, n_wkey, GROUP * Q_BLK)
    wka_front = jnp.broadcast_to(jnp.tile(pad_row, KV_HEADS).astype(BF16), (bp, WINDOW, KV_HEADS * LANES))
    wka_pad = jnp.concatenate([wka_front, wka.reshape(bp, tp, -1)], axis=1)
    wv_pad = jnp.concatenate([jnp.zeros((kw, bp, WINDOW), BF16), wv.reshape(kw, bp, tp)], axis=2)
    ow_p = _win_prompt(qflag, wka_pad, wv_pad.reshape(kw, bp * (WINDOW + tp)), wb, gate_p, bp, tp)
    wo = b_w_out[0].astype(BF16)
    y_p = _nsa_out(oc_p.reshape(bp * tp, dq), os_p.reshape(bp * tp, dq), ow_p.reshape(bp * tp, dq), sz_p, xp2,
                   wo, b_norm_post[0], tm)

    n_chunk_s = kvc_s.shape[1]
    nc_s = n_chunk_s - L_CMP // CMP_STRIDE + 1
    nb_s = -(-(past + tq) // L_SLC)
    n_sk = -(-(nb_s * L_SLC) // LANES) * LANES
    n_wk = -(-(w_keep + tq) // LANES) * LANES
    m_s = _overlap_matrix(n_chunk_s, 0, nc_s, nb_s)
    e_np = np.zeros((LANES, n_sk), np.float32)
    e_np[HEAD_DIM + np.arange(n_sk) // L_SLC, np.arange(n_sk)] = 1.0
    e_s = jnp.asarray(e_np, BF16)
    cbs = _bias_tile(rel_bias, tq, n_chunk_s, -CMP_STRIDE, past - (L_CMP - 1))
    sbs = _bias_tile(rel_bias, tq, n_sk, -1, past)
    wbs = _bias_tile(rel_bias, tq, n_wk, -1, w_keep, WINDOW)
    oc_s, os_s, ow_s = _sample_attn(qa_s, kvc_s, cache_slc.reshape(-1, PAGE_SIZE, width), pt_flat, slc_s,
                                    state_win.reshape(bd, w_keep, width), win_s, m_s, e_s, cbs, sbs, wbs, gate_s,
                                    bd, tq, n_pages)
    y_s = _nsa_out(oc_s, os_s, ow_s, sz_s, xs, wo, b_norm_post[0], tm_s)

    kv5 = lambda a, b, t: a.reshape(b, t, 2, KV_HEADS, HEAD_DIM)
    win_p5 = kv5(win_p, bp, tp)
    win_s5 = kv5(win_s, bd, tq)
    win_all = jnp.concatenate([state_win, win_s5], axis=1)
    return (y_p.reshape(bp, tp, d), y_s.reshape(bd, tq, d), jnp.stack(conv_p), jnp.stack(conv_s),
            kv5(cmp_p, bp, tp), kv5(cmp_s, bd, tq), kv5(slc_p, bp, tp), kv5(slc_s, bd, tq),
            win_p5[:, -min(WINDOW, tp):], win_all[:, -min(WINDOW, win_all.shape[1]):])
```

```python
import functools
import math

import jax
import jax.numpy as jnp
from jax import lax
from jax.experimental import pallas as pl
from jax.experimental.pallas import tpu as pltpu

F32 = jnp.float32
BF16 = jnp.bfloat16

EPS = 1e-6
NEG = -1e30
BIG = 1e9
M_FLOOR = -1e20

HEAD_DIM = 64
KV_HEADS = 4
N_HEADS = 16
GROUP = N_HEADS // KV_HEADS
N_BRANCH = 3
CONV_W = 31
L_CMP = 32
CMP_STRIDE = 16
L_SLC = 64
N_SEL = 16
WINDOW = 512
Q_BLK = 128
NUM_BUCKETS = 32
MAX_DISTANCE = 128
MAX_EXACT = NUM_BUCKETS // 2
PAGE_SIZE = 128
LANES = 128
HALO = 32

VMEM_LIMIT = 56 * 1024 * 1024


def _cparams(*sem):
    return pltpu.CompilerParams(dimension_semantics=sem, vmem_limit_bytes=VMEM_LIMIT)


def _sigmoid(x):
    return 1.0 / (1.0 + jnp.exp(-x))


def _rms(x, g):
    return x * lax.rsqrt(jnp.mean(x * x, axis=-1, keepdims=True) + EPS) * g


def _lane(shape):
    return lax.broadcasted_iota(jnp.int32, shape, len(shape) - 1)


def _swap_halves(x):
    return pltpu.roll(x, HEAD_DIM, axis=x.ndim - 1)


def _glu_proj_body(x_ref, g_ref, w_ref, glu_ref, sz_ref):
    di = glu_ref.shape[-1]
    h = _rms(x_ref[...], g_ref[...])
    u = jnp.dot(h.astype(BF16), w_ref[...], preferred_element_type=F32)
    z = u[:, 2 * di:]
    glu_ref[...] = u[:, :di] * _sigmoid(u[:, di:2 * di])
    sz_ref[...] = z * _sigmoid(z)


def _glu_proj(x, g, w_bf, tm):
    rows, d = x.shape
    di = w_bf.shape[1] // 3
    return pl.pallas_call(
        _glu_proj_body,
        grid=(rows // tm,),
        in_specs=[pl.BlockSpec((tm, d), lambda i: (i, 0)),
                  pl.BlockSpec((1, d), lambda i: (0, 0)),
                  pl.BlockSpec((d, 3 * di), lambda i: (0, 0))],
        out_specs=[pl.BlockSpec((tm, di), lambda i: (i, 0)),
                   pl.BlockSpec((tm, di), lambda i: (i, 0))],
        out_shape=[jax.ShapeDtypeStruct((rows, di), F32)] * 2,
        compiler_params=_cparams("parallel"),
        name="glu_proj",
    )(x, g.reshape(1, d), w_bf)


def _conv_tail(c, sz, x, lg_ref, lb_ref, wo_ref, gp_ref):
    mu = jnp.mean(c, axis=-1, keepdims=True)
    cc = c - mu
    var = jnp.mean(cc * cc, axis=-1, keepdims=True)
    y = cc * lax.rsqrt(var + EPS) * lg_ref[...] + lb_ref[...]
    y = y * _sigmoid(y) * sz
    o = jnp.dot(y.astype(BF16), wo_ref[...], preferred_element_type=F32)
    return x + _rms(o, gp_ref[...])


def _conv_prompt_body(glu_ref, prev_ref, sz_ref, x_ref, cw_ref, cb_ref, lg_ref, lb_ref, wo_ref, gp_ref,
                      out_ref, full_ref, c_ref, *, tm):
    t = pl.program_id(1)
    full_ref[0:HALO, :] = jnp.where(t > 0, prev_ref[0], 0.0)
    full_ref[HALO:HALO + tm, :] = glu_ref[0]
    d = c_ref.shape[-1]
    first = HALO - (CONV_W - 1)
    n_phase = 8
    n_grp = -(-CONV_W // n_phase)
    for lc in range(d // LANES):
        ln = slice(lc * LANES, (lc + 1) * LANES)
        acc = jnp.zeros((tm, LANES), F32)
        for b in range(n_phase):
            sb = full_ref[first + b:first + b + tm + 8 * (n_grp - 1), ln]
            for a in range(n_grp):
                j = 8 * a + b
                if j < CONV_W:
                    acc = acc + cw_ref[j:j + 1, ln] * sb[8 * a:8 * a + tm]
        c_ref[:, ln] = acc + cb_ref[:, ln]
    out_ref[0] = _conv_tail(c_ref[...], sz_ref[0], x_ref[0], lg_ref, lb_ref, wo_ref, gp_ref)


def _conv_prompt(glu, sz, x, cw, cb, lg, lb, wo_bf, gp, tm):
    b, t, d = x.shape
    di = glu.shape[-1]
    per = tm // HALO
    vec = lambda n: pl.BlockSpec((1, n), lambda i, j: (0, 0))
    return pl.pallas_call(
        functools.partial(_conv_prompt_body, tm=tm),
        grid=(b, t // tm),
        in_specs=[pl.BlockSpec((1, tm, di), lambda i, j: (i, j, 0)),
                  pl.BlockSpec((1, HALO, di), lambda i, j: (i, jnp.maximum(j * per - 1, 0), 0)),
                  pl.BlockSpec((1, tm, di), lambda i, j: (i, j, 0)),
                  pl.BlockSpec((1, tm, d), lambda i, j: (i, j, 0)),
                  pl.BlockSpec((CONV_W, di), lambda i, j: (0, 0)),
                  vec(di), vec(di), vec(di),
                  pl.BlockSpec((di, d), lambda i, j: (0, 0)),
                  vec(d)],
        out_specs=pl.BlockSpec((1, tm, d), lambda i, j: (i, j, 0)),
        out_shape=jax.ShapeDtypeStruct((b, t, d), F32),
        scratch_shapes=[pltpu.VMEM((HALO + tm, di), F32), pltpu.VMEM((tm, di), F32)],
        compiler_params=_cparams("parallel", "arbitrary"),
        name="conv_prompt",
    )(glu, glu, sz, x, cw, cb.reshape(1, di), lg.reshape(1, di), lb.reshape(1, di), wo_bf, gp.reshape(1, d))


def _conv_sample_body(glu_ref, st_ref, sz_ref, x_ref, cw_ref, cb_ref, lg_ref, lb_ref, wo_ref, gp_ref,
                      out_ref, full_ref, c_ref, *, nb, tq):
    d = c_ref.shape[-1]
    first = HALO - (CONV_W - 1)
    full_ref[:, first:HALO, :] = st_ref[...]
    full_ref[:, HALO:HALO + tq, :] = glu_ref[...].reshape(nb, tq, d)
    for lc in range(d // LANES):
        ln = slice(lc * LANES, (lc + 1) * LANES)
        acc = jnp.zeros((nb, tq, LANES), F32)
        for j in range(CONV_W):
            acc = acc + cw_ref[j:j + 1, ln] * full_ref[:, first + j:first + j + tq, ln]
        c_ref[:, ln] = (acc + cb_ref[:, ln]).reshape(nb * tq, LANES)
    out_ref[...] = _conv_tail(c_ref[...], sz_ref[...], x_ref[...], lg_ref, lb_ref, wo_ref, gp_ref)


def _conv_sample(glu, state, sz, x, cw, cb, lg, lb, wo_bf, gp, nb):
    n_seq = state.shape[0]
    rows, d = x.shape
    di = glu.shape[-1]
    tq = rows // n_seq
    vec = lambda n: pl.BlockSpec((1, n), lambda i: (0, 0))
    return pl.pallas_call(
        functools.partial(_conv_sample_body, nb=nb, tq=tq),
        grid=(n_seq // nb,),
        in_specs=[pl.BlockSpec((nb * tq, di), lambda i: (i, 0)),
                  pl.BlockSpec((nb, CONV_W - 1, di), lambda i: (i, 0, 0)),
                  pl.BlockSpec((nb * tq, di), lambda i: (i, 0)),
                  pl.BlockSpec((nb * tq, d), lambda i: (i, 0)),
                  pl.BlockSpec((CONV_W, di), lambda i: (0, 0)),
                  vec(di), vec(di), vec(di),
                  pl.BlockSpec((di, d), lambda i: (0, 0)),
                  vec(d)],
        out_specs=pl.BlockSpec((nb * tq, d), lambda i: (i, 0)),
        out_shape=jax.ShapeDtypeStruct((rows, d), F32),
        scratch_shapes=[pltpu.VMEM((nb, HALO + tq, di), F32), pltpu.VMEM((nb * tq, di), F32)],
        compiler_params=_cparams("parallel"),
        name="conv_sample",
    )(glu, state, sz, x, cw, cb.reshape(1, di), lg.reshape(1, di), lb.reshape(1, di), wo_bf, gp.reshape(1, d))


def _head_major(col_pair, head):
    return col_pair if head % 2 == 0 else _swap_halves(col_pair)


def _nsa_proj_body(x_ref, gkv_ref, gq_ref, wkv_ref, wq_ref, wz_ref, wg_ref,
                   cmp_ref, slc_ref, win_ref, qa_ref, sz_ref, gate_ref, *aug_refs, seq_len, tm):
    x = x_ref[...]
    xn = x * lax.rsqrt(jnp.mean(x * x, axis=-1, keepdims=True) + EPS)
    hkv = (xn * gkv_ref[...]).astype(BF16)
    hq = (xn * gq_ref[...]).astype(BF16)
    kv = jnp.dot(hkv, wkv_ref[...], preferred_element_type=F32)
    width = 2 * KV_HEADS * HEAD_DIM
    cmp_ref[...] = kv[:, :width]
    slc_ref[...] = kv[:, width:2 * width]
    win_ref[...] = kv[:, 2 * width:]
    lane = _lane((tm, LANES))
    low = lane < HEAD_DIM
    uq = jnp.dot(hq, wq_ref[...], preferred_element_type=F32)
    scale = HEAD_DIM ** -0.5
    for h in range(N_HEADS):
        qh = _head_major(uq[:, (h // 2) * LANES:(h // 2 + 1) * LANES], h)
        qa_ref[:, h * LANES:(h + 1) * LANES] = jnp.where(low, qh * scale, 0.0).astype(BF16)
    z = jnp.dot(hq, wz_ref[...], preferred_element_type=F32)
    sz_ref[...] = z * _sigmoid(z)
    gate_ref[...] = _sigmoid(jnp.dot(hq, wg_ref[...], preferred_element_type=F32))
    if aug_refs:
        ska_ref, sv_ref, wka_ref, wv_ref = aug_refs
        row = pl.program_id(0) * tm + lax.broadcasted_iota(jnp.int32, (tm, LANES), 0)
        blk = (row % seq_len) // L_SLC
        onehot = (lane - HEAD_DIM == blk).astype(F32)
        kw = KV_HEADS * HEAD_DIM
        for h in range(KV_HEADS):
            ks = _head_major(kv[:, width + (h // 2) * LANES:width + (h // 2 + 1) * LANES], h)
            ska_ref[:, h * LANES:(h + 1) * LANES] = jnp.where(low, ks, onehot).astype(BF16)
            kwn = _head_major(kv[:, 2 * width + (h // 2) * LANES:2 * width + (h // 2 + 1) * LANES], h)
            wka_ref[:, h * LANES:(h + 1) * LANES] = jnp.where(low, kwn, 0.0).astype(BF16)
        sv_ref[...] = kv[:, width + kw:2 * width].T.astype(BF16)
        wv_ref[...] = kv[:, 2 * width + kw:].T.astype(BF16)


def _nsa_proj(x, g_kv, g_q, wkv_bf, wq_bf, wz_bf, wg_bf, tm, seq_len=None):
    rows, d = x.shape
    width = 2 * KV_HEADS * HEAD_DIM
    row = lambda n: pl.BlockSpec((tm, n), lambda i: (i, 0))
    full = lambda a: pl.BlockSpec(a.shape, lambda i: (0, 0))
    out_specs = [row(width), row(width), row(width), row(N_HEADS * LANES), row(wz_bf.shape[1]), row(LANES)]
    out_shape = [jax.ShapeDtypeStruct((rows, width), F32)] * 3 + [
        jax.ShapeDtypeStruct((rows, N_HEADS * LANES), BF16),
        jax.ShapeDtypeStruct((rows, wz_bf.shape[1]), F32),
        jax.ShapeDtypeStruct((rows, LANES), F32)]
    if seq_len is not None:
        out_specs += [row(KV_HEADS * LANES), pl.BlockSpec((width // 2, tm), lambda i: (0, i))] * 2
        out_shape += [jax.ShapeDtypeStruct((rows, KV_HEADS * LANES), BF16),
                      jax.ShapeDtypeStruct((width // 2, rows), BF16)] * 2
    gkv = g_kv.reshape(1, d)
    gq = g_q.reshape(1, d)
    return pl.pallas_call(
        functools.partial(_nsa_proj_body, seq_len=seq_len, tm=tm),
        grid=(rows // tm,),
        in_specs=[row(d), full(gkv), full(gq), full(wkv_bf), full(wq_bf), full(wz_bf), full(wg_bf)],
        out_specs=out_specs,
        out_shape=out_shape,
        compiler_params=_cparams("parallel"),
        name="nsa_proj",
    )(x, gkv, gq, wkv_bf, wq_bf, wz_bf, wg_bf)


def _compress_body(pt_ref, *refs, n_pages, n_col, transposed):
    del pt_ref
    page_refs = refs[:n_pages * n_col]
    w1_ref, pe_ref, w2_ref, out_ref, a_ref = refs[n_pages * n_col:n_pages * n_col + 5]
    page_rows = page_refs[0].shape[2 if transposed else 1]
    cpp = page_rows // CMP_STRIDE
    n_chunk = n_pages * cpp
    hid = w2_ref.shape[2]
    low = _lane((cpp, LANES)) < HEAD_DIM
    for p in range(n_pages):
        for c in range(n_col):
            pr = page_refs[p * n_col + c]
            if transposed:
                t_ref = refs[-1]
                t_ref[0] = pr[0].T
                pr = t_ref
            for s in range(0, CMP_STRIDE, 2):
                b0 = pr[0, pl.ds(s, cpp, stride=CMP_STRIDE), :]
                b1 = pr[0, pl.ds(s + 1, cpp, stride=CMP_STRIDE), :]
                dst = (slice(p * cpp, (p + 1) * cpp), slice((s // 2) * LANES, (s // 2 + 1) * LANES))
                a_ref[(2 * c,) + dst] = jnp.where(low, b0, _swap_halves(b1))
                a_ref[(2 * c + 1,) + dst] = jnp.where(low, _swap_halves(b0), b1)
    for kv in range(2):
        w1 = w1_ref[kv]
        pe = jnp.dot(pe_ref[kv].astype(BF16), w1, preferred_element_type=F32)
        pe_term = pe[0:1, :hid] + pe[1:2, hid:]
        for hp in range(KV_HEADS // 2):
            pair = jnp.zeros((n_chunk, LANES), F32)
            for par in range(2):
                part = jnp.dot(a_ref[kv * KV_HEADS + 2 * hp + par].astype(BF16), w1, preferred_element_type=F32)
                pre = part[:, :hid] + pltpu.roll(part[:, hid:], n_chunk - 1, axis=0) + pe_term
                mid = pre * _sigmoid(pre)
                pair = pair + jnp.dot(mid.astype(BF16), w2_ref[kv, par], preferred_element_type=F32)
            col = kv * (KV_HEADS // 2) + hp
            if transposed:
                out_ref[0, col * LANES:(col + 1) * LANES, :] = pair.T
            else:
                out_ref[0, :, col * LANES:(col + 1) * LANES] = pair


def _compress(pages, page_table, w1cat_bf, pe8, w2_bf, n_pages, transposed=False):
    n_seq = page_table.shape[0] // n_pages
    page_rows, width = pages.shape[1:][::-1] if transposed else pages.shape[1:]
    n_chunk = n_pages * page_rows // CMP_STRIDE
    n_col = width // LANES
    if transposed:
        page_spec = lambda j, c: pl.BlockSpec((1, LANES, page_rows), lambda i, pt: (pt[i * n_pages + j], c, 0))
    else:
        page_spec = lambda j, c: pl.BlockSpec((1, page_rows, LANES), lambda i, pt: (pt[i * n_pages + j], 0, c))
    full = lambda a: pl.BlockSpec(a.shape, lambda i, pt: (0,) * a.ndim)
    out_dims = (width, n_chunk) if transposed else (n_chunk, width)
    scratch = [pltpu.VMEM((2 * KV_HEADS, n_chunk, CMP_STRIDE * HEAD_DIM), F32)]
    if transposed:
        scratch.append(pltpu.VMEM((1, page_rows, LANES), F32))
    grid_spec = pltpu.PrefetchScalarGridSpec(
        num_scalar_prefetch=1,
        grid=(n_seq,),
        in_specs=[page_spec(j, c) for j in range(n_pages) for c in range(n_col)]
        + [full(w1cat_bf), full(pe8), full(w2_bf)],
        out_specs=pl.BlockSpec((1,) + out_dims, lambda i, pt: (i, 0, 0)),
        scratch_shapes=scratch,
    )
    return pl.pallas_call(
        functools.partial(_compress_body, n_pages=n_pages, n_col=n_col, transposed=transposed),
        grid_spec=grid_spec,
        out_shape=jax.ShapeDtypeStruct((n_seq,) + out_dims, F32),
        compiler_params=_cparams("parallel"),
        name="compress",
    )(page_table, *([pages] * (n_pages * n_col)), w1cat_bf, pe8, w2_bf)


def _bias_tile_body(rb_ref, out_ref, *, row_step, lane_step, offset, hi):
    h = pl.program_id(0)
    shape = out_ref.shape[1:]
    d = row_step * lax.broadcasted_iota(jnp.int32, shape, 0) + lane_step * _lane(shape) + offset
    n = jnp.maximum(d, 0)
    nf = jnp.maximum(n, 1).astype(F32)
    large = MAX_EXACT + (jnp.log(nf / MAX_EXACT) / math.log(MAX_DISTANCE / MAX_EXACT)
                         * (NUM_BUCKETS - MAX_EXACT)).astype(jnp.int32)
    large = jnp.minimum(large, NUM_BUCKETS - 1)
    bucket = jnp.where(n < MAX_EXACT, n, large)
    far = rb_ref[NUM_BUCKETS - 1, h]
    val = jnp.zeros(shape, F32)
    for k in range(NUM_BUCKETS - 1):
        val = jnp.where(bucket == k, rb_ref[k, h] - far, val)
    out_ref[0] = jnp.where((d >= 0) & (d <= hi), val, NEG)


def _bias_tile(rel_bias, rows, width, lane_step, offset, hi=1 << 30, row_step=1):
    return pl.pallas_call(
        functools.partial(_bias_tile_body, row_step=row_step, lane_step=lane_step, offset=offset, hi=hi),
        grid=(N_HEADS,),
        in_specs=[pl.BlockSpec(memory_space=pltpu.SMEM)],
        out_specs=pl.BlockSpec((1, rows, width), lambda h: (h, 0, 0)),
        out_shape=jax.ShapeDtypeStruct((N_HEADS, rows, width), F32),
        compiler_params=_cparams("parallel"),
        name="bias_tile",
    )(rel_bias)


_NT = (((1,), (1,)), ((), ()))


def _softmax_rows(s):
    m = jnp.maximum(jnp.max(s, axis=-1, keepdims=True), M_FLOOR)
    p = jnp.exp(s - m)
    return p, jnp.sum(p, axis=-1, keepdims=True)


def _gate_col(gate, col):
    return jnp.sum(jnp.where(_lane(gate.shape) == col, gate, 0.0), axis=-1, keepdims=True)


def _pair_columns(o_even, o_odd, valid_half):
    lo = o_even if valid_half == 0 else _swap_halves(o_even)
    hi = o_odd if valid_half == 1 else _swap_halves(o_odd)
    return jnp.where(_lane(lo.shape) < HEAD_DIM, lo, hi)


def _select_blocks_t(imp_t, tb, n_sel):
    nb, nt = imp_t.shape
    sub = 8
    j = lax.broadcasted_iota(jnp.int32, (nb, nt), 0)
    valid = j <= tb
    forced = (j == 0) | (j == tb) | (j == tb - 1)
    score = jnp.where(valid, jnp.where(forced, BIG, imp_t), -BIG)
    groups = [score[lo:lo + sub] for lo in range(0, nb, sub)]
    jr = lax.broadcasted_iota(jnp.int32, (sub, nt), 0)
    counts = [jnp.zeros((sub, nt), F32) for _ in groups]
    for i in range(nb):
        row = score[i:i + 1]
        for r, grp in enumerate(groups):
            lo = r * sub
            if lo > i:
                one = jnp.where(row >= grp, 1.0, 0.0)
            elif lo + sub - 1 <= i:
                one = jnp.where(row > grp, 1.0, 0.0)
            else:
                one = jnp.where(jr + lo > i, jnp.where(row >= grp, 1.0, 0.0), jnp.where(row > grp, 1.0, 0.0))
            counts[r] = counts[r] + one
    rank = jnp.concatenate(counts, axis=0)
    return jnp.where((rank < n_sel) & valid, 0.0, NEG)


def _select_blocks(imp, tb, n_sel):
    lane = _lane(imp.shape)
    lane_f = lane.astype(F32)
    j = lane - HEAD_DIM
    valid = (j >= 0) & (j <= tb)
    forced = (j == 0) | (j == tb) | (j == tb - 1)
    score = jnp.where(valid, jnp.where(forced, BIG, imp), -BIG)
    score = jnp.where(j >= 0, score, -jnp.inf)
    sel = jnp.zeros(imp.shape, jnp.bool_)
    for _ in range(n_sel):
        best = jnp.max(score, axis=-1, keepdims=True)
        first = jnp.min(jnp.where(score == best, lane_f, 4.0 * LANES), axis=-1, keepdims=True)
        pick = lane_f == first
        sel = sel | pick
        score = jnp.where(pick, -jnp.inf, score)
    return jnp.where(sel & valid, 0.0, jnp.where(j >= 0, NEG, 0.0))


def _cmp_prompt_body(qa_ref, kc_ref, vc_ref, m_ref, cb_ref, gate_ref, oc_ref, qsel_ref, qflag_ref, *, n_key):
    qt = pl.program_id(1)
    start = pl.multiple_of(8 * qt + 8, 8)
    kwin = kc_ref[0, pl.ds(start, n_key), :]
    vwin = vc_ref[0, pl.ds(start, n_key), :]
    mwin_t = m_ref[pl.ds(start, n_key), :].T
    feat = lax.broadcasted_iota(jnp.int32, (LANES, Q_BLK), 0)
    pos = qt * Q_BLK + lax.broadcasted_iota(jnp.int32, (1, Q_BLK), 1)
    tb = pos // L_SLC
    gate = gate_ref[...]
    n_pair = GROUP // 2
    for kvh in range(KV_HEADS):
        ka = kwin[:, kvh * LANES:(kvh + 1) * LANES].astype(BF16)
        vpt = vwin[:, (kvh // 2) * LANES:(kvh // 2 + 1) * LANES].T.astype(BF16)
        qts = [qa_ref[0, :, h * LANES:(h + 1) * LANES].astype(F32).T for h in range(kvh * GROUP, (kvh + 1) * GROUP)]
        pc_sum = jnp.zeros((n_key, Q_BLK), F32)
        outs = []
        for gp in range(n_pair):
            qa = jnp.concatenate([jnp.where(feat == HEAD_DIM, NEG, qts[2 * gp + e]) for e in range(2)], axis=1)
            qa = qa.astype(BF16)
            qflag_ref[0, kvh, :, 2 * gp * Q_BLK:(2 * gp + 2) * Q_BLK] = qa
            s = jnp.dot(ka, qa, preferred_element_type=F32)
            s = s + cb_ref[kvh, :, 2 * gp * Q_BLK:(2 * gp + 2) * Q_BLK]
            m = jnp.maximum(jnp.max(s, axis=0, keepdims=True), M_FLOOR)
            p = jnp.exp(s - m)
            inv = 1.0 / jnp.maximum(jnp.sum(p, axis=0, keepdims=True), 1e-30)
            pc = p * inv
            pc_sum = pc_sum + pc[:, :Q_BLK] + pc[:, Q_BLK:]
            o = jnp.dot(vpt, p.astype(BF16), preferred_element_type=F32) * inv
            for e in range(2):
                h = kvh * GROUP + 2 * gp + e
                outs.append(o[:, e * Q_BLK:(e + 1) * Q_BLK].T * _gate_col(gate, h))
        for gp in range(n_pair):
            col = kvh * n_pair + gp
            oc_ref[0, :, col * LANES:(col + 1) * LANES] = _pair_columns(outs[2 * gp], outs[2 * gp + 1], kvh % 2)
        imp_t = jnp.dot(mwin_t, pc_sum, preferred_element_type=F32, precision=lax.Precision.HIGHEST)
        selneg_t = _select_blocks_t(imp_t[HEAD_DIM:], tb, N_SEL)
        for g in range(GROUP):
            qsel_ref[0, kvh, :, g * Q_BLK:(g + 1) * Q_BLK] = jnp.concatenate(
                [qts[g][:HEAD_DIM], selneg_t], axis=0).astype(BF16)


def _cmp_prompt(qa, kc_pad, vc_pad, m_pad, cb, gate, batch, seq):
    nq = seq // Q_BLK
    n_key = kc_pad.shape[1] // 2
    d_out = N_HEADS * HEAD_DIM
    return pl.pallas_call(
        functools.partial(_cmp_prompt_body, n_key=n_key),
        grid=(batch, nq),
        in_specs=[pl.BlockSpec((1, Q_BLK, N_HEADS * LANES), lambda b, t: (b, t, 0)),
                  pl.BlockSpec((1,) + kc_pad.shape[1:], lambda b, t: (b, 0, 0)),
                  pl.BlockSpec((1,) + vc_pad.shape[1:], lambda b, t: (b, 0, 0)),
                  pl.BlockSpec(m_pad.shape, lambda b, t: (0, 0)),
                  pl.BlockSpec(cb.shape, lambda b, t: (0, 0, 0)),
                  pl.BlockSpec((Q_BLK, LANES), lambda b, t: (b * nq + t, 0))],
        out_specs=[pl.BlockSpec((1, Q_BLK, d_out), lambda b, t: (b, t, 0))]
        + [pl.BlockSpec((1, KV_HEADS, LANES, GROUP * Q_BLK), lambda b, t: (b * nq + t, 0, 0, 0))] * 2,
        out_shape=[jax.ShapeDtypeStruct((batch, seq, d_out), F32)]
        + [jax.ShapeDtypeStruct((batch * nq, KV_HEADS, LANES, GROUP * Q_BLK), BF16)] * 2,
        compiler_params=_cparams("parallel", "parallel"),
        name="cmp_prompt",
    )(qa.reshape(batch, seq, -1), kc_pad, vc_pad, m_pad, cb, gate)


KEY_TILE = 256
COL_CHAIN = 256


def _store_chain(out_ref, o_t, gate, gate_col0, lane_col):
    pair = jnp.concatenate([o_t[:, :Q_BLK], o_t[:, Q_BLK:]], axis=0).T
    g = jnp.where(_lane(pair.shape) < HEAD_DIM, _gate_col(gate, gate_col0), _gate_col(gate, gate_col0 + 1))
    out_ref[0, :, lane_col * LANES:(lane_col + 1) * LANES] = pair * g


_PAIR_CHAINS = [(e, c) for e in range(2) for c in range(GROUP * Q_BLK // COL_CHAIN)]


def _slc_prompt_body(q_ref, k_ref, v_ref, sb_ref, gate_ref, out_ref, m_ref, l_ref, acc_ref):
    pr = pl.program_id(1)
    qt = pl.program_id(2)
    m_ref[...] = jnp.full(m_ref.shape, M_FLOOR, F32)
    l_ref[...] = jnp.zeros(l_ref.shape, F32)
    acc_ref[...] = jnp.zeros(acc_ref.shape, F32)
    last = (qt * Q_BLK) // KEY_TILE
    cs = lambda c: slice(c * COL_CHAIN, (c + 1) * COL_CHAIN)

    def make_step(with_bias):
        def step(kt, carry):
            base = pl.multiple_of(kt * KEY_TILE, KEY_TILE)
            case = qt % 2 + 2 * (last - kt)
            ks = [k_ref[0, pl.ds(base, KEY_TILE), e * LANES:(e + 1) * LANES] for e in range(2)]
            ss = [jnp.dot(ks[e], q_ref[0, e, :, cs(c)], preferred_element_type=F32) for e, c in _PAIR_CHAINS]
            if with_bias:
                ss = [s + sb_ref[case, e, :, cs(c)] for s, (e, c) in zip(ss, _PAIR_CHAINS)]
            ps, alphas = [], []
            for i, s in enumerate(ss):
                m_old = m_ref[i]
                m_new = jnp.maximum(m_old, jnp.max(s, axis=0, keepdims=True))
                alpha = jnp.exp(m_old - m_new)
                p = jnp.exp(s - m_new)
                l_ref[i] = alpha * l_ref[i] + jnp.sum(p, axis=0, keepdims=True)
                m_ref[i] = m_new
                ps.append(p.astype(BF16))
                alphas.append(alpha)
            for i, (e, c) in enumerate(_PAIR_CHAINS):
                vt = v_ref[e * HEAD_DIM:(e + 1) * HEAD_DIM, pl.ds(base, KEY_TILE)]
                acc_ref[i] = alphas[i] * acc_ref[i] + jnp.dot(vt, ps[i], preferred_element_type=F32)
            return carry
        return step

    n_far = jnp.maximum(last - 1, 0)
    lax.fori_loop(0, n_far, make_step(False), 0)
    lax.fori_loop(n_far, last + 1, make_step(True), 0)
    gate = gate_ref[...]
    for i, (e, c) in enumerate(_PAIR_CHAINS):
        o = acc_ref[i] * (1.0 / jnp.maximum(l_ref[i], 1e-30))
        _store_chain(out_ref, o, gate, N_HEADS + (2 * pr + e) * GROUP + 2 * c, i)


def _slc_prompt(qsel_t, ska, sv_t, sb_t, gate, batch, seq):
    nq = seq // Q_BLK
    cols = GROUP * Q_BLK
    d_out = N_HEADS * HEAD_DIM
    n_chain = len(_PAIR_CHAINS)
    return pl.pallas_call(
        _slc_prompt_body,
        grid=(batch, KV_HEADS // 2, nq),
        in_specs=[pl.BlockSpec((1, 2, LANES, cols), lambda b, p, t: (b * nq + t, p, 0, 0)),
                  pl.BlockSpec((1, seq, 2 * LANES), lambda b, p, t: (b, 0, p)),
                  pl.BlockSpec((2 * HEAD_DIM, seq), lambda b, p, t: (p, b)),
                  pl.BlockSpec((4, 2, KEY_TILE, cols), lambda b, p, t: (0, p, 0, 0)),
                  pl.BlockSpec((Q_BLK, LANES), lambda b, p, t: (b * nq + t, 0))],
        out_specs=pl.BlockSpec((1, Q_BLK, 2 * GROUP * HEAD_DIM), lambda b, p, t: (b, t, p)),
        out_shape=jax.ShapeDtypeStruct((batch, seq, d_out), F32),
        scratch_shapes=[pltpu.VMEM((n_chain, 1, COL_CHAIN), F32), pltpu.VMEM((n_chain, 1, COL_CHAIN), F32),
                        pltpu.VMEM((n_chain, HEAD_DIM, COL_CHAIN), F32)],
        compiler_params=_cparams("parallel", "parallel", "arbitrary"),
        name="slc_prompt",
    )(qsel_t, ska.reshape(batch, seq, -1), sv_t, sb_t, gate)


def _win_prompt_body(q_ref, k_ref, v_ref, wb_ref, gate_ref, out_ref, *, n_key):
    pr = pl.program_id(1)
    qt = pl.program_id(2)
    base = pl.multiple_of(qt * Q_BLK, Q_BLK)
    cs = lambda c: slice(c * COL_CHAIN, (c + 1) * COL_CHAIN)
    ks = [k_ref[0, pl.ds(base, n_key), e * LANES:(e + 1) * LANES] for e in range(2)]
    ss = [jnp.dot(ks[e], q_ref[0, e, :, cs(c)], preferred_element_type=F32) + wb_ref[e, :, cs(c)]
          for e, c in _PAIR_CHAINS]
    ps, invs = [], []
    for s in ss:
        m = jnp.maximum(jnp.max(s, axis=0, keepdims=True), M_FLOOR)
        p = jnp.exp(s - m)
        invs.append(1.0 / jnp.maximum(jnp.sum(p, axis=0, keepdims=True), 1e-30))
        ps.append(p.astype(BF16))
    gate = gate_ref[...]
    for i, (e, c) in enumerate(_PAIR_CHAINS):
        vt = v_ref[e * HEAD_DIM:(e + 1) * HEAD_DIM, pl.ds(base, n_key)]
        o = jnp.dot(vt, ps[i], preferred_element_type=F32) * invs[i]
        _store_chain(out_ref, o, gate, 2 * N_HEADS + (2 * pr + e) * GROUP + 2 * c, i)


def _win_prompt(q_t, wka_pad, wv_t_pad, wb_t, gate, batch, seq):
    nq = seq // Q_BLK
    n_key = WINDOW + Q_BLK
    cols = GROUP * Q_BLK
    d_out = N_HEADS * HEAD_DIM
    padded = wka_pad.shape[1]
    return pl.pallas_call(
        functools.partial(_win_prompt_body, n_key=n_key),
        grid=(batch, KV_HEADS // 2, nq),
        in_specs=[pl.BlockSpec((1, 2, LANES, cols), lambda b, p, t: (b * nq + t, p, 0, 0)),
                  pl.BlockSpec((1, padded, 2 * LANES), lambda b, p, t: (b, 0, p)),
                  pl.BlockSpec((2 * HEAD_DIM, padded), lambda b, p, t: (p, b)),
                  pl.BlockSpec((2, n_key, cols), lambda b, p, t: (p, 0, 0)),
                  pl.BlockSpec((Q_BLK, LANES), lambda b, p, t: (b * nq + t, 0))],
        out_specs=pl.BlockSpec((1, Q_BLK, 2 * GROUP * HEAD_DIM), lambda b, p, t: (b, t, p)),
        out_shape=jax.ShapeDtypeStruct((batch, seq, d_out), F32),
        compiler_params=_cparams("parallel", "parallel", "parallel"),
        name="win_prompt",
    )(q_t, wka_pad, wv_t_pad, wb_t, gate)


def _nsa_out_body(oc_ref, os_ref, ow_ref, sz_ref, x_ref, wo_ref, gp_ref, out_ref):
    d = oc_ref.shape[-1]
    y = oc_ref[...] * sz_ref[:, :d] + os_ref[...] * sz_ref[:, d:2 * d] + ow_ref[...] * sz_ref[:, 2 * d:]
    o = jnp.dot(y.astype(BF16), wo_ref[...], preferred_element_type=F32)
    out_ref[...] = x_ref[...] + _rms(o, gp_ref[...])


def _nsa_out(oc, os_, ow, sz, x, wo_bf, gp, tm):
    rows, d = x.shape
    dq = oc.shape[-1]
    row = lambda n: pl.BlockSpec((tm, n), lambda i: (i, 0))
    return pl.pallas_call(
        _nsa_out_body,
        grid=(rows // tm,),
        in_specs=[row(dq), row(dq), row(dq), row(N_BRANCH * dq), row(d),
                  pl.BlockSpec((dq, d), lambda i: (0, 0)), pl.BlockSpec((1, d), lambda i: (0, 0))],
        out_specs=row(d),
        out_shape=jax.ShapeDtypeStruct((rows, d), F32),
        compiler_params=_cparams("parallel"),
        name="nsa_out",
    )(oc, os_, ow, sz, x, wo_bf, gp.reshape(1, d))


def _sample_attn_body(pt_ref, *refs, n_pages, tq, w_keep):
    del pt_ref
    page_refs = refs[:n_pages]
    (qa_ref, kvc_ref, snew_ref, wst_ref, wnew_ref, m_ref, e_ref, cbs_ref, sbs_ref, wbs_ref, gate_ref,
     oc_ref, os_ref, ow_ref, ksel_ref, kwin_ref) = refs[n_pages:]
    n_past = n_pages * PAGE_SIZE
    n_sk = ksel_ref.shape[0]
    n_wk = kwin_ref.shape[0]
    width = ksel_ref.shape[1]
    kw = width // 2
    for p in range(n_pages):
        ksel_ref[p * PAGE_SIZE:(p + 1) * PAGE_SIZE, :] = page_refs[p][0]
    ksel_ref[n_past:n_past + tq, :] = snew_ref[...]
    ksel_ref[n_past + tq:, :] = jnp.zeros((n_sk - n_past - tq, width), F32)
    kwin_ref[0:w_keep, :] = wst_ref[0]
    kwin_ref[w_keep:w_keep + tq, :] = wnew_ref[...]
    kwin_ref[w_keep + tq:, :] = jnp.zeros((n_wk - w_keep - tq, width), F32)
    pos = n_past + lax.broadcasted_iota(jnp.int32, (tq, 1), 0)
    tb = pos // L_SLC
    gate = gate_ref[...]
    rows = GROUP * tq

    def attend(q, k, v, bias, extra=None):
        s = lax.dot_general(q, k, _NT, preferred_element_type=F32)
        if extra is not None:
            s = s + extra
        s = (s.reshape(GROUP, tq, s.shape[-1]) + bias).reshape(rows, s.shape[-1])
        p, l = _softmax_rows(s)
        inv = 1.0 / jnp.maximum(l, 1e-30)
        return p, inv, jnp.dot(p.astype(BF16), v, preferred_element_type=F32)

    def store(out_ref, o, kvh, branch):
        outs = [o[g * tq:(g + 1) * tq] * _gate_col(gate, branch * N_HEADS + kvh * GROUP + g) for g in range(GROUP)]
        for gp in range(GROUP // 2):
            col = kvh * (GROUP // 2) + gp
            out_ref[:, col * LANES:(col + 1) * LANES] = _pair_columns(outs[2 * gp], outs[2 * gp + 1], kvh % 2)

    def pair_cols(ref, pr):
        kcol = slice(pr * LANES, (pr + 1) * LANES)
        vcol = slice(kw + pr * LANES, kw + (pr + 1) * LANES)
        if len(ref.shape) == 3:
            return ref[0, :, kcol].astype(BF16), ref[0, :, vcol].astype(BF16)
        return ref[:, kcol].astype(BF16), ref[:, vcol].astype(BF16)

    qs, imps = [], []
    for kvh in range(KV_HEADS):
        q = jnp.concatenate([qa_ref[:, h * LANES:(h + 1) * LANES]
                             for h in range(kvh * GROUP, (kvh + 1) * GROUP)], axis=0)
        qs.append((q if kvh % 2 == 0 else _swap_halves(q)).astype(BF16))
    for pr in range(KV_HEADS // 2):
        kc, vc = pair_cols(kvc_ref, pr)
        for kvh in (2 * pr, 2 * pr + 1):
            p, inv, o = attend(qs[kvh], kc, vc, cbs_ref[kvh * GROUP:(kvh + 1) * GROUP])
            store(oc_ref, o * inv, kvh, 0)
            pc_sum = jnp.sum((p * inv).reshape(GROUP, tq, p.shape[-1]), axis=0)
            imps.append(jnp.dot(pc_sum, m_ref[...], preferred_element_type=F32, precision=lax.Precision.HIGHEST))
    selneg = _select_blocks(jnp.concatenate(imps, axis=0), jnp.concatenate([tb] * KV_HEADS, axis=0), N_SEL)
    for pr in range(KV_HEADS // 2):
        ksl, vsl = pair_cols(ksel_ref, pr)
        kwn, vwn = pair_cols(kwin_ref, pr)
        for kvh in (2 * pr, 2 * pr + 1):
            heads = slice(kvh * GROUP, (kvh + 1) * GROUP)
            qm = jnp.concatenate([selneg[kvh * tq:(kvh + 1) * tq]] * GROUP, axis=0).astype(BF16)
            block_mask = jnp.dot(qm, e_ref[...], preferred_element_type=F32)
            _, inv, o = attend(qs[kvh], ksl, vsl, sbs_ref[heads], block_mask)
            store(os_ref, o * inv, kvh, 1)
            _, inv, o = attend(qs[kvh], kwn, vwn, wbs_ref[heads])
            store(ow_ref, o * inv, kvh, 2)


def _sample_attn(qa, kvc, pages, page_table, slc_new, win_state, win_new, m_s, e_s, cbs, sbs, wbs, gate,
                 n_seq, tq, n_pages):
    width = pages.shape[-1]
    n_sk = sbs.shape[-1]
    n_wk = wbs.shape[-1]
    w_keep = win_state.shape[1]
    d_out = N_HEADS * HEAD_DIM
    page_spec = lambda j: pl.BlockSpec((1, PAGE_SIZE, width), lambda i, pt: (pt[i * n_pages + j], 0, 0))
    full = lambda a: pl.BlockSpec(a.shape, lambda i, pt: (0,) * a.ndim)
    row = lambda n: pl.BlockSpec((tq, n), lambda i, pt: (i, 0))
    grid_spec = pltpu.PrefetchScalarGridSpec(
        num_scalar_prefetch=1,
        grid=(n_seq,),
        in_specs=[page_spec(j) for j in range(n_pages)] + [
            row(N_HEADS * LANES),
            pl.BlockSpec((1,) + kvc.shape[1:], lambda i, pt: (i, 0, 0)),
            row(width),
            pl.BlockSpec((1, w_keep, width), lambda i, pt: (i, 0, 0)),
            row(width),
            full(m_s), full(e_s), full(cbs), full(sbs), full(wbs),
            row(LANES)],
        out_specs=[row(d_out)] * 3,
        scratch_shapes=[pltpu.VMEM((n_sk, width), F32), pltpu.VMEM((n_wk, width), F32)],
    )
    return pl.pallas_call(
        functools.partial(_sample_attn_body, n_pages=n_pages, tq=tq, w_keep=w_keep),
        grid_spec=grid_spec,
        out_shape=[jax.ShapeDtypeStruct((n_seq * tq, d_out), F32)] * 3,
        compiler_params=_cparams("parallel"),
        name="sample_attn",
    )(page_table, *([pages] * n_pages), qa, kvc, slc_new, win_state, win_new, m_s, e_s, cbs, sbs, wbs, gate)


def _decode_attn_body(pt_ref, *refs, n_pages, tq, w_keep):
    del pt_ref
    page_refs = refs[:n_pages]
    (qa_ref, kvc_ref, snew_ref, wst_ref, wnew_ref, mt_ref, e_ref, cbs_ref, sbs_ref, wbs_ref, gate_ref,
     oc_ref, os_ref, ow_ref, kt_ref, vt_ref, kwt_ref, vwt_ref) = refs[n_pages:]
    n_past = n_pages * PAGE_SIZE
    kw = KV_HEADS * HEAD_DIM
    rows = N_HEADS * tq

    def new_tile(ref):
        return jnp.concatenate([ref[...], jnp.zeros((LANES - tq, 2 * kw), F32)], axis=0).T

    def fill(k_dst, v_dst, col0, tile):
        k_dst[:, col0:col0 + tile.shape[1]] = tile[:kw].astype(BF16)
        v_dst[:, col0:col0 + tile.shape[1]] = tile[kw:].astype(BF16)

    for p in range(n_pages):
        fill(kt_ref, vt_ref, p * PAGE_SIZE, page_refs[p][0])
    fill(kt_ref, vt_ref, n_past, new_tile(snew_ref))
    fill(kwt_ref, vwt_ref, 0, wst_ref[0])
    fill(kwt_ref, vwt_ref, w_keep, new_tile(wnew_ref))

    zero = jnp.zeros((tq, LANES), F32)
    tiles = [[], []]
    for h in range(N_HEADS):
        kvh = h // GROUP
        piece = qa_ref[:, h * LANES:(h + 1) * LANES]
        piece = piece if kvh % 2 == 0 else _swap_halves(piece)
        for ct in range(2):
            tiles[ct].append(piece if kvh // 2 == ct else zero)
    qbd = jnp.concatenate([jnp.concatenate(t, axis=0) for t in tiles], axis=1).astype(BF16)
    gate = gate_ref[...]

    def branch(k_t, v_t, bias, extra=None):
        s = jnp.dot(qbd, k_t, preferred_element_type=F32) + bias
        if extra is not None:
            s = s + extra
        p, l = _softmax_rows(s)
        inv = 1.0 / jnp.maximum(l, 1e-30)
        o_t = lax.dot_general(v_t, p.astype(BF16), _NT, preferred_element_type=F32)
        return p, inv, o_t.T

    def store(out_ref, o, scale):
        o = o * scale
        for kvh in range(KV_HEADS):
            tile = o[:, (kvh // 2) * LANES:(kvh // 2 + 1) * LANES]
            for gp in range(GROUP // 2):
                h0 = kvh * GROUP + 2 * gp
                col = kvh * (GROUP // 2) + gp
                out_ref[:, col * LANES:(col + 1) * LANES] = _pair_columns(
                    tile[h0 * tq:(h0 + 1) * tq], tile[(h0 + 1) * tq:(h0 + 2) * tq], kvh % 2)

    p, inv, o = branch(kvc_ref[0, :kw, :].astype(BF16), kvc_ref[0, kw:, :].astype(BF16), cbs_ref[...])
    store(oc_ref, o, inv * _gate_col(gate, 0))
    pc = p * inv
    pc_sum = jnp.concatenate(
        [sum(pc[(kvh * GROUP + g) * tq:(kvh * GROUP + g + 1) * tq] for g in range(GROUP)) for kvh in range(KV_HEADS)],
        axis=0)
    imp_t = lax.dot_general(mt_ref[...], pc_sum, _NT, preferred_element_type=F32, precision=lax.Precision.HIGHEST)
    n_col = KV_HEADS * tq
    nb_pad = -(-(-(-(n_past + tq) // L_SLC)) // 8) * 8
    tb = (n_past + lax.broadcasted_iota(jnp.int32, (1, n_col), 1) % tq) // L_SLC
    selneg_t = _select_blocks_t(imp_t[HEAD_DIM:HEAD_DIM + nb_pad], tb, N_SEL)
    sel_t = jnp.concatenate([jnp.zeros((HEAD_DIM, n_col), F32), selneg_t,
                             jnp.zeros((LANES - HEAD_DIM - nb_pad, n_col), F32)], axis=0)
    sel = jnp.concatenate([sel_t, jnp.zeros((LANES, LANES - n_col), F32)], axis=1).T
    sel = jnp.concatenate([sel[kvh * tq:(kvh + 1) * tq] for kvh in range(KV_HEADS) for _ in range(GROUP)], axis=0)
    block_mask = jnp.dot(sel.astype(BF16), e_ref[...], preferred_element_type=F32)
    _, inv, o = branch(kt_ref[...], vt_ref[...], sbs_ref[...], block_mask)
    store(os_ref, o, inv * _gate_col(gate, 1))
    _, inv, o = branch(kwt_ref[...], vwt_ref[...], wbs_ref[...])
    store(ow_ref, o, inv * _gate_col(gate, 2))


def _decode_attn(qa, kvc_t, pages_t, page_table, slc_new, win_state_t, win_new, m_t, e_s, cbs, sbs, wbs, gate_r,
                 n_seq, tq, n_pages):
    width = pages_t.shape[1]
    n_sk = sbs.shape[-1]
    n_wk = wbs.shape[-1]
    w_keep = win_state_t.shape[2]
    d_out = N_HEADS * HEAD_DIM
    rows = N_HEADS * tq
    page_spec = lambda j: pl.BlockSpec((1, width, PAGE_SIZE), lambda i, pt: (pt[i * n_pages + j], 0, 0))
    full = lambda a: pl.BlockSpec(a.shape, lambda i, pt: (0,) * a.ndim)
    row = lambda n: pl.BlockSpec((tq, n), lambda i, pt: (i, 0))
    grid_spec = pltpu.PrefetchScalarGridSpec(
        num_scalar_prefetch=1,
        grid=(n_seq,),
        in_specs=[page_spec(j) for j in range(n_pages)] + [
            row(N_HEADS * LANES),
            pl.BlockSpec((1,) + kvc_t.shape[1:], lambda i, pt: (i, 0, 0)),
            row(width),
            pl.BlockSpec((1, width, w_keep), lambda i, pt: (i, 0, 0)),
            row(width),
            full(m_t), full(e_s), full(cbs), full(sbs), full(wbs),
            pl.BlockSpec((rows, LANES), lambda i, pt: (i, 0))],
        out_specs=[row(d_out)] * 3,
        scratch_shapes=[pltpu.VMEM((width // 2, n_sk), BF16), pltpu.VMEM((width // 2, n_sk), BF16),
                        pltpu.VMEM((width // 2, n_wk), BF16), pltpu.VMEM((width // 2, n_wk), BF16)],
    )
    return pl.pallas_call(
        functools.partial(_decode_attn_body, n_pages=n_pages, tq=tq, w_keep=w_keep),
        grid_spec=grid_spec,
        out_shape=[jax.ShapeDtypeStruct((n_seq * tq, d_out), F32)] * 3,
        compiler_params=_cparams("parallel"),
        name="decode_attn",
    )(page_table, *([pages_t] * n_pages), qa, kvc_t, slc_new, win_state_t, win_new, m_t, e_s, cbs, sbs, wbs, gate_r)


def _overlap_matrix(n_rows, row0, n_cmp, n_blk):
    import numpy as np
    m = np.zeros((n_rows, LANES), np.float32)
    cs = np.arange(n_cmp)[:, None] * CMP_STRIDE
    js = np.arange(n_blk)[None, :] * L_SLC
    m[row0:row0 + n_cmp, HEAD_DIM:HEAD_DIM + n_blk] = (cs <= js + L_SLC - 1) & (cs + L_CMP - 1 >= js)
    return jnp.asarray(m)


def kernel(x_prompt, x_sample, state_conv, cache_cmp, cache_slc, state_win, page_table, rel_bias, a_norm_pre, a_w_in, a_conv_w, a_conv_b, a_ln_g, a_ln_b, a_w_out, a_norm_post, kv_norm, w_kv, cmp_pe, cmp_w1, cmp_w2, b_norm_pre, b_w_in, b_w_out, b_norm_post):
    import numpy as np
    bp, tp, d = x_prompt.shape
    bd, tq, _ = x_sample.shape
    n_pages = page_table.shape[1]
    past = n_pages * PAGE_SIZE
    w_keep = state_win.shape[1]
    width = 2 * KV_HEADS * HEAD_DIM
    kw = KV_HEADS * HEAD_DIM
    dq = N_HEADS * HEAD_DIM
    assert b_w_in.shape[0] == 1 and tp % (2 * KEY_TILE) == 0 and tp // L_SLC <= HEAD_DIM
    tm = 256
    tm_s = min(tm, bd * tq)

    xp = x_prompt
    xs = x_sample.reshape(bd * tq, d)
    conv_p, conv_s = [], []
    for l in range(a_w_in.shape[0]):
        w_in = a_w_in[l].astype(BF16)
        w_out = a_w_out[l].astype(BF16)
        di = a_w_out.shape[1]
        tail = (a_conv_w[l], a_conv_b[l], a_ln_g[l], a_ln_b[l], w_out, a_norm_post[l])
        glu, sz = _glu_proj(xp.reshape(bp * tp, d), a_norm_pre[l], w_in, tm)
        glu = glu.reshape(bp, tp, di)
        xp = _conv_prompt(glu, sz.reshape(bp, tp, di), xp, *tail, tm)
        conv_p.append(glu[:, -(CONV_W - 1):])
        glu, sz = _glu_proj(xs, a_norm_pre[l], w_in, tm_s)
        xs = _conv_sample(glu, state_conv[l], sz, xs, *tail, 16)
        conv_s.append(jnp.concatenate([state_conv[l], glu.reshape(bd, tq, di)], axis=1)[:, -(CONV_W - 1):])

    bw = b_w_in[0]
    n_gate = N_BRANCH * N_HEADS
    wq = bw[:, :dq].astype(BF16)
    wz = bw[:, dq:dq * (1 + N_BRANCH)].astype(BF16)
    wg = jnp.pad(bw[:, dq * (1 + N_BRANCH):], ((0, 0), (0, LANES - n_gate))).astype(BF16)
    wkv = w_kv.astype(BF16)
    xp2 = xp.reshape(bp * tp, d)
    (cmp_p, slc_p, win_p, qa_p, sz_p, gate_p, ska, sv, wka, wv) = _nsa_proj(
        xp2, kv_norm, b_norm_pre[0], wkv, wq, wz, wg, tm, seq_len=tp)
    cmp_s, slc_s, win_s, qa_s, sz_s, gate_s = _nsa_proj(xs, kv_norm, b_norm_pre[0], wkv, wq, wz, wg, tm_s)
    qa_s = qa_s.astype(F32)

    w1cat = jnp.concatenate([cmp_w1[:, :CMP_STRIDE * HEAD_DIM], cmp_w1[:, CMP_STRIDE * HEAD_DIM:]], axis=2).astype(BF16)
    pe8 = jnp.pad(cmp_pe.reshape(2, L_CMP // CMP_STRIDE, CMP_STRIDE * HEAD_DIM), ((0, 0), (0, 6), (0, 0)))
    zero = jnp.zeros_like(cmp_w2)
    w2h = jnp.stack([jnp.concatenate([cmp_w2, zero], axis=2), jnp.concatenate([zero, cmp_w2], axis=2)], axis=1).astype(BF16)
    prompt_page = min(1024, tp)
    pp = tp // prompt_page
    kvc_p = _compress(cmp_p.reshape(bp * pp, prompt_page, width), jnp.arange(bp * pp, dtype=jnp.int32), w1cat, pe8, w2h, pp)
    pt_flat = page_table.reshape(-1).astype(jnp.int32)
    cmp_pages_t = cache_cmp.transpose(0, 2, 3, 4, 1).reshape(-1, width, PAGE_SIZE)
    kvc_s = _compress(cmp_pages_t, pt_flat, w1cat, pe8, w2h, n_pages, transposed=True)

    n_chunk_p = kvc_p.shape[1]
    nc_p = n_chunk_p - L_CMP // CMP_STRIDE + 1
    real = (jnp.arange(n_chunk_p) < nc_p)[None, :, None]
    flag = jnp.zeros((LANES - HEAD_DIM,), F32).at[0].set(1.0)
    kc4 = jnp.where(real, kvc_p[:, :, :kw], 0.0).reshape(bp, n_chunk_p, KV_HEADS, HEAD_DIM)
    aug = jnp.where(real[..., None], 0.0, flag) * jnp.ones((bp, n_chunk_p, KV_HEADS, 1), F32)
    kc_real = jnp.concatenate([kc4, aug], axis=-1).reshape(bp, n_chunk_p, KV_HEADS * LANES)
    pad_row = jnp.concatenate([jnp.zeros((HEAD_DIM,), F32), flag])
    kc_front = jnp.broadcast_to(jnp.tile(pad_row, KV_HEADS), (bp, n_chunk_p, KV_HEADS * LANES))
    kc_pad = jnp.concatenate([kc_front, kc_real], axis=1)
    vc_pad = jnp.concatenate([jnp.zeros((bp, n_chunk_p, kw), F32), jnp.where(real, kvc_p[:, :, kw:], 0.0)], axis=1)
    m_pad = _overlap_matrix(2 * n_chunk_p, n_chunk_p, nc_p, tp // L_SLC)
    cb = _bias_tile(rel_bias, n_chunk_p, Q_BLK, 1, CMP_STRIDE * (n_chunk_p - Q_BLK // CMP_STRIDE) - (L_CMP - 1),
                    row_step=-CMP_STRIDE)
    cb = cb.reshape(KV_HEADS, GROUP, n_chunk_p, Q_BLK).transpose(0, 2, 1, 3).reshape(KV_HEADS, n_chunk_p, GROUP * Q_BLK)
    oc_p, qsel, qflag = _cmp_prompt(qa_p, kc_pad, vc_pad, m_pad, cb, gate_p, bp, tp)
    sb = jnp.stack([_bias_tile(rel_bias, KEY_TILE, Q_BLK, 1, off, row_step=-1)
                    for off in (0, Q_BLK, 2 * Q_BLK, 3 * Q_BLK)])
    sb = sb.reshape(4, KV_HEADS, GROUP, KEY_TILE, Q_BLK).transpose(0, 1, 3, 2, 4).reshape(4, KV_HEADS, KEY_TILE, GROUP * Q_BLK)
    os_p = _slc_prompt(qsel, ska, sv, sb, gate_p, bp, tp)
    n_wkey = WINDOW + Q_BLK
    wb = _bias_tile(rel_bias, n_wkey, Q_BLK, 1, WINDOW, WINDOW, row_step=-1)
    wb = wb.reshape(KV_HEADS, GROUP, n_wkey, Q_BLK).transpose(0, 2, 1, 3).reshape(KV_HEADS, n_wkey, GROUP * Q_BLK)
    wka_front = jnp.broadcast_to(jnp.tile(pad_row, KV_HEADS).astype(BF16), (bp, WINDOW, KV_HEADS * LANES))
    wka_pad = jnp.concatenate([wka_front, wka.reshape(bp, tp, -1)], axis=1)
    wv_pad = jnp.concatenate([jnp.zeros((kw, bp, WINDOW), BF16), wv.reshape(kw, bp, tp)], axis=2)
    ow_p = _win_prompt(qflag, wka_pad, wv_pad.reshape(kw, bp * (WINDOW + tp)), wb, gate_p, bp, tp)
    wo = b_w_out[0].astype(BF16)
    y_p = _nsa_out(oc_p.reshape(bp * tp, dq), os_p.reshape(bp * tp, dq), ow_p.reshape(bp * tp, dq), sz_p, xp2,
                   wo, b_norm_post[0], tm)

    n_chunk_s = kvc_s.shape[2]
    nc_s = n_chunk_s - L_CMP // CMP_STRIDE + 1
    nb_s = -(-(past + tq) // L_SLC)
    n_sk = past + LANES
    n_wk = w_keep + LANES
    m_t = _overlap_matrix(n_chunk_s, 0, nc_s, nb_s).T
    e_np = np.zeros((LANES, n_sk), np.float32)
    e_np[HEAD_DIM + np.arange(n_sk) // L_SLC, np.arange(n_sk)] = 1.0
    e_s = jnp.asarray(e_np, BF16)
    rows_s = N_HEADS * tq
    cbs = _bias_tile(rel_bias, tq, n_chunk_s, -CMP_STRIDE, past - (L_CMP - 1)).reshape(rows_s, n_chunk_s)
    sbs = _bias_tile(rel_bias, tq, n_sk, -1, past).reshape(rows_s, n_sk)
    wbs = _bias_tile(rel_bias, tq, n_wk, -1, w_keep, WINDOW).reshape(rows_s, n_wk)
    gate_r = gate_s[:, :n_gate].reshape(bd, tq, N_BRANCH, N_HEADS).transpose(0, 3, 1, 2).reshape(bd * rows_s, N_BRANCH)
    gate_r = jnp.pad(gate_r, ((0, 0), (0, LANES - N_BRANCH)))
    to_t = lambda a: a.transpose(0, 2, 3, 4, 1).reshape(a.shape[0], width, a.shape[1])
    oc_s, os_s, ow_s = _decode_attn(qa_s, kvc_s, to_t(cache_slc), pt_flat, slc_s, to_t(state_win), win_s,
                                    m_t, e_s, cbs, sbs, wbs, gate_r, bd, tq, n_pages)
    y_s = _nsa_out(oc_s, os_s, ow_s, sz_s, xs, wo, b_norm_post[0], tm_s)

    kv5 = lambda a, b, t: a.reshape(b, t, 2, KV_HEADS, HEAD_DIM)
    win_p5 = kv5(win_p, bp, tp)
    win_s5 = kv5(win_s, bd, tq)
    win_all = jnp.concatenate([state_win, win_s5], axis=1)
    return (y_p.reshape(bp, tp, d), y_s.reshape(bd, tq, d), jnp.stack(conv_p), jnp.stack(conv_s),
            kv5(cmp_p, bp, tp), kv5(cmp_s, bd, tq), kv5(slc_p, bp, tp), kv5(slc_s, bd, tq),
            win_p5[:, -min(WINDOW, tp):], win_all[:, -min(WINDOW, win_all.shape[1]):])
```

```python
import functools
import math

import jax
import jax.numpy as jnp
from jax import lax
from jax.experimental import pallas as pl
from jax.experimental.pallas import tpu as pltpu

F32 = jnp.float32
BF16 = jnp.bfloat16

EPS = 1e-6
NEG = -1e30
BIG = 1e9
M_FLOOR = -1e20

HEAD_DIM = 64
KV_HEADS = 4
N_HEADS = 16
GROUP = N_HEADS // KV_HEADS
N_BRANCH = 3
CONV_W = 31
L_CMP = 32
CMP_STRIDE = 16
L_SLC = 64
N_SEL = 16
WINDOW = 512
Q_BLK = 128
NUM_BUCKETS = 32
MAX_DISTANCE = 128
MAX_EXACT = NUM_BUCKETS // 2
PAGE_SIZE = 128
LANES = 128
HALO = 32

VMEM_LIMIT = 56 * 1024 * 1024


def _cparams(*sem):
    return pltpu.CompilerParams(dimension_semantics=sem, vmem_limit_bytes=VMEM_LIMIT)


def _sigmoid(x):
    return 1.0 / (1.0 + jnp.exp(-x))


def _rms(x, g):
    return x * lax.rsqrt(jnp.mean(x * x, axis=-1, keepdims=True) + EPS) * g


def _lane(shape):
    return lax.broadcasted_iota(jnp.int32, shape, len(shape) - 1)


def _swap_halves(x):
    return pltpu.roll(x, HEAD_DIM, axis=x.ndim - 1)


def _glu_proj_body(x_ref, g_ref, w_ref, glu_ref, sz_ref):
    di = glu_ref.shape[-1]
    h = _rms(x_ref[...], g_ref[...])
    u = jnp.dot(h.astype(BF16), w_ref[...], preferred_element_type=F32)
    z = u[:, 2 * di:]
    glu_ref[...] = u[:, :di] * _sigmoid(u[:, di:2 * di])
    sz_ref[...] = z * _sigmoid(z)


def _glu_proj(x, g, w_bf, tm):
    rows, d = x.shape
    di = w_bf.shape[1] // 3
    return pl.pallas_call(
        _glu_proj_body,
        grid=(rows // tm,),
        in_specs=[pl.BlockSpec((tm, d), lambda i: (i, 0)),
                  pl.BlockSpec((1, d), lambda i: (0, 0)),
                  pl.BlockSpec((d, 3 * di), lambda i: (0, 0))],
        out_specs=[pl.BlockSpec((tm, di), lambda i: (i, 0)),
                   pl.BlockSpec((tm, di), lambda i: (i, 0))],
        out_shape=[jax.ShapeDtypeStruct((rows, di), F32)] * 2,
        compiler_params=_cparams("parallel"),
        name="glu_proj",
    )(x, g.reshape(1, d), w_bf)


def _conv_tail(c, sz, x, lg_ref, lb_ref, wo_ref, gp_ref):
    mu = jnp.mean(c, axis=-1, keepdims=True)
    cc = c - mu
    var = jnp.mean(cc * cc, axis=-1, keepdims=True)
    y = cc * lax.rsqrt(var + EPS) * lg_ref[...] + lb_ref[...]
    y = y * _sigmoid(y) * sz
    o = jnp.dot(y.astype(BF16), wo_ref[...], preferred_element_type=F32)
    return x + _rms(o, gp_ref[...])


def _conv_prompt_body(glu_ref, prev_ref, sz_ref, x_ref, cw_ref, cb_ref, lg_ref, lb_ref, wo_ref, gp_ref,
                      out_ref, full_ref, c_ref, *, tm):
    t = pl.program_id(1)
    full_ref[0:HALO, :] = jnp.where(t > 0, prev_ref[0], 0.0)
    full_ref[HALO:HALO + tm, :] = glu_ref[0]
    d = c_ref.shape[-1]
    first = HALO - (CONV_W - 1)
    n_phase = 8
    n_grp = -(-CONV_W // n_phase)
    for lc in range(d // LANES):
        ln = slice(lc * LANES, (lc + 1) * LANES)
        acc = jnp.zeros((tm, LANES), F32)
        for b in range(n_phase):
            sb = full_ref[first + b:first + b + tm + 8 * (n_grp - 1), ln]
            for a in range(n_grp):
                j = 8 * a + b
                if j < CONV_W:
                    acc = acc + cw_ref[j:j + 1, ln] * sb[8 * a:8 * a + tm]
        c_ref[:, ln] = acc + cb_ref[:, ln]
    out_ref[0] = _conv_tail(c_ref[...], sz_ref[0], x_ref[0], lg_ref, lb_ref, wo_ref, gp_ref)


def _conv_prompt(glu, sz, x, cw, cb, lg, lb, wo_bf, gp, tm):
    b, t, d = x.shape
    di = glu.shape[-1]
    per = tm // HALO
    vec = lambda n: pl.BlockSpec((1, n), lambda i, j: (0, 0))
    return pl.pallas_call(
        functools.partial(_conv_prompt_body, tm=tm),
        grid=(b, t // tm),
        in_specs=[pl.BlockSpec((1, tm, di), lambda i, j: (i, j, 0)),
                  pl.BlockSpec((1, HALO, di), lambda i, j: (i, jnp.maximum(j * per - 1, 0), 0)),
                  pl.BlockSpec((1, tm, di), lambda i, j: (i, j, 0)),
                  pl.BlockSpec((1, tm, d), lambda i, j: (i, j, 0)),
                  pl.BlockSpec((CONV_W, di), lambda i, j: (0, 0)),
                  vec(di), vec(di), vec(di),
                  pl.BlockSpec((di, d), lambda i, j: (0, 0)),
                  vec(d)],
        out_specs=pl.BlockSpec((1, tm, d), lambda i, j: (i, j, 0)),
        out_shape=jax.ShapeDtypeStruct((b, t, d), F32),
        scratch_shapes=[pltpu.VMEM((HALO + tm, di), F32), pltpu.VMEM((tm, di), F32)],
        compiler_params=_cparams("parallel", "arbitrary"),
        name="conv_prompt",
    )(glu, glu, sz, x, cw, cb.reshape(1, di), lg.reshape(1, di), lb.reshape(1, di), wo_bf, gp.reshape(1, d))


def _conv_sample_body(glu_ref, st_ref, sz_ref, x_ref, cw_ref, cb_ref, lg_ref, lb_ref, wo_ref, gp_ref,
                      out_ref, full_ref, c_ref, *, nb, tq):
    d = c_ref.shape[-1]
    first = HALO - (CONV_W - 1)
    full_ref[:, first:HALO, :] = st_ref[...]
    full_ref[:, HALO:HALO + tq, :] = glu_ref[...].reshape(nb, tq, d)
    for lc in range(d // LANES):
        ln = slice(lc * LANES, (lc + 1) * LANES)
        acc = jnp.zeros((nb, tq, LANES), F32)
        for j in range(CONV_W):
            acc = acc + cw_ref[j:j + 1, ln] * full_ref[:, first + j:first + j + tq, ln]
        c_ref[:, ln] = (acc + cb_ref[:, ln]).reshape(nb * tq, LANES)
    out_ref[...] = _conv_tail(c_ref[...], sz_ref[...], x_ref[...], lg_ref, lb_ref, wo_ref, gp_ref)


def _conv_sample(glu, state, sz, x, cw, cb, lg, lb, wo_bf, gp, nb):
    n_seq = state.shape[0]
    rows, d = x.shape
    di = glu.shape[-1]
    tq = rows // n_seq
    vec = lambda n: pl.BlockSpec((1, n), lambda i: (0, 0))
    return pl.pallas_call(
        functools.partial(_conv_sample_body, nb=nb, tq=tq),
        grid=(n_seq // nb,),
        in_specs=[pl.BlockSpec((nb * tq, di), lambda i: (i, 0)),
                  pl.BlockSpec((nb, CONV_W - 1, di), lambda i: (i, 0, 0)),
                  pl.BlockSpec((nb * tq, di), lambda i: (i, 0)),
                  pl.BlockSpec((nb * tq, d), lambda i: (i, 0)),
                  pl.BlockSpec((CONV_W, di), lambda i: (0, 0)),
                  vec(di), vec(di), vec(di),
                  pl.BlockSpec((di, d), lambda i: (0, 0)),
                  vec(d)],
        out_specs=pl.BlockSpec((nb * tq, d), lambda i: (i, 0)),
        out_shape=jax.ShapeDtypeStruct((rows, d), F32),
        scratch_shapes=[pltpu.VMEM((nb, HALO + tq, di), F32), pltpu.VMEM((nb * tq, di), F32)],
        compiler_params=_cparams("parallel"),
        name="conv_sample",
    )(glu, state, sz, x, cw, cb.reshape(1, di), lg.reshape(1, di), lb.reshape(1, di), wo_bf, gp.reshape(1, d))


def _head_major(col_pair, head):
    return col_pair if head % 2 == 0 else _swap_halves(col_pair)


def _nsa_proj_body(x_ref, gkv_ref, gq_ref, wkv_ref, wq_ref, wz_ref, wg_ref,
                   cmp_ref, slc_ref, win_ref, qa_ref, sz_ref, gate_ref, *aug_refs, seq_len, tm):
    x = x_ref[...]
    xn = x * lax.rsqrt(jnp.mean(x * x, axis=-1, keepdims=True) + EPS)
    hkv = (xn * gkv_ref[...]).astype(BF16)
    hq = (xn * gq_ref[...]).astype(BF16)
    kv = jnp.dot(hkv, wkv_ref[...], preferred_element_type=F32)
    width = 2 * KV_HEADS * HEAD_DIM
    if aug_refs:
        kv_t = [kv[:, b * width:(b + 1) * width].T for b in range(N_BRANCH)]
        cmp_ref[0] = kv_t[0]
        slc_ref[0] = kv_t[1]
        win_ref[0] = kv_t[2]
    else:
        cmp_ref[...] = kv[:, :width]
        slc_ref[...] = kv[:, width:2 * width]
        win_ref[...] = kv[:, 2 * width:]
    lane = _lane((tm, LANES))
    low = lane < HEAD_DIM
    uq = jnp.dot(hq, wq_ref[...], preferred_element_type=F32)
    scale = HEAD_DIM ** -0.5
    for h in range(N_HEADS):
        qh = _head_major(uq[:, (h // 2) * LANES:(h // 2 + 1) * LANES], h)
        qa_ref[:, h * LANES:(h + 1) * LANES] = jnp.where(low, qh * scale, 0.0).astype(BF16)
    z = jnp.dot(hq, wz_ref[...], preferred_element_type=F32)
    sz_ref[...] = z * _sigmoid(z)
    gate_ref[...] = _sigmoid(jnp.dot(hq, wg_ref[...], preferred_element_type=F32))
    if aug_refs:
        ska_ref, sv_ref, wka_ref, wv_ref = aug_refs
        row = pl.program_id(0) * tm + lax.broadcasted_iota(jnp.int32, (tm, LANES), 0)
        blk = (row % seq_len) // L_SLC
        onehot = (lane - HEAD_DIM == blk).astype(F32)
        kw = KV_HEADS * HEAD_DIM
        for h in range(KV_HEADS):
            ks = _head_major(kv[:, width + (h // 2) * LANES:width + (h // 2 + 1) * LANES], h)
            ska_ref[:, h * LANES:(h + 1) * LANES] = jnp.where(low, ks, onehot).astype(BF16)
            kwn = _head_major(kv[:, 2 * width + (h // 2) * LANES:2 * width + (h // 2 + 1) * LANES], h)
            wka_ref[:, h * LANES:(h + 1) * LANES] = jnp.where(low, kwn, 0.0).astype(BF16)
        sv_ref[...] = kv_t[1][kw:].astype(BF16)
        wv_ref[...] = kv_t[2][kw:].astype(BF16)


def _nsa_proj(x, g_kv, g_q, wkv_bf, wq_bf, wz_bf, wg_bf, tm, seq_len=None):
    rows, d = x.shape
    width = 2 * KV_HEADS * HEAD_DIM
    row = lambda n: pl.BlockSpec((tm, n), lambda i: (i, 0))
    full = lambda a: pl.BlockSpec(a.shape, lambda i: (0, 0))
    out_specs = [row(width), row(width), row(width), row(N_HEADS * LANES), row(wz_bf.shape[1]), row(LANES)]
    out_shape = [jax.ShapeDtypeStruct((rows, width), F32)] * 3 + [
        jax.ShapeDtypeStruct((rows, N_HEADS * LANES), BF16),
        jax.ShapeDtypeStruct((rows, wz_bf.shape[1]), F32),
        jax.ShapeDtypeStruct((rows, LANES), F32)]
    if seq_len is not None:
        per_seq = seq_len // tm
        kv_t_spec = pl.BlockSpec((1, width, tm), lambda i: (i // per_seq, 0, i % per_seq))
        out_specs[:N_BRANCH] = [kv_t_spec] * N_BRANCH
        out_shape[:N_BRANCH] = [jax.ShapeDtypeStruct((rows // seq_len, width, seq_len), F32)] * N_BRANCH
        out_specs += [row(KV_HEADS * LANES), pl.BlockSpec((width // 2, tm), lambda i: (0, i))] * 2
        out_shape += [jax.ShapeDtypeStruct((rows, KV_HEADS * LANES), BF16),
                      jax.ShapeDtypeStruct((width // 2, rows), BF16)] * 2
    gkv = g_kv.reshape(1, d)
    gq = g_q.reshape(1, d)
    return pl.pallas_call(
        functools.partial(_nsa_proj_body, seq_len=seq_len, tm=tm),
        grid=(rows // tm,),
        in_specs=[row(d), full(gkv), full(gq), full(wkv_bf), full(wq_bf), full(wz_bf), full(wg_bf)],
        out_specs=out_specs,
        out_shape=out_shape,
        compiler_params=_cparams("parallel"),
        name="nsa_proj",
    )(x, gkv, gq, wkv_bf, wq_bf, wz_bf, wg_bf)


def _compress_body(pt_ref, *refs, n_pages):
    del pt_ref
    page_refs = refs[:n_pages]
    w1_ref, pe_ref, w2_ref, out_ref, a_ref = refs[n_pages:]
    width, page_rows = page_refs[0].shape[1:]
    n_col = width // LANES
    cpp = page_rows // CMP_STRIDE
    n_chunk = n_pages * cpp
    hid = w2_ref.shape[2]
    low = _lane((cpp, LANES)) < HEAD_DIM
    def assemble(c):
        for p in range(n_pages):
            col_t = page_refs[p][0, c * LANES:(c + 1) * LANES, :]
            by_s = pltpu.einshape("(ns)f->(sn)f", col_t.T, n=cpp)
            for s in range(0, CMP_STRIDE, 2):
                b0 = by_s[s * cpp:(s + 1) * cpp]
                b1 = by_s[(s + 1) * cpp:(s + 2) * cpp]
                dst = (slice(p * cpp, (p + 1) * cpp), slice((s // 2) * LANES, (s // 2 + 1) * LANES))
                a_ref[(2 * c,) + dst] = jnp.where(low, b0, _swap_halves(b1))
                a_ref[(2 * c + 1,) + dst] = jnp.where(low, _swap_halves(b0), b1)

    for kv in range(2):
        for c in range(kv * n_col // 2, (kv + 1) * n_col // 2):
            assemble(c)
        w1 = w1_ref[kv]
        pe = jnp.dot(pe_ref[kv].astype(BF16), w1, preferred_element_type=F32)
        pe_term = pe[0:1, :hid] + pe[1:2, hid:]
        a_kv = a_ref[kv * KV_HEADS:(kv + 1) * KV_HEADS].reshape(KV_HEADS * n_chunk, CMP_STRIDE * HEAD_DIM)
        parts = jnp.dot(a_kv.astype(BF16), w1, preferred_element_type=F32)
        for hp in range(KV_HEADS // 2):
            pair = jnp.zeros((n_chunk, LANES), F32)
            for par in range(2):
                part = parts[(2 * hp + par) * n_chunk:(2 * hp + par + 1) * n_chunk]
                pre = part[:, :hid] + pltpu.roll(part[:, hid:], n_chunk - 1, axis=0) + pe_term
                mid = pre * _sigmoid(pre)
                pair = pair + jnp.dot(mid.astype(BF16), w2_ref[kv, par], preferred_element_type=F32)
            col = kv * (KV_HEADS // 2) + hp
            out_ref[0, col * LANES:(col + 1) * LANES, :] = pair.T


def _compress(rows_t, page_table, w1cat_bf, pe8, w2_bf, n_pages, page_rows):
    width = rows_t.shape[1]
    n_chunk = n_pages * page_rows // CMP_STRIDE
    if page_table is None:
        n_seq = rows_t.shape[0]
        page_table = jnp.zeros((1,), jnp.int32)
        page_spec = lambda j: pl.BlockSpec((1, width, page_rows), lambda i, pt: (i, 0, j))
    else:
        n_seq = page_table.shape[0] // n_pages
        page_spec = lambda j: pl.BlockSpec((1, width, page_rows), lambda i, pt: (pt[i * n_pages + j], 0, 0))
    full = lambda a: pl.BlockSpec(a.shape, lambda i, pt: (0,) * a.ndim)
    grid_spec = pltpu.PrefetchScalarGridSpec(
        num_scalar_prefetch=1,
        grid=(n_seq,),
        in_specs=[page_spec(j) for j in range(n_pages)] + [full(w1cat_bf), full(pe8), full(w2_bf)],
        out_specs=pl.BlockSpec((1, width, n_chunk), lambda i, pt: (i, 0, 0)),
        scratch_shapes=[pltpu.VMEM((2 * KV_HEADS, n_chunk, CMP_STRIDE * HEAD_DIM), F32)],
    )
    return pl.pallas_call(
        functools.partial(_compress_body, n_pages=n_pages),
        grid_spec=grid_spec,
        out_shape=jax.ShapeDtypeStruct((n_seq, width, n_chunk), F32),
        compiler_params=_cparams("parallel"),
        name="compress",
    )(page_table, *([rows_t] * n_pages), w1cat_bf, pe8, w2_bf)


def _bias_tile_body(rb_ref, out_ref, *, row_step, lane_step, offset, hi):
    h = pl.program_id(0)
    shape = out_ref.shape[1:]
    d = row_step * lax.broadcasted_iota(jnp.int32, shape, 0) + lane_step * _lane(shape) + offset
    n = jnp.maximum(d, 0)
    nf = jnp.maximum(n, 1).astype(F32)
    large = MAX_EXACT + (jnp.log(nf / MAX_EXACT) / math.log(MAX_DISTANCE / MAX_EXACT)
                         * (NUM_BUCKETS - MAX_EXACT)).astype(jnp.int32)
    large = jnp.minimum(large, NUM_BUCKETS - 1)
    bucket = jnp.where(n < MAX_EXACT, n, large)
    far = rb_ref[NUM_BUCKETS - 1, h]
    val = jnp.zeros(shape, F32)
    for k in range(NUM_BUCKETS - 1):
        val = jnp.where(bucket == k, rb_ref[k, h] - far, val)
    out_ref[0] = jnp.where((d >= 0) & (d <= hi), val, NEG)


def _bias_tile(rel_bias, rows, width, lane_step, offset, hi=1 << 30, row_step=1):
    return pl.pallas_call(
        functools.partial(_bias_tile_body, row_step=row_step, lane_step=lane_step, offset=offset, hi=hi),
        grid=(N_HEADS,),
        in_specs=[pl.BlockSpec(memory_space=pltpu.SMEM)],
        out_specs=pl.BlockSpec((1, rows, width), lambda h: (h, 0, 0)),
        out_shape=jax.ShapeDtypeStruct((N_HEADS, rows, width), F32),
        compiler_params=_cparams("parallel"),
        name="bias_tile",
    )(rel_bias)


_NT = (((1,), (1,)), ((), ()))


def _softmax_rows(s):
    m = jnp.maximum(jnp.max(s, axis=-1, keepdims=True), M_FLOOR)
    p = jnp.exp(s - m)
    return p, jnp.sum(p, axis=-1, keepdims=True)


def _gate_col(gate, col):
    return jnp.sum(jnp.where(_lane(gate.shape) == col, gate, 0.0), axis=-1, keepdims=True)


def _pair_columns(o_even, o_odd, valid_half):
    lo = o_even if valid_half == 0 else _swap_halves(o_even)
    hi = o_odd if valid_half == 1 else _swap_halves(o_odd)
    return jnp.where(_lane(lo.shape) < HEAD_DIM, lo, hi)


def _select_blocks_t(imp_t, tb, n_sel):
    nb, nt = imp_t.shape
    sub = 8
    j = lax.broadcasted_iota(jnp.int32, (nb, nt), 0)
    valid = j <= tb
    forced = (j == 0) | (j == tb) | (j == tb - 1)
    score = jnp.where(valid, jnp.where(forced, BIG, imp_t), -BIG)
    groups = [score[lo:lo + sub] for lo in range(0, nb, sub)]
    jr = lax.broadcasted_iota(jnp.int32, (sub, nt), 0)
    counts = [jnp.zeros((sub, nt), F32) for _ in groups]
    for i in range(nb):
        row = score[i:i + 1]
        for r, grp in enumerate(groups):
            lo = r * sub
            if lo > i:
                one = jnp.where(row >= grp, 1.0, 0.0)
            elif lo + sub - 1 <= i:
                one = jnp.where(row > grp, 1.0, 0.0)
            else:
                one = jnp.where(jr + lo > i, jnp.where(row >= grp, 1.0, 0.0), jnp.where(row > grp, 1.0, 0.0))
            counts[r] = counts[r] + one
    rank = jnp.concatenate(counts, axis=0)
    return jnp.where((rank < n_sel) & valid, 0.0, NEG)


def _select_blocks(imp, tb, n_sel):
    lane = _lane(imp.shape)
    lane_f = lane.astype(F32)
    j = lane - HEAD_DIM
    valid = (j >= 0) & (j <= tb)
    forced = (j == 0) | (j == tb) | (j == tb - 1)
    score = jnp.where(valid, jnp.where(forced, BIG, imp), -BIG)
    score = jnp.where(j >= 0, score, -jnp.inf)
    sel = jnp.zeros(imp.shape, jnp.bool_)
    for _ in range(n_sel):
        best = jnp.max(score, axis=-1, keepdims=True)
        first = jnp.min(jnp.where(score == best, lane_f, 4.0 * LANES), axis=-1, keepdims=True)
        pick = lane_f == first
        sel = sel | pick
        score = jnp.where(pick, -jnp.inf, score)
    return jnp.where(sel & valid, 0.0, jnp.where(j >= 0, NEG, 0.0))


def _cmp_prompt_body(qa_ref, kc_ref, vc_ref, m_ref, cb_ref, gate_ref, oc_ref, qsel_ref, qflag_ref, *, n_key):
    qt = pl.program_id(1)
    start = pl.multiple_of(8 * qt + 8, 8)
    kwin = kc_ref[0, pl.ds(start, n_key), :]
    vwin = vc_ref[0, pl.ds(start, n_key), :]
    mwin_t = m_ref[pl.ds(start, n_key), :].T
    feat = lax.broadcasted_iota(jnp.int32, (LANES, Q_BLK), 0)
    pos = qt * Q_BLK + lax.broadcasted_iota(jnp.int32, (1, Q_BLK), 1)
    tb = pos // L_SLC
    gate = gate_ref[...]
    n_pair = GROUP // 2
    for kvh in range(KV_HEADS):
        ka = kwin[:, kvh * LANES:(kvh + 1) * LANES].astype(BF16)
        vpt = vwin[:, (kvh // 2) * LANES:(kvh // 2 + 1) * LANES].T
        vpt = vpt[(kvh % 2) * HEAD_DIM:(kvh % 2 + 1) * HEAD_DIM].astype(BF16)
        qts = [qa_ref[0, :, h * LANES:(h + 1) * LANES].astype(F32).T for h in range(kvh * GROUP, (kvh + 1) * GROUP)]
        pc_sum = jnp.zeros((n_key, Q_BLK), F32)
        outs = []
        for gp in range(n_pair):
            qa = jnp.concatenate([jnp.where(feat == HEAD_DIM, NEG, qts[2 * gp + e]) for e in range(2)], axis=1)
            qa = qa.astype(BF16)
            qflag_ref[0, kvh, :, 2 * gp * Q_BLK:(2 * gp + 2) * Q_BLK] = qa
            s = jnp.dot(ka, qa, preferred_element_type=F32)
            s = s + cb_ref[kvh, :, 2 * gp * Q_BLK:(2 * gp + 2) * Q_BLK]
            m = jnp.maximum(jnp.max(s, axis=0, keepdims=True), M_FLOOR)
            p = jnp.exp(s - m)
            inv = 1.0 / jnp.maximum(jnp.sum(p, axis=0, keepdims=True), 1e-30)
            pc = p * inv
            pc_sum = pc_sum + pc[:, :Q_BLK] + pc[:, Q_BLK:]
            o = jnp.dot(vpt, p.astype(BF16), preferred_element_type=F32) * inv
            _store_chain(oc_ref, o, gate, kvh * GROUP + 2 * gp, kvh * n_pair + gp)
        imp_t = jnp.dot(mwin_t, pc_sum, preferred_element_type=F32, precision=lax.Precision.HIGHEST)
        selneg_t = _select_blocks_t(imp_t[HEAD_DIM:], tb, N_SEL)
        for g in range(GROUP):
            qsel_ref[0, kvh, :, g * Q_BLK:(g + 1) * Q_BLK] = jnp.concatenate(
                [qts[g][:HEAD_DIM], selneg_t], axis=0).astype(BF16)


def _cmp_prompt(qa, kc_pad, vc_pad, m_pad, cb, gate, batch, seq):
    nq = seq // Q_BLK
    n_key = kc_pad.shape[1] // 2
    d_out = N_HEADS * HEAD_DIM
    return pl.pallas_call(
        functools.partial(_cmp_prompt_body, n_key=n_key),
        grid=(batch, nq),
        in_specs=[pl.BlockSpec((1, Q_BLK, N_HEADS * LANES), lambda b, t: (b, t, 0)),
                  pl.BlockSpec((1,) + kc_pad.shape[1:], lambda b, t: (b, 0, 0)),
                  pl.BlockSpec((1,) + vc_pad.shape[1:], lambda b, t: (b, 0, 0)),
                  pl.BlockSpec(m_pad.shape, lambda b, t: (0, 0)),
                  pl.BlockSpec(cb.shape, lambda b, t: (0, 0, 0)),
                  pl.BlockSpec((Q_BLK, LANES), lambda b, t: (b * nq + t, 0))],
        out_specs=[pl.BlockSpec((1, Q_BLK, d_out), lambda b, t: (b, t, 0))]
        + [pl.BlockSpec((1, KV_HEADS, LANES, GROUP * Q_BLK), lambda b, t: (b * nq + t, 0, 0, 0))] * 2,
        out_shape=[jax.ShapeDtypeStruct((batch, seq, d_out), F32)]
        + [jax.ShapeDtypeStruct((batch * nq, KV_HEADS, LANES, GROUP * Q_BLK), BF16)] * 2,
        compiler_params=_cparams("parallel", "parallel"),
        name="cmp_prompt",
    )(qa.reshape(batch, seq, -1), kc_pad, vc_pad, m_pad, cb, gate)


KEY_TILE = 256
COL_CHAIN = 256


def _store_chain(out_ref, o_t, gate, gate_col0, lane_col):
    pair = jnp.concatenate([o_t[:, :Q_BLK], o_t[:, Q_BLK:]], axis=0).T
    g = jnp.where(_lane(pair.shape) < HEAD_DIM, _gate_col(gate, gate_col0), _gate_col(gate, gate_col0 + 1))
    out_ref[0, :, lane_col * LANES:(lane_col + 1) * LANES] = pair * g


_PAIR_CHAINS = [(e, c) for e in range(2) for c in range(GROUP * Q_BLK // COL_CHAIN)]


def _slc_prompt_body(q_ref, k_ref, v_ref, sb_ref, gate_ref, out_ref, m_ref, l_ref, acc_ref):
    pr = pl.program_id(1)
    qt = pl.program_id(2)
    m_ref[...] = jnp.full(m_ref.shape, M_FLOOR, F32)
    l_ref[...] = jnp.zeros(l_ref.shape, F32)
    acc_ref[...] = jnp.zeros(acc_ref.shape, F32)
    last = (qt * Q_BLK) // KEY_TILE
    cs = lambda c: slice(c * COL_CHAIN, (c + 1) * COL_CHAIN)

    def make_step(with_bias):
        def step(kt, carry):
            base = pl.multiple_of(kt * KEY_TILE, KEY_TILE)
            case = qt % 2 + 2 * (last - kt)
            ks = [k_ref[0, pl.ds(base, KEY_TILE), e * LANES:(e + 1) * LANES] for e in range(2)]
            ss = [jnp.dot(ks[e], q_ref[0, e, :, cs(c)], preferred_element_type=F32) for e, c in _PAIR_CHAINS]
            if with_bias:
                ss = [s + sb_ref[case, e, :, cs(c)] for s, (e, c) in zip(ss, _PAIR_CHAINS)]
            ps, alphas = [], []
            for i, s in enumerate(ss):
                m_old = m_ref[i]
                m_new = jnp.maximum(m_old, jnp.max(s, axis=0, keepdims=True))
                alpha = jnp.exp(m_old - m_new)
                p = jnp.exp(s - m_new)
                l_ref[i] = alpha * l_ref[i] + jnp.sum(p, axis=0, keepdims=True)
                m_ref[i] = m_new
                ps.append(p.astype(BF16))
                alphas.append(alpha)
            for i, (e, c) in enumerate(_PAIR_CHAINS):
                vt = v_ref[e * HEAD_DIM:(e + 1) * HEAD_DIM, pl.ds(base, KEY_TILE)]
                acc_ref[i] = alphas[i] * acc_ref[i] + jnp.dot(vt, ps[i], preferred_element_type=F32)
            return carry
        return step

    n_far = jnp.maximum(last - 1, 0)
    lax.fori_loop(0, n_far, make_step(False), 0)
    lax.fori_loop(n_far, last + 1, make_step(True), 0)
    gate = gate_ref[...]
    for i, (e, c) in enumerate(_PAIR_CHAINS):
        o = acc_ref[i] * (1.0 / jnp.maximum(l_ref[i], 1e-30))
        _store_chain(out_ref, o, gate, N_HEADS + (2 * pr + e) * GROUP + 2 * c, i)


def _slc_prompt(qsel_t, ska, sv_t, sb_t, gate, batch, seq):
    nq = seq // Q_BLK
    cols = GROUP * Q_BLK
    d_out = N_HEADS * HEAD_DIM
    n_chain = len(_PAIR_CHAINS)
    return pl.pallas_call(
        _slc_prompt_body,
        grid=(batch, KV_HEADS // 2, nq),
        in_specs=[pl.BlockSpec((1, 2, LANES, cols), lambda b, p, t: (b * nq + t, p, 0, 0)),
                  pl.BlockSpec((1, seq, 2 * LANES), lambda b, p, t: (b, 0, p)),
                  pl.BlockSpec((2 * HEAD_DIM, seq), lambda b, p, t: (p, b)),
                  pl.BlockSpec((4, 2, KEY_TILE, cols), lambda b, p, t: (0, p, 0, 0)),
                  pl.BlockSpec((Q_BLK, LANES), lambda b, p, t: (b * nq + t, 0))],
        out_specs=pl.BlockSpec((1, Q_BLK, 2 * GROUP * HEAD_DIM), lambda b, p, t: (b, t, p)),
        out_shape=jax.ShapeDtypeStruct((batch, seq, d_out), F32),
        scratch_shapes=[pltpu.VMEM((n_chain, 1, COL_CHAIN), F32), pltpu.VMEM((n_chain, 1, COL_CHAIN), F32),
                        pltpu.VMEM((n_chain, HEAD_DIM, COL_CHAIN), F32)],
        compiler_params=_cparams("parallel", "parallel", "arbitrary"),
        name="slc_prompt",
    )(qsel_t, ska.reshape(batch, seq, -1), sv_t, sb_t, gate)


def _win_prompt_body(q_ref, k_ref, v_ref, wb_ref, gate_ref, out_ref, *, n_key):
    pr = pl.program_id(1)
    qt = pl.program_id(2)
    base = pl.multiple_of(qt * Q_BLK, Q_BLK)
    cs = lambda c: slice(c * COL_CHAIN, (c + 1) * COL_CHAIN)
    ks = [k_ref[0, pl.ds(base, n_key), e * LANES:(e + 1) * LANES] for e in range(2)]
    ss = [jnp.dot(ks[e], q_ref[0, e, :, cs(c)], preferred_element_type=F32) + wb_ref[e, :, cs(c)]
          for e, c in _PAIR_CHAINS]
    ps, invs = [], []
    for s in ss:
        m = jnp.maximum(jnp.max(s, axis=0, keepdims=True), M_FLOOR)
        p = jnp.exp(s - m)
        invs.append(1.0 / jnp.maximum(jnp.sum(p, axis=0, keepdims=True), 1e-30))
        ps.append(p.astype(BF16))
    gate = gate_ref[...]
    for i, (e, c) in enumerate(_PAIR_CHAINS):
        vt = v_ref[e * HEAD_DIM:(e + 1) * HEAD_DIM, pl.ds(base, n_key)]
        o = jnp.dot(vt, ps[i], preferred_element_type=F32) * invs[i]
        _store_chain(out_ref, o, gate, 2 * N_HEADS + (2 * pr + e) * GROUP + 2 * c, i)


def _win_prompt(q_t, wka_pad, wv_t_pad, wb_t, gate, batch, seq):
    nq = seq // Q_BLK
    n_key = WINDOW + Q_BLK
    cols = GROUP * Q_BLK
    d_out = N_HEADS * HEAD_DIM
    padded = wka_pad.shape[1]
    return pl.pallas_call(
        functools.partial(_win_prompt_body, n_key=n_key),
        grid=(batch, KV_HEADS // 2, nq),
        in_specs=[pl.BlockSpec((1, 2, LANES, cols), lambda b, p, t: (b * nq + t, p, 0, 0)),
                  pl.BlockSpec((1, padded, 2 * LANES), lambda b, p, t: (b, 0, p)),
                  pl.BlockSpec((2 * HEAD_DIM, padded), lambda b, p, t: (p, b)),
                  pl.BlockSpec((2, n_key, cols), lambda b, p, t: (p, 0, 0)),
                  pl.BlockSpec((Q_BLK, LANES), lambda b, p, t: (b * nq + t, 0))],
        out_specs=pl.BlockSpec((1, Q_BLK, 2 * GROUP * HEAD_DIM), lambda b, p, t: (b, t, p)),
        out_shape=jax.ShapeDtypeStruct((batch, seq, d_out), F32),
        compiler_params=_cparams("parallel", "parallel", "parallel"),
        name="win_prompt",
    )(q_t, wka_pad, wv_t_pad, wb_t, gate)


def _nsa_out_body(oc_ref, os_ref, ow_ref, sz_ref, x_ref, wo_ref, gp_ref, out_ref):
    d = oc_ref.shape[-1]
    y = oc_ref[...] * sz_ref[:, :d] + os_ref[...] * sz_ref[:, d:2 * d] + ow_ref[...] * sz_ref[:, 2 * d:]
    o = jnp.dot(y.astype(BF16), wo_ref[...], preferred_element_type=F32)
    out_ref[...] = x_ref[...] + _rms(o, gp_ref[...])


def _nsa_out(oc, os_, ow, sz, x, wo_bf, gp, tm):
    rows, d = x.shape
    dq = oc.shape[-1]
    row = lambda n: pl.BlockSpec((tm, n), lambda i: (i, 0))
    return pl.pallas_call(
        _nsa_out_body,
        grid=(rows // tm,),
        in_specs=[row(dq), row(dq), row(dq), row(N_BRANCH * dq), row(d),
                  pl.BlockSpec((dq, d), lambda i: (0, 0)), pl.BlockSpec((1, d), lambda i: (0, 0))],
        out_specs=row(d),
        out_shape=jax.ShapeDtypeStruct((rows, d), F32),
        compiler_params=_cparams("parallel"),
        name="nsa_out",
    )(oc, os_, ow, sz, x, wo_bf, gp.reshape(1, d))


def _sample_attn_body(pt_ref, *refs, n_pages, tq, w_keep):
    del pt_ref
    page_refs = refs[:n_pages]
    (qa_ref, kvc_ref, snew_ref, wst_ref, wnew_ref, m_ref, e_ref, cbs_ref, sbs_ref, wbs_ref, gate_ref,
     oc_ref, os_ref, ow_ref, ksel_ref, kwin_ref) = refs[n_pages:]
    n_past = n_pages * PAGE_SIZE
    n_sk = ksel_ref.shape[0]
    n_wk = kwin_ref.shape[0]
    width = ksel_ref.shape[1]
    kw = width // 2
    for p in range(n_pages):
        ksel_ref[p * PAGE_SIZE:(p + 1) * PAGE_SIZE, :] = page_refs[p][0]
    ksel_ref[n_past:n_past + tq, :] = snew_ref[...]
    ksel_ref[n_past + tq:, :] = jnp.zeros((n_sk - n_past - tq, width), F32)
    kwin_ref[0:w_keep, :] = wst_ref[0]
    kwin_ref[w_keep:w_keep + tq, :] = wnew_ref[...]
    kwin_ref[w_keep + tq:, :] = jnp.zeros((n_wk - w_keep - tq, width), F32)
    pos = n_past + lax.broadcasted_iota(jnp.int32, (tq, 1), 0)
    tb = pos // L_SLC
    gate = gate_ref[...]
    rows = GROUP * tq

    def attend(q, k, v, bias, extra=None):
        s = lax.dot_general(q, k, _NT, preferred_element_type=F32)
        if extra is not None:
            s = s + extra
        s = (s.reshape(GROUP, tq, s.shape[-1]) + bias).reshape(rows, s.shape[-1])
        p, l = _softmax_rows(s)
        inv = 1.0 / jnp.maximum(l, 1e-30)
        return p, inv, jnp.dot(p.astype(BF16), v, preferred_element_type=F32)

    def store(out_ref, o, kvh, branch):
        outs = [o[g * tq:(g + 1) * tq] * _gate_col(gate, branch * N_HEADS + kvh * GROUP + g) for g in range(GROUP)]
        for gp in range(GROUP // 2):
            col = kvh * (GROUP // 2) + gp
            out_ref[:, col * LANES:(col + 1) * LANES] = _pair_columns(outs[2 * gp], outs[2 * gp + 1], kvh % 2)

    def pair_cols(ref, pr):
        kcol = slice(pr * LANES, (pr + 1) * LANES)
        vcol = slice(kw + pr * LANES, kw + (pr + 1) * LANES)
        if len(ref.shape) == 3:
            return ref[0, :, kcol].astype(BF16), ref[0, :, vcol].astype(BF16)
        return ref[:, kcol].astype(BF16), ref[:, vcol].astype(BF16)

    qs, imps = [], []
    for kvh in range(KV_HEADS):
        q = jnp.concatenate([qa_ref[:, h * LANES:(h + 1) * LANES]
                             for h in range(kvh * GROUP, (kvh + 1) * GROUP)], axis=0)
        qs.append((q if kvh % 2 == 0 else _swap_halves(q)).astype(BF16))
    for pr in range(KV_HEADS // 2):
        kc, vc = pair_cols(kvc_ref, pr)
        for kvh in (2 * pr, 2 * pr + 1):
            p, inv, o = attend(qs[kvh], kc, vc, cbs_ref[kvh * GROUP:(kvh + 1) * GROUP])
            store(oc_ref, o * inv, kvh, 0)
            pc_sum = jnp.sum((p * inv).reshape(GROUP, tq, p.shape[-1]), axis=0)
            imps.append(jnp.dot(pc_sum, m_ref[...], preferred_element_type=F32, precision=lax.Precision.HIGHEST))
    selneg = _select_blocks(jnp.concatenate(imps, axis=0), jnp.concatenate([tb] * KV_HEADS, axis=0), N_SEL)
    for pr in range(KV_HEADS // 2):
        ksl, vsl = pair_cols(ksel_ref, pr)
        kwn, vwn = pair_cols(kwin_ref, pr)
        for kvh in (2 * pr, 2 * pr + 1):
            heads = slice(kvh * GROUP, (kvh + 1) * GROUP)
            qm = jnp.concatenate([selneg[kvh * tq:(kvh + 1) * tq]] * GROUP, axis=0).astype(BF16)
            block_mask = jnp.dot(qm, e_ref[...], preferred_element_type=F32)
            _, inv, o = attend(qs[kvh], ksl, vsl, sbs_ref[heads], block_mask)
            store(os_ref, o * inv, kvh, 1)
            _, inv, o = attend(qs[kvh], kwn, vwn, wbs_ref[heads])
            store(ow_ref, o * inv, kvh, 2)


def _sample_attn(qa, kvc, pages, page_table, slc_new, win_state, win_new, m_s, e_s, cbs, sbs, wbs, gate,
                 n_seq, tq, n_pages):
    width = pages.shape[-1]
    n_sk = sbs.shape[-1]
    n_wk = wbs.shape[-1]
    w_keep = win_state.shape[1]
    d_out = N_HEADS * HEAD_DIM
    page_spec = lambda j: pl.BlockSpec((1, PAGE_SIZE, width), lambda i, pt: (pt[i * n_pages + j], 0, 0))
    full = lambda a: pl.BlockSpec(a.shape, lambda i, pt: (0,) * a.ndim)
    row = lambda n: pl.BlockSpec((tq, n), lambda i, pt: (i, 0))
    grid_spec = pltpu.PrefetchScalarGridSpec(
        num_scalar_prefetch=1,
        grid=(n_seq,),
        in_specs=[page_spec(j) for j in range(n_pages)] + [
            row(N_HEADS * LANES),
            pl.BlockSpec((1,) + kvc.shape[1:], lambda i, pt: (i, 0, 0)),
            row(width),
            pl.BlockSpec((1, w_keep, width), lambda i, pt: (i, 0, 0)),
            row(width),
            full(m_s), full(e_s), full(cbs), full(sbs), full(wbs),
            row(LANES)],
        out_specs=[row(d_out)] * 3,
        scratch_shapes=[pltpu.VMEM((n_sk, width), F32), pltpu.VMEM((n_wk, width), F32)],
    )
    return pl.pallas_call(
        functools.partial(_sample_attn_body, n_pages=n_pages, tq=tq, w_keep=w_keep),
        grid_spec=grid_spec,
        out_shape=[jax.ShapeDtypeStruct((n_seq * tq, d_out), F32)] * 3,
        compiler_params=_cparams("parallel"),
        name="sample_attn",
    )(page_table, *([pages] * n_pages), qa, kvc, slc_new, win_state, win_new, m_s, e_s, cbs, sbs, wbs, gate)


def _decode_attn_body(pt_ref, *refs, n_pages, tq, w_keep):
    del pt_ref
    page_refs = refs[:n_pages]
    (qa_ref, kvc_ref, snew_ref, wst_ref, wnew_ref, mt_ref, e_ref, cbs_ref, sbs_ref, wbs_ref, gate_ref,
     oc_ref, os_ref, ow_ref, kt_ref, vt_ref, kwt_ref, vwt_ref) = refs[n_pages:]
    n_past = n_pages * PAGE_SIZE
    kw = KV_HEADS * HEAD_DIM
    rows = N_HEADS * tq

    def new_tile(ref):
        return jnp.concatenate([ref[...], jnp.zeros((LANES - tq, 2 * kw), F32)], axis=0).T

    def fill(k_dst, v_dst, col0, tile):
        k_dst[:, col0:col0 + tile.shape[1]] = tile[:kw].astype(BF16)
        v_dst[:, col0:col0 + tile.shape[1]] = tile[kw:].astype(BF16)

    for p in range(n_pages):
        fill(kt_ref, vt_ref, p * PAGE_SIZE, page_refs[p][0])
    fill(kt_ref, vt_ref, n_past, new_tile(snew_ref))
    fill(kwt_ref, vwt_ref, 0, wst_ref[0])
    fill(kwt_ref, vwt_ref, w_keep, new_tile(wnew_ref))

    zero = jnp.zeros((tq, LANES), F32)
    tiles = [[], []]
    for h in range(N_HEADS):
        kvh = h // GROUP
        piece = qa_ref[:, h * LANES:(h + 1) * LANES]
        piece = piece if kvh % 2 == 0 else _swap_halves(piece)
        for ct in range(2):
            tiles[ct].append(piece if kvh // 2 == ct else zero)
    qbd = jnp.concatenate([jnp.concatenate(t, axis=0) for t in tiles], axis=1).astype(BF16)
    gate = gate_ref[...]

    def branch(k_t, v_t, bias, extra=None):
        s = jnp.dot(qbd, k_t, preferred_element_type=F32) + bias
        if extra is not None:
            s = s + extra
        p, l = _softmax_rows(s)
        inv = 1.0 / jnp.maximum(l, 1e-30)
        o_t = lax.dot_general(v_t, p.astype(BF16), _NT, preferred_element_type=F32)
        return p, inv, o_t.T

    def store(out_ref, o, scale):
        o = o * scale
        for kvh in range(KV_HEADS):
            tile = o[:, (kvh // 2) * LANES:(kvh // 2 + 1) * LANES]
            for gp in range(GROUP // 2):
                h0 = kvh * GROUP + 2 * gp
                col = kvh * (GROUP // 2) + gp
                out_ref[:, col * LANES:(col + 1) * LANES] = _pair_columns(
                    tile[h0 * tq:(h0 + 1) * tq], tile[(h0 + 1) * tq:(h0 + 2) * tq], kvh % 2)

    p, inv, o = branch(kvc_ref[0, :kw, :].astype(BF16), kvc_ref[0, kw:, :].astype(BF16), cbs_ref[...])
    store(oc_ref, o, inv * _gate_col(gate, 0))
    pc = p * inv
    pc_sum = jnp.concatenate(
        [sum(pc[(kvh * GROUP + g) * tq:(kvh * GROUP + g + 1) * tq] for g in range(GROUP)) for kvh in range(KV_HEADS)],
        axis=0)
    imp_t = lax.dot_general(mt_ref[...], pc_sum, _NT, preferred_element_type=F32, precision=lax.Precision.HIGHEST)
    n_col = KV_HEADS * tq
    nb_pad = -(-(-(-(n_past + tq) // L_SLC)) // 8) * 8
    tb = (n_past + lax.broadcasted_iota(jnp.int32, (1, n_col), 1) % tq) // L_SLC
    selneg_t = _select_blocks_t(imp_t[HEAD_DIM:HEAD_DIM + nb_pad], tb, N_SEL)
    sel_t = jnp.concatenate([jnp.zeros((HEAD_DIM, n_col), F32), selneg_t,
                             jnp.zeros((LANES - HEAD_DIM - nb_pad, n_col), F32)], axis=0)
    sel = jnp.concatenate([sel_t, jnp.zeros((LANES, LANES - n_col), F32)], axis=1).T
    sel = jnp.concatenate([sel[kvh * tq:(kvh + 1) * tq] for kvh in range(KV_HEADS) for _ in range(GROUP)], axis=0)
    block_mask = jnp.dot(sel.astype(BF16), e_ref[...], preferred_element_type=F32)
    _, inv, o = branch(kt_ref[...], vt_ref[...], sbs_ref[...], block_mask)
    store(os_ref, o, inv * _gate_col(gate, 1))
    _, inv, o = branch(kwt_ref[...], vwt_ref[...], wbs_ref[...])
    store(ow_ref, o, inv * _gate_col(gate, 2))


def _decode_attn(qa, kvc_t, pages_t, page_table, slc_new, win_state_t, win_new, m_t, e_s, cbs, sbs, wbs, gate_r,
                 n_seq, tq, n_pages):
    width = pages_t.shape[1]
    n_sk = sbs.shape[-1]
    n_wk = wbs.shape[-1]
    w_keep = win_state_t.shape[2]
    d_out = N_HEADS * HEAD_DIM
    rows = N_HEADS * tq
    page_spec = lambda j: pl.BlockSpec((1, width, PAGE_SIZE), lambda i, pt: (pt[i * n_pages + j], 0, 0))
    full = lambda a: pl.BlockSpec(a.shape, lambda i, pt: (0,) * a.ndim)
    row = lambda n: pl.BlockSpec((tq, n), lambda i, pt: (i, 0))
    grid_spec = pltpu.PrefetchScalarGridSpec(
        num_scalar_prefetch=1,
        grid=(n_seq,),
        in_specs=[page_spec(j) for j in range(n_pages)] + [
            row(N_HEADS * LANES),
            pl.BlockSpec((1,) + kvc_t.shape[1:], lambda i, pt: (i, 0, 0)),
            row(width),
            pl.BlockSpec((1, width, w_keep), lambda i, pt: (i, 0, 0)),
            row(width),
            full(m_t), full(e_s), full(cbs), full(sbs), full(wbs),
            pl.BlockSpec((rows, LANES), lambda i, pt: (i, 0))],
        out_specs=[row(d_out)] * 3,
        scratch_shapes=[pltpu.VMEM((width // 2, n_sk), BF16), pltpu.VMEM((width // 2, n_sk), BF16),
                        pltpu.VMEM((width // 2, n_wk), BF16), pltpu.VMEM((width // 2, n_wk), BF16)],
    )
    return pl.pallas_call(
        functools.partial(_decode_attn_body, n_pages=n_pages, tq=tq, w_keep=w_keep),
        grid_spec=grid_spec,
        out_shape=[jax.ShapeDtypeStruct((n_seq * tq, d_out), F32)] * 3,
        compiler_params=_cparams("parallel"),
        name="decode_attn",
    )(page_table, *([pages_t] * n_pages), qa, kvc_t, slc_new, win_state_t, win_new, m_t, e_s, cbs, sbs, wbs, gate_r)


def _overlap_matrix(n_rows, row0, n_cmp, n_blk):
    import numpy as np
    m = np.zeros((n_rows, LANES), np.float32)
    cs = np.arange(n_cmp)[:, None] * CMP_STRIDE
    js = np.arange(n_blk)[None, :] * L_SLC
    m[row0:row0 + n_cmp, HEAD_DIM:HEAD_DIM + n_blk] = (cs <= js + L_SLC - 1) & (cs + L_CMP - 1 >= js)
    return jnp.asarray(m)


def kernel(x_prompt, x_sample, state_conv, cache_cmp, cache_slc, state_win, page_table, rel_bias, a_norm_pre, a_w_in, a_conv_w, a_conv_b, a_ln_g, a_ln_b, a_w_out, a_norm_post, kv_norm, w_kv, cmp_pe, cmp_w1, cmp_w2, b_norm_pre, b_w_in, b_w_out, b_norm_post):
    import numpy as np
    bp, tp, d = x_prompt.shape
    bd, tq, _ = x_sample.shape
    n_pages = page_table.shape[1]
    past = n_pages * PAGE_SIZE
    w_keep = state_win.shape[1]
    width = 2 * KV_HEADS * HEAD_DIM
    kw = KV_HEADS * HEAD_DIM
    dq = N_HEADS * HEAD_DIM
    assert b_w_in.shape[0] == 1 and tp % (2 * KEY_TILE) == 0 and tp // L_SLC <= HEAD_DIM
    tm = 256
    tm_s = min(tm, bd * tq)

    xp = x_prompt
    xs = x_sample.reshape(bd * tq, d)
    conv_p, conv_s = [], []
    for l in range(a_w_in.shape[0]):
        w_in = a_w_in[l].astype(BF16)
        w_out = a_w_out[l].astype(BF16)
        di = a_w_out.shape[1]
        tail = (a_conv_w[l], a_conv_b[l], a_ln_g[l], a_ln_b[l], w_out, a_norm_post[l])
        glu, sz = _glu_proj(xp.reshape(bp * tp, d), a_norm_pre[l], w_in, tm)
        glu = glu.reshape(bp, tp, di)
        xp = _conv_prompt(glu, sz.reshape(bp, tp, di), xp, *tail, tm)
        conv_p.append(glu[:, -(CONV_W - 1):])
        glu, sz = _glu_proj(xs, a_norm_pre[l], w_in, tm_s)
        xs = _conv_sample(glu, state_conv[l], sz, xs, *tail, 16)
        conv_s.append(jnp.concatenate([state_conv[l], glu.reshape(bd, tq, di)], axis=1)[:, -(CONV_W - 1):])

    bw = b_w_in[0]
    n_gate = N_BRANCH * N_HEADS
    wq = bw[:, :dq].astype(BF16)
    wz = bw[:, dq:dq * (1 + N_BRANCH)].astype(BF16)
    wg = jnp.pad(bw[:, dq * (1 + N_BRANCH):], ((0, 0), (0, LANES - n_gate))).astype(BF16)
    wkv = w_kv.astype(BF16)
    xp2 = xp.reshape(bp * tp, d)
    (cmp_p, slc_p, win_p, qa_p, sz_p, gate_p, ska, sv, wka, wv) = _nsa_proj(
        xp2, kv_norm, b_norm_pre[0], wkv, wq, wz, wg, tm, seq_len=tp)
    cmp_s, slc_s, win_s, qa_s, sz_s, gate_s = _nsa_proj(xs, kv_norm, b_norm_pre[0], wkv, wq, wz, wg, tm_s)
    qa_s = qa_s.astype(F32)

    w1cat = jnp.concatenate([cmp_w1[:, :CMP_STRIDE * HEAD_DIM], cmp_w1[:, CMP_STRIDE * HEAD_DIM:]], axis=2).astype(BF16)
    pe8 = jnp.pad(cmp_pe.reshape(2, L_CMP // CMP_STRIDE, CMP_STRIDE * HEAD_DIM), ((0, 0), (0, 6), (0, 0)))
    zero = jnp.zeros_like(cmp_w2)
    w2h = jnp.stack([jnp.concatenate([cmp_w2, zero], axis=2), jnp.concatenate([zero, cmp_w2], axis=2)], axis=1).astype(BF16)
    prompt_page = min(1024, tp)
    pp = tp // prompt_page
    kvc_p = _compress(cmp_p, None, w1cat, pe8, w2h, pp, prompt_page).transpose(0, 2, 1)
    pt_flat = page_table.reshape(-1).astype(jnp.int32)
    cmp_pages_t = cache_cmp.transpose(0, 2, 3, 4, 1).reshape(-1, width, PAGE_SIZE)
    kvc_s = _compress(cmp_pages_t, pt_flat, w1cat, pe8, w2h, n_pages, PAGE_SIZE)

    n_chunk_p = kvc_p.shape[1]
    nc_p = n_chunk_p - L_CMP // CMP_STRIDE + 1
    real = (jnp.arange(n_chunk_p) < nc_p)[None, :, None]
    flag = jnp.zeros((LANES - HEAD_DIM,), F32).at[0].set(1.0)
    kc4 = jnp.where(real, kvc_p[:, :, :kw], 0.0).reshape(bp, n_chunk_p, KV_HEADS, HEAD_DIM)
    aug = jnp.where(real[..., None], 0.0, flag) * jnp.ones((bp, n_chunk_p, KV_HEADS, 1), F32)
    kc_real = jnp.concatenate([kc4, aug], axis=-1).reshape(bp, n_chunk_p, KV_HEADS * LANES)
    pad_row = jnp.concatenate([jnp.zeros((HEAD_DIM,), F32), flag])
    kc_front = jnp.broadcast_to(jnp.tile(pad_row, KV_HEADS), (bp, n_chunk_p, KV_HEADS * LANES))
    kc_pad = jnp.concatenate([kc_front, kc_real], axis=1)
    vc_pad = jnp.concatenate([jnp.zeros((bp, n_chunk_p, kw), F32), jnp.where(real, kvc_p[:, :, kw:], 0.0)], axis=1)
    m_pad = _overlap_matrix(2 * n_chunk_p, n_chunk_p, nc_p, tp // L_SLC)
    cb = _bias_tile(rel_bias, n_chunk_p, Q_BLK, 1, CMP_STRIDE * (n_chunk_p - Q_BLK // CMP_STRIDE) - (L_CMP - 1),
                    row_step=-CMP_STRIDE)
    cb = cb.reshape(KV_HEADS, GROUP, n_chunk_p, Q_BLK).transpose(0, 2, 1, 3).reshape(KV_HEADS, n_chunk_p, GROUP * Q_BLK)
    oc_p, qsel, qflag = _cmp_prompt(qa_p, kc_pad, vc_pad, m_pad, cb, gate_p, bp, tp)
    sb = jnp.stack([_bias_tile(rel_bias, KEY_TILE, Q_BLK, 1, off, row_step=-1)
                    for off in (0, Q_BLK, 2 * Q_BLK, 3 * Q_BLK)])
    sb = sb.reshape(4, KV_HEADS, GROUP, KEY_TILE, Q_BLK).transpose(0, 1, 3, 2, 4).reshape(4, KV_HEADS, KEY_TILE, GROUP * Q_BLK)
    os_p = _slc_prompt(qsel, ska, sv, sb, gate_p, bp, tp)
    n_wkey = WINDOW + Q_BLK
    wb = _bias_tile(rel_bias, n_wkey, Q_BLK, 1, WINDOW, WINDOW, row_step=-1)
    wb = wb.reshape(KV_HEADS, GROUP, n_wkey, Q_BLK).transpose(0, 2, 1, 3).reshape(KV_HEADS, n_wkey, GROUP * Q_BLK)
    wka_front = jnp.broadcast_to(jnp.tile(pad_row, KV_HEADS).astype(BF16), (bp, WINDOW, KV_HEADS * LANES))
    wka_pad = jnp.concatenate([wka_front, wka.reshape(bp, tp, -1)], axis=1)
    wv_pad = jnp.concatenate([jnp.zeros((kw, bp, WINDOW), BF16), wv.reshape(kw, bp, tp)], axis=2)
    ow_p = _win_prompt(qflag, wka_pad, wv_pad.reshape(kw, bp * (WINDOW + tp)), wb, gate_p, bp, tp)
    wo = b_w_out[0].astype(BF16)
    y_p = _nsa_out(oc_p.reshape(bp * tp, dq), os_p.reshape(bp * tp, dq), ow_p.reshape(bp * tp, dq), sz_p, xp2,
                   wo, b_norm_post[0], tm)

    n_chunk_s = kvc_s.shape[2]
    nc_s = n_chunk_s - L_CMP // CMP_STRIDE + 1
    nb_s = -(-(past + tq) // L_SLC)
    n_sk = past + LANES
    n_wk = w_keep + LANES
    m_t = _overlap_matrix(n_chunk_s, 0, nc_s, nb_s).T
    e_np = np.zeros((LANES, n_sk), np.float32)
    e_np[HEAD_DIM + np.arange(n_sk) // L_SLC, np.arange(n_sk)] = 1.0
    e_s = jnp.asarray(e_np, BF16)
    rows_s = N_HEADS * tq
    cbs = _bias_tile(rel_bias, tq, n_chunk_s, -CMP_STRIDE, past - (L_CMP - 1)).reshape(rows_s, n_chunk_s)
    sbs = _bias_tile(rel_bias, tq, n_sk, -1, past).reshape(rows_s, n_sk)
    wbs = _bias_tile(rel_bias, tq, n_wk, -1, w_keep, WINDOW).reshape(rows_s, n_wk)
    gate_r = gate_s[:, :n_gate].reshape(bd, tq, N_BRANCH, N_HEADS).transpose(0, 3, 1, 2).reshape(bd * rows_s, N_BRANCH)
    gate_r = jnp.pad(gate_r, ((0, 0), (0, LANES - N_BRANCH)))
    to_t = lambda a: a.transpose(0, 2, 3, 4, 1).reshape(a.shape[0], width, a.shape[1])
    oc_s, os_s, ow_s = _decode_attn(qa_s, kvc_s, to_t(cache_slc), pt_flat, slc_s, to_t(state_win), win_s,
                                    m_t, e_s, cbs, sbs, wbs, gate_r, bd, tq, n_pages)
    y_s = _nsa_out(oc_s, os_s, ow_s, sz_s, xs, wo, b_norm_post[0], tm_s)

    kv5 = lambda a, b, t: a.reshape(b, t, 2, KV_HEADS, HEAD_DIM)
    kv5_t = lambda a: a.reshape(bp, 2, KV_HEADS, HEAD_DIM, a.shape[2]).transpose(0, 4, 1, 2, 3)
    win_all = jnp.concatenate([state_win, kv5(win_s, bd, tq)], axis=1)
    return (y_p.reshape(bp, tp, d), y_s.reshape(bd, tq, d), jnp.stack(conv_p), jnp.stack(conv_s),
            kv5_t(cmp_p), kv5(cmp_s, bd, tq), kv5_t(slc_p), kv5(slc_s, bd, tq),
            kv5_t(win_p[:, :, -min(WINDOW, tp):]), win_all[:, -min(WINDOW, win_all.shape[1]):])
```

```python
import functools
import math

import jax
import jax.numpy as jnp
from jax import lax
from jax.experimental import pallas as pl
from jax.experimental.pallas import tpu as pltpu

F32 = jnp.float32
BF16 = jnp.bfloat16

EPS = 1e-6
NEG = -1e30
BIG = 1e9
M_FLOOR = -1e20

HEAD_DIM = 64
KV_HEADS = 4
N_HEADS = 16
GROUP = N_HEADS // KV_HEADS
N_BRANCH = 3
CONV_W = 31
L_CMP = 32
CMP_STRIDE = 16
L_SLC = 64
N_SEL = 16
WINDOW = 512
Q_BLK = 128
NUM_BUCKETS = 32
MAX_DISTANCE = 128
MAX_EXACT = NUM_BUCKETS // 2
PAGE_SIZE = 128
LANES = 128
HALO = 32
CONV_TAIL = 16

VMEM_LIMIT = 56 * 1024 * 1024


def _cparams(*sem):
    return pltpu.CompilerParams(dimension_semantics=sem, vmem_limit_bytes=VMEM_LIMIT)


def _sigmoid(x):
    return 1.0 / (1.0 + jnp.exp(-x))


def _rms(x, g):
    return x * lax.rsqrt(jnp.mean(x * x, axis=-1, keepdims=True) + EPS) * g


def _lane(shape):
    return lax.broadcasted_iota(jnp.int32, shape, len(shape) - 1)


def _swap_halves(x):
    return pltpu.roll(x, HEAD_DIM, axis=x.ndim - 1)


def _glu_proj_body(x_ref, g_ref, w_ref, glu_ref, sz_ref):
    di = glu_ref.shape[-1]
    h = _rms(x_ref[...], g_ref[...])
    u = jnp.dot(h.astype(BF16), w_ref[...], preferred_element_type=F32)
    z = u[:, 2 * di:]
    glu_ref[...] = u[:, :di] * _sigmoid(u[:, di:2 * di])
    sz_ref[...] = z * _sigmoid(z)


def _glu_proj(x, g, w_bf, tm):
    rows, d = x.shape
    di = w_bf.shape[1] // 3
    return pl.pallas_call(
        _glu_proj_body,
        grid=(rows // tm,),
        in_specs=[pl.BlockSpec((tm, d), lambda i: (i, 0)),
                  pl.BlockSpec((1, d), lambda i: (0, 0)),
                  pl.BlockSpec((d, 3 * di), lambda i: (0, 0))],
        out_specs=[pl.BlockSpec((tm, di), lambda i: (i, 0)),
                   pl.BlockSpec((tm, di), lambda i: (i, 0))],
        out_shape=[jax.ShapeDtypeStruct((rows, di), F32)] * 2,
        compiler_params=_cparams("parallel"),
        name="glu_proj",
    )(x, g.reshape(1, d), w_bf)


def _conv_tail(c, sz, x, lg_ref, lb_ref, wo_ref, gp_ref):
    mu = jnp.mean(c, axis=-1, keepdims=True)
    cc = c - mu
    var = jnp.mean(cc * cc, axis=-1, keepdims=True)
    y = cc * lax.rsqrt(var + EPS) * lg_ref[...] + lb_ref[...]
    y = y * _sigmoid(y) * sz
    o = jnp.dot(y.astype(BF16), wo_ref[...], preferred_element_type=F32)
    return x + _rms(o, gp_ref[...])


def _conv_prompt_body(glu_ref, prev_ref, sz_ref, x_ref, cw_ref, cb_ref, lg_ref, lb_ref, wo_ref, gp_ref,
                      out_ref, full_ref, c_ref, *, tm):
    t = pl.program_id(1)
    full_ref[0:HALO, :] = jnp.where(t > 0, prev_ref[0], 0.0)
    full_ref[HALO:HALO + tm, :] = glu_ref[0]
    full_ref[HALO + tm:, :] = jnp.zeros((CONV_TAIL, full_ref.shape[1]), F32)
    d = c_ref.shape[-1]
    first = HALO - (CONV_W - 1)
    sub = 8
    half = tm // 2
    for lc in range(d // LANES):
        ln = slice(lc * LANES, (lc + 1) * LANES)
        for r0 in range(0, tm, half):
            acc = jnp.zeros((half, LANES), F32)
            for b in range(sub):
                z = None
                for a in range(-(-(CONV_W - b) // sub)):
                    x = full_ref[r0 + sub * a:r0 + sub * a + half + 2 * sub, ln]
                    term = cw_ref[sub * a + b:sub * a + b + 1, ln] * x
                    z = term if z is None else z + term
                acc = acc + z[first + b:first + b + half]
            c_ref[r0:r0 + half, ln] = acc + cb_ref[:, ln]
    out_ref[0] = _conv_tail(c_ref[...], sz_ref[0], x_ref[0], lg_ref, lb_ref, wo_ref, gp_ref)


def _conv_prompt(glu, sz, x, cw, cb, lg, lb, wo_bf, gp, tm):
    b, t, d = x.shape
    di = glu.shape[-1]
    per = tm // HALO
    vec = lambda n: pl.BlockSpec((1, n), lambda i, j: (0, 0))
    return pl.pallas_call(
        functools.partial(_conv_prompt_body, tm=tm),
        grid=(b, t // tm),
        in_specs=[pl.BlockSpec((1, tm, di), lambda i, j: (i, j, 0)),
                  pl.BlockSpec((1, HALO, di), lambda i, j: (i, jnp.maximum(j * per - 1, 0), 0)),
                  pl.BlockSpec((1, tm, di), lambda i, j: (i, j, 0)),
                  pl.BlockSpec((1, tm, d), lambda i, j: (i, j, 0)),
                  pl.BlockSpec((CONV_W, di), lambda i, j: (0, 0)),
                  vec(di), vec(di), vec(di),
                  pl.BlockSpec((di, d), lambda i, j: (0, 0)),
                  vec(d)],
        out_specs=pl.BlockSpec((1, tm, d), lambda i, j: (i, j, 0)),
        out_shape=jax.ShapeDtypeStruct((b, t, d), F32),
        scratch_shapes=[pltpu.VMEM((HALO + tm + CONV_TAIL, di), F32), pltpu.VMEM((tm, di), F32)],
        compiler_params=_cparams("parallel", "arbitrary"),
        name="conv_prompt",
    )(glu, glu, sz, x, cw, cb.reshape(1, di), lg.reshape(1, di), lb.reshape(1, di), wo_bf, gp.reshape(1, d))


def _conv_sample_body(glu_ref, st_ref, sz_ref, x_ref, cw_ref, cb_ref, lg_ref, lb_ref, wo_ref, gp_ref,
                      out_ref, full_ref, c_ref, *, nb, tq):
    d = c_ref.shape[-1]
    first = HALO - (CONV_W - 1)
    full_ref[:, first:HALO, :] = st_ref[...]
    full_ref[:, HALO:HALO + tq, :] = glu_ref[...].reshape(nb, tq, d)
    for lc in range(d // LANES):
        ln = slice(lc * LANES, (lc + 1) * LANES)
        acc = jnp.zeros((nb, tq, LANES), F32)
        for j in range(CONV_W):
            acc = acc + cw_ref[j:j + 1, ln] * full_ref[:, first + j:first + j + tq, ln]
        c_ref[:, ln] = (acc + cb_ref[:, ln]).reshape(nb * tq, LANES)
    out_ref[...] = _conv_tail(c_ref[...], sz_ref[...], x_ref[...], lg_ref, lb_ref, wo_ref, gp_ref)


def _conv_sample(glu, state, sz, x, cw, cb, lg, lb, wo_bf, gp, nb):
    n_seq = state.shape[0]
    rows, d = x.shape
    di = glu.shape[-1]
    tq = rows // n_seq
    vec = lambda n: pl.BlockSpec((1, n), lambda i: (0, 0))
    return pl.pallas_call(
        functools.partial(_conv_sample_body, nb=nb, tq=tq),
        grid=(n_seq // nb,),
        in_specs=[pl.BlockSpec((nb * tq, di), lambda i: (i, 0)),
                  pl.BlockSpec((nb, CONV_W - 1, di), lambda i: (i, 0, 0)),
                  pl.BlockSpec((nb * tq, di), lambda i: (i, 0)),
                  pl.BlockSpec((nb * tq, d), lambda i: (i, 0)),
                  pl.BlockSpec((CONV_W, di), lambda i: (0, 0)),
                  vec(di), vec(di), vec(di),
                  pl.BlockSpec((di, d), lambda i: (0, 0)),
                  vec(d)],
        out_specs=pl.BlockSpec((nb * tq, d), lambda i: (i, 0)),
        out_shape=jax.ShapeDtypeStruct((rows, d), F32),
        scratch_shapes=[pltpu.VMEM((nb, HALO + tq, di), F32), pltpu.VMEM((nb * tq, di), F32)],
        compiler_params=_cparams("parallel"),
        name="conv_sample",
    )(glu, state, sz, x, cw, cb.reshape(1, di), lg.reshape(1, di), lb.reshape(1, di), wo_bf, gp.reshape(1, d))


def _head_major(col_pair, head):
    return col_pair if head % 2 == 0 else _swap_halves(col_pair)


def _nsa_proj_body(x_ref, gkv_ref, gq_ref, wkv_ref, wq_ref, wz_ref, wg_ref,
                   cmp_ref, slc_ref, win_ref, qa_ref, sz_ref, gate_ref, *aug_refs, seq_len, tm):
    x = x_ref[...]
    xn = x * lax.rsqrt(jnp.mean(x * x, axis=-1, keepdims=True) + EPS)
    hkv = (xn * gkv_ref[...]).astype(BF16)
    hq = (xn * gq_ref[...]).astype(BF16)
    kv = jnp.dot(hkv, wkv_ref[...], preferred_element_type=F32)
    width = 2 * KV_HEADS * HEAD_DIM
    if aug_refs:
        kv_t = [kv[:, b * width:(b + 1) * width].T for b in range(N_BRANCH)]
        cmp_ref[0] = kv_t[0]
        slc_ref[0] = kv_t[1]
        win_ref[0] = kv_t[2]
    else:
        cmp_ref[...] = kv[:, :width]
        slc_ref[...] = kv[:, width:2 * width]
        win_ref[...] = kv[:, 2 * width:]
    lane = _lane((tm, LANES))
    low = lane < HEAD_DIM
    uq = jnp.dot(hq, wq_ref[...], preferred_element_type=F32)
    scale = HEAD_DIM ** -0.5
    for h in range(N_HEADS):
        qh = _head_major(uq[:, (h // 2) * LANES:(h // 2 + 1) * LANES], h)
        qa_ref[:, h * LANES:(h + 1) * LANES] = jnp.where(low, qh * scale, 0.0).astype(BF16)
    z = jnp.dot(hq, wz_ref[...], preferred_element_type=F32)
    sz_ref[...] = z * _sigmoid(z)
    gate_ref[...] = _sigmoid(jnp.dot(hq, wg_ref[...], preferred_element_type=F32))
    if aug_refs:
        ska_ref, sv_ref, wka_ref, wv_ref = aug_refs
        row = pl.program_id(0) * tm + lax.broadcasted_iota(jnp.int32, (tm, LANES), 0)
        blk = (row % seq_len) // L_SLC
        onehot = (lane - HEAD_DIM == blk).astype(F32)
        kw = KV_HEADS * HEAD_DIM
        for h in range(KV_HEADS):
            ks = _head_major(kv[:, width + (h // 2) * LANES:width + (h // 2 + 1) * LANES], h)
            ska_ref[:, h * LANES:(h + 1) * LANES] = jnp.where(low, ks, onehot).astype(BF16)
            kwn = _head_major(kv[:, 2 * width + (h // 2) * LANES:2 * width + (h // 2 + 1) * LANES], h)
            wka_ref[:, h * LANES:(h + 1) * LANES] = jnp.where(low, kwn, 0.0).astype(BF16)
        sv_ref[...] = kv_t[1][kw:].astype(BF16)
        wv_ref[...] = kv_t[2][kw:].astype(BF16)


def _nsa_proj(x, g_kv, g_q, wkv_bf, wq_bf, wz_bf, wg_bf, tm, seq_len=None):
    rows, d = x.shape
    width = 2 * KV_HEADS * HEAD_DIM
    row = lambda n: pl.BlockSpec((tm, n), lambda i: (i, 0))
    full = lambda a: pl.BlockSpec(a.shape, lambda i: (0, 0))
    out_specs = [row(width), row(width), row(width), row(N_HEADS * LANES), row(wz_bf.shape[1]), row(LANES)]
    out_shape = [jax.ShapeDtypeStruct((rows, width), F32)] * 3 + [
        jax.ShapeDtypeStruct((rows, N_HEADS * LANES), BF16),
        jax.ShapeDtypeStruct((rows, wz_bf.shape[1]), F32),
        jax.ShapeDtypeStruct((rows, LANES), F32)]
    if seq_len is not None:
        per_seq = seq_len // tm
        kv_t_spec = pl.BlockSpec((1, width, tm), lambda i: (i // per_seq, 0, i % per_seq))
        out_specs[:N_BRANCH] = [kv_t_spec] * N_BRANCH
        out_shape[:N_BRANCH] = [jax.ShapeDtypeStruct((rows // seq_len, width, seq_len), F32)] * N_BRANCH
        out_specs += [row(KV_HEADS * LANES), pl.BlockSpec((width // 2, tm), lambda i: (0, i))] * 2
        out_shape += [jax.ShapeDtypeStruct((rows, KV_HEADS * LANES), BF16),
                      jax.ShapeDtypeStruct((width // 2, rows), BF16)] * 2
    gkv = g_kv.reshape(1, d)
    gq = g_q.reshape(1, d)
    return pl.pallas_call(
        functools.partial(_nsa_proj_body, seq_len=seq_len, tm=tm),
        grid=(rows // tm,),
        in_specs=[row(d), full(gkv), full(gq), full(wkv_bf), full(wq_bf), full(wz_bf), full(wg_bf)],
        out_specs=out_specs,
        out_shape=out_shape,
        compiler_params=_cparams("parallel"),
        name="nsa_proj",
    )(x, gkv, gq, wkv_bf, wq_bf, wz_bf, wg_bf)


def _compress_body(pt_ref, *refs, n_pages):
    del pt_ref
    page_refs = refs[:n_pages]
    w1_ref, pe_ref, w2_ref, out_ref, a_ref = refs[n_pages:]
    width, page_rows = page_refs[0].shape[1:]
    n_col = width // LANES
    cpp = page_rows // CMP_STRIDE
    n_chunk = n_pages * cpp
    hid = w2_ref.shape[2]
    low = _lane((cpp, LANES)) < HEAD_DIM
    def assemble(c):
        for p in range(n_pages):
            col_t = page_refs[p][0, c * LANES:(c + 1) * LANES, :]
            by_s = pltpu.einshape("(ns)f->(sn)f", col_t.T, n=cpp)
            for s in range(0, CMP_STRIDE, 2):
                b0 = by_s[s * cpp:(s + 1) * cpp]
                b1 = by_s[(s + 1) * cpp:(s + 2) * cpp]
                dst = (slice(p * cpp, (p + 1) * cpp), slice((s // 2) * LANES, (s // 2 + 1) * LANES))
                a_ref[(2 * c,) + dst] = jnp.where(low, b0, _swap_halves(b1))
                a_ref[(2 * c + 1,) + dst] = jnp.where(low, _swap_halves(b0), b1)

    for kv in range(2):
        for c in range(kv * n_col // 2, (kv + 1) * n_col // 2):
            assemble(c)
        w1 = w1_ref[kv]
        pe = jnp.dot(pe_ref[kv].astype(BF16), w1, preferred_element_type=F32)
        pe_term = pe[0:1, :hid] + pe[1:2, hid:]
        a_kv = a_ref[kv * KV_HEADS:(kv + 1) * KV_HEADS].reshape(KV_HEADS * n_chunk, CMP_STRIDE * HEAD_DIM)
        parts = jnp.dot(a_kv.astype(BF16), w1, preferred_element_type=F32)
        for hp in range(KV_HEADS // 2):
            pair = jnp.zeros((n_chunk, LANES), F32)
            for par in range(2):
                part = parts[(2 * hp + par) * n_chunk:(2 * hp + par + 1) * n_chunk]
                pre = part[:, :hid] + pltpu.roll(part[:, hid:], n_chunk - 1, axis=0) + pe_term
                mid = pre * _sigmoid(pre)
                pair = pair + jnp.dot(mid.astype(BF16), w2_ref[kv, par], preferred_element_type=F32)
            col = kv * (KV_HEADS // 2) + hp
            out_ref[0, col * LANES:(col + 1) * LANES, :] = pair.T


def _compress(rows_t, page_table, w1cat_bf, pe8, w2_bf, n_pages, page_rows):
    width = rows_t.shape[1]
    n_chunk = n_pages * page_rows // CMP_STRIDE
    if page_table is None:
        n_seq = rows_t.shape[0]
        page_table = jnp.zeros((1,), jnp.int32)
        page_spec = lambda j: pl.BlockSpec((1, width, page_rows), lambda i, pt: (i, 0, j))
    else:
        n_seq = page_table.shape[0] // n_pages
        page_spec = lambda j: pl.BlockSpec((1, width, page_rows), lambda i, pt: (pt[i * n_pages + j], 0, 0))
    full = lambda a: pl.BlockSpec(a.shape, lambda i, pt: (0,) * a.ndim)
    grid_spec = pltpu.PrefetchScalarGridSpec(
        num_scalar_prefetch=1,
        grid=(n_seq,),
        in_specs=[page_spec(j) for j in range(n_pages)] + [full(w1cat_bf), full(pe8), full(w2_bf)],
        out_specs=pl.BlockSpec((1, width, n_chunk), lambda i, pt: (i, 0, 0)),
        scratch_shapes=[pltpu.VMEM((2 * KV_HEADS, n_chunk, CMP_STRIDE * HEAD_DIM), F32)],
    )
    return pl.pallas_call(
        functools.partial(_compress_body, n_pages=n_pages),
        grid_spec=grid_spec,
        out_shape=jax.ShapeDtypeStruct((n_seq, width, n_chunk), F32),
        compiler_params=_cparams("parallel"),
        name="compress",
    )(page_table, *([rows_t] * n_pages), w1cat_bf, pe8, w2_bf)


def _bias_tile_body(rb_ref, out_ref, *, row_step, lane_step, offset, hi):
    h = pl.program_id(0)
    shape = out_ref.shape[1:]
    d = row_step * lax.broadcasted_iota(jnp.int32, shape, 0) + lane_step * _lane(shape) + offset
    n = jnp.maximum(d, 0)
    nf = jnp.maximum(n, 1).astype(F32)
    large = MAX_EXACT + (jnp.log(nf / MAX_EXACT) / math.log(MAX_DISTANCE / MAX_EXACT)
                         * (NUM_BUCKETS - MAX_EXACT)).astype(jnp.int32)
    large = jnp.minimum(large, NUM_BUCKETS - 1)
    bucket = jnp.where(n < MAX_EXACT, n, large)
    far = rb_ref[NUM_BUCKETS - 1, h]
    val = jnp.zeros(shape, F32)
    for k in range(NUM_BUCKETS - 1):
        val = jnp.where(bucket == k, rb_ref[k, h] - far, val)
    out_ref[0] = jnp.where((d >= 0) & (d <= hi), val, NEG)


def _bias_tile(rel_bias, rows, width, lane_step, offset, hi=1 << 30, row_step=1):
    return pl.pallas_call(
        functools.partial(_bias_tile_body, row_step=row_step, lane_step=lane_step, offset=offset, hi=hi),
        grid=(N_HEADS,),
        in_specs=[pl.BlockSpec(memory_space=pltpu.SMEM)],
        out_specs=pl.BlockSpec((1, rows, width), lambda h: (h, 0, 0)),
        out_shape=jax.ShapeDtypeStruct((N_HEADS, rows, width), F32),
        compiler_params=_cparams("parallel"),
        name="bias_tile",
    )(rel_bias)


_NT = (((1,), (1,)), ((), ()))


def _softmax_rows(s):
    m = jnp.maximum(jnp.max(s, axis=-1, keepdims=True), M_FLOOR)
    p = jnp.exp(s - m)
    return p, jnp.sum(p, axis=-1, keepdims=True)


def _gate_col(gate, col):
    return jnp.sum(jnp.where(_lane(gate.shape) == col, gate, 0.0), axis=-1, keepdims=True)


def _pair_columns(o_even, o_odd, valid_half):
    lo = o_even if valid_half == 0 else _swap_halves(o_even)
    hi = o_odd if valid_half == 1 else _swap_halves(o_odd)
    return jnp.where(_lane(lo.shape) < HEAD_DIM, lo, hi)


def _select_blocks_t(imp_t, tb, n_sel):
    nb, nt = imp_t.shape
    sub = 8
    j = lax.broadcasted_iota(jnp.int32, (nb, nt), 0)
    valid = j <= tb
    forced = (j == 0) | (j == tb) | (j == tb - 1)
    score = jnp.where(valid, jnp.where(forced, BIG, imp_t), -BIG)
    groups = [score[lo:lo + sub] for lo in range(0, nb, sub)]
    jr = lax.broadcasted_iota(jnp.int32, (sub, nt), 0)
    counts = [jnp.zeros((sub, nt), F32) for _ in groups]
    for i in range(nb):
        row = score[i:i + 1]
        for r, grp in enumerate(groups):
            lo = r * sub
            if lo > i:
                one = jnp.where(row >= grp, 1.0, 0.0)
            elif lo + sub - 1 <= i:
                one = jnp.where(row > grp, 1.0, 0.0)
            else:
                one = jnp.where(jr + lo > i, jnp.where(row >= grp, 1.0, 0.0), jnp.where(row > grp, 1.0, 0.0))
            counts[r] = counts[r] + one
    rank = jnp.concatenate(counts, axis=0)
    return jnp.where((rank < n_sel) & valid, 0.0, NEG)


def _select_blocks(imp, tb, n_sel):
    lane = _lane(imp.shape)
    lane_f = lane.astype(F32)
    j = lane - HEAD_DIM
    valid = (j >= 0) & (j <= tb)
    forced = (j == 0) | (j == tb) | (j == tb - 1)
    score = jnp.where(valid, jnp.where(forced, BIG, imp), -BIG)
    score = jnp.where(j >= 0, score, -jnp.inf)
    sel = jnp.zeros(imp.shape, jnp.bool_)
    for _ in range(n_sel):
        best = jnp.max(score, axis=-1, keepdims=True)
        first = jnp.min(jnp.where(score == best, lane_f, 4.0 * LANES), axis=-1, keepdims=True)
        pick = lane_f == first
        sel = sel | pick
        score = jnp.where(pick, -jnp.inf, score)
    return jnp.where(sel & valid, 0.0, jnp.where(j >= 0, NEG, 0.0))


def _cmp_prompt_body(qa_ref, kc_ref, vc_ref, m_ref, cb_ref, gate_ref, oc_ref, qsel_ref, qflag_ref, *, n_key):
    qt = pl.program_id(1)
    start = pl.multiple_of(8 * qt + 8, 8)
    kwin = kc_ref[0, pl.ds(start, n_key), :]
    vwin = vc_ref[0, pl.ds(start, n_key), :]
    mwin_t = m_ref[pl.ds(start, n_key), :].T
    feat = lax.broadcasted_iota(jnp.int32, (LANES, Q_BLK), 0)
    pos = qt * Q_BLK + lax.broadcasted_iota(jnp.int32, (1, Q_BLK), 1)
    tb = pos // L_SLC
    gate = gate_ref[...]
    n_pair = GROUP // 2
    for kvh in range(KV_HEADS):
        ka = kwin[:, kvh * LANES:(kvh + 1) * LANES].astype(BF16)
        vpt = vwin[:, (kvh // 2) * LANES:(kvh // 2 + 1) * LANES].T
        vpt = vpt[(kvh % 2) * HEAD_DIM:(kvh % 2 + 1) * HEAD_DIM].astype(BF16)
        qts = [qa_ref[0, :, h * LANES:(h + 1) * LANES].astype(F32).T for h in range(kvh * GROUP, (kvh + 1) * GROUP)]
        pc_sum = jnp.zeros((n_key, Q_BLK), F32)
        outs = []
        for gp in range(n_pair):
            qa = jnp.concatenate([jnp.where(feat == HEAD_DIM, NEG, qts[2 * gp + e]) for e in range(2)], axis=1)
            qa = qa.astype(BF16)
            qflag_ref[0, kvh, :, 2 * gp * Q_BLK:(2 * gp + 2) * Q_BLK] = qa
            s = jnp.dot(ka, qa, preferred_element_type=F32)
            s = s + cb_ref[kvh, :, 2 * gp * Q_BLK:(2 * gp + 2) * Q_BLK]
            m = jnp.maximum(jnp.max(s, axis=0, keepdims=True), M_FLOOR)
            p = jnp.exp(s - m)
            inv = 1.0 / jnp.maximum(jnp.sum(p, axis=0, keepdims=True), 1e-30)
            pc = p * inv
            pc_sum = pc_sum + pc[:, :Q_BLK] + pc[:, Q_BLK:]
            o = jnp.dot(vpt, p.astype(BF16), preferred_element_type=F32) * inv
            _store_chain(oc_ref, o, gate, kvh * GROUP + 2 * gp, kvh * n_pair + gp)
        imp_t = jnp.dot(mwin_t, pc_sum, preferred_element_type=F32, precision=lax.Precision.HIGHEST)
        selneg_t = _select_blocks_t(imp_t[HEAD_DIM:], tb, N_SEL)
        for g in range(GROUP):
            qsel_ref[0, kvh, :, g * Q_BLK:(g + 1) * Q_BLK] = jnp.concatenate(
                [qts[g][:HEAD_DIM], selneg_t], axis=0).astype(BF16)


def _cmp_prompt(qa, kc_pad, vc_pad, m_pad, cb, gate, batch, seq):
    nq = seq // Q_BLK
    n_key = kc_pad.shape[1] // 2
    d_out = N_HEADS * HEAD_DIM
    return pl.pallas_call(
        functools.partial(_cmp_prompt_body, n_key=n_key),
        grid=(batch, nq),
        in_specs=[pl.BlockSpec((1, Q_BLK, N_HEADS * LANES), lambda b, t: (b, t, 0)),
                  pl.BlockSpec((1,) + kc_pad.shape[1:], lambda b, t: (b, 0, 0)),
                  pl.BlockSpec((1,) + vc_pad.shape[1:], lambda b, t: (b, 0, 0)),
                  pl.BlockSpec(m_pad.shape, lambda b, t: (0, 0)),
                  pl.BlockSpec(cb.shape, lambda b, t: (0, 0, 0)),
                  pl.BlockSpec((Q_BLK, LANES), lambda b, t: (b * nq + t, 0))],
        out_specs=[pl.BlockSpec((1, Q_BLK, d_out), lambda b, t: (b, t, 0))]
        + [pl.BlockSpec((1, KV_HEADS, LANES, GROUP * Q_BLK), lambda b, t: (b * nq + t, 0, 0, 0))] * 2,
        out_shape=[jax.ShapeDtypeStruct((batch, seq, d_out), F32)]
        + [jax.ShapeDtypeStruct((batch * nq, KV_HEADS, LANES, GROUP * Q_BLK), BF16)] * 2,
        compiler_params=_cparams("parallel", "parallel"),
        name="cmp_prompt",
    )(qa.reshape(batch, seq, -1), kc_pad, vc_pad, m_pad, cb, gate)


KEY_TILE = 256
COL_CHAIN = 256


def _store_chain(out_ref, o_t, gate, gate_col0, lane_col):
    pair = jnp.concatenate([o_t[:, :Q_BLK], o_t[:, Q_BLK:]], axis=0).T
    g = jnp.where(_lane(pair.shape) < HEAD_DIM, _gate_col(gate, gate_col0), _gate_col(gate, gate_col0 + 1))
    out_ref[0, :, lane_col * LANES:(lane_col + 1) * LANES] = pair * g


_PAIR_CHAINS = [(e, c) for e in range(2) for c in range(GROUP * Q_BLK // COL_CHAIN)]


def _slc_prompt_body(q_ref, k_ref, v_ref, sb_ref, gate_ref, out_ref, m_ref, l_ref, acc_ref, p_ref):
    pr = pl.program_id(1)
    qt = pl.program_id(2)
    m_ref[...] = jnp.full(m_ref.shape, M_FLOOR, F32)
    l_ref[...] = jnp.zeros(l_ref.shape, F32)
    acc_ref[...] = jnp.zeros(acc_ref.shape, F32)
    p_ref[...] = jnp.zeros(p_ref.shape, BF16)
    last = (qt * Q_BLK) // KEY_TILE
    cs = lambda c: slice(c * COL_CHAIN, (c + 1) * COL_CHAIN)

    def pv_of(kt):
        base = pl.multiple_of(kt * KEY_TILE, KEY_TILE)
        return [jnp.dot(v_ref[e * HEAD_DIM:(e + 1) * HEAD_DIM, pl.ds(base, KEY_TILE)], p_ref[i],
                        preferred_element_type=F32) for i, (e, c) in enumerate(_PAIR_CHAINS)]

    def make_step(with_bias):
        def step(kt, carry):
            base = pl.multiple_of(kt * KEY_TILE, KEY_TILE)
            case = qt % 2 + 2 * (last - kt)
            ks = [k_ref[0, pl.ds(base, KEY_TILE), e * LANES:(e + 1) * LANES] for e in range(2)]
            ss = [jnp.dot(ks[e], q_ref[0, e, :, cs(c)], preferred_element_type=F32) for e, c in _PAIR_CHAINS]
            pvs = pv_of(jnp.maximum(kt - 1, 0))
            if with_bias:
                ss = [s + sb_ref[case, e, :, cs(c)] for s, (e, c) in zip(ss, _PAIR_CHAINS)]
            for i, s in enumerate(ss):
                m_old = m_ref[i]
                m_new = jnp.maximum(m_old, jnp.max(s, axis=0, keepdims=True))
                alpha = jnp.exp(m_old - m_new)
                p = jnp.exp(s - m_new)
                l_ref[i] = alpha * l_ref[i] + jnp.sum(p, axis=0, keepdims=True)
                m_ref[i] = m_new
                acc_ref[i] = alpha * (acc_ref[i] + pvs[i])
                p_ref[i] = p.astype(BF16)
            return carry
        return step

    n_far = jnp.maximum(last - 1, 0)
    lax.fori_loop(0, n_far, make_step(False), 0)
    lax.fori_loop(n_far, last + 1, make_step(True), 0)
    gate = gate_ref[...]
    pvs = pv_of(last)
    for i, (e, c) in enumerate(_PAIR_CHAINS):
        o = (acc_ref[i] + pvs[i]) * (1.0 / jnp.maximum(l_ref[i], 1e-30))
        _store_chain(out_ref, o, gate, N_HEADS + (2 * pr + e) * GROUP + 2 * c, i)


def _slc_prompt(qsel_t, ska, sv_t, sb_t, gate, batch, seq):
    nq = seq // Q_BLK
    cols = GROUP * Q_BLK
    d_out = N_HEADS * HEAD_DIM
    n_chain = len(_PAIR_CHAINS)
    return pl.pallas_call(
        _slc_prompt_body,
        grid=(batch, KV_HEADS // 2, nq),
        in_specs=[pl.BlockSpec((1, 2, LANES, cols), lambda b, p, t: (b * nq + t, p, 0, 0)),
                  pl.BlockSpec((1, seq, 2 * LANES), lambda b, p, t: (b, 0, p)),
                  pl.BlockSpec((2 * HEAD_DIM, seq), lambda b, p, t: (p, b)),
                  pl.BlockSpec((4, 2, KEY_TILE, cols), lambda b, p, t: (0, p, 0, 0)),
                  pl.BlockSpec((Q_BLK, LANES), lambda b, p, t: (b * nq + t, 0))],
        out_specs=pl.BlockSpec((1, Q_BLK, 2 * GROUP * HEAD_DIM), lambda b, p, t: (b, t, p)),
        out_shape=jax.ShapeDtypeStruct((batch, seq, d_out), F32),
        scratch_shapes=[pltpu.VMEM((n_chain, 1, COL_CHAIN), F32), pltpu.VMEM((n_chain, 1, COL_CHAIN), F32),
                        pltpu.VMEM((n_chain, HEAD_DIM, COL_CHAIN), F32),
                        pltpu.VMEM((n_chain, KEY_TILE, COL_CHAIN), BF16)],
        compiler_params=_cparams("parallel", "parallel", "arbitrary"),
        name="slc_prompt",
    )(qsel_t, ska.reshape(batch, seq, -1), sv_t, sb_t, gate)


def _win_prompt_body(q_ref, k_ref, v_ref, wb_ref, gate_ref, out_ref, *, n_key):
    pr = pl.program_id(1)
    qt = pl.program_id(2)
    base = pl.multiple_of(qt * Q_BLK, Q_BLK)
    cs = lambda c: slice(c * COL_CHAIN, (c + 1) * COL_CHAIN)
    ks = [k_ref[0, pl.ds(base, n_key), e * LANES:(e + 1) * LANES] for e in range(2)]
    ss = [jnp.dot(ks[e], q_ref[0, e, :, cs(c)], preferred_element_type=F32) + wb_ref[e, :, cs(c)]
          for e, c in _PAIR_CHAINS]
    ps, invs = [], []
    for s in ss:
        m = jnp.maximum(jnp.max(s, axis=0, keepdims=True), M_FLOOR)
        p = jnp.exp(s - m)
        invs.append(1.0 / jnp.maximum(jnp.sum(p, axis=0, keepdims=True), 1e-30))
        ps.append(p.astype(BF16))
    gate = gate_ref[...]
    for i, (e, c) in enumerate(_PAIR_CHAINS):
        vt = v_ref[e * HEAD_DIM:(e + 1) * HEAD_DIM, pl.ds(base, n_key)]
        o = jnp.dot(vt, ps[i], preferred_element_type=F32) * invs[i]
        _store_chain(out_ref, o, gate, 2 * N_HEADS + (2 * pr + e) * GROUP + 2 * c, i)


def _win_prompt(q_t, wka_pad, wv_t_pad, wb_t, gate, batch, seq):
    nq = seq // Q_BLK
    n_key = WINDOW + Q_BLK
    cols = GROUP * Q_BLK
    d_out = N_HEADS * HEAD_DIM
    padded = wka_pad.shape[1]
    return pl.pallas_call(
        functools.partial(_win_prompt_body, n_key=n_key),
        grid=(batch, KV_HEADS // 2, nq),
        in_specs=[pl.BlockSpec((1, 2, LANES, cols), lambda b, p, t: (b * nq + t, p, 0, 0)),
                  pl.BlockSpec((1, padded, 2 * LANES), lambda b, p, t: (b, 0, p)),
                  pl.BlockSpec((2 * HEAD_DIM, padded), lambda b, p, t: (p, b)),
                  pl.BlockSpec((2, n_key, cols), lambda b, p, t: (p, 0, 0)),
                  pl.BlockSpec((Q_BLK, LANES), lambda b, p, t: (b * nq + t, 0))],
        out_specs=pl.BlockSpec((1, Q_BLK, 2 * GROUP * HEAD_DIM), lambda b, p, t: (b, t, p)),
        out_shape=jax.ShapeDtypeStruct((batch, seq, d_out), F32),
        compiler_params=_cparams("parallel", "parallel", "parallel"),
        name="win_prompt",
    )(q_t, wka_pad, wv_t_pad, wb_t, gate)


def _nsa_out_body(oc_ref, os_ref, ow_ref, sz_ref, x_ref, wo_ref, gp_ref, out_ref):
    d = oc_ref.shape[-1]
    y = oc_ref[...] * sz_ref[:, :d] + os_ref[...] * sz_ref[:, d:2 * d] + ow_ref[...] * sz_ref[:, 2 * d:]
    o = jnp.dot(y.astype(BF16), wo_ref[...], preferred_element_type=F32)
    out_ref[...] = x_ref[...] + _rms(o, gp_ref[...])


def _nsa_out(oc, os_, ow, sz, x, wo_bf, gp, tm):
    rows, d = x.shape
    dq = oc.shape[-1]
    row = lambda n: pl.BlockSpec((tm, n), lambda i: (i, 0))
    return pl.pallas_call(
        _nsa_out_body,
        grid=(rows // tm,),
        in_specs=[row(dq), row(dq), row(dq), row(N_BRANCH * dq), row(d),
                  pl.BlockSpec((dq, d), lambda i: (0, 0)), pl.BlockSpec((1, d), lambda i: (0, 0))],
        out_specs=row(d),
        out_shape=jax.ShapeDtypeStruct((rows, d), F32),
        compiler_params=_cparams("parallel"),
        name="nsa_out",
    )(oc, os_, ow, sz, x, wo_bf, gp.reshape(1, d))


def _sample_attn_body(pt_ref, *refs, n_pages, tq, w_keep):
    del pt_ref
    page_refs = refs[:n_pages]
    (qa_ref, kvc_ref, snew_ref, wst_ref, wnew_ref, m_ref, e_ref, cbs_ref, sbs_ref, wbs_ref, gate_ref,
     oc_ref, os_ref, ow_ref, ksel_ref, kwin_ref) = refs[n_pages:]
    n_past = n_pages * PAGE_SIZE
    n_sk = ksel_ref.shape[0]
    n_wk = kwin_ref.shape[0]
    width = ksel_ref.shape[1]
    kw = width // 2
    for p in range(n_pages):
        ksel_ref[p * PAGE_SIZE:(p + 1) * PAGE_SIZE, :] = page_refs[p][0]
    ksel_ref[n_past:n_past + tq, :] = snew_ref[...]
    ksel_ref[n_past + tq:, :] = jnp.zeros((n_sk - n_past - tq, width), F32)
    kwin_ref[0:w_keep, :] = wst_ref[0]
    kwin_ref[w_keep:w_keep + tq, :] = wnew_ref[...]
    kwin_ref[w_keep + tq:, :] = jnp.zeros((n_wk - w_keep - tq, width), F32)
    pos = n_past + lax.broadcasted_iota(jnp.int32, (tq, 1), 0)
    tb = pos // L_SLC
    gate = gate_ref[...]
    rows = GROUP * tq

    def attend(q, k, v, bias, extra=None):
        s = lax.dot_general(q, k, _NT, preferred_element_type=F32)
        if extra is not None:
            s = s + extra
        s = (s.reshape(GROUP, tq, s.shape[-1]) + bias).reshape(rows, s.shape[-1])
        p, l = _softmax_rows(s)
        inv = 1.0 / jnp.maximum(l, 1e-30)
        return p, inv, jnp.dot(p.astype(BF16), v, preferred_element_type=F32)

    def store(out_ref, o, kvh, branch):
        outs = [o[g * tq:(g + 1) * tq] * _gate_col(gate, branch * N_HEADS + kvh * GROUP + g) for g in range(GROUP)]
        for gp in range(GROUP // 2):
            col = kvh * (GROUP // 2) + gp
            out_ref[:, col * LANES:(col + 1) * LANES] = _pair_columns(outs[2 * gp], outs[2 * gp + 1], kvh % 2)

    def pair_cols(ref, pr):
        kcol = slice(pr * LANES, (pr + 1) * LANES)
        vcol = slice(kw + pr * LANES, kw + (pr + 1) * LANES)
        if len(ref.shape) == 3:
            return ref[0, :, kcol].astype(BF16), ref[0, :, vcol].astype(BF16)
        return ref[:, kcol].astype(BF16), ref[:, vcol].astype(BF16)

    qs, imps = [], []
    for kvh in range(KV_HEADS):
        q = jnp.concatenate([qa_ref[:, h * LANES:(h + 1) * LANES]
                             for h in range(kvh * GROUP, (kvh + 1) * GROUP)], axis=0)
        qs.append((q if kvh % 2 == 0 else _swap_halves(q)).astype(BF16))
    for pr in range(KV_HEADS // 2):
        kc, vc = pair_cols(kvc_ref, pr)
        for kvh in (2 * pr, 2 * pr + 1):
            p, inv, o = attend(qs[kvh], kc, vc, cbs_ref[kvh * GROUP:(kvh + 1) * GROUP])
            store(oc_ref, o * inv, kvh, 0)
            pc_sum = jnp.sum((p * inv).reshape(GROUP, tq, p.shape[-1]), axis=0)
            imps.append(jnp.dot(pc_sum, m_ref[...], preferred_element_type=F32, precision=lax.Precision.HIGHEST))
    selneg = _select_blocks(jnp.concatenate(imps, axis=0), jnp.concatenate([tb] * KV_HEADS, axis=0), N_SEL)
    for pr in range(KV_HEADS // 2):
        ksl, vsl = pair_cols(ksel_ref, pr)
        kwn, vwn = pair_cols(kwin_ref, pr)
        for kvh in (2 * pr, 2 * pr + 1):
            heads = slice(kvh * GROUP, (kvh + 1) * GROUP)
            qm = jnp.concatenate([selneg[kvh * tq:(kvh + 1) * tq]] * GROUP, axis=0).astype(BF16)
            block_mask = jnp.dot(qm, e_ref[...], preferred_element_type=F32)
            _, inv, o = attend(qs[kvh], ksl, vsl, sbs_ref[heads], block_mask)
            store(os_ref, o * inv, kvh, 1)
            _, inv, o = attend(qs[kvh], kwn, vwn, wbs_ref[heads])
            store(ow_ref, o * inv, kvh, 2)


def _sample_attn(qa, kvc, pages, page_table, slc_new, win_state, win_new, m_s, e_s, cbs, sbs, wbs, gate,
                 n_seq, tq, n_pages):
    width = pages.shape[-1]
    n_sk = sbs.shape[-1]
    n_wk = wbs.shape[-1]
    w_keep = win_state.shape[1]
    d_out = N_HEADS * HEAD_DIM
    page_spec = lambda j: pl.BlockSpec((1, PAGE_SIZE, width), lambda i, pt: (pt[i * n_pages + j], 0, 0))
    full = lambda a: pl.BlockSpec(a.shape, lambda i, pt: (0,) * a.ndim)
    row = lambda n: pl.BlockSpec((tq, n), lambda i, pt: (i, 0))
    grid_spec = pltpu.PrefetchScalarGridSpec(
        num_scalar_prefetch=1,
        grid=(n_seq,),
        in_specs=[page_spec(j) for j in range(n_pages)] + [
            row(N_HEADS * LANES),
            pl.BlockSpec((1,) + kvc.shape[1:], lambda i, pt: (i, 0, 0)),
            row(width),
            pl.BlockSpec((1, w_keep, width), lambda i, pt: (i, 0, 0)),
            row(width),
            full(m_s), full(e_s), full(cbs), full(sbs), full(wbs),
            row(LANES)],
        out_specs=[row(d_out)] * 3,
        scratch_shapes=[pltpu.VMEM((n_sk, width), F32), pltpu.VMEM((n_wk, width), F32)],
    )
    return pl.pallas_call(
        functools.partial(_sample_attn_body, n_pages=n_pages, tq=tq, w_keep=w_keep),
        grid_spec=grid_spec,
        out_shape=[jax.ShapeDtypeStruct((n_seq * tq, d_out), F32)] * 3,
        compiler_params=_cparams("parallel"),
        name="sample_attn",
    )(page_table, *([pages] * n_pages), qa, kvc, slc_new, win_state, win_new, m_s, e_s, cbs, sbs, wbs, gate)


def _decode_attn_body(pt_ref, *refs, n_pages, tq, w_keep):
    del pt_ref
    page_refs = refs[:n_pages]
    (qa_ref, kvc_ref, snew_ref, wst_ref, wnew_ref, mt_ref, e_ref, cbs_ref, sbs_ref, wbs_ref, gate_ref,
     oc_ref, os_ref, ow_ref, kt_ref, vt_ref, kwt_ref, vwt_ref) = refs[n_pages:]
    n_past = n_pages * PAGE_SIZE
    kw = KV_HEADS * HEAD_DIM
    rows = N_HEADS * tq

    def new_tile(ref):
        return jnp.concatenate([ref[...], jnp.zeros((LANES - tq, 2 * kw), F32)], axis=0).T

    def fill(k_dst, v_dst, col0, tile):
        k_dst[:, col0:col0 + tile.shape[1]] = tile[:kw].astype(BF16)
        v_dst[:, col0:col0 + tile.shape[1]] = tile[kw:].astype(BF16)

    for p in range(n_pages):
        fill(kt_ref, vt_ref, p * PAGE_SIZE, page_refs[p][0])
    fill(kt_ref, vt_ref, n_past, new_tile(snew_ref))
    fill(kwt_ref, vwt_ref, 0, wst_ref[0])
    fill(kwt_ref, vwt_ref, w_keep, new_tile(wnew_ref))

    zero = jnp.zeros((tq, LANES), F32)
    tiles = [[], []]
    for h in range(N_HEADS):
        kvh = h // GROUP
        piece = qa_ref[:, h * LANES:(h + 1) * LANES]
        piece = piece if kvh % 2 == 0 else _swap_halves(piece)
        for ct in range(2):
            tiles[ct].append(piece if kvh // 2 == ct else zero)
    qbd = jnp.concatenate([jnp.concatenate(t, axis=0) for t in tiles], axis=1).astype(BF16)
    gate = gate_ref[...]

    def branch(k_t, v_t, bias, extra=None):
        s = jnp.dot(qbd, k_t, preferred_element_type=F32) + bias
        if extra is not None:
            s = s + extra
        p, l = _softmax_rows(s)
        inv = 1.0 / jnp.maximum(l, 1e-30)
        o_t = lax.dot_general(v_t, p.astype(BF16), _NT, preferred_element_type=F32)
        return p, inv, o_t.T

    def store(out_ref, o, scale):
        o = o * scale
        for kvh in range(KV_HEADS):
            tile = o[:, (kvh // 2) * LANES:(kvh // 2 + 1) * LANES]
            for gp in range(GROUP // 2):
                h0 = kvh * GROUP + 2 * gp
                col = kvh * (GROUP // 2) + gp
                out_ref[:, col * LANES:(col + 1) * LANES] = _pair_columns(
                    tile[h0 * tq:(h0 + 1) * tq], tile[(h0 + 1) * tq:(h0 + 2) * tq], kvh % 2)

    p, inv, o = branch(kvc_ref[0, :kw, :].astype(BF16), kvc_ref[0, kw:, :].astype(BF16), cbs_ref[...])
    store(oc_ref, o, inv * _gate_col(gate, 0))
    pc = p * inv
    pc_sum = jnp.concatenate(
        [sum(pc[(kvh * GROUP + g) * tq:(kvh * GROUP + g + 1) * tq] for g in range(GROUP)) for kvh in range(KV_HEADS)],
        axis=0)
    imp_t = lax.dot_general(mt_ref[...], pc_sum, _NT, preferred_element_type=F32, precision=lax.Precision.HIGHEST)
    n_col = KV_HEADS * tq
    nb_pad = -(-(-(-(n_past + tq) // L_SLC)) // 8) * 8
    tb = (n_past + lax.broadcasted_iota(jnp.int32, (1, n_col), 1) % tq) // L_SLC
    selneg_t = _select_blocks_t(imp_t[HEAD_DIM:HEAD_DIM + nb_pad], tb, N_SEL)
    sel_t = jnp.concatenate([jnp.zeros((HEAD_DIM, n_col), F32), selneg_t,
                             jnp.zeros((LANES - HEAD_DIM - nb_pad, n_col), F32)], axis=0)
    sel = jnp.concatenate([sel_t, jnp.zeros((LANES, LANES - n_col), F32)], axis=1).T
    sel = jnp.concatenate([sel[kvh * tq:(kvh + 1) * tq] for kvh in range(KV_HEADS) for _ in range(GROUP)], axis=0)
    block_mask = jnp.dot(sel.astype(BF16), e_ref[...], preferred_element_type=F32)
    _, inv, o = branch(kt_ref[...], vt_ref[...], sbs_ref[...], block_mask)
    store(os_ref, o, inv * _gate_col(gate, 1))
    _, inv, o = branch(kwt_ref[...], vwt_ref[...], wbs_ref[...])
    store(ow_ref, o, inv * _gate_col(gate, 2))


def _decode_attn(qa, kvc_t, pages_t, page_table, slc_new, win_state_t, win_new, m_t, e_s, cbs, sbs, wbs, gate_r,
                 n_seq, tq, n_pages):
    width = pages_t.shape[1]
    n_sk = sbs.shape[-1]
    n_wk = wbs.shape[-1]
    w_keep = win_state_t.shape[2]
    d_out = N_HEADS * HEAD_DIM
    rows = N_HEADS * tq
    page_spec = lambda j: pl.BlockSpec((1, width, PAGE_SIZE), lambda i, pt: (pt[i * n_pages + j], 0, 0))
    full = lambda a: pl.BlockSpec(a.shape, lambda i, pt: (0,) * a.ndim)
    row = lambda n: pl.BlockSpec((tq, n), lambda i, pt: (i, 0))
    grid_spec = pltpu.PrefetchScalarGridSpec(
        num_scalar_prefetch=1,
        grid=(n_seq,),
        in_specs=[page_spec(j) for j in range(n_pages)] + [
            row(N_HEADS * LANES),
            pl.BlockSpec((1,) + kvc_t.shape[1:], lambda i, pt: (i, 0, 0)),
            row(width),
            pl.BlockSpec((1, width, w_keep), lambda i, pt: (i, 0, 0)),
            row(width),
            full(m_t), full(e_s), full(cbs), full(sbs), full(wbs),
            pl.BlockSpec((rows, LANES), lambda i, pt: (i, 0))],
        out_specs=[row(d_out)] * 3,
        scratch_shapes=[pltpu.VMEM((width // 2, n_sk), BF16), pltpu.VMEM((width // 2, n_sk), BF16),
                        pltpu.VMEM((width // 2, n_wk), BF16), pltpu.VMEM((width // 2, n_wk), BF16)],
    )
    return pl.pallas_call(
        functools.partial(_decode_attn_body, n_pages=n_pages, tq=tq, w_keep=w_keep),
        grid_spec=grid_spec,
        out_shape=[jax.ShapeDtypeStruct((n_seq * tq, d_out), F32)] * 3,
        compiler_params=_cparams("parallel"),
        name="decode_attn",
    )(page_table, *([pages_t] * n_pages), qa, kvc_t, slc_new, win_state_t, win_new, m_t, e_s, cbs, sbs, wbs, gate_r)


def _overlap_matrix(n_rows, row0, n_cmp, n_blk):
    import numpy as np
    m = np.zeros((n_rows, LANES), np.float32)
    cs = np.arange(n_cmp)[:, None] * CMP_STRIDE
    js = np.arange(n_blk)[None, :] * L_SLC
    m[row0:row0 + n_cmp, HEAD_DIM:HEAD_DIM + n_blk] = (cs <= js + L_SLC - 1) & (cs + L_CMP - 1 >= js)
    return jnp.asarray(m)


def kernel(x_prompt, x_sample, state_conv, cache_cmp, cache_slc, state_win, page_table, rel_bias, a_norm_pre, a_w_in, a_conv_w, a_conv_b, a_ln_g, a_ln_b, a_w_out, a_norm_post, kv_norm, w_kv, cmp_pe, cmp_w1, cmp_w2, b_norm_pre, b_w_in, b_w_out, b_norm_post):
    import numpy as np
    bp, tp, d = x_prompt.shape
    bd, tq, _ = x_sample.shape
    n_pages = page_table.shape[1]
    past = n_pages * PAGE_SIZE
    w_keep = state_win.shape[1]
    width = 2 * KV_HEADS * HEAD_DIM
    kw = KV_HEADS * HEAD_DIM
    dq = N_HEADS * HEAD_DIM
    assert b_w_in.shape[0] == 1 and tp % (2 * KEY_TILE) == 0 and tp // L_SLC <= HEAD_DIM
    tm = 256
    tm_s = min(tm, bd * tq)

    xp = x_prompt
    xs = x_sample.reshape(bd * tq, d)
    conv_p, conv_s = [], []
    for l in range(a_w_in.shape[0]):
        w_in = a_w_in[l].astype(BF16)
        w_out = a_w_out[l].astype(BF16)
        di = a_w_out.shape[1]
        tail = (a_conv_w[l], a_conv_b[l], a_ln_g[l], a_ln_b[l], w_out, a_norm_post[l])
        glu, sz = _glu_proj(xp.reshape(bp * tp, d), a_norm_pre[l], w_in, tm)
        glu = glu.reshape(bp, tp, di)
        xp = _conv_prompt(glu, sz.reshape(bp, tp, di), xp, *tail, tm)
        conv_p.append(glu[:, -(CONV_W - 1):])
        glu, sz = _glu_proj(xs, a_norm_pre[l], w_in, tm_s)
        xs = _conv_sample(glu, state_conv[l], sz, xs, *tail, 16)
        conv_s.append(jnp.concatenate([state_conv[l], glu.reshape(bd, tq, di)], axis=1)[:, -(CONV_W - 1):])

    bw = b_w_in[0]
    n_gate = N_BRANCH * N_HEADS
    wq = bw[:, :dq].astype(BF16)
    wz = bw[:, dq:dq * (1 + N_BRANCH)].astype(BF16)
    wg = jnp.pad(bw[:, dq * (1 + N_BRANCH):], ((0, 0), (0, LANES - n_gate))).astype(BF16)
    wkv = w_kv.astype(BF16)
    xp2 = xp.reshape(bp * tp, d)
    (cmp_p, slc_p, win_p, qa_p, sz_p, gate_p, ska, sv, wka, wv) = _nsa_proj(
        xp2, kv_norm, b_norm_pre[0], wkv, wq, wz, wg, tm, seq_len=tp)
    cmp_s, slc_s, win_s, qa_s, sz_s, gate_s = _nsa_proj(xs, kv_norm, b_norm_pre[0], wkv, wq, wz, wg, tm_s)
    qa_s = qa_s.astype(F32)

    w1cat = jnp.concatenate([cmp_w1[:, :CMP_STRIDE * HEAD_DIM], cmp_w1[:, CMP_STRIDE * HEAD_DIM:]], axis=2).astype(BF16)
    pe8 = jnp.pad(cmp_pe.reshape(2, L_CMP // CMP_STRIDE, CMP_STRIDE * HEAD_DIM), ((0, 0), (0, 6), (0, 0)))
    zero = jnp.zeros_like(cmp_w2)
    w2h = jnp.stack([jnp.concatenate([cmp_w2, zero], axis=2), jnp.concatenate([zero, cmp_w2], axis=2)], axis=1).astype(BF16)
    prompt_page = min(1024, tp)
    pp = tp // prompt_page
    kvc_p = _compress(cmp_p, None, w1cat, pe8, w2h, pp, prompt_page).transpose(0, 2, 1)
    pt_flat = page_table.reshape(-1).astype(jnp.int32)
    cmp_pages_t = cache_cmp.transpose(0, 2, 3, 4, 1).reshape(-1, width, PAGE_SIZE)
    kvc_s = _compress(cmp_pages_t, pt_flat, w1cat, pe8, w2h, n_pages, PAGE_SIZE)

    n_chunk_p = kvc_p.shape[1]
    nc_p = n_chunk_p - L_CMP // CMP_STRIDE + 1
    real = (jnp.arange(n_chunk_p) < nc_p)[None, :, None]
    flag = jnp.zeros((LANES - HEAD_DIM,), F32).at[0].set(1.0)
    kc4 = jnp.where(real, kvc_p[:, :, :kw], 0.0).reshape(bp, n_chunk_p, KV_HEADS, HEAD_DIM)
    aug = jnp.where(real[..., None], 0.0, flag) * jnp.ones((bp, n_chunk_p, KV_HEADS, 1), F32)
    kc_real = jnp.concatenate([kc4, aug], axis=-1).reshape(bp, n_chunk_p, KV_HEADS * LANES)
    pad_row = jnp.concatenate([jnp.zeros((HEAD_DIM,), F32), flag])
    kc_front = jnp.broadcast_to(jnp.tile(pad_row, KV_HEADS), (bp, n_chunk_p, KV_HEADS * LANES))
    kc_pad = jnp.concatenate([kc_front, kc_real], axis=1)
    vc_pad = jnp.concatenate([jnp.zeros((bp, n_chunk_p, kw), F32), jnp.where(real, kvc_p[:, :, kw:], 0.0)], axis=1)
    m_pad = _overlap_matrix(2 * n_chunk_p, n_chunk_p, nc_p, tp // L_SLC)
    cb = _bias_tile(rel_bias, n_chunk_p, Q_BLK, 1, CMP_STRIDE * (n_chunk_p - Q_BLK // CMP_STRIDE) - (L_CMP - 1),
                    row_step=-CMP_STRIDE)
    cb = cb.reshape(KV_HEADS, GROUP, n_chunk_p, Q_BLK).transpose(0, 2, 1, 3).reshape(KV_HEADS, n_chunk_p, GROUP * Q_BLK)
    oc_p, qsel, qflag = _cmp_prompt(qa_p, kc_pad, vc_pad, m_pad, cb, gate_p, bp, tp)
    sb = jnp.stack([_bias_tile(rel_bias, KEY_TILE, Q_BLK, 1, off, row_step=-1)
                    for off in (0, Q_BLK, 2 * Q_BLK, 3 * Q_BLK)])
    sb = sb.reshape(4, KV_HEADS, GROUP, KEY_TILE, Q_BLK).transpose(0, 1, 3, 2, 4).reshape(4, KV_HEADS, KEY_TILE, GROUP * Q_BLK)
    os_p = _slc_prompt(qsel, ska, sv, sb, gate_p, bp, tp)
    n_wkey = WINDOW + Q_BLK
    wb = _bias_tile(rel_bias, n_wkey, Q_BLK, 1, WINDOW, WINDOW, row_step=-1)
    wb = wb.reshape(KV_HEADS, GROUP, n_wkey, Q_BLK).transpose(0, 2, 1, 3).reshape(KV_HEADS, n_wkey, GROUP * Q_BLK)
    wka_front = jnp.broadcast_to(jnp.tile(pad_row, KV_HEADS).astype(BF16), (bp, WINDOW, KV_HEADS * LANES))
    wka_pad = jnp.concatenate([wka_front, wka.reshape(bp, tp, -1)], axis=1)
    wv_pad = jnp.concatenate([jnp.zeros((kw, bp, WINDOW), BF16), wv.reshape(kw, bp, tp)], axis=2)
    ow_p = _win_prompt(qflag, wka_pad, wv_pad.reshape(kw, bp * (WINDOW + tp)), wb, gate_p, bp, tp)
    wo = b_w_out[0].astype(BF16)
    y_p = _nsa_out(oc_p.reshape(bp * tp, dq), os_p.reshape(bp * tp, dq), ow_p.reshape(bp * tp, dq), sz_p, xp2,
                   wo, b_norm_post[0], tm)

    n_chunk_s = kvc_s.shape[2]
    nc_s = n_chunk_s - L_CMP // CMP_STRIDE + 1
    nb_s = -(-(past + tq) // L_SLC)
    n_sk = past + LANES
    n_wk = w_keep + LANES
    m_t = _overlap_matrix(n_chunk_s, 0, nc_s, nb_s).T
    e_np = np.zeros((LANES, n_sk), np.float32)
    e_np[HEAD_DIM + np.arange(n_sk) // L_SLC, np.arange(n_sk)] = 1.0
    e_s = jnp.asarray(e_np, BF16)
    rows_s = N_HEADS * tq
    cbs = _bias_tile(rel_bias, tq, n_chunk_s, -CMP_STRIDE, past - (L_CMP - 1)).reshape(rows_s, n_chunk_s)
    sbs = _bias_tile(rel_bias, tq, n_sk, -1, past).reshape(rows_s, n_sk)
    wbs = _bias_tile(rel_bias, tq, n_wk, -1, w_keep, WINDOW).reshape(rows_s, n_wk)
    gate_r = gate_s[:, :n_gate].reshape(bd, tq, N_BRANCH, N_HEADS).transpose(0, 3, 1, 2).reshape(bd * rows_s, N_BRANCH)
    gate_r = jnp.pad(gate_r, ((0, 0), (0, LANES - N_BRANCH)))
    to_t = lambda a: a.transpose(0, 2, 3, 4, 1).reshape(a.shape[0], width, a.shape[1])
    oc_s, os_s, ow_s = _decode_attn(qa_s, kvc_s, to_t(cache_slc), pt_flat, slc_s, to_t(state_win), win_s,
                                    m_t, e_s, cbs, sbs, wbs, gate_r, bd, tq, n_pages)
    y_s = _nsa_out(oc_s, os_s, ow_s, sz_s, xs, wo, b_norm_post[0], tm_s)

    kv5 = lambda a, b, t: a.reshape(b, t, 2, KV_HEADS, HEAD_DIM)
    kv5_t = lambda a: a.reshape(bp, 2, KV_HEADS, HEAD_DIM, a.shape[2]).transpose(0, 4, 1, 2, 3)
    win_all = jnp.concatenate([state_win, kv5(win_s, bd, tq)], axis=1)
    return (y_p.reshape(bp, tp, d), y_s.reshape(bd, tq, d), jnp.stack(conv_p), jnp.stack(conv_s),
            kv5_t(cmp_p), kv5(cmp_s, bd, tq), kv5_t(slc_p), kv5(slc_s, bd, tq),
            kv5_t(win_p[:, :, -min(WINDOW, tp):]), win_all[:, -min(WINDOW, win_all.shape[1]):])
```

```python
import functools
import math

import jax
import jax.numpy as jnp
from jax import lax
from jax.experimental import pallas as pl
from jax.experimental.pallas import tpu as pltpu

F32 = jnp.float32
BF16 = jnp.bfloat16

EPS = 1e-6
NEG = -1e30
BIG = 1e9
M_FLOOR = -1e20

HEAD_DIM = 64
KV_HEADS = 4
N_HEADS = 16
GROUP = N_HEADS // KV_HEADS
N_BRANCH = 3
CONV_W = 31
L_CMP = 32
CMP_STRIDE = 16
L_SLC = 64
N_SEL = 16
WINDOW = 512
Q_BLK = 128
NUM_BUCKETS = 32
MAX_DISTANCE = 128
MAX_EXACT = NUM_BUCKETS // 2
PAGE_SIZE = 128
LANES = 128
HALO = 32
CONV_TAIL = 16

VMEM_LIMIT = 56 * 1024 * 1024


def _cparams(*sem):
    return pltpu.CompilerParams(dimension_semantics=sem, vmem_limit_bytes=VMEM_LIMIT)


def _sigmoid(x):
    return 1.0 / (1.0 + jnp.exp(-x))


def _rms(x, g):
    return x * lax.rsqrt(jnp.mean(x * x, axis=-1, keepdims=True) + EPS) * g


def _lane(shape):
    return lax.broadcasted_iota(jnp.int32, shape, len(shape) - 1)


def _swap_halves(x):
    return pltpu.roll(x, HEAD_DIM, axis=x.ndim - 1)


def _glu_proj_body(x_ref, g_ref, w_ref, glu_ref, sz_ref):
    di = glu_ref.shape[-1]
    h = _rms(x_ref[...], g_ref[...])
    u = jnp.dot(h.astype(BF16), w_ref[...], preferred_element_type=F32)
    z = u[:, 2 * di:]
    glu_ref[...] = u[:, :di] * _sigmoid(u[:, di:2 * di])
    sz_ref[...] = z * _sigmoid(z)


def _glu_proj(x, g, w_bf, tm):
    rows, d = x.shape
    di = w_bf.shape[1] // 3
    return pl.pallas_call(
        _glu_proj_body,
        grid=(rows // tm,),
        in_specs=[pl.BlockSpec((tm, d), lambda i: (i, 0)),
                  pl.BlockSpec((1, d), lambda i: (0, 0)),
                  pl.BlockSpec((d, 3 * di), lambda i: (0, 0))],
        out_specs=[pl.BlockSpec((tm, di), lambda i: (i, 0)),
                   pl.BlockSpec((tm, di), lambda i: (i, 0))],
        out_shape=[jax.ShapeDtypeStruct((rows, di), F32)] * 2,
        compiler_params=_cparams("parallel"),
        name="glu_proj",
    )(x, g.reshape(1, d), w_bf)


def _conv_tail(c, sz, x, lg_ref, lb_ref, wo_ref, gp_ref):
    mu = jnp.mean(c, axis=-1, keepdims=True)
    cc = c - mu
    var = jnp.mean(cc * cc, axis=-1, keepdims=True)
    y = cc * lax.rsqrt(var + EPS) * lg_ref[...] + lb_ref[...]
    y = y * _sigmoid(y) * sz
    o = jnp.dot(y.astype(BF16), wo_ref[...], preferred_element_type=F32)
    return x + _rms(o, gp_ref[...])


def _conv_prompt_body(glu_ref, prev_ref, sz_ref, x_ref, cw_ref, cb_ref, lg_ref, lb_ref, wo_ref, gp_ref,
                      out_ref, full_ref, c_ref, *, tm):
    t = pl.program_id(1)
    full_ref[0:HALO, :] = jnp.where(t > 0, prev_ref[0], 0.0)
    full_ref[HALO:HALO + tm, :] = glu_ref[0]
    full_ref[HALO + tm:, :] = jnp.zeros((CONV_TAIL, full_ref.shape[1]), F32)
    d = c_ref.shape[-1]
    first = HALO - (CONV_W - 1)
    sub = 8
    half = tm // 2
    for lc in range(d // LANES):
        ln = slice(lc * LANES, (lc + 1) * LANES)
        for r0 in range(0, tm, half):
            acc = jnp.zeros((half, LANES), F32)
            for b in range(sub):
                z = None
                for a in range(-(-(CONV_W - b) // sub)):
                    x = full_ref[r0 + sub * a:r0 + sub * a + half + 2 * sub, ln]
                    term = cw_ref[sub * a + b:sub * a + b + 1, ln] * x
                    z = term if z is None else z + term
                acc = acc + z[first + b:first + b + half]
            c_ref[r0:r0 + half, ln] = acc + cb_ref[:, ln]
    out_ref[0] = _conv_tail(c_ref[...], sz_ref[0], x_ref[0], lg_ref, lb_ref, wo_ref, gp_ref)


def _conv_prompt(glu, sz, x, cw, cb, lg, lb, wo_bf, gp, tm):
    b, t, d = x.shape
    di = glu.shape[-1]
    per = tm // HALO
    vec = lambda n: pl.BlockSpec((1, n), lambda i, j: (0, 0))
    return pl.pallas_call(
        functools.partial(_conv_prompt_body, tm=tm),
        grid=(b, t // tm),
        in_specs=[pl.BlockSpec((1, tm, di), lambda i, j: (i, j, 0)),
                  pl.BlockSpec((1, HALO, di), lambda i, j: (i, jnp.maximum(j * per - 1, 0), 0)),
                  pl.BlockSpec((1, tm, di), lambda i, j: (i, j, 0)),
                  pl.BlockSpec((1, tm, d), lambda i, j: (i, j, 0)),
                  pl.BlockSpec((CONV_W, di), lambda i, j: (0, 0)),
                  vec(di), vec(di), vec(di),
                  pl.BlockSpec((di, d), lambda i, j: (0, 0)),
                  vec(d)],
        out_specs=pl.BlockSpec((1, tm, d), lambda i, j: (i, j, 0)),
        out_shape=jax.ShapeDtypeStruct((b, t, d), F32),
        scratch_shapes=[pltpu.VMEM((HALO + tm + CONV_TAIL, di), F32), pltpu.VMEM((tm, di), F32)],
        compiler_params=_cparams("parallel", "arbitrary"),
        name="conv_prompt",
    )(glu, glu, sz, x, cw, cb.reshape(1, di), lg.reshape(1, di), lb.reshape(1, di), wo_bf, gp.reshape(1, d))


def _conv_sample_body(glu_ref, st_ref, sz_ref, x_ref, cw_ref, cb_ref, lg_ref, lb_ref, wo_ref, gp_ref,
                      out_ref, full_ref, c_ref, *, nb, tq):
    d = c_ref.shape[-1]
    first = HALO - (CONV_W - 1)
    full_ref[:, first:HALO, :] = st_ref[...]
    full_ref[:, HALO:HALO + tq, :] = glu_ref[...].reshape(nb, tq, d)
    for lc in range(d // LANES):
        ln = slice(lc * LANES, (lc + 1) * LANES)
        acc = jnp.zeros((nb, tq, LANES), F32)
        for j in range(CONV_W):
            acc = acc + cw_ref[j:j + 1, ln] * full_ref[:, first + j:first + j + tq, ln]
        c_ref[:, ln] = (acc + cb_ref[:, ln]).reshape(nb * tq, LANES)
    out_ref[...] = _conv_tail(c_ref[...], sz_ref[...], x_ref[...], lg_ref, lb_ref, wo_ref, gp_ref)


def _conv_sample(glu, state, sz, x, cw, cb, lg, lb, wo_bf, gp, nb):
    n_seq = state.shape[0]
    rows, d = x.shape
    di = glu.shape[-1]
    tq = rows // n_seq
    vec = lambda n: pl.BlockSpec((1, n), lambda i: (0, 0))
    return pl.pallas_call(
        functools.partial(_conv_sample_body, nb=nb, tq=tq),
        grid=(n_seq // nb,),
        in_specs=[pl.BlockSpec((nb * tq, di), lambda i: (i, 0)),
                  pl.BlockSpec((nb, CONV_W - 1, di), lambda i: (i, 0, 0)),
                  pl.BlockSpec((nb * tq, di), lambda i: (i, 0)),
                  pl.BlockSpec((nb * tq, d), lambda i: (i, 0)),
                  pl.BlockSpec((CONV_W, di), lambda i: (0, 0)),
                  vec(di), vec(di), vec(di),
                  pl.BlockSpec((di, d), lambda i: (0, 0)),
                  vec(d)],
        out_specs=pl.BlockSpec((nb * tq, d), lambda i: (i, 0)),
        out_shape=jax.ShapeDtypeStruct((rows, d), F32),
        scratch_shapes=[pltpu.VMEM((nb, HALO + tq, di), F32), pltpu.VMEM((nb * tq, di), F32)],
        compiler_params=_cparams("parallel"),
        name="conv_sample",
    )(glu, state, sz, x, cw, cb.reshape(1, di), lg.reshape(1, di), lb.reshape(1, di), wo_bf, gp.reshape(1, d))


def _head_major(col_pair, head):
    return col_pair if head % 2 == 0 else _swap_halves(col_pair)


def _nsa_proj_body(x_ref, gkv_ref, gq_ref, wkv_ref, wq_ref, wz_ref, wg_ref,
                   cmp_ref, slc_ref, win_ref, qa_ref, sz_ref, gate_ref, *aug_refs, seq_len, tm):
    x = x_ref[...]
    xn = x * lax.rsqrt(jnp.mean(x * x, axis=-1, keepdims=True) + EPS)
    hkv = (xn * gkv_ref[...]).astype(BF16)
    hq = (xn * gq_ref[...]).astype(BF16)
    kv = jnp.dot(hkv, wkv_ref[...], preferred_element_type=F32)
    width = 2 * KV_HEADS * HEAD_DIM
    if aug_refs:
        kv_t = [kv[:, b * width:(b + 1) * width].T for b in range(N_BRANCH)]
        cmp_ref[0] = kv_t[0]
        slc_ref[0] = kv_t[1]
        win_ref[0] = kv_t[2]
    else:
        cmp_ref[...] = kv[:, :width]
        slc_ref[...] = kv[:, width:2 * width]
        win_ref[...] = kv[:, 2 * width:]
    lane = _lane((tm, LANES))
    low = lane < HEAD_DIM
    uq = jnp.dot(hq, wq_ref[...], preferred_element_type=F32)
    scale = HEAD_DIM ** -0.5
    for h in range(N_HEADS):
        qh = _head_major(uq[:, (h // 2) * LANES:(h // 2 + 1) * LANES], h)
        qa_ref[:, h * LANES:(h + 1) * LANES] = jnp.where(low, qh * scale, 0.0).astype(BF16)
    z = jnp.dot(hq, wz_ref[...], preferred_element_type=F32)
    sz_ref[...] = z * _sigmoid(z)
    gate_ref[...] = _sigmoid(jnp.dot(hq, wg_ref[...], preferred_element_type=F32))
    if aug_refs:
        ska_ref, sv_ref, wka_ref, wv_ref = aug_refs
        row = pl.program_id(0) * tm + lax.broadcasted_iota(jnp.int32, (tm, LANES), 0)
        blk = (row % seq_len) // L_SLC
        onehot = (lane - HEAD_DIM == blk).astype(F32)
        kw = KV_HEADS * HEAD_DIM
        for h in range(KV_HEADS):
            ks = _head_major(kv[:, width + (h // 2) * LANES:width + (h // 2 + 1) * LANES], h)
            ska_ref[:, h * LANES:(h + 1) * LANES] = jnp.where(low, ks, onehot).astype(BF16)
            kwn = _head_major(kv[:, 2 * width + (h // 2) * LANES:2 * width + (h // 2 + 1) * LANES], h)
            wka_ref[:, h * LANES:(h + 1) * LANES] = jnp.where(low, kwn, 0.0).astype(BF16)
        sv_ref[...] = kv_t[1][kw:].astype(BF16)
        wv_ref[...] = kv_t[2][kw:].astype(BF16)


def _nsa_proj(x, g_kv, g_q, wkv_bf, wq_bf, wz_bf, wg_bf, tm, seq_len=None):
    rows, d = x.shape
    width = 2 * KV_HEADS * HEAD_DIM
    row = lambda n: pl.BlockSpec((tm, n), lambda i: (i, 0))
    full = lambda a: pl.BlockSpec(a.shape, lambda i: (0, 0))
    out_specs = [row(width), row(width), row(width), row(N_HEADS * LANES), row(wz_bf.shape[1]), row(LANES)]
    out_shape = [jax.ShapeDtypeStruct((rows, width), F32)] * 3 + [
        jax.ShapeDtypeStruct((rows, N_HEADS * LANES), BF16),
        jax.ShapeDtypeStruct((rows, wz_bf.shape[1]), F32),
        jax.ShapeDtypeStruct((rows, LANES), F32)]
    if seq_len is not None:
        per_seq = seq_len // tm
        kv_t_spec = pl.BlockSpec((1, width, tm), lambda i: (i // per_seq, 0, i % per_seq))
        out_specs[:N_BRANCH] = [kv_t_spec] * N_BRANCH
        out_shape[:N_BRANCH] = [jax.ShapeDtypeStruct((rows // seq_len, width, seq_len), F32)] * N_BRANCH
        out_specs += [row(KV_HEADS * LANES), pl.BlockSpec((width // 2, tm), lambda i: (0, i))] * 2
        out_shape += [jax.ShapeDtypeStruct((rows, KV_HEADS * LANES), BF16),
                      jax.ShapeDtypeStruct((width // 2, rows), BF16)] * 2
    gkv = g_kv.reshape(1, d)
    gq = g_q.reshape(1, d)
    return pl.pallas_call(
        functools.partial(_nsa_proj_body, seq_len=seq_len, tm=tm),
        grid=(rows // tm,),
        in_specs=[row(d), full(gkv), full(gq), full(wkv_bf), full(wq_bf), full(wz_bf), full(wg_bf)],
        out_specs=out_specs,
        out_shape=out_shape,
        compiler_params=_cparams("parallel"),
        name="nsa_proj",
    )(x, gkv, gq, wkv_bf, wq_bf, wz_bf, wg_bf)


def _compress_body(pt_ref, *refs, n_pages):
    del pt_ref
    page_refs = refs[:n_pages]
    w1_ref, pe_ref, w2_ref, out_ref, a_ref = refs[n_pages:]
    width, page_rows = page_refs[0].shape[1:]
    n_col = width // LANES
    cpp = page_rows // CMP_STRIDE
    n_chunk = n_pages * cpp
    hid = w2_ref.shape[2]
    low = _lane((cpp, LANES)) < HEAD_DIM
    def assemble(c):
        for p in range(n_pages):
            col_t = page_refs[p][0, c * LANES:(c + 1) * LANES, :]
            by_s = pltpu.einshape("(ns)f->(sn)f", col_t.T, n=cpp)
            for s in range(0, CMP_STRIDE, 2):
                b0 = by_s[s * cpp:(s + 1) * cpp]
                b1 = by_s[(s + 1) * cpp:(s + 2) * cpp]
                dst = (slice(p * cpp, (p + 1) * cpp), slice((s // 2) * LANES, (s // 2 + 1) * LANES))
                a_ref[(2 * c,) + dst] = jnp.where(low, b0, _swap_halves(b1))
                a_ref[(2 * c + 1,) + dst] = jnp.where(low, _swap_halves(b0), b1)

    for kv in range(2):
        for c in range(kv * n_col // 2, (kv + 1) * n_col // 2):
            assemble(c)
        w1 = w1_ref[kv]
        pe = jnp.dot(pe_ref[kv].astype(BF16), w1, preferred_element_type=F32)
        pe_term = pe[0:1, :hid] + pe[1:2, hid:]
        a_kv = a_ref[kv * KV_HEADS:(kv + 1) * KV_HEADS].reshape(KV_HEADS * n_chunk, CMP_STRIDE * HEAD_DIM)
        parts = jnp.dot(a_kv.astype(BF16), w1, preferred_element_type=F32)
        for hp in range(KV_HEADS // 2):
            pair = jnp.zeros((n_chunk, LANES), F32)
            for par in range(2):
                part = parts[(2 * hp + par) * n_chunk:(2 * hp + par + 1) * n_chunk]
                pre = part[:, :hid] + pltpu.roll(part[:, hid:], n_chunk - 1, axis=0) + pe_term
                mid = pre * _sigmoid(pre)
                pair = pair + jnp.dot(mid.astype(BF16), w2_ref[kv, par], preferred_element_type=F32)
            col = kv * (KV_HEADS // 2) + hp
            out_ref[0, col * LANES:(col + 1) * LANES, :] = pair.T


def _compress(rows_t, page_table, w1cat_bf, pe8, w2_bf, n_pages, page_rows):
    width = rows_t.shape[1]
    n_chunk = n_pages * page_rows // CMP_STRIDE
    if page_table is None:
        n_seq = rows_t.shape[0]
        page_table = jnp.zeros((1,), jnp.int32)
        page_spec = lambda j: pl.BlockSpec((1, width, page_rows), lambda i, pt: (i, 0, j))
    else:
        n_seq = page_table.shape[0] // n_pages
        page_spec = lambda j: pl.BlockSpec((1, width, page_rows), lambda i, pt: (pt[i * n_pages + j], 0, 0))
    full = lambda a: pl.BlockSpec(a.shape, lambda i, pt: (0,) * a.ndim)
    grid_spec = pltpu.PrefetchScalarGridSpec(
        num_scalar_prefetch=1,
        grid=(n_seq,),
        in_specs=[page_spec(j) for j in range(n_pages)] + [full(w1cat_bf), full(pe8), full(w2_bf)],
        out_specs=pl.BlockSpec((1, width, n_chunk), lambda i, pt: (i, 0, 0)),
        scratch_shapes=[pltpu.VMEM((2 * KV_HEADS, n_chunk, CMP_STRIDE * HEAD_DIM), F32)],
    )
    return pl.pallas_call(
        functools.partial(_compress_body, n_pages=n_pages),
        grid_spec=grid_spec,
        out_shape=jax.ShapeDtypeStruct((n_seq, width, n_chunk), F32),
        compiler_params=_cparams("parallel"),
        name="compress",
    )(page_table, *([rows_t] * n_pages), w1cat_bf, pe8, w2_bf)


def _bias_tile_body(rb_ref, out_ref, *, row_step, lane_step, offset, hi):
    h = pl.program_id(0)
    shape = out_ref.shape[1:]
    d = row_step * lax.broadcasted_iota(jnp.int32, shape, 0) + lane_step * _lane(shape) + offset
    n = jnp.maximum(d, 0)
    nf = jnp.maximum(n, 1).astype(F32)
    large = MAX_EXACT + (jnp.log(nf / MAX_EXACT) / math.log(MAX_DISTANCE / MAX_EXACT)
                         * (NUM_BUCKETS - MAX_EXACT)).astype(jnp.int32)
    large = jnp.minimum(large, NUM_BUCKETS - 1)
    bucket = jnp.where(n < MAX_EXACT, n, large)
    far = rb_ref[NUM_BUCKETS - 1, h]
    val = jnp.zeros(shape, F32)
    for k in range(NUM_BUCKETS - 1):
        val = jnp.where(bucket == k, rb_ref[k, h] - far, val)
    out_ref[0] = jnp.where((d >= 0) & (d <= hi), val, NEG)


def _bias_tile(rel_bias, rows, width, lane_step, offset, hi=1 << 30, row_step=1):
    return pl.pallas_call(
        functools.partial(_bias_tile_body, row_step=row_step, lane_step=lane_step, offset=offset, hi=hi),
        grid=(N_HEADS,),
        in_specs=[pl.BlockSpec(memory_space=pltpu.SMEM)],
        out_specs=pl.BlockSpec((1, rows, width), lambda h: (h, 0, 0)),
        out_shape=jax.ShapeDtypeStruct((N_HEADS, rows, width), F32),
        compiler_params=_cparams("parallel"),
        name="bias_tile",
    )(rel_bias)


_NT = (((1,), (1,)), ((), ()))


def _softmax_rows(s):
    m = jnp.maximum(jnp.max(s, axis=-1, keepdims=True), M_FLOOR)
    p = jnp.exp(s - m)
    return p, jnp.sum(p, axis=-1, keepdims=True)


def _gate_col(gate, col):
    return jnp.sum(jnp.where(_lane(gate.shape) == col, gate, 0.0), axis=-1, keepdims=True)


def _pair_columns(o_even, o_odd, valid_half):
    lo = o_even if valid_half == 0 else _swap_halves(o_even)
    hi = o_odd if valid_half == 1 else _swap_halves(o_odd)
    return jnp.where(_lane(lo.shape) < HEAD_DIM, lo, hi)


def _select_blocks_t(imp_t, tb, n_sel):
    nb, nt = imp_t.shape
    sub = 8
    j = lax.broadcasted_iota(jnp.int32, (nb, nt), 0)
    valid = j <= tb
    forced = (j == 0) | (j == tb) | (j == tb - 1)
    score = jnp.where(valid, jnp.where(forced, BIG, imp_t), -BIG)
    groups = [score[lo:lo + sub] for lo in range(0, nb, sub)]
    jr = lax.broadcasted_iota(jnp.int32, (sub, nt), 0)
    counts = [jnp.zeros((sub, nt), F32) for _ in groups]
    for i in range(nb):
        row = score[i:i + 1]
        for r, grp in enumerate(groups):
            lo = r * sub
            if lo > i:
                one = jnp.where(row >= grp, 1.0, 0.0)
            elif lo + sub - 1 <= i:
                one = jnp.where(row > grp, 1.0, 0.0)
            else:
                one = jnp.where(jr + lo > i, jnp.where(row >= grp, 1.0, 0.0), jnp.where(row > grp, 1.0, 0.0))
            counts[r] = counts[r] + one
    rank = jnp.concatenate(counts, axis=0)
    return jnp.where((rank < n_sel) & valid, 0.0, NEG)


def _select_blocks(imp, tb, n_sel):
    lane = _lane(imp.shape)
    lane_f = lane.astype(F32)
    j = lane - HEAD_DIM
    valid = (j >= 0) & (j <= tb)
    forced = (j == 0) | (j == tb) | (j == tb - 1)
    score = jnp.where(valid, jnp.where(forced, BIG, imp), -BIG)
    score = jnp.where(j >= 0, score, -jnp.inf)
    sel = jnp.zeros(imp.shape, jnp.bool_)
    for _ in range(n_sel):
        best = jnp.max(score, axis=-1, keepdims=True)
        first = jnp.min(jnp.where(score == best, lane_f, 4.0 * LANES), axis=-1, keepdims=True)
        pick = lane_f == first
        sel = sel | pick
        score = jnp.where(pick, -jnp.inf, score)
    return jnp.where(sel & valid, 0.0, jnp.where(j >= 0, NEG, 0.0))


def _cmp_prompt_body(qa_ref, kc_ref, vc_ref, m_ref, cb_ref, gate_ref, oc_ref, qsel_ref, qflag_ref, *, n_key):
    qt = pl.program_id(1)
    start = pl.multiple_of(8 * qt + 8, 8)
    kwin = kc_ref[0, pl.ds(start, n_key), :]
    vwin = vc_ref[0, pl.ds(start, n_key), :]
    mwin_t = m_ref[pl.ds(start, n_key), :].T
    feat = lax.broadcasted_iota(jnp.int32, (LANES, Q_BLK), 0)
    pos = qt * Q_BLK + lax.broadcasted_iota(jnp.int32, (1, Q_BLK), 1)
    tb = pos // L_SLC
    gate = gate_ref[...]
    n_pair = GROUP // 2
    for kvh in range(KV_HEADS):
        ka = kwin[:, kvh * LANES:(kvh + 1) * LANES].astype(BF16)
        vpt = vwin[:, (kvh // 2) * LANES:(kvh // 2 + 1) * LANES].T
        vpt = vpt[(kvh % 2) * HEAD_DIM:(kvh % 2 + 1) * HEAD_DIM].astype(BF16)
        qts = [qa_ref[0, :, h * LANES:(h + 1) * LANES].astype(F32).T for h in range(kvh * GROUP, (kvh + 1) * GROUP)]
        pc_sum = jnp.zeros((n_key, Q_BLK), F32)
        outs = []
        for gp in range(n_pair):
            qa = jnp.concatenate([jnp.where(feat == HEAD_DIM, NEG, qts[2 * gp + e]) for e in range(2)], axis=1)
            qa = qa.astype(BF16)
            qflag_ref[0, kvh, :, 2 * gp * Q_BLK:(2 * gp + 2) * Q_BLK] = qa
            s = jnp.dot(ka, qa, preferred_element_type=F32)
            s = s + cb_ref[kvh, :, 2 * gp * Q_BLK:(2 * gp + 2) * Q_BLK]
            m = jnp.maximum(jnp.max(s, axis=0, keepdims=True), M_FLOOR)
            p = jnp.exp(s - m)
            inv = 1.0 / jnp.maximum(jnp.sum(p, axis=0, keepdims=True), 1e-30)
            pc = p * inv
            pc_sum = pc_sum + pc[:, :Q_BLK] + pc[:, Q_BLK:]
            o = jnp.dot(vpt, p.astype(BF16), preferred_element_type=F32) * inv
            _store_chain(oc_ref, o, gate, kvh * GROUP + 2 * gp, kvh * n_pair + gp)
        imp_t = jnp.dot(mwin_t, pc_sum, preferred_element_type=F32, precision=lax.Precision.HIGHEST)
        selneg_t = _select_blocks_t(imp_t[HEAD_DIM:], tb, N_SEL)
        for g in range(GROUP):
            qsel_ref[0, kvh, :, g * Q_BLK:(g + 1) * Q_BLK] = jnp.concatenate(
                [qts[g][:HEAD_DIM], selneg_t], axis=0).astype(BF16)


def _cmp_prompt(qa, kc_pad, vc_pad, m_pad, cb, gate, batch, seq):
    nq = seq // Q_BLK
    n_key = kc_pad.shape[1] // 2
    d_out = N_HEADS * HEAD_DIM
    return pl.pallas_call(
        functools.partial(_cmp_prompt_body, n_key=n_key),
        grid=(batch, nq),
        in_specs=[pl.BlockSpec((1, Q_BLK, N_HEADS * LANES), lambda b, t: (b, t, 0)),
                  pl.BlockSpec((1,) + kc_pad.shape[1:], lambda b, t: (b, 0, 0)),
                  pl.BlockSpec((1,) + vc_pad.shape[1:], lambda b, t: (b, 0, 0)),
                  pl.BlockSpec(m_pad.shape, lambda b, t: (0, 0)),
                  pl.BlockSpec(cb.shape, lambda b, t: (0, 0, 0)),
                  pl.BlockSpec((Q_BLK, LANES), lambda b, t: (b * nq + t, 0))],
        out_specs=[pl.BlockSpec((1, Q_BLK, d_out), lambda b, t: (b, t, 0))]
        + [pl.BlockSpec((1, KV_HEADS, LANES, GROUP * Q_BLK), lambda b, t: (b * nq + t, 0, 0, 0))] * 2,
        out_shape=[jax.ShapeDtypeStruct((batch, seq, d_out), F32)]
        + [jax.ShapeDtypeStruct((batch * nq, KV_HEADS, LANES, GROUP * Q_BLK), BF16)] * 2,
        compiler_params=_cparams("parallel", "parallel"),
        name="cmp_prompt",
    )(qa.reshape(batch, seq, -1), kc_pad, vc_pad, m_pad, cb, gate)


KEY_TILE = 256
COL_CHAIN = 256
V_PAD = 16
V_ROWS = HEAD_DIM + V_PAD


def _with_ones_rows(v_t):
    n = v_t.shape[1]
    extra = jnp.zeros((KV_HEADS, V_PAD, n), v_t.dtype).at[:, 0].set(1)
    return jnp.concatenate([v_t.reshape(KV_HEADS, HEAD_DIM, n), extra], axis=1).reshape(KV_HEADS * V_ROWS, n)


def _store_chain(out_ref, o_t, gate, gate_col0, lane_col):
    pair = jnp.concatenate([o_t[:, :Q_BLK], o_t[:, Q_BLK:]], axis=0).T
    g = jnp.where(_lane(pair.shape) < HEAD_DIM, _gate_col(gate, gate_col0), _gate_col(gate, gate_col0 + 1))
    out_ref[0, :, lane_col * LANES:(lane_col + 1) * LANES] = pair * g


_PAIR_CHAINS = [(e, c) for e in range(2) for c in range(GROUP * Q_BLK // COL_CHAIN)]


def _slc_prompt_body(q_ref, k_ref, v_ref, sb_ref, gate_ref, out_ref, m_ref, acc_ref, p_ref):
    pr = pl.program_id(1)
    qt = pl.program_id(2)
    m_ref[...] = jnp.full(m_ref.shape, M_FLOOR, F32)
    acc_ref[...] = jnp.zeros(acc_ref.shape, F32)
    p_ref[...] = jnp.zeros(p_ref.shape, BF16)
    last = (qt * Q_BLK) // KEY_TILE
    cs = lambda c: slice(c * COL_CHAIN, (c + 1) * COL_CHAIN)

    def pv_of(kt):
        base = pl.multiple_of(kt * KEY_TILE, KEY_TILE)
        return [jnp.dot(v_ref[e * V_ROWS:(e + 1) * V_ROWS, pl.ds(base, KEY_TILE)], p_ref[i],
                        preferred_element_type=F32) for i, (e, c) in enumerate(_PAIR_CHAINS)]

    def make_step(with_bias):
        def step(kt, carry):
            base = pl.multiple_of(kt * KEY_TILE, KEY_TILE)
            case = qt % 2 + 2 * (last - kt)
            ks = [k_ref[0, pl.ds(base, KEY_TILE), e * LANES:(e + 1) * LANES] for e in range(2)]
            ss = [jnp.dot(ks[e], q_ref[0, e, :, cs(c)], preferred_element_type=F32) for e, c in _PAIR_CHAINS]
            pvs = pv_of(jnp.maximum(kt - 1, 0))
            if with_bias:
                ss = [s + sb_ref[case, e, :, cs(c)] for s, (e, c) in zip(ss, _PAIR_CHAINS)]
            for i, s in enumerate(ss):
                m_old = m_ref[i]
                m_new = jnp.maximum(m_old, jnp.max(s, axis=0, keepdims=True))
                alpha = jnp.exp(m_old - m_new)
                p = jnp.exp(s - m_new)
                m_ref[i] = m_new
                acc_ref[i] = alpha * (acc_ref[i] + pvs[i])
                p_ref[i] = p.astype(BF16)
            return carry
        return step

    n_far = jnp.maximum(last - 1, 0)
    lax.fori_loop(0, n_far, make_step(False), 0)
    lax.fori_loop(n_far, last + 1, make_step(True), 0)
    gate = gate_ref[...]
    pvs = pv_of(last)
    for i, (e, c) in enumerate(_PAIR_CHAINS):
        o = acc_ref[i] + pvs[i]
        o = o[:HEAD_DIM] * (1.0 / jnp.maximum(o[HEAD_DIM:HEAD_DIM + 1], 1e-30))
        _store_chain(out_ref, o, gate, N_HEADS + (2 * pr + e) * GROUP + 2 * c, i)


def _slc_prompt(qsel_t, ska, sv_t, sb_t, gate, batch, seq):
    nq = seq // Q_BLK
    cols = GROUP * Q_BLK
    d_out = N_HEADS * HEAD_DIM
    n_chain = len(_PAIR_CHAINS)
    return pl.pallas_call(
        _slc_prompt_body,
        grid=(batch, KV_HEADS // 2, nq),
        in_specs=[pl.BlockSpec((1, 2, LANES, cols), lambda b, p, t: (b * nq + t, p, 0, 0)),
                  pl.BlockSpec((1, seq, 2 * LANES), lambda b, p, t: (b, 0, p)),
                  pl.BlockSpec((2 * V_ROWS, seq), lambda b, p, t: (p, b)),
                  pl.BlockSpec((4, 2, KEY_TILE, cols), lambda b, p, t: (0, p, 0, 0)),
                  pl.BlockSpec((Q_BLK, LANES), lambda b, p, t: (b * nq + t, 0))],
        out_specs=pl.BlockSpec((1, Q_BLK, 2 * GROUP * HEAD_DIM), lambda b, p, t: (b, t, p)),
        out_shape=jax.ShapeDtypeStruct((batch, seq, d_out), F32),
        scratch_shapes=[pltpu.VMEM((n_chain, 1, COL_CHAIN), F32),
                        pltpu.VMEM((n_chain, V_ROWS, COL_CHAIN), F32),
                        pltpu.VMEM((n_chain, KEY_TILE, COL_CHAIN), BF16)],
        compiler_params=_cparams("parallel", "parallel", "arbitrary"),
        name="slc_prompt",
    )(qsel_t, ska.reshape(batch, seq, -1), sv_t, sb_t, gate)


def _win_prompt_body(q_ref, k_ref, v_ref, wb_ref, gate_ref, out_ref, *, n_key):
    pr = pl.program_id(1)
    qt = pl.program_id(2)
    base = pl.multiple_of(qt * Q_BLK, Q_BLK)
    cs = lambda c: slice(c * COL_CHAIN, (c + 1) * COL_CHAIN)
    ks = [k_ref[0, pl.ds(base, n_key), e * LANES:(e + 1) * LANES] for e in range(2)]
    ss = [jnp.dot(ks[e], q_ref[0, e, :, cs(c)], preferred_element_type=F32) + wb_ref[e, :, cs(c)]
          for e, c in _PAIR_CHAINS]
    ps = []
    for s in ss:
        m = jnp.maximum(jnp.max(s, axis=0, keepdims=True), M_FLOOR)
        ps.append(jnp.exp(s - m).astype(BF16))
    gate = gate_ref[...]
    for i, (e, c) in enumerate(_PAIR_CHAINS):
        o = jnp.dot(v_ref[e * V_ROWS:(e + 1) * V_ROWS, pl.ds(base, n_key)], ps[i], preferred_element_type=F32)
        o = o[:HEAD_DIM] * (1.0 / jnp.maximum(o[HEAD_DIM:HEAD_DIM + 1], 1e-30))
        _store_chain(out_ref, o, gate, 2 * N_HEADS + (2 * pr + e) * GROUP + 2 * c, i)


def _win_prompt(q_t, wka_pad, wv_t_pad, wb_t, gate, batch, seq):
    nq = seq // Q_BLK
    n_key = WINDOW + Q_BLK
    cols = GROUP * Q_BLK
    d_out = N_HEADS * HEAD_DIM
    padded = wka_pad.shape[1]
    return pl.pallas_call(
        functools.partial(_win_prompt_body, n_key=n_key),
        grid=(batch, KV_HEADS // 2, nq),
        in_specs=[pl.BlockSpec((1, 2, LANES, cols), lambda b, p, t: (b * nq + t, p, 0, 0)),
                  pl.BlockSpec((1, padded, 2 * LANES), lambda b, p, t: (b, 0, p)),
                  pl.BlockSpec((2 * V_ROWS, padded), lambda b, p, t: (p, b)),
                  pl.BlockSpec((2, n_key, cols), lambda b, p, t: (p, 0, 0)),
                  pl.BlockSpec((Q_BLK, LANES), lambda b, p, t: (b * nq + t, 0))],
        out_specs=pl.BlockSpec((1, Q_BLK, 2 * GROUP * HEAD_DIM), lambda b, p, t: (b, t, p)),
        out_shape=jax.ShapeDtypeStruct((batch, seq, d_out), F32),
        compiler_params=_cparams("parallel", "parallel", "parallel"),
        name="win_prompt",
    )(q_t, wka_pad, wv_t_pad, wb_t, gate)


def _nsa_out_body(oc_ref, os_ref, ow_ref, sz_ref, x_ref, wo_ref, gp_ref, out_ref):
    d = oc_ref.shape[-1]
    y = oc_ref[...] * sz_ref[:, :d] + os_ref[...] * sz_ref[:, d:2 * d] + ow_ref[...] * sz_ref[:, 2 * d:]
    o = jnp.dot(y.astype(BF16), wo_ref[...], preferred_element_type=F32)
    out_ref[...] = x_ref[...] + _rms(o, gp_ref[...])


def _nsa_out(oc, os_, ow, sz, x, wo_bf, gp, tm):
    rows, d = x.shape
    dq = oc.shape[-1]
    row = lambda n: pl.BlockSpec((tm, n), lambda i: (i, 0))
    return pl.pallas_call(
        _nsa_out_body,
        grid=(rows // tm,),
        in_specs=[row(dq), row(dq), row(dq), row(N_BRANCH * dq), row(d),
                  pl.BlockSpec((dq, d), lambda i: (0, 0)), pl.BlockSpec((1, d), lambda i: (0, 0))],
        out_specs=row(d),
        out_shape=jax.ShapeDtypeStruct((rows, d), F32),
        compiler_params=_cparams("parallel"),
        name="nsa_out",
    )(oc, os_, ow, sz, x, wo_bf, gp.reshape(1, d))


def _sample_attn_body(pt_ref, *refs, n_pages, tq, w_keep):
    del pt_ref
    page_refs = refs[:n_pages]
    (qa_ref, kvc_ref, snew_ref, wst_ref, wnew_ref, m_ref, e_ref, cbs_ref, sbs_ref, wbs_ref, gate_ref,
     oc_ref, os_ref, ow_ref, ksel_ref, kwin_ref) = refs[n_pages:]
    n_past = n_pages * PAGE_SIZE
    n_sk = ksel_ref.shape[0]
    n_wk = kwin_ref.shape[0]
    width = ksel_ref.shape[1]
    kw = width // 2
    for p in range(n_pages):
        ksel_ref[p * PAGE_SIZE:(p + 1) * PAGE_SIZE, :] = page_refs[p][0]
    ksel_ref[n_past:n_past + tq, :] = snew_ref[...]
    ksel_ref[n_past + tq:, :] = jnp.zeros((n_sk - n_past - tq, width), F32)
    kwin_ref[0:w_keep, :] = wst_ref[0]
    kwin_ref[w_keep:w_keep + tq, :] = wnew_ref[...]
    kwin_ref[w_keep + tq:, :] = jnp.zeros((n_wk - w_keep - tq, width), F32)
    pos = n_past + lax.broadcasted_iota(jnp.int32, (tq, 1), 0)
    tb = pos // L_SLC
    gate = gate_ref[...]
    rows = GROUP * tq

    def attend(q, k, v, bias, extra=None):
        s = lax.dot_general(q, k, _NT, preferred_element_type=F32)
        if extra is not None:
            s = s + extra
        s = (s.reshape(GROUP, tq, s.shape[-1]) + bias).reshape(rows, s.shape[-1])
        p, l = _softmax_rows(s)
        inv = 1.0 / jnp.maximum(l, 1e-30)
        return p, inv, jnp.dot(p.astype(BF16), v, preferred_element_type=F32)

    def store(out_ref, o, kvh, branch):
        outs = [o[g * tq:(g + 1) * tq] * _gate_col(gate, branch * N_HEADS + kvh * GROUP + g) for g in range(GROUP)]
        for gp in range(GROUP // 2):
            col = kvh * (GROUP // 2) + gp
            out_ref[:, col * LANES:(col + 1) * LANES] = _pair_columns(outs[2 * gp], outs[2 * gp + 1], kvh % 2)

    def pair_cols(ref, pr):
        kcol = slice(pr * LANES, (pr + 1) * LANES)
        vcol = slice(kw + pr * LANES, kw + (pr + 1) * LANES)
        if len(ref.shape) == 3:
            return ref[0, :, kcol].astype(BF16), ref[0, :, vcol].astype(BF16)
        return ref[:, kcol].astype(BF16), ref[:, vcol].astype(BF16)

    qs, imps = [], []
    for kvh in range(KV_HEADS):
        q = jnp.concatenate([qa_ref[:, h * LANES:(h + 1) * LANES]
                             for h in range(kvh * GROUP, (kvh + 1) * GROUP)], axis=0)
        qs.append((q if kvh % 2 == 0 else _swap_halves(q)).astype(BF16))
    for pr in range(KV_HEADS // 2):
        kc, vc = pair_cols(kvc_ref, pr)
        for kvh in (2 * pr, 2 * pr + 1):
            p, inv, o = attend(qs[kvh], kc, vc, cbs_ref[kvh * GROUP:(kvh + 1) * GROUP])
            store(oc_ref, o * inv, kvh, 0)
            pc_sum = jnp.sum((p * inv).reshape(GROUP, tq, p.shape[-1]), axis=0)
            imps.append(jnp.dot(pc_sum, m_ref[...], preferred_element_type=F32, precision=lax.Precision.HIGHEST))
    selneg = _select_blocks(jnp.concatenate(imps, axis=0), jnp.concatenate([tb] * KV_HEADS, axis=0), N_SEL)
    for pr in range(KV_HEADS // 2):
        ksl, vsl = pair_cols(ksel_ref, pr)
        kwn, vwn = pair_cols(kwin_ref, pr)
        for kvh in (2 * pr, 2 * pr + 1):
            heads = slice(kvh * GROUP, (kvh + 1) * GROUP)
            qm = jnp.concatenate([selneg[kvh * tq:(kvh + 1) * tq]] * GROUP, axis=0).astype(BF16)
            block_mask = jnp.dot(qm, e_ref[...], preferred_element_type=F32)
            _, inv, o = attend(qs[kvh], ksl, vsl, sbs_ref[heads], block_mask)
            store(os_ref, o * inv, kvh, 1)
            _, inv, o = attend(qs[kvh], kwn, vwn, wbs_ref[heads])
            store(ow_ref, o * inv, kvh, 2)


def _sample_attn(qa, kvc, pages, page_table, slc_new, win_state, win_new, m_s, e_s, cbs, sbs, wbs, gate,
                 n_seq, tq, n_pages):
    width = pages.shape[-1]
    n_sk = sbs.shape[-1]
    n_wk = wbs.shape[-1]
    w_keep = win_state.shape[1]
    d_out = N_HEADS * HEAD_DIM
    page_spec = lambda j: pl.BlockSpec((1, PAGE_SIZE, width), lambda i, pt: (pt[i * n_pages + j], 0, 0))
    full = lambda a: pl.BlockSpec(a.shape, lambda i, pt: (0,) * a.ndim)
    row = lambda n: pl.BlockSpec((tq, n), lambda i, pt: (i, 0))
    grid_spec = pltpu.PrefetchScalarGridSpec(
        num_scalar_prefetch=1,
        grid=(n_seq,),
        in_specs=[page_spec(j) for j in range(n_pages)] + [
            row(N_HEADS * LANES),
            pl.BlockSpec((1,) + kvc.shape[1:], lambda i, pt: (i, 0, 0)),
            row(width),
            pl.BlockSpec((1, w_keep, width), lambda i, pt: (i, 0, 0)),
            row(width),
            full(m_s), full(e_s), full(cbs), full(sbs), full(wbs),
            row(LANES)],
        out_specs=[row(d_out)] * 3,
        scratch_shapes=[pltpu.VMEM((n_sk, width), F32), pltpu.VMEM((n_wk, width), F32)],
    )
    return pl.pallas_call(
        functools.partial(_sample_attn_body, n_pages=n_pages, tq=tq, w_keep=w_keep),
        grid_spec=grid_spec,
        out_shape=[jax.ShapeDtypeStruct((n_seq * tq, d_out), F32)] * 3,
        compiler_params=_cparams("parallel"),
        name="sample_attn",
    )(page_table, *([pages] * n_pages), qa, kvc, slc_new, win_state, win_new, m_s, e_s, cbs, sbs, wbs, gate)


def _decode_attn_body(pt_ref, *refs, n_pages, tq, w_keep):
    del pt_ref
    page_refs = refs[:n_pages]
    (qa_ref, kvc_ref, snew_ref, wst_ref, wnew_ref, mt_ref, e_ref, cbs_ref, sbs_ref, wbs_ref, gate_ref,
     oc_ref, os_ref, ow_ref, kt_ref, vt_ref, kwt_ref, vwt_ref) = refs[n_pages:]
    n_past = n_pages * PAGE_SIZE
    kw = KV_HEADS * HEAD_DIM
    rows = N_HEADS * tq

    def new_tile(ref):
        return jnp.concatenate([ref[...], jnp.zeros((LANES - tq, 2 * kw), F32)], axis=0).T

    def fill(k_dst, v_dst, col0, tile):
        k_dst[:, col0:col0 + tile.shape[1]] = tile[:kw].astype(BF16)
        v_dst[:, col0:col0 + tile.shape[1]] = tile[kw:].astype(BF16)

    for p in range(n_pages):
        fill(kt_ref, vt_ref, p * PAGE_SIZE, page_refs[p][0])
    fill(kt_ref, vt_ref, n_past, new_tile(snew_ref))
    fill(kwt_ref, vwt_ref, 0, wst_ref[0])
    fill(kwt_ref, vwt_ref, w_keep, new_tile(wnew_ref))

    zero = jnp.zeros((tq, LANES), F32)
    tiles = [[], []]
    for h in range(N_HEADS):
        kvh = h // GROUP
        piece = qa_ref[:, h * LANES:(h + 1) * LANES]
        piece = piece if kvh % 2 == 0 else _swap_halves(piece)
        for ct in range(2):
            tiles[ct].append(piece if kvh // 2 == ct else zero)
    qbd = jnp.concatenate([jnp.concatenate(t, axis=0) for t in tiles], axis=1).astype(BF16)
    gate = gate_ref[...]

    def branch(k_t, v_t, bias, extra=None):
        s = jnp.dot(qbd, k_t, preferred_element_type=F32) + bias
        if extra is not None:
            s = s + extra
        p, l = _softmax_rows(s)
        inv = 1.0 / jnp.maximum(l, 1e-30)
        o_t = lax.dot_general(v_t, p.astype(BF16), _NT, preferred_element_type=F32)
        return p, inv, o_t.T

    def store(out_ref, o, scale):
        o = o * scale
        for kvh in range(KV_HEADS):
            tile = o[:, (kvh // 2) * LANES:(kvh // 2 + 1) * LANES]
            for gp in range(GROUP // 2):
                h0 = kvh * GROUP + 2 * gp
                col = kvh * (GROUP // 2) + gp
                out_ref[:, col * LANES:(col + 1) * LANES] = _pair_columns(
                    tile[h0 * tq:(h0 + 1) * tq], tile[(h0 + 1) * tq:(h0 + 2) * tq], kvh % 2)

    p, inv, o = branch(kvc_ref[0, :kw, :].astype(BF16), kvc_ref[0, kw:, :].astype(BF16), cbs_ref[...])
    store(oc_ref, o, inv * _gate_col(gate, 0))
    pc = p * inv
    pc_sum = jnp.concatenate(
        [sum(pc[(kvh * GROUP + g) * tq:(kvh * GROUP + g + 1) * tq] for g in range(GROUP)) for kvh in range(KV_HEADS)],
        axis=0)
    imp_t = lax.dot_general(mt_ref[...], pc_sum, _NT, preferred_element_type=F32, precision=lax.Precision.HIGHEST)
    n_col = KV_HEADS * tq
    nb_pad = -(-(-(-(n_past + tq) // L_SLC)) // 8) * 8
    tb = (n_past + lax.broadcasted_iota(jnp.int32, (1, n_col), 1) % tq) // L_SLC
    selneg_t = _select_blocks_t(imp_t[HEAD_DIM:HEAD_DIM + nb_pad], tb, N_SEL)
    sel_t = jnp.concatenate([jnp.zeros((HEAD_DIM, n_col), F32), selneg_t,
                             jnp.zeros((LANES - HEAD_DIM - nb_pad, n_col), F32)], axis=0)
    sel = jnp.concatenate([sel_t, jnp.zeros((LANES, LANES - n_col), F32)], axis=1).T
    sel = jnp.concatenate([sel[kvh * tq:(kvh + 1) * tq] for kvh in range(KV_HEADS) for _ in range(GROUP)], axis=0)
    block_mask = jnp.dot(sel.astype(BF16), e_ref[...], preferred_element_type=F32)
    _, inv, o = branch(kt_ref[...], vt_ref[...], sbs_ref[...], block_mask)
    store(os_ref, o, inv * _gate_col(gate, 1))
    _, inv, o = branch(kwt_ref[...], vwt_ref[...], wbs_ref[...])
    store(ow_ref, o, inv * _gate_col(gate, 2))


def _decode_attn(qa, kvc_t, pages_t, page_table, slc_new, win_state_t, win_new, m_t, e_s, cbs, sbs, wbs, gate_r,
                 n_seq, tq, n_pages):
    width = pages_t.shape[1]
    n_sk = sbs.shape[-1]
    n_wk = wbs.shape[-1]
    w_keep = win_state_t.shape[2]
    d_out = N_HEADS * HEAD_DIM
    rows = N_HEADS * tq
    page_spec = lambda j: pl.BlockSpec((1, width, PAGE_SIZE), lambda i, pt: (pt[i * n_pages + j], 0, 0))
    full = lambda a: pl.BlockSpec(a.shape, lambda i, pt: (0,) * a.ndim)
    row = lambda n: pl.BlockSpec((tq, n), lambda i, pt: (i, 0))
    grid_spec = pltpu.PrefetchScalarGridSpec(
        num_scalar_prefetch=1,
        grid=(n_seq,),
        in_specs=[page_spec(j) for j in range(n_pages)] + [
            row(N_HEADS * LANES),
            pl.BlockSpec((1,) + kvc_t.shape[1:], lambda i, pt: (i, 0, 0)),
            row(width),
            pl.BlockSpec((1, width, w_keep), lambda i, pt: (i, 0, 0)),
            row(width),
            full(m_t), full(e_s), full(cbs), full(sbs), full(wbs),
            pl.BlockSpec((rows, LANES), lambda i, pt: (i, 0))],
        out_specs=[row(d_out)] * 3,
        scratch_shapes=[pltpu.VMEM((width // 2, n_sk), BF16), pltpu.VMEM((width // 2, n_sk), BF16),
                        pltpu.VMEM((width // 2, n_wk), BF16), pltpu.VMEM((width // 2, n_wk), BF16)],
    )
    return pl.pallas_call(
        functools.partial(_decode_attn_body, n_pages=n_pages, tq=tq, w_keep=w_keep),
        grid_spec=grid_spec,
        out_shape=[jax.ShapeDtypeStruct((n_seq * tq, d_out), F32)] * 3,
        compiler_params=_cparams("parallel"),
        name="decode_attn",
    )(page_table, *([pages_t] * n_pages), qa, kvc_t, slc_new, win_state_t, win_new, m_t, e_s, cbs, sbs, wbs, gate_r)


def _overlap_matrix(n_rows, row0, n_cmp, n_blk):
    import numpy as np
    m = np.zeros((n_rows, LANES), np.float32)
    cs = np.arange(n_cmp)[:, None] * CMP_STRIDE
    js = np.arange(n_blk)[None, :] * L_SLC
    m[row0:row0 + n_cmp, HEAD_DIM:HEAD_DIM + n_blk] = (cs <= js + L_SLC - 1) & (cs + L_CMP - 1 >= js)
    return jnp.asarray(m)


def kernel(x_prompt, x_sample, state_conv, cache_cmp, cache_slc, state_win, page_table, rel_bias, a_norm_pre, a_w_in, a_conv_w, a_conv_b, a_ln_g, a_ln_b, a_w_out, a_norm_post, kv_norm, w_kv, cmp_pe, cmp_w1, cmp_w2, b_norm_pre, b_w_in, b_w_out, b_norm_post):
    import numpy as np
    bp, tp, d = x_prompt.shape
    bd, tq, _ = x_sample.shape
    n_pages = page_table.shape[1]
    past = n_pages * PAGE_SIZE
    w_keep = state_win.shape[1]
    width = 2 * KV_HEADS * HEAD_DIM
    kw = KV_HEADS * HEAD_DIM
    dq = N_HEADS * HEAD_DIM
    assert b_w_in.shape[0] == 1 and tp % (2 * KEY_TILE) == 0 and tp // L_SLC <= HEAD_DIM
    tm = 256
    tm_s = min(tm, bd * tq)

    xp = x_prompt
    xs = x_sample.reshape(bd * tq, d)
    conv_p, conv_s = [], []
    for l in range(a_w_in.shape[0]):
        w_in = a_w_in[l].astype(BF16)
        w_out = a_w_out[l].astype(BF16)
        di = a_w_out.shape[1]
        tail = (a_conv_w[l], a_conv_b[l], a_ln_g[l], a_ln_b[l], w_out, a_norm_post[l])
        glu, sz = _glu_proj(xp.reshape(bp * tp, d), a_norm_pre[l], w_in, tm)
        glu = glu.reshape(bp, tp, di)
        xp = _conv_prompt(glu, sz.reshape(bp, tp, di), xp, *tail, tm)
        conv_p.append(glu[:, -(CONV_W - 1):])
        glu, sz = _glu_proj(xs, a_norm_pre[l], w_in, tm_s)
        xs = _conv_sample(glu, state_conv[l], sz, xs, *tail, 16)
        conv_s.append(jnp.concatenate([state_conv[l], glu.reshape(bd, tq, di)], axis=1)[:, -(CONV_W - 1):])

    bw = b_w_in[0]
    n_gate = N_BRANCH * N_HEADS
    wq = bw[:, :dq].astype(BF16)
    wz = bw[:, dq:dq * (1 + N_BRANCH)].astype(BF16)
    wg = jnp.pad(bw[:, dq * (1 + N_BRANCH):], ((0, 0), (0, LANES - n_gate))).astype(BF16)
    wkv = w_kv.astype(BF16)
    xp2 = xp.reshape(bp * tp, d)
    (cmp_p, slc_p, win_p, qa_p, sz_p, gate_p, ska, sv, wka, wv) = _nsa_proj(
        xp2, kv_norm, b_norm_pre[0], wkv, wq, wz, wg, tm, seq_len=tp)
    cmp_s, slc_s, win_s, qa_s, sz_s, gate_s = _nsa_proj(xs, kv_norm, b_norm_pre[0], wkv, wq, wz, wg, tm_s)
    qa_s = qa_s.astype(F32)

    w1cat = jnp.concatenate([cmp_w1[:, :CMP_STRIDE * HEAD_DIM], cmp_w1[:, CMP_STRIDE * HEAD_DIM:]], axis=2).astype(BF16)
    pe8 = jnp.pad(cmp_pe.reshape(2, L_CMP // CMP_STRIDE, CMP_STRIDE * HEAD_DIM), ((0, 0), (0, 6), (0, 0)))
    zero = jnp.zeros_like(cmp_w2)
    w2h = jnp.stack([jnp.concatenate([cmp_w2, zero], axis=2), jnp.concatenate([zero, cmp_w2], axis=2)], axis=1).astype(BF16)
    prompt_page = min(1024, tp)
    pp = tp // prompt_page
    kvc_p = _compress(cmp_p, None, w1cat, pe8, w2h, pp, prompt_page).transpose(0, 2, 1)
    pt_flat = page_table.reshape(-1).astype(jnp.int32)
    cmp_pages_t = cache_cmp.transpose(0, 2, 3, 4, 1).reshape(-1, width, PAGE_SIZE)
    kvc_s = _compress(cmp_pages_t, pt_flat, w1cat, pe8, w2h, n_pages, PAGE_SIZE)

    n_chunk_p = kvc_p.shape[1]
    nc_p = n_chunk_p - L_CMP // CMP_STRIDE + 1
    real = (jnp.arange(n_chunk_p) < nc_p)[None, :, None]
    flag = jnp.zeros((LANES - HEAD_DIM,), F32).at[0].set(1.0)
    kc4 = jnp.where(real, kvc_p[:, :, :kw], 0.0).reshape(bp, n_chunk_p, KV_HEADS, HEAD_DIM)
    aug = jnp.where(real[..., None], 0.0, flag) * jnp.ones((bp, n_chunk_p, KV_HEADS, 1), F32)
    kc_real = jnp.concatenate([kc4, aug], axis=-1).reshape(bp, n_chunk_p, KV_HEADS * LANES)
    pad_row = jnp.concatenate([jnp.zeros((HEAD_DIM,), F32), flag])
    kc_front = jnp.broadcast_to(jnp.tile(pad_row, KV_HEADS), (bp, n_chunk_p, KV_HEADS * LANES))
    kc_pad = jnp.concatenate([kc_front, kc_real], axis=1)
    vc_pad = jnp.concatenate([jnp.zeros((bp, n_chunk_p, kw), F32), jnp.where(real, kvc_p[:, :, kw:], 0.0)], axis=1)
    m_pad = _overlap_matrix(2 * n_chunk_p, n_chunk_p, nc_p, tp // L_SLC)
    cb = _bias_tile(rel_bias, n_chunk_p, Q_BLK, 1, CMP_STRIDE * (n_chunk_p - Q_BLK // CMP_STRIDE) - (L_CMP - 1),
                    row_step=-CMP_STRIDE)
    cb = cb.reshape(KV_HEADS, GROUP, n_chunk_p, Q_BLK).transpose(0, 2, 1, 3).reshape(KV_HEADS, n_chunk_p, GROUP * Q_BLK)
    oc_p, qsel, qflag = _cmp_prompt(qa_p, kc_pad, vc_pad, m_pad, cb, gate_p, bp, tp)
    sb = jnp.stack([_bias_tile(rel_bias, KEY_TILE, Q_BLK, 1, off, row_step=-1)
                    for off in (0, Q_BLK, 2 * Q_BLK, 3 * Q_BLK)])
    sb = sb.reshape(4, KV_HEADS, GROUP, KEY_TILE, Q_BLK).transpose(0, 1, 3, 2, 4).reshape(4, KV_HEADS, KEY_TILE, GROUP * Q_BLK)
    os_p = _slc_prompt(qsel, ska, _with_ones_rows(sv), sb, gate_p, bp, tp)
    n_wkey = WINDOW + Q_BLK
    wb = _bias_tile(rel_bias, n_wkey, Q_BLK, 1, WINDOW, WINDOW, row_step=-1)
    wb = wb.reshape(KV_HEADS, GROUP, n_wkey, Q_BLK).transpose(0, 2, 1, 3).reshape(KV_HEADS, n_wkey, GROUP * Q_BLK)
    wka_front = jnp.broadcast_to(jnp.tile(pad_row, KV_HEADS).astype(BF16), (bp, WINDOW, KV_HEADS * LANES))
    wka_pad = jnp.concatenate([wka_front, wka.reshape(bp, tp, -1)], axis=1)
    wv_pad = jnp.concatenate([jnp.zeros((kw, bp, WINDOW), BF16), wv.reshape(kw, bp, tp)], axis=2)
    ow_p = _win_prompt(qflag, wka_pad, _with_ones_rows(wv_pad.reshape(kw, bp * (WINDOW + tp))), wb, gate_p, bp, tp)
    wo = b_w_out[0].astype(BF16)
    y_p = _nsa_out(oc_p.reshape(bp * tp, dq), os_p.reshape(bp * tp, dq), ow_p.reshape(bp * tp, dq), sz_p, xp2,
                   wo, b_norm_post[0], tm)

    n_chunk_s = kvc_s.shape[2]
    nc_s = n_chunk_s - L_CMP // CMP_STRIDE + 1
    nb_s = -(-(past + tq) // L_SLC)
    n_sk = past + LANES
    n_wk = w_keep + LANES
    m_t = _overlap_matrix(n_chunk_s, 0, nc_s, nb_s).T
    e_np = np.zeros((LANES, n_sk), np.float32)
    e_np[HEAD_DIM + np.arange(n_sk) // L_SLC, np.arange(n_sk)] = 1.0
    e_s = jnp.asarray(e_np, BF16)
    rows_s = N_HEADS * tq
    cbs = _bias_tile(rel_bias, tq, n_chunk_s, -CMP_STRIDE, past - (L_CMP - 1)).reshape(rows_s, n_chunk_s)
    sbs = _bias_tile(rel_bias, tq, n_sk, -1, past).reshape(rows_s, n_sk)
    wbs = _bias_tile(rel_bias, tq, n_wk, -1, w_keep, WINDOW).reshape(rows_s, n_wk)
    gate_r = gate_s[:, :n_gate].reshape(bd, tq, N_BRANCH, N_HEADS).transpose(0, 3, 1, 2).reshape(bd * rows_s, N_BRANCH)
    gate_r = jnp.pad(gate_r, ((0, 0), (0, LANES - N_BRANCH)))
    to_t = lambda a: a.transpose(0, 2, 3, 4, 1).reshape(a.shape[0], width, a.shape[1])
    oc_s, os_s, ow_s = _decode_attn(qa_s, kvc_s, to_t(cache_slc), pt_flat, slc_s, to_t(state_win), win_s,
                                    m_t, e_s, cbs, sbs, wbs, gate_r, bd, tq, n_pages)
    y_s = _nsa_out(oc_s, os_s, ow_s, sz_s, xs, wo, b_norm_post[0], tm_s)

    kv5 = lambda a, b, t: a.reshape(b, t, 2, KV_HEADS, HEAD_DIM)
    kv5_t = lambda a: a.reshape(bp, 2, KV_HEADS, HEAD_DIM, a.shape[2]).transpose(0, 4, 1, 2, 3)
    win_all = jnp.concatenate([state_win, kv5(win_s, bd, tq)], axis=1)
    return (y_p.reshape(bp, tp, d), y_s.reshape(bd, tq, d), jnp.stack(conv_p), jnp.stack(conv_s),
            kv5_t(cmp_p), kv5(cmp_s, bd, tq), kv5_t(slc_p), kv5(slc_s, bd, tq),
            kv5_t(win_p[:, :, -min(WINDOW, tp):]), win_all[:, -min(WINDOW, win_all.shape[1]):])
```

```python
import functools
import math

import jax
import jax.numpy as jnp
from jax import lax
from jax.experimental import pallas as pl
from jax.experimental.pallas import tpu as pltpu

F32 = jnp.float32
BF16 = jnp.bfloat16

EPS = 1e-6
NEG = -1e30
BIG = 1e9
M_FLOOR = -1e20

HEAD_DIM = 64
KV_HEADS = 4
N_HEADS = 16
GROUP = N_HEADS // KV_HEADS
N_BRANCH = 3
CONV_W = 31
L_CMP = 32
CMP_STRIDE = 16
L_SLC = 64
N_SEL = 16
WINDOW = 512
Q_BLK = 128
NUM_BUCKETS = 32
MAX_DISTANCE = 128
MAX_EXACT = NUM_BUCKETS // 2
PAGE_SIZE = 128
LANES = 128
HALO = 32
CONV_TAIL = 16

VMEM_LIMIT = 56 * 1024 * 1024


def _cparams(*sem):
    return pltpu.CompilerParams(dimension_semantics=sem, vmem_limit_bytes=VMEM_LIMIT)


def _sigmoid(x):
    return 1.0 / (1.0 + jnp.exp(-x))


def _rms(x, g):
    return x * lax.rsqrt(jnp.mean(x * x, axis=-1, keepdims=True) + EPS) * g


def _lane(shape):
    return lax.broadcasted_iota(jnp.int32, shape, len(shape) - 1)


def _swap_halves(x):
    return pltpu.roll(x, HEAD_DIM, axis=x.ndim - 1)


def _glu_proj_body(x_ref, g_ref, w_ref, glu_ref, sz_ref):
    di = glu_ref.shape[-1]
    h = _rms(x_ref[...], g_ref[...])
    u = jnp.dot(h.astype(BF16), w_ref[...], preferred_element_type=F32)
    z = u[:, 2 * di:]
    glu_ref[...] = u[:, :di] * _sigmoid(u[:, di:2 * di])
    sz_ref[...] = z * _sigmoid(z)


def _glu_proj(x, g, w_bf, tm):
    rows, d = x.shape
    di = w_bf.shape[1] // 3
    return pl.pallas_call(
        _glu_proj_body,
        grid=(rows // tm,),
        in_specs=[pl.BlockSpec((tm, d), lambda i: (i, 0)),
                  pl.BlockSpec((1, d), lambda i: (0, 0)),
                  pl.BlockSpec((d, 3 * di), lambda i: (0, 0))],
        out_specs=[pl.BlockSpec((tm, di), lambda i: (i, 0)),
                   pl.BlockSpec((tm, di), lambda i: (i, 0))],
        out_shape=[jax.ShapeDtypeStruct((rows, di), F32)] * 2,
        compiler_params=_cparams("parallel"),
        name="glu_proj",
    )(x, g.reshape(1, d), w_bf)


def _conv_tail(c, sz, x, lg_ref, lb_ref, wo_ref, gp_ref):
    mu = jnp.mean(c, axis=-1, keepdims=True)
    cc = c - mu
    var = jnp.mean(cc * cc, axis=-1, keepdims=True)
    y = cc * lax.rsqrt(var + EPS) * lg_ref[...] + lb_ref[...]
    y = y * _sigmoid(y) * sz
    o = jnp.dot(y.astype(BF16), wo_ref[...], preferred_element_type=F32)
    return x + _rms(o, gp_ref[...])


def _conv_prompt_body(glu_ref, prev_ref, sz_ref, x_ref, cw_ref, cb_ref, lg_ref, lb_ref, wo_ref, gp_ref,
                      out_ref, full_ref, c_ref, *, tm):
    t = pl.program_id(1)
    full_ref[0:HALO, :] = jnp.where(t > 0, prev_ref[0], 0.0)
    full_ref[HALO:HALO + tm, :] = glu_ref[0]
    full_ref[HALO + tm:, :] = jnp.zeros((CONV_TAIL, full_ref.shape[1]), F32)
    d = c_ref.shape[-1]
    first = HALO - (CONV_W - 1)
    sub = 8
    half = tm // 2
    for lc in range(d // LANES):
        ln = slice(lc * LANES, (lc + 1) * LANES)
        for r0 in range(0, tm, half):
            acc = jnp.zeros((half, LANES), F32)
            for b in range(sub):
                z = None
                for a in range(-(-(CONV_W - b) // sub)):
                    x = full_ref[r0 + sub * a:r0 + sub * a + half + 2 * sub, ln]
                    term = cw_ref[sub * a + b:sub * a + b + 1, ln] * x
                    z = term if z is None else z + term
                acc = acc + z[first + b:first + b + half]
            c_ref[r0:r0 + half, ln] = acc + cb_ref[:, ln]
    out_ref[0] = _conv_tail(c_ref[...], sz_ref[0], x_ref[0], lg_ref, lb_ref, wo_ref, gp_ref)


def _conv_prompt(glu, sz, x, cw, cb, lg, lb, wo_bf, gp, tm):
    b, t, d = x.shape
    di = glu.shape[-1]
    per = tm // HALO
    vec = lambda n: pl.BlockSpec((1, n), lambda i, j: (0, 0))
    return pl.pallas_call(
        functools.partial(_conv_prompt_body, tm=tm),
        grid=(b, t // tm),
        in_specs=[pl.BlockSpec((1, tm, di), lambda i, j: (i, j, 0)),
                  pl.BlockSpec((1, HALO, di), lambda i, j: (i, jnp.maximum(j * per - 1, 0), 0)),
                  pl.BlockSpec((1, tm, di), lambda i, j: (i, j, 0)),
                  pl.BlockSpec((1, tm, d), lambda i, j: (i, j, 0)),
                  pl.BlockSpec((CONV_W, di), lambda i, j: (0, 0)),
                  vec(di), vec(di), vec(di),
                  pl.BlockSpec((di, d), lambda i, j: (0, 0)),
                  vec(d)],
        out_specs=pl.BlockSpec((1, tm, d), lambda i, j: (i, j, 0)),
        out_shape=jax.ShapeDtypeStruct((b, t, d), F32),
        scratch_shapes=[pltpu.VMEM((HALO + tm + CONV_TAIL, di), F32), pltpu.VMEM((tm, di), F32)],
        compiler_params=_cparams("parallel", "arbitrary"),
        name="conv_prompt",
    )(glu, glu, sz, x, cw, cb.reshape(1, di), lg.reshape(1, di), lb.reshape(1, di), wo_bf, gp.reshape(1, d))


def _conv_sample_body(glu_ref, st_ref, sz_ref, x_ref, cw_ref, cb_ref, lg_ref, lb_ref, wo_ref, gp_ref,
                      out_ref, full_ref, c_ref, *, nb, tq):
    d = c_ref.shape[-1]
    first = HALO - (CONV_W - 1)
    full_ref[:, first:HALO, :] = st_ref[...]
    full_ref[:, HALO:HALO + tq, :] = glu_ref[...].reshape(nb, tq, d)
    for lc in range(d // LANES):
        ln = slice(lc * LANES, (lc + 1) * LANES)
        acc = jnp.zeros((nb, tq, LANES), F32)
        for j in range(CONV_W):
            acc = acc + cw_ref[j:j + 1, ln] * full_ref[:, first + j:first + j + tq, ln]
        c_ref[:, ln] = (acc + cb_ref[:, ln]).reshape(nb * tq, LANES)
    out_ref[...] = _conv_tail(c_ref[...], sz_ref[...], x_ref[...], lg_ref, lb_ref, wo_ref, gp_ref)


def _conv_sample(glu, state, sz, x, cw, cb, lg, lb, wo_bf, gp, nb):
    n_seq = state.shape[0]
    rows, d = x.shape
    di = glu.shape[-1]
    tq = rows // n_seq
    vec = lambda n: pl.BlockSpec((1, n), lambda i: (0, 0))
    return pl.pallas_call(
        functools.partial(_conv_sample_body, nb=nb, tq=tq),
        grid=(n_seq // nb,),
        in_specs=[pl.BlockSpec((nb * tq, di), lambda i: (i, 0)),
                  pl.BlockSpec((nb, CONV_W - 1, di), lambda i: (i, 0, 0)),
                  pl.BlockSpec((nb * tq, di), lambda i: (i, 0)),
                  pl.BlockSpec((nb * tq, d), lambda i: (i, 0)),
                  pl.BlockSpec((CONV_W, di), lambda i: (0, 0)),
                  vec(di), vec(di), vec(di),
                  pl.BlockSpec((di, d), lambda i: (0, 0)),
                  vec(d)],
        out_specs=pl.BlockSpec((nb * tq, d), lambda i: (i, 0)),
        out_shape=jax.ShapeDtypeStruct((rows, d), F32),
        scratch_shapes=[pltpu.VMEM((nb, HALO + tq, di), F32), pltpu.VMEM((nb * tq, di), F32)],
        compiler_params=_cparams("parallel"),
        name="conv_sample",
    )(glu, state, sz, x, cw, cb.reshape(1, di), lg.reshape(1, di), lb.reshape(1, di), wo_bf, gp.reshape(1, d))


def _head_major(col_pair, head):
    return col_pair if head % 2 == 0 else _swap_halves(col_pair)


def _nsa_proj_body(x_ref, gkv_ref, gq_ref, wkv_ref, wq_ref, wz_ref, wg_ref,
                   cmp_ref, slc_ref, win_ref, qa_ref, sz_ref, gate_ref, *aug_refs, seq_len, tm):
    x = x_ref[...]
    xn = x * lax.rsqrt(jnp.mean(x * x, axis=-1, keepdims=True) + EPS)
    hkv = (xn * gkv_ref[...]).astype(BF16)
    hq = (xn * gq_ref[...]).astype(BF16)
    kv = jnp.dot(hkv, wkv_ref[...], preferred_element_type=F32)
    width = 2 * KV_HEADS * HEAD_DIM
    if aug_refs:
        kv_t = [kv[:, b * width:(b + 1) * width].T for b in range(N_BRANCH)]
        cmp_ref[0] = kv_t[0]
        slc_ref[0] = kv_t[1]
        win_ref[0] = kv_t[2]
    else:
        cmp_ref[...] = kv[:, :width]
        slc_ref[...] = kv[:, width:2 * width]
        win_ref[...] = kv[:, 2 * width:]
    lane = _lane((tm, LANES))
    low = lane < HEAD_DIM
    uq = jnp.dot(hq, wq_ref[...], preferred_element_type=F32)
    scale = HEAD_DIM ** -0.5
    for h in range(N_HEADS):
        qh = _head_major(uq[:, (h // 2) * LANES:(h // 2 + 1) * LANES], h)
        qa_ref[:, h * LANES:(h + 1) * LANES] = jnp.where(low, qh * scale, 0.0).astype(BF16)
    z = jnp.dot(hq, wz_ref[...], preferred_element_type=F32)
    sz_ref[...] = z * _sigmoid(z)
    gate_ref[...] = _sigmoid(jnp.dot(hq, wg_ref[...], preferred_element_type=F32))
    if aug_refs:
        ska_ref, sv_ref, wka_ref, wv_ref = aug_refs
        row = pl.program_id(0) * tm + lax.broadcasted_iota(jnp.int32, (tm, LANES), 0)
        blk = (row % seq_len) // L_SLC
        onehot = (lane - HEAD_DIM == blk).astype(F32)
        kw = KV_HEADS * HEAD_DIM
        for h in range(KV_HEADS):
            ks = _head_major(kv[:, width + (h // 2) * LANES:width + (h // 2 + 1) * LANES], h)
            ska_ref[:, h * LANES:(h + 1) * LANES] = jnp.where(low, ks, onehot).astype(BF16)
            kwn = _head_major(kv[:, 2 * width + (h // 2) * LANES:2 * width + (h // 2 + 1) * LANES], h)
            wka_ref[:, h * LANES:(h + 1) * LANES] = jnp.where(low, kwn, 0.0).astype(BF16)
        sv_ref[...] = kv_t[1][kw:].astype(BF16)
        wv_ref[...] = kv_t[2][kw:].astype(BF16)


def _nsa_proj(x, g_kv, g_q, wkv_bf, wq_bf, wz_bf, wg_bf, tm, seq_len=None):
    rows, d = x.shape
    width = 2 * KV_HEADS * HEAD_DIM
    row = lambda n: pl.BlockSpec((tm, n), lambda i: (i, 0))
    full = lambda a: pl.BlockSpec(a.shape, lambda i: (0, 0))
    out_specs = [row(width), row(width), row(width), row(N_HEADS * LANES), row(wz_bf.shape[1]), row(LANES)]
    out_shape = [jax.ShapeDtypeStruct((rows, width), F32)] * 3 + [
        jax.ShapeDtypeStruct((rows, N_HEADS * LANES), BF16),
        jax.ShapeDtypeStruct((rows, wz_bf.shape[1]), F32),
        jax.ShapeDtypeStruct((rows, LANES), F32)]
    if seq_len is not None:
        per_seq = seq_len // tm
        kv_t_spec = pl.BlockSpec((1, width, tm), lambda i: (i // per_seq, 0, i % per_seq))
        out_specs[:N_BRANCH] = [kv_t_spec] * N_BRANCH
        out_shape[:N_BRANCH] = [jax.ShapeDtypeStruct((rows // seq_len, width, seq_len), F32)] * N_BRANCH
        out_specs += [row(KV_HEADS * LANES), pl.BlockSpec((width // 2, tm), lambda i: (0, i))] * 2
        out_shape += [jax.ShapeDtypeStruct((rows, KV_HEADS * LANES), BF16),
                      jax.ShapeDtypeStruct((width // 2, rows), BF16)] * 2
    gkv = g_kv.reshape(1, d)
    gq = g_q.reshape(1, d)
    return pl.pallas_call(
        functools.partial(_nsa_proj_body, seq_len=seq_len, tm=tm),
        grid=(rows // tm,),
        in_specs=[row(d), full(gkv), full(gq), full(wkv_bf), full(wq_bf), full(wz_bf), full(wg_bf)],
        out_specs=out_specs,
        out_shape=out_shape,
        compiler_params=_cparams("parallel"),
        name="nsa_proj",
    )(x, gkv, gq, wkv_bf, wq_bf, wz_bf, wg_bf)


def _compress_body(pt_ref, *refs, n_pages):
    del pt_ref
    page_refs = refs[:n_pages]
    w1_ref, pe_ref, w2_ref, out_ref, a_ref = refs[n_pages:]
    width, page_rows = page_refs[0].shape[1:]
    n_col = width // LANES
    cpp = page_rows // CMP_STRIDE
    n_chunk = n_pages * cpp
    hid = w2_ref.shape[2]
    low = _lane((cpp, LANES)) < HEAD_DIM
    def assemble(c):
        for p in range(n_pages):
            col_t = page_refs[p][0, c * LANES:(c + 1) * LANES, :]
            by_s = pltpu.einshape("(ns)f->(sn)f", col_t.T, n=cpp)
            for s in range(0, CMP_STRIDE, 2):
                b0 = by_s[s * cpp:(s + 1) * cpp]
                b1 = by_s[(s + 1) * cpp:(s + 2) * cpp]
                dst = (slice(p * cpp, (p + 1) * cpp), slice((s // 2) * LANES, (s // 2 + 1) * LANES))
                a_ref[(2 * c,) + dst] = jnp.where(low, b0, _swap_halves(b1))
                a_ref[(2 * c + 1,) + dst] = jnp.where(low, _swap_halves(b0), b1)

    for kv in range(2):
        for c in range(kv * n_col // 2, (kv + 1) * n_col // 2):
            assemble(c)
        w1 = w1_ref[kv]
        pe = jnp.dot(pe_ref[kv].astype(BF16), w1, preferred_element_type=F32)
        pe_term = pe[0:1, :hid] + pe[1:2, hid:]
        a_kv = a_ref[kv * KV_HEADS:(kv + 1) * KV_HEADS].reshape(KV_HEADS * n_chunk, CMP_STRIDE * HEAD_DIM)
        parts = jnp.dot(a_kv.astype(BF16), w1, preferred_element_type=F32)
        for hp in range(KV_HEADS // 2):
            pair = jnp.zeros((n_chunk, LANES), F32)
            for par in range(2):
                part = parts[(2 * hp + par) * n_chunk:(2 * hp + par + 1) * n_chunk]
                pre = part[:, :hid] + pltpu.roll(part[:, hid:], n_chunk - 1, axis=0) + pe_term
                mid = pre * _sigmoid(pre)
                pair = pair + jnp.dot(mid.astype(BF16), w2_ref[kv, par], preferred_element_type=F32)
            col = kv * (KV_HEADS // 2) + hp
            out_ref[0, col * LANES:(col + 1) * LANES, :] = pair.T


def _compress(rows_t, page_table, w1cat_bf, pe8, w2_bf, n_pages, page_rows):
    width = rows_t.shape[1]
    n_chunk = n_pages * page_rows // CMP_STRIDE
    if page_table is None:
        n_seq = rows_t.shape[0]
        page_table = jnp.zeros((1,), jnp.int32)
        page_spec = lambda j: pl.BlockSpec((1, width, page_rows), lambda i, pt: (i, 0, j))
    else:
        n_seq = page_table.shape[0] // n_pages
        page_spec = lambda j: pl.BlockSpec((1, width, page_rows), lambda i, pt: (pt[i * n_pages + j], 0, 0))
    full = lambda a: pl.BlockSpec(a.shape, lambda i, pt: (0,) * a.ndim)
    grid_spec = pltpu.PrefetchScalarGridSpec(
        num_scalar_prefetch=1,
        grid=(n_seq,),
        in_specs=[page_spec(j) for j in range(n_pages)] + [full(w1cat_bf), full(pe8), full(w2_bf)],
        out_specs=pl.BlockSpec((1, width, n_chunk), lambda i, pt: (i, 0, 0)),
        scratch_shapes=[pltpu.VMEM((2 * KV_HEADS, n_chunk, CMP_STRIDE * HEAD_DIM), F32)],
    )
    return pl.pallas_call(
        functools.partial(_compress_body, n_pages=n_pages),
        grid_spec=grid_spec,
        out_shape=jax.ShapeDtypeStruct((n_seq, width, n_chunk), F32),
        compiler_params=_cparams("parallel"),
        name="compress",
    )(page_table, *([rows_t] * n_pages), w1cat_bf, pe8, w2_bf)


def _bias_tile_body(rb_ref, out_ref, *, row_step, lane_step, offset, hi):
    h = pl.program_id(0)
    shape = out_ref.shape[1:]
    d = row_step * lax.broadcasted_iota(jnp.int32, shape, 0) + lane_step * _lane(shape) + offset
    n = jnp.maximum(d, 0)
    nf = jnp.maximum(n, 1).astype(F32)
    large = MAX_EXACT + (jnp.log(nf / MAX_EXACT) / math.log(MAX_DISTANCE / MAX_EXACT)
                         * (NUM_BUCKETS - MAX_EXACT)).astype(jnp.int32)
    large = jnp.minimum(large, NUM_BUCKETS - 1)
    bucket = jnp.where(n < MAX_EXACT, n, large)
    far = rb_ref[NUM_BUCKETS - 1, h]
    val = jnp.zeros(shape, F32)
    for k in range(NUM_BUCKETS - 1):
        val = jnp.where(bucket == k, rb_ref[k, h] - far, val)
    out_ref[0] = jnp.where((d >= 0) & (d <= hi), val, NEG)


def _bias_tile(rel_bias, rows, width, lane_step, offset, hi=1 << 30, row_step=1):
    return pl.pallas_call(
        functools.partial(_bias_tile_body, row_step=row_step, lane_step=lane_step, offset=offset, hi=hi),
        grid=(N_HEADS,),
        in_specs=[pl.BlockSpec(memory_space=pltpu.SMEM)],
        out_specs=pl.BlockSpec((1, rows, width), lambda h: (h, 0, 0)),
        out_shape=jax.ShapeDtypeStruct((N_HEADS, rows, width), F32),
        compiler_params=_cparams("parallel"),
        name="bias_tile",
    )(rel_bias)


_NT = (((1,), (1,)), ((), ()))


def _softmax_rows(s):
    m = jnp.maximum(jnp.max(s, axis=-1, keepdims=True), M_FLOOR)
    p = jnp.exp(s - m)
    return p, jnp.sum(p, axis=-1, keepdims=True)


def _gate_col(gate, col):
    return jnp.sum(jnp.where(_lane(gate.shape) == col, gate, 0.0), axis=-1, keepdims=True)


def _pair_columns(o_even, o_odd, valid_half):
    lo = o_even if valid_half == 0 else _swap_halves(o_even)
    hi = o_odd if valid_half == 1 else _swap_halves(o_odd)
    return jnp.where(_lane(lo.shape) < HEAD_DIM, lo, hi)


def _select_blocks_t(imp_t, tb, n_sel):
    nb, nt = imp_t.shape
    sub = 8
    j = lax.broadcasted_iota(jnp.int32, (nb, nt), 0)
    valid = j <= tb
    forced = (j == 0) | (j == tb) | (j == tb - 1)
    score = jnp.where(valid, jnp.where(forced, BIG, imp_t), -BIG)
    groups = [score[lo:lo + sub] for lo in range(0, nb, sub)]
    jr = lax.broadcasted_iota(jnp.int32, (sub, nt), 0)
    counts = [jnp.zeros((sub, nt), F32) for _ in groups]
    for i in range(nb):
        row = score[i:i + 1]
        for r, grp in enumerate(groups):
            lo = r * sub
            if lo > i:
                one = jnp.where(row >= grp, 1.0, 0.0)
            elif lo + sub - 1 <= i:
                one = jnp.where(row > grp, 1.0, 0.0)
            else:
                one = jnp.where(jr + lo > i, jnp.where(row >= grp, 1.0, 0.0), jnp.where(row > grp, 1.0, 0.0))
            counts[r] = counts[r] + one
    rank = jnp.concatenate(counts, axis=0)
    return jnp.where((rank < n_sel) & valid, 0.0, NEG)


def _select_blocks(imp, tb, n_sel):
    lane = _lane(imp.shape)
    lane_f = lane.astype(F32)
    j = lane - HEAD_DIM
    valid = (j >= 0) & (j <= tb)
    forced = (j == 0) | (j == tb) | (j == tb - 1)
    score = jnp.where(valid, jnp.where(forced, BIG, imp), -BIG)
    score = jnp.where(j >= 0, score, -jnp.inf)
    sel = jnp.zeros(imp.shape, jnp.bool_)
    for _ in range(n_sel):
        best = jnp.max(score, axis=-1, keepdims=True)
        first = jnp.min(jnp.where(score == best, lane_f, 4.0 * LANES), axis=-1, keepdims=True)
        pick = lane_f == first
        sel = sel | pick
        score = jnp.where(pick, -jnp.inf, score)
    return jnp.where(sel & valid, 0.0, jnp.where(j >= 0, NEG, 0.0))


def _cmp_prompt_body(qa_ref, kc_ref, vc_ref, m_ref, cb_ref, gate_ref, oc_ref, qsel_ref, qflag_ref, *, n_key):
    qt = pl.program_id(1)
    start = pl.multiple_of(8 * qt + 8, 8)
    kwin = kc_ref[0, pl.ds(start, n_key), :]
    vwin = vc_ref[0, pl.ds(start, n_key), :]
    mwin_t = m_ref[pl.ds(start, n_key), :].T
    feat = lax.broadcasted_iota(jnp.int32, (LANES, Q_BLK), 0)
    pos = qt * Q_BLK + lax.broadcasted_iota(jnp.int32, (1, Q_BLK), 1)
    tb = pos // L_SLC
    gate = gate_ref[...]
    n_pair = GROUP // 2
    for kvh in range(KV_HEADS):
        ka = kwin[:, kvh * LANES:(kvh + 1) * LANES].astype(BF16)
        vpt = vwin[:, (kvh // 2) * LANES:(kvh // 2 + 1) * LANES].T
        vpt = vpt[(kvh % 2) * HEAD_DIM:(kvh % 2 + 1) * HEAD_DIM].astype(BF16)
        qts = [qa_ref[0, :, h * LANES:(h + 1) * LANES].astype(F32).T for h in range(kvh * GROUP, (kvh + 1) * GROUP)]
        pc_sum = jnp.zeros((n_key, Q_BLK), F32)
        outs = []
        for gp in range(n_pair):
            qa = jnp.concatenate([jnp.where(feat == HEAD_DIM, NEG, qts[2 * gp + e]) for e in range(2)], axis=1)
            qa = qa.astype(BF16)
            qflag_ref[0, kvh, :, 2 * gp * Q_BLK:(2 * gp + 2) * Q_BLK] = qa
            s = jnp.dot(ka, qa, preferred_element_type=F32)
            s = s + cb_ref[kvh, :, 2 * gp * Q_BLK:(2 * gp + 2) * Q_BLK]
            m = jnp.maximum(jnp.max(s, axis=0, keepdims=True), M_FLOOR)
            p = jnp.exp(s - m)
            inv = 1.0 / jnp.maximum(jnp.sum(p, axis=0, keepdims=True), 1e-30)
            pc = p * inv
            pc_sum = pc_sum + pc[:, :Q_BLK] + pc[:, Q_BLK:]
            o = jnp.dot(vpt, p.astype(BF16), preferred_element_type=F32) * inv
            _store_chain(oc_ref, o, gate, kvh * GROUP + 2 * gp, kvh * n_pair + gp)
        imp_t = jnp.dot(mwin_t, pc_sum, preferred_element_type=F32, precision=lax.Precision.HIGHEST)
        selneg_t = _select_blocks_t(imp_t[HEAD_DIM:], tb, N_SEL)
        for g in range(GROUP):
            qsel_ref[0, kvh, :, g * Q_BLK:(g + 1) * Q_BLK] = jnp.concatenate(
                [qts[g][:HEAD_DIM], selneg_t], axis=0).astype(BF16)


def _cmp_prompt(qa, kc_pad, vc_pad, m_pad, cb, gate, batch, seq):
    nq = seq // Q_BLK
    n_key = kc_pad.shape[1] // 2
    d_out = N_HEADS * HEAD_DIM
    return pl.pallas_call(
        functools.partial(_cmp_prompt_body, n_key=n_key),
        grid=(batch, nq),
        in_specs=[pl.BlockSpec((1, Q_BLK, N_HEADS * LANES), lambda b, t: (b, t, 0)),
                  pl.BlockSpec((1,) + kc_pad.shape[1:], lambda b, t: (b, 0, 0)),
                  pl.BlockSpec((1,) + vc_pad.shape[1:], lambda b, t: (b, 0, 0)),
                  pl.BlockSpec(m_pad.shape, lambda b, t: (0, 0)),
                  pl.BlockSpec(cb.shape, lambda b, t: (0, 0, 0)),
                  pl.BlockSpec((Q_BLK, LANES), lambda b, t: (b * nq + t, 0))],
        out_specs=[pl.BlockSpec((1, Q_BLK, d_out), lambda b, t: (b, t, 0))]
        + [pl.BlockSpec((1, KV_HEADS, LANES, GROUP * Q_BLK), lambda b, t: (b * nq + t, 0, 0, 0))] * 2,
        out_shape=[jax.ShapeDtypeStruct((batch, seq, d_out), F32)]
        + [jax.ShapeDtypeStruct((batch * nq, KV_HEADS, LANES, GROUP * Q_BLK), BF16)] * 2,
        compiler_params=_cparams("parallel", "parallel"),
        name="cmp_prompt",
    )(qa.reshape(batch, seq, -1), kc_pad, vc_pad, m_pad, cb, gate)


KEY_TILE = 256
COL_CHAIN = 256
V_PAD = 16
V_ROWS = HEAD_DIM + V_PAD


def _with_ones_rows(v_t):
    n = v_t.shape[1]
    extra = jnp.zeros((KV_HEADS, V_PAD, n), v_t.dtype).at[:, 0].set(1)
    return jnp.concatenate([v_t.reshape(KV_HEADS, HEAD_DIM, n), extra], axis=1).reshape(KV_HEADS * V_ROWS, n)


def _store_chain(out_ref, o_t, gate, gate_col0, lane_col):
    pair = jnp.concatenate([o_t[:, :Q_BLK], o_t[:, Q_BLK:]], axis=0).T
    g = jnp.where(_lane(pair.shape) < HEAD_DIM, _gate_col(gate, gate_col0), _gate_col(gate, gate_col0 + 1))
    out_ref[0, :, lane_col * LANES:(lane_col + 1) * LANES] = pair * g


_PAIR_CHAINS = [(e, c) for e in range(2) for c in range(GROUP * Q_BLK // COL_CHAIN)]


def _slc_prompt_body(q_ref, k_ref, v_ref, sb_ref, gate_ref, out_ref, m_ref, acc_ref, p_ref, s_ref):
    pr = pl.program_id(1)
    qt = pl.program_id(2)
    m_ref[...] = jnp.full(m_ref.shape, M_FLOOR, F32)
    acc_ref[...] = jnp.zeros(acc_ref.shape, F32)
    p_ref[...] = jnp.zeros(p_ref.shape, BF16)
    last = (qt * Q_BLK) // KEY_TILE
    cs = lambda c: slice(c * COL_CHAIN, (c + 1) * COL_CHAIN)

    def pv_of(kt):
        base = pl.multiple_of(kt * KEY_TILE, KEY_TILE)
        return [jnp.dot(v_ref[e * V_ROWS:(e + 1) * V_ROWS, pl.ds(base, KEY_TILE)], p_ref[i],
                        preferred_element_type=F32) for i, (e, c) in enumerate(_PAIR_CHAINS)]

    def scores_of(kt):
        base = pl.multiple_of(kt * KEY_TILE, KEY_TILE)
        ks = [k_ref[0, pl.ds(base, KEY_TILE), e * LANES:(e + 1) * LANES] for e in range(2)]
        return [jnp.dot(ks[e], q_ref[0, e, :, cs(c)], preferred_element_type=F32) for e, c in _PAIR_CHAINS]

    def make_step(with_bias):
        def step(kt, carry):
            s_next = scores_of(jnp.minimum(kt + 1, last))
            pvs = pv_of(jnp.maximum(kt - 1, 0))
            ss = [s_ref[i] for i in range(len(_PAIR_CHAINS))]
            if with_bias:
                case = qt % 2 + 2 * (last - kt)
                ss = [s + sb_ref[case, e, :, cs(c)] for s, (e, c) in zip(ss, _PAIR_CHAINS)]
            alphas = []
            for i, s in enumerate(ss):
                m_old = m_ref[i]
                m_new = jnp.maximum(m_old, jnp.max(s, axis=0, keepdims=True))
                alphas.append(jnp.exp(m_old - m_new))
                m_ref[i] = m_new
                p_ref[i] = jnp.exp(s - m_new).astype(BF16)
            for i in range(len(_PAIR_CHAINS)):
                acc_ref[i] = alphas[i] * (acc_ref[i] + pvs[i])
                s_ref[i] = s_next[i]
            return carry
        return step

    for i, s in enumerate(scores_of(0)):
        s_ref[i] = s
    n_far = jnp.maximum(last - 1, 0)
    lax.fori_loop(0, n_far, make_step(False), 0)
    lax.fori_loop(n_far, last + 1, make_step(True), 0)
    gate = gate_ref[...]
    pvs = pv_of(last)
    for i, (e, c) in enumerate(_PAIR_CHAINS):
        o = acc_ref[i] + pvs[i]
        o = o[:HEAD_DIM] * (1.0 / jnp.maximum(o[HEAD_DIM:HEAD_DIM + 1], 1e-30))
        _store_chain(out_ref, o, gate, N_HEADS + (2 * pr + e) * GROUP + 2 * c, i)


def _slc_prompt(qsel_t, ska, sv_t, sb_t, gate, batch, seq):
    nq = seq // Q_BLK
    cols = GROUP * Q_BLK
    d_out = N_HEADS * HEAD_DIM
    n_chain = len(_PAIR_CHAINS)
    return pl.pallas_call(
        _slc_prompt_body,
        grid=(batch, KV_HEADS // 2, nq),
        in_specs=[pl.BlockSpec((1, 2, LANES, cols), lambda b, p, t: (b * nq + t, p, 0, 0)),
                  pl.BlockSpec((1, seq, 2 * LANES), lambda b, p, t: (b, 0, p)),
                  pl.BlockSpec((2 * V_ROWS, seq), lambda b, p, t: (p, b)),
                  pl.BlockSpec((4, 2, KEY_TILE, cols), lambda b, p, t: (0, p, 0, 0)),
                  pl.BlockSpec((Q_BLK, LANES), lambda b, p, t: (b * nq + t, 0))],
        out_specs=pl.BlockSpec((1, Q_BLK, 2 * GROUP * HEAD_DIM), lambda b, p, t: (b, t, p)),
        out_shape=jax.ShapeDtypeStruct((batch, seq, d_out), F32),
        scratch_shapes=[pltpu.VMEM((n_chain, 1, COL_CHAIN), F32),
                        pltpu.VMEM((n_chain, V_ROWS, COL_CHAIN), F32),
                        pltpu.VMEM((n_chain, KEY_TILE, COL_CHAIN), BF16),
                        pltpu.VMEM((n_chain, KEY_TILE, COL_CHAIN), F32)],
        compiler_params=_cparams("parallel", "parallel", "arbitrary"),
        name="slc_prompt",
    )(qsel_t, ska.reshape(batch, seq, -1), sv_t, sb_t, gate)


def _win_prompt_body(q_ref, k_ref, v_ref, wb_ref, gate_ref, out_ref, *, n_key):
    pr = pl.program_id(1)
    qt = pl.program_id(2)
    base = pl.multiple_of(qt * Q_BLK, Q_BLK)
    cs = lambda c: slice(c * COL_CHAIN, (c + 1) * COL_CHAIN)
    ks = [k_ref[0, pl.ds(base, n_key), e * LANES:(e + 1) * LANES] for e in range(2)]
    ss = [jnp.dot(ks[e], q_ref[0, e, :, cs(c)], preferred_element_type=F32) + wb_ref[e, :, cs(c)]
          for e, c in _PAIR_CHAINS]
    ps = []
    for s in ss:
        m = jnp.maximum(jnp.max(s, axis=0, keepdims=True), M_FLOOR)
        ps.append(jnp.exp(s - m).astype(BF16))
    gate = gate_ref[...]
    for i, (e, c) in enumerate(_PAIR_CHAINS):
        o = jnp.dot(v_ref[e * V_ROWS:(e + 1) * V_ROWS, pl.ds(base, n_key)], ps[i], preferred_element_type=F32)
        o = o[:HEAD_DIM] * (1.0 / jnp.maximum(o[HEAD_DIM:HEAD_DIM + 1], 1e-30))
        _store_chain(out_ref, o, gate, 2 * N_HEADS + (2 * pr + e) * GROUP + 2 * c, i)


def _win_prompt(q_t, wka_pad, wv_t_pad, wb_t, gate, batch, seq):
    nq = seq // Q_BLK
    n_key = WINDOW + Q_BLK
    cols = GROUP * Q_BLK
    d_out = N_HEADS * HEAD_DIM
    padded = wka_pad.shape[1]
    return pl.pallas_call(
        functools.partial(_win_prompt_body, n_key=n_key),
        grid=(batch, KV_HEADS // 2, nq),
        in_specs=[pl.BlockSpec((1, 2, LANES, cols), lambda b, p, t: (b * nq + t, p, 0, 0)),
                  pl.BlockSpec((1, padded, 2 * LANES), lambda b, p, t: (b, 0, p)),
                  pl.BlockSpec((2 * V_ROWS, padded), lambda b, p, t: (p, b)),
                  pl.BlockSpec((2, n_key, cols), lambda b, p, t: (p, 0, 0)),
                  pl.BlockSpec((Q_BLK, LANES), lambda b, p, t: (b * nq + t, 0))],
        out_specs=pl.BlockSpec((1, Q_BLK, 2 * GROUP * HEAD_DIM), lambda b, p, t: (b, t, p)),
        out_shape=jax.ShapeDtypeStruct((batch, seq, d_out), F32),
        compiler_params=_cparams("parallel", "parallel", "parallel"),
        name="win_prompt",
    )(q_t, wka_pad, wv_t_pad, wb_t, gate)


def _nsa_out_body(oc_ref, os_ref, ow_ref, sz_ref, x_ref, wo_ref, gp_ref, out_ref):
    d = oc_ref.shape[-1]
    y = oc_ref[...] * sz_ref[:, :d] + os_ref[...] * sz_ref[:, d:2 * d] + ow_ref[...] * sz_ref[:, 2 * d:]
    o = jnp.dot(y.astype(BF16), wo_ref[...], preferred_element_type=F32)
    out_ref[...] = x_ref[...] + _rms(o, gp_ref[...])


def _nsa_out(oc, os_, ow, sz, x, wo_bf, gp, tm):
    rows, d = x.shape
    dq = oc.shape[-1]
    row = lambda n: pl.BlockSpec((tm, n), lambda i: (i, 0))
    return pl.pallas_call(
        _nsa_out_body,
        grid=(rows // tm,),
        in_specs=[row(dq), row(dq), row(dq), row(N_BRANCH * dq), row(d),
                  pl.BlockSpec((dq, d), lambda i: (0, 0)), pl.BlockSpec((1, d), lambda i: (0, 0))],
        out_specs=row(d),
        out_shape=jax.ShapeDtypeStruct((rows, d), F32),
        compiler_params=_cparams("parallel"),
        name="nsa_out",
    )(oc, os_, ow, sz, x, wo_bf, gp.reshape(1, d))


def _sample_attn_body(pt_ref, *refs, n_pages, tq, w_keep):
    del pt_ref
    page_refs = refs[:n_pages]
    (qa_ref, kvc_ref, snew_ref, wst_ref, wnew_ref, m_ref, e_ref, cbs_ref, sbs_ref, wbs_ref, gate_ref,
     oc_ref, os_ref, ow_ref, ksel_ref, kwin_ref) = refs[n_pages:]
    n_past = n_pages * PAGE_SIZE
    n_sk = ksel_ref.shape[0]
    n_wk = kwin_ref.shape[0]
    width = ksel_ref.shape[1]
    kw = width // 2
    for p in range(n_pages):
        ksel_ref[p * PAGE_SIZE:(p + 1) * PAGE_SIZE, :] = page_refs[p][0]
    ksel_ref[n_past:n_past + tq, :] = snew_ref[...]
    ksel_ref[n_past + tq:, :] = jnp.zeros((n_sk - n_past - tq, width), F32)
    kwin_ref[0:w_keep, :] = wst_ref[0]
    kwin_ref[w_keep:w_keep + tq, :] = wnew_ref[...]
    kwin_ref[w_keep + tq:, :] = jnp.zeros((n_wk - w_keep - tq, width), F32)
    pos = n_past + lax.broadcasted_iota(jnp.int32, (tq, 1), 0)
    tb = pos // L_SLC
    gate = gate_ref[...]
    rows = GROUP * tq

    def attend(q, k, v, bias, extra=None):
        s = lax.dot_general(q, k, _NT, preferred_element_type=F32)
        if extra is not None:
            s = s + extra
        s = (s.reshape(GROUP, tq, s.shape[-1]) + bias).reshape(rows, s.shape[-1])
        p, l = _softmax_rows(s)
        inv = 1.0 / jnp.maximum(l, 1e-30)
        return p, inv, jnp.dot(p.astype(BF16), v, preferred_element_type=F32)

    def store(out_ref, o, kvh, branch):
        outs = [o[g * tq:(g + 1) * tq] * _gate_col(gate, branch * N_HEADS + kvh * GROUP + g) for g in range(GROUP)]
        for gp in range(GROUP // 2):
            col = kvh * (GROUP // 2) + gp
            out_ref[:, col * LANES:(col + 1) * LANES] = _pair_columns(outs[2 * gp], outs[2 * gp + 1], kvh % 2)

    def pair_cols(ref, pr):
        kcol = slice(pr * LANES, (pr + 1) * LANES)
        vcol = slice(kw + pr * LANES, kw + (pr + 1) * LANES)
        if len(ref.shape) == 3:
            return ref[0, :, kcol].astype(BF16), ref[0, :, vcol].astype(BF16)
        return ref[:, kcol].astype(BF16), ref[:, vcol].astype(BF16)

    qs, imps = [], []
    for kvh in range(KV_HEADS):
        q = jnp.concatenate([qa_ref[:, h * LANES:(h + 1) * LANES]
                             for h in range(kvh * GROUP, (kvh + 1) * GROUP)], axis=0)
        qs.append((q if kvh % 2 == 0 else _swap_halves(q)).astype(BF16))
    for pr in range(KV_HEADS // 2):
        kc, vc = pair_cols(kvc_ref, pr)
        for kvh in (2 * pr, 2 * pr + 1):
            p, inv, o = attend(qs[kvh], kc, vc, cbs_ref[kvh * GROUP:(kvh + 1) * GROUP])
            store(oc_ref, o * inv, kvh, 0)
            pc_sum = jnp.sum((p * inv).reshape(GROUP, tq, p.shape[-1]), axis=0)
            imps.append(jnp.dot(pc_sum, m_ref[...], preferred_element_type=F32, precision=lax.Precision.HIGHEST))
    selneg = _select_blocks(jnp.concatenate(imps, axis=0), jnp.concatenate([tb] * KV_HEADS, axis=0), N_SEL)
    for pr in range(KV_HEADS // 2):
        ksl, vsl = pair_cols(ksel_ref, pr)
        kwn, vwn = pair_cols(kwin_ref, pr)
        for kvh in (2 * pr, 2 * pr + 1):
            heads = slice(kvh * GROUP, (kvh + 1) * GROUP)
            qm = jnp.concatenate([selneg[kvh * tq:(kvh + 1) * tq]] * GROUP, axis=0).astype(BF16)
            block_mask = jnp.dot(qm, e_ref[...], preferred_element_type=F32)
            _, inv, o = attend(qs[kvh], ksl, vsl, sbs_ref[heads], block_mask)
            store(os_ref, o * inv, kvh, 1)
            _, inv, o = attend(qs[kvh], kwn, vwn, wbs_ref[heads])
            store(ow_ref, o * inv, kvh, 2)


def _sample_attn(qa, kvc, pages, page_table, slc_new, win_state, win_new, m_s, e_s, cbs, sbs, wbs, gate,
                 n_seq, tq, n_pages):
    width = pages.shape[-1]
    n_sk = sbs.shape[-1]
    n_wk = wbs.shape[-1]
    w_keep = win_state.shape[1]
    d_out = N_HEADS * HEAD_DIM
    page_spec = lambda j: pl.BlockSpec((1, PAGE_SIZE, width), lambda i, pt: (pt[i * n_pages + j], 0, 0))
    full = lambda a: pl.BlockSpec(a.shape, lambda i, pt: (0,) * a.ndim)
    row = lambda n: pl.BlockSpec((tq, n), lambda i, pt: (i, 0))
    grid_spec = pltpu.PrefetchScalarGridSpec(
        num_scalar_prefetch=1,
        grid=(n_seq,),
        in_specs=[page_spec(j) for j in range(n_pages)] + [
            row(N_HEADS * LANES),
            pl.BlockSpec((1,) + kvc.shape[1:], lambda i, pt: (i, 0, 0)),
            row(width),
            pl.BlockSpec((1, w_keep, width), lambda i, pt: (i, 0, 0)),
            row(width),
            full(m_s), full(e_s), full(cbs), full(sbs), full(wbs),
            row(LANES)],
        out_specs=[row(d_out)] * 3,
        scratch_shapes=[pltpu.VMEM((n_sk, width), F32), pltpu.VMEM((n_wk, width), F32)],
    )
    return pl.pallas_call(
        functools.partial(_sample_attn_body, n_pages=n_pages, tq=tq, w_keep=w_keep),
        grid_spec=grid_spec,
        out_shape=[jax.ShapeDtypeStruct((n_seq * tq, d_out), F32)] * 3,
        compiler_params=_cparams("parallel"),
        name="sample_attn",
    )(page_table, *([pages] * n_pages), qa, kvc, slc_new, win_state, win_new, m_s, e_s, cbs, sbs, wbs, gate)


def _decode_attn_body(pt_ref, *refs, n_pages, tq, w_keep):
    del pt_ref
    page_refs = refs[:n_pages]
    (qa_ref, kvc_ref, snew_ref, wst_ref, wnew_ref, mt_ref, e_ref, cbs_ref, sbs_ref, wbs_ref, gate_ref,
     oc_ref, os_ref, ow_ref, kt_ref, vt_ref, kwt_ref, vwt_ref) = refs[n_pages:]
    n_past = n_pages * PAGE_SIZE
    kw = KV_HEADS * HEAD_DIM
    rows = N_HEADS * tq

    def new_tile(ref):
        return jnp.concatenate([ref[...], jnp.zeros((LANES - tq, 2 * kw), F32)], axis=0).T

    def fill(k_dst, v_dst, col0, tile):
        k_dst[:, col0:col0 + tile.shape[1]] = tile[:kw].astype(BF16)
        v_dst[:, col0:col0 + tile.shape[1]] = tile[kw:].astype(BF16)

    for p in range(n_pages):
        fill(kt_ref, vt_ref, p * PAGE_SIZE, page_refs[p][0])
    fill(kt_ref, vt_ref, n_past, new_tile(snew_ref))
    fill(kwt_ref, vwt_ref, 0, wst_ref[0])
    fill(kwt_ref, vwt_ref, w_keep, new_tile(wnew_ref))

    zero = jnp.zeros((tq, LANES), F32)
    tiles = [[], []]
    for h in range(N_HEADS):
        kvh = h // GROUP
        piece = qa_ref[:, h * LANES:(h + 1) * LANES]
        piece = piece if kvh % 2 == 0 else _swap_halves(piece)
        for ct in range(2):
            tiles[ct].append(piece if kvh // 2 == ct else zero)
    qbd = jnp.concatenate([jnp.concatenate(t, axis=0) for t in tiles], axis=1).astype(BF16)
    gate = gate_ref[...]

    def branch(k_t, v_t, bias, extra=None):
        s = jnp.dot(qbd, k_t, preferred_element_type=F32) + bias
        if extra is not None:
            s = s + extra
        p, l = _softmax_rows(s)
        inv = 1.0 / jnp.maximum(l, 1e-30)
        o_t = lax.dot_general(v_t, p.astype(BF16), _NT, preferred_element_type=F32)
        return p, inv, o_t.T

    def store(out_ref, o, scale):
        o = o * scale
        for kvh in range(KV_HEADS):
            tile = o[:, (kvh // 2) * LANES:(kvh // 2 + 1) * LANES]
            for gp in range(GROUP // 2):
                h0 = kvh * GROUP + 2 * gp
                col = kvh * (GROUP // 2) + gp
                out_ref[:, col * LANES:(col + 1) * LANES] = _pair_columns(
                    tile[h0 * tq:(h0 + 1) * tq], tile[(h0 + 1) * tq:(h0 + 2) * tq], kvh % 2)

    p, inv, o = branch(kvc_ref[0, :kw, :].astype(BF16), kvc_ref[0, kw:, :].astype(BF16), cbs_ref[...])
    store(oc_ref, o, inv * _gate_col(gate, 0))
    pc = p * inv
    pc_sum = jnp.concatenate(
        [sum(pc[(kvh * GROUP + g) * tq:(kvh * GROUP + g + 1) * tq] for g in range(GROUP)) for kvh in range(KV_HEADS)],
        axis=0)
    imp_t = lax.dot_general(mt_ref[...], pc_sum, _NT, preferred_element_type=F32, precision=lax.Precision.HIGHEST)
    n_col = KV_HEADS * tq
    nb_pad = -(-(-(-(n_past + tq) // L_SLC)) // 8) * 8
    tb = (n_past + lax.broadcasted_iota(jnp.int32, (1, n_col), 1) % tq) // L_SLC
    selneg_t = _select_blocks_t(imp_t[HEAD_DIM:HEAD_DIM + nb_pad], tb, N_SEL)
    sel_t = jnp.concatenate([jnp.zeros((HEAD_DIM, n_col), F32), selneg_t,
                             jnp.zeros((LANES - HEAD_DIM - nb_pad, n_col), F32)], axis=0)
    sel = jnp.concatenate([sel_t, jnp.zeros((LANES, LANES - n_col), F32)], axis=1).T
    sel = jnp.concatenate([sel[kvh * tq:(kvh + 1) * tq] for kvh in range(KV_HEADS) for _ in range(GROUP)], axis=0)
    block_mask = jnp.dot(sel.astype(BF16), e_ref[...], preferred_element_type=F32)
    _, inv, o = branch(kt_ref[...], vt_ref[...], sbs_ref[...], block_mask)
    store(os_ref, o, inv * _gate_col(gate, 1))
    _, inv, o = branch(kwt_ref[...], vwt_ref[...], wbs_ref[...])
    store(ow_ref, o, inv * _gate_col(gate, 2))


def _decode_attn(qa, kvc_t, pages_t, page_table, slc_new, win_state_t, win_new, m_t, e_s, cbs, sbs, wbs, gate_r,
                 n_seq, tq, n_pages):
    width = pages_t.shape[1]
    n_sk = sbs.shape[-1]
    n_wk = wbs.shape[-1]
    w_keep = win_state_t.shape[2]
    d_out = N_HEADS * HEAD_DIM
    rows = N_HEADS * tq
    page_spec = lambda j: pl.BlockSpec((1, width, PAGE_SIZE), lambda i, pt: (pt[i * n_pages + j], 0, 0))
    full = lambda a: pl.BlockSpec(a.shape, lambda i, pt: (0,) * a.ndim)
    row = lambda n: pl.BlockSpec((tq, n), lambda i, pt: (i, 0))
    grid_spec = pltpu.PrefetchScalarGridSpec(
        num_scalar_prefetch=1,
        grid=(n_seq,),
        in_specs=[page_spec(j) for j in range(n_pages)] + [
            row(N_HEADS * LANES),
            pl.BlockSpec((1,) + kvc_t.shape[1:], lambda i, pt: (i, 0, 0)),
            row(width),
            pl.BlockSpec((1, width, w_keep), lambda i, pt: (i, 0, 0)),
            row(width),
            full(m_t), full(e_s), full(cbs), full(sbs), full(wbs),
            pl.BlockSpec((rows, LANES), lambda i, pt: (i, 0))],
        out_specs=[row(d_out)] * 3,
        scratch_shapes=[pltpu.VMEM((width // 2, n_sk), BF16), pltpu.VMEM((width // 2, n_sk), BF16),
                        pltpu.VMEM((width // 2, n_wk), BF16), pltpu.VMEM((width // 2, n_wk), BF16)],
    )
    return pl.pallas_call(
        functools.partial(_decode_attn_body, n_pages=n_pages, tq=tq, w_keep=w_keep),
        grid_spec=grid_spec,
        out_shape=[jax.ShapeDtypeStruct((n_seq * tq, d_out), F32)] * 3,
        compiler_params=_cparams("parallel"),
        name="decode_attn",
    )(page_table, *([pages_t] * n_pages), qa, kvc_t, slc_new, win_state_t, win_new, m_t, e_s, cbs, sbs, wbs, gate_r)


def _overlap_matrix(n_rows, row0, n_cmp, n_blk):
    import numpy as np
    m = np.zeros((n_rows, LANES), np.float32)
    cs = np.arange(n_cmp)[:, None] * CMP_STRIDE
    js = np.arange(n_blk)[None, :] * L_SLC
    m[row0:row0 + n_cmp, HEAD_DIM:HEAD_DIM + n_blk] = (cs <= js + L_SLC - 1) & (cs + L_CMP - 1 >= js)
    return jnp.asarray(m)


def kernel(x_prompt, x_sample, state_conv, cache_cmp, cache_slc, state_win, page_table, rel_bias, a_norm_pre, a_w_in, a_conv_w, a_conv_b, a_ln_g, a_ln_b, a_w_out, a_norm_post, kv_norm, w_kv, cmp_pe, cmp_w1, cmp_w2, b_norm_pre, b_w_in, b_w_out, b_norm_post):
    import numpy as np
    bp, tp, d = x_prompt.shape
    bd, tq, _ = x_sample.shape
    n_pages = page_table.shape[1]
    past = n_pages * PAGE_SIZE
    w_keep = state_win.shape[1]
    width = 2 * KV_HEADS * HEAD_DIM
    kw = KV_HEADS * HEAD_DIM
    dq = N_HEADS * HEAD_DIM
    assert b_w_in.shape[0] == 1 and tp % (2 * KEY_TILE) == 0 and tp // L_SLC <= HEAD_DIM
    tm = 256
    tm_s = min(tm, bd * tq)

    xp = x_prompt
    xs = x_sample.reshape(bd * tq, d)
    conv_p, conv_s = [], []
    for l in range(a_w_in.shape[0]):
        w_in = a_w_in[l].astype(BF16)
        w_out = a_w_out[l].astype(BF16)
        di = a_w_out.shape[1]
        tail = (a_conv_w[l], a_conv_b[l], a_ln_g[l], a_ln_b[l], w_out, a_norm_post[l])
        glu, sz = _glu_proj(xp.reshape(bp * tp, d), a_norm_pre[l], w_in, tm)
        glu = glu.reshape(bp, tp, di)
        xp = _conv_prompt(glu, sz.reshape(bp, tp, di), xp, *tail, tm)
        conv_p.append(glu[:, -(CONV_W - 1):])
        glu, sz = _glu_proj(xs, a_norm_pre[l], w_in, tm_s)
        xs = _conv_sample(glu, state_conv[l], sz, xs, *tail, 16)
        conv_s.append(jnp.concatenate([state_conv[l], glu.reshape(bd, tq, di)], axis=1)[:, -(CONV_W - 1):])

    bw = b_w_in[0]
    n_gate = N_BRANCH * N_HEADS
    wq = bw[:, :dq].astype(BF16)
    wz = bw[:, dq:dq * (1 + N_BRANCH)].astype(BF16)
    wg = jnp.pad(bw[:, dq * (1 + N_BRANCH):], ((0, 0), (0, LANES - n_gate))).astype(BF16)
    wkv = w_kv.astype(BF16)
    xp2 = xp.reshape(bp * tp, d)
    (cmp_p, slc_p, win_p, qa_p, sz_p, gate_p, ska, sv, wka, wv) = _nsa_proj(
        xp2, kv_norm, b_norm_pre[0], wkv, wq, wz, wg, tm, seq_len=tp)
    cmp_s, slc_s, win_s, qa_s, sz_s, gate_s = _nsa_proj(xs, kv_norm, b_norm_pre[0], wkv, wq, wz, wg, tm_s)
    qa_s = qa_s.astype(F32)

    w1cat = jnp.concatenate([cmp_w1[:, :CMP_STRIDE * HEAD_DIM], cmp_w1[:, CMP_STRIDE * HEAD_DIM:]], axis=2).astype(BF16)
    pe8 = jnp.pad(cmp_pe.reshape(2, L_CMP // CMP_STRIDE, CMP_STRIDE * HEAD_DIM), ((0, 0), (0, 6), (0, 0)))
    zero = jnp.zeros_like(cmp_w2)
    w2h = jnp.stack([jnp.concatenate([cmp_w2, zero], axis=2), jnp.concatenate([zero, cmp_w2], axis=2)], axis=1).astype(BF16)
    prompt_page = min(1024, tp)
    pp = tp // prompt_page
    kvc_p = _compress(cmp_p, None, w1cat, pe8, w2h, pp, prompt_page).transpose(0, 2, 1)
    pt_flat = page_table.reshape(-1).astype(jnp.int32)
    cmp_pages_t = cache_cmp.transpose(0, 2, 3, 4, 1).reshape(-1, width, PAGE_SIZE)
    kvc_s = _compress(cmp_pages_t, pt_flat, w1cat, pe8, w2h, n_pages, PAGE_SIZE)

    n_chunk_p = kvc_p.shape[1]
    nc_p = n_chunk_p - L_CMP // CMP_STRIDE + 1
    real = (jnp.arange(n_chunk_p) < nc_p)[None, :, None]
    flag = jnp.zeros((LANES - HEAD_DIM,), F32).at[0].set(1.0)
    kc4 = jnp.where(real, kvc_p[:, :, :kw], 0.0).reshape(bp, n_chunk_p, KV_HEADS, HEAD_DIM)
    aug = jnp.where(real[..., None], 0.0, flag) * jnp.ones((bp, n_chunk_p, KV_HEADS, 1), F32)
    kc_real = jnp.concatenate([kc4, aug], axis=-1).reshape(bp, n_chunk_p, KV_HEADS * LANES)
    pad_row = jnp.concatenate([jnp.zeros((HEAD_DIM,), F32), flag])
    kc_front = jnp.broadcast_to(jnp.tile(pad_row, KV_HEADS), (bp, n_chunk_p, KV_HEADS * LANES))
    kc_pad = jnp.concatenate([kc_front, kc_real], axis=1)
    vc_pad = jnp.concatenate([jnp.zeros((bp, n_chunk_p, kw), F32), jnp.where(real, kvc_p[:, :, kw:], 0.0)], axis=1)
    m_pad = _overlap_matrix(2 * n_chunk_p, n_chunk_p, nc_p, tp // L_SLC)
    cb = _bias_tile(rel_bias, n_chunk_p, Q_BLK, 1, CMP_STRIDE * (n_chunk_p - Q_BLK // CMP_STRIDE) - (L_CMP - 1),
                    row_step=-CMP_STRIDE)
    cb = cb.reshape(KV_HEADS, GROUP, n_chunk_p, Q_BLK).transpose(0, 2, 1, 3).reshape(KV_HEADS, n_chunk_p, GROUP * Q_BLK)
    oc_p, qsel, qflag = _cmp_prompt(qa_p, kc_pad, vc_pad, m_pad, cb, gate_p, bp, tp)
    sb = jnp.stack([_bias_tile(rel_bias, KEY_TILE, Q_BLK, 1, off, row_step=-1)
                    for off in (0, Q_BLK, 2 * Q_BLK, 3 * Q_BLK)])
    sb = sb.reshape(4, KV_HEADS, GROUP, KEY_TILE, Q_BLK).transpose(0, 1, 3, 2, 4).reshape(4, KV_HEADS, KEY_TILE, GROUP * Q_BLK)
    os_p = _slc_prompt(qsel, ska, _with_ones_rows(sv), sb, gate_p, bp, tp)
    n_wkey = WINDOW + Q_BLK
    wb = _bias_tile(rel_bias, n_wkey, Q_BLK, 1, WINDOW, WINDOW, row_step=-1)
    wb = wb.reshape(KV_HEADS, GROUP, n_wkey, Q_BLK).transpose(0, 2, 1, 3).reshape(KV_HEADS, n_wkey, GROUP * Q_BLK)
    wka_front = jnp.broadcast_to(jnp.tile(pad_row, KV_HEADS).astype(BF16), (bp, WINDOW, KV_HEADS * LANES))
    wka_pad = jnp.concatenate([wka_front, wka.reshape(bp, tp, -1)], axis=1)
    wv_pad = jnp.concatenate([jnp.zeros((kw, bp, WINDOW), BF16), wv.reshape(kw, bp, tp)], axis=2)
    ow_p = _win_prompt(qflag, wka_pad, _with_ones_rows(wv_pad.reshape(kw, bp * (WINDOW + tp))), wb, gate_p, bp, tp)
    wo = b_w_out[0].astype(BF16)
    y_p = _nsa_out(oc_p.reshape(bp * tp, dq), os_p.reshape(bp * tp, dq), ow_p.reshape(bp * tp, dq), sz_p, xp2,
                   wo, b_norm_post[0], tm)

    n_chunk_s = kvc_s.shape[2]
    nc_s = n_chunk_s - L_CMP // CMP_STRIDE + 1
    nb_s = -(-(past + tq) // L_SLC)
    n_sk = past + LANES
    n_wk = w_keep + LANES
    m_t = _overlap_matrix(n_chunk_s, 0, nc_s, nb_s).T
    e_np = np.zeros((LANES, n_sk), np.float32)
    e_np[HEAD_DIM + np.arange(n_sk) // L_SLC, np.arange(n_sk)] = 1.0
    e_s = jnp.asarray(e_np, BF16)
    rows_s = N_HEADS * tq
    cbs = _bias_tile(rel_bias, tq, n_chunk_s, -CMP_STRIDE, past - (L_CMP - 1)).reshape(rows_s, n_chunk_s)
    sbs = _bias_tile(rel_bias, tq, n_sk, -1, past).reshape(rows_s, n_sk)
    wbs = _bias_tile(rel_bias, tq, n_wk, -1, w_keep, WINDOW).reshape(rows_s, n_wk)
    gate_r = gate_s[:, :n_gate].reshape(bd, tq, N_BRANCH, N_HEADS).transpose(0, 3, 1, 2).reshape(bd * rows_s, N_BRANCH)
    gate_r = jnp.pad(gate_r, ((0, 0), (0, LANES - N_BRANCH)))
    to_t = lambda a: a.transpose(0, 2, 3, 4, 1).reshape(a.shape[0], width, a.shape[1])
    oc_s, os_s, ow_s = _decode_attn(qa_s, kvc_s, to_t(cache_slc), pt_flat, slc_s, to_t(state_win), win_s,
                                    m_t, e_s, cbs, sbs, wbs, gate_r, bd, tq, n_pages)
    y_s = _nsa_out(oc_s, os_s, ow_s, sz_s, xs, wo, b_norm_post[0], tm_s)

    kv5 = lambda a, b, t: a.reshape(b, t, 2, KV_HEADS, HEAD_DIM)
    kv5_t = lambda a: a.reshape(bp, 2, KV_HEADS, HEAD_DIM, a.shape[2]).transpose(0, 4, 1, 2, 3)
    win_all = jnp.concatenate([state_win, kv5(win_s, bd, tq)], axis=1)
    return (y_p.reshape(bp, tp, d), y_s.reshape(bd, tq, d), jnp.stack(conv_p), jnp.stack(conv_s),
            kv5_t(cmp_p), kv5(cmp_s, bd, tq), kv5_t(slc_p), kv5(slc_s, bd, tq),
            kv5_t(win_p[:, :, -min(WINDOW, tp):]), win_all[:, -min(WINDOW, win_all.shape[1]):])
```

```python
import functools
import math

import jax
import jax.numpy as jnp
from jax import lax
from jax.experimental import pallas as pl
from jax.experimental.pallas import tpu as pltpu

F32 = jnp.float32
BF16 = jnp.bfloat16

EPS = 1e-6
NEG = -1e30
BIG = 1e9
M_FLOOR = -1e20

HEAD_DIM = 64
KV_HEADS = 4
N_HEADS = 16
GROUP = N_HEADS // KV_HEADS
N_BRANCH = 3
CONV_W = 31
L_CMP = 32
CMP_STRIDE = 16
L_SLC = 64
N_SEL = 16
WINDOW = 512
Q_BLK = 128
NUM_BUCKETS = 32
MAX_DISTANCE = 128
MAX_EXACT = NUM_BUCKETS // 2
PAGE_SIZE = 128
LANES = 128
HALO = 32
CONV_TAIL = 16

VMEM_LIMIT = 56 * 1024 * 1024


def _cparams(*sem):
    return pltpu.CompilerParams(dimension_semantics=sem, vmem_limit_bytes=VMEM_LIMIT)


def _sigmoid(x):
    return 1.0 / (1.0 + jnp.exp(-x))


def _rms(x, g):
    return x * lax.rsqrt(jnp.mean(x * x, axis=-1, keepdims=True) + EPS) * g


def _lane(shape):
    return lax.broadcasted_iota(jnp.int32, shape, len(shape) - 1)


def _swap_halves(x):
    return pltpu.roll(x, HEAD_DIM, axis=x.ndim - 1)


def _glu_proj_body(x_ref, g_ref, w_ref, glu_ref, sz_ref):
    di = glu_ref.shape[-1]
    h = _rms(x_ref[...], g_ref[...])
    u = jnp.dot(h.astype(BF16), w_ref[...], preferred_element_type=F32)
    z = u[:, 2 * di:]
    glu_ref[...] = u[:, :di] * _sigmoid(u[:, di:2 * di])
    sz_ref[...] = z * _sigmoid(z)


def _glu_proj(x, g, w_bf, tm):
    rows, d = x.shape
    di = w_bf.shape[1] // 3
    return pl.pallas_call(
        _glu_proj_body,
        grid=(rows // tm,),
        in_specs=[pl.BlockSpec((tm, d), lambda i: (i, 0)),
                  pl.BlockSpec((1, d), lambda i: (0, 0)),
                  pl.BlockSpec((d, 3 * di), lambda i: (0, 0))],
        out_specs=[pl.BlockSpec((tm, di), lambda i: (i, 0)),
                   pl.BlockSpec((tm, di), lambda i: (i, 0))],
        out_shape=[jax.ShapeDtypeStruct((rows, di), F32)] * 2,
        compiler_params=_cparams("parallel"),
        name="glu_proj",
    )(x, g.reshape(1, d), w_bf)


def _conv_tail(c, sz, x, lg_ref, lb_ref, wo_ref, gp_ref):
    mu = jnp.mean(c, axis=-1, keepdims=True)
    cc = c - mu
    var = jnp.mean(cc * cc, axis=-1, keepdims=True)
    y = cc * lax.rsqrt(var + EPS) * lg_ref[...] + lb_ref[...]
    y = y * _sigmoid(y) * sz
    o = jnp.dot(y.astype(BF16), wo_ref[...], preferred_element_type=F32)
    return x + _rms(o, gp_ref[...])


def _conv_prompt_body(glu_ref, prev_ref, sz_ref, x_ref, cw_ref, cb_ref, lg_ref, lb_ref, wo_ref, gp_ref,
                      out_ref, full_ref, c_ref, *, tm):
    t = pl.program_id(1)
    full_ref[0:HALO, :] = jnp.where(t > 0, prev_ref[0], 0.0)
    full_ref[HALO:HALO + tm, :] = glu_ref[0]
    full_ref[HALO + tm:, :] = jnp.zeros((CONV_TAIL, full_ref.shape[1]), F32)
    d = c_ref.shape[-1]
    first = HALO - (CONV_W - 1)
    sub = 8
    half = tm // 2
    for lc in range(d // LANES):
        ln = slice(lc * LANES, (lc + 1) * LANES)
        for r0 in range(0, tm, half):
            acc = jnp.zeros((half, LANES), F32)
            for b in range(sub):
                z = None
                for a in range(-(-(CONV_W - b) // sub)):
                    x = full_ref[r0 + sub * a:r0 + sub * a + half + 2 * sub, ln]
                    term = cw_ref[sub * a + b:sub * a + b + 1, ln] * x
                    z = term if z is None else z + term
                acc = acc + z[first + b:first + b + half]
            c_ref[r0:r0 + half, ln] = acc + cb_ref[:, ln]
    out_ref[0] = _conv_tail(c_ref[...], sz_ref[0], x_ref[0], lg_ref, lb_ref, wo_ref, gp_ref)


def _conv_prompt(glu, sz, x, cw, cb, lg, lb, wo_bf, gp, tm):
    b, t, d = x.shape
    di = glu.shape[-1]
    per = tm // HALO
    vec = lambda n: pl.BlockSpec((1, n), lambda i, j: (0, 0))
    return pl.pallas_call(
        functools.partial(_conv_prompt_body, tm=tm),
        grid=(b, t // tm),
        in_specs=[pl.BlockSpec((1, tm, di), lambda i, j: (i, j, 0)),
                  pl.BlockSpec((1, HALO, di), lambda i, j: (i, jnp.maximum(j * per - 1, 0), 0)),
                  pl.BlockSpec((1, tm, di), lambda i, j: (i, j, 0)),
                  pl.BlockSpec((1, tm, d), lambda i, j: (i, j, 0)),
                  pl.BlockSpec((CONV_W, di), lambda i, j: (0, 0)),
                  vec(di), vec(di), vec(di),
                  pl.BlockSpec((di, d), lambda i, j: (0, 0)),
                  vec(d)],
        out_specs=pl.BlockSpec((1, tm, d), lambda i, j: (i, j, 0)),
        out_shape=jax.ShapeDtypeStruct((b, t, d), F32),
        scratch_shapes=[pltpu.VMEM((HALO + tm + CONV_TAIL, di), F32), pltpu.VMEM((tm, di), F32)],
        compiler_params=_cparams("parallel", "arbitrary"),
        name="conv_prompt",
    )(glu, glu, sz, x, cw, cb.reshape(1, di), lg.reshape(1, di), lb.reshape(1, di), wo_bf, gp.reshape(1, d))


def _conv_sample_body(glu_ref, st_ref, sz_ref, x_ref, cw_ref, cb_ref, lg_ref, lb_ref, wo_ref, gp_ref,
                      out_ref, full_ref, c_ref, *, nb, tq):
    d = c_ref.shape[-1]
    first = HALO - (CONV_W - 1)
    full_ref[:, first:HALO, :] = st_ref[...]
    full_ref[:, HALO:HALO + tq, :] = glu_ref[...].reshape(nb, tq, d)
    for lc in range(d // LANES):
        ln = slice(lc * LANES, (lc + 1) * LANES)
        acc = jnp.zeros((nb, tq, LANES), F32)
        for j in range(CONV_W):
            acc = acc + cw_ref[j:j + 1, ln] * full_ref[:, first + j:first + j + tq, ln]
        c_ref[:, ln] = (acc + cb_ref[:, ln]).reshape(nb * tq, LANES)
    out_ref[...] = _conv_tail(c_ref[...], sz_ref[...], x_ref[...], lg_ref, lb_ref, wo_ref, gp_ref)


def _conv_sample(glu, state, sz, x, cw, cb, lg, lb, wo_bf, gp, nb):
    n_seq = state.shape[0]
    rows, d = x.shape
    di = glu.shape[-1]
    tq = rows // n_seq
    vec = lambda n: pl.BlockSpec((1, n), lambda i: (0, 0))
    return pl.pallas_call(
        functools.partial(_conv_sample_body, nb=nb, tq=tq),
        grid=(n_seq // nb,),
        in_specs=[pl.BlockSpec((nb * tq, di), lambda i: (i, 0)),
                  pl.BlockSpec((nb, CONV_W - 1, di), lambda i: (i, 0, 0)),
                  pl.BlockSpec((nb * tq, di), lambda i: (i, 0)),
                  pl.BlockSpec((nb * tq, d), lambda i: (i, 0)),
                  pl.BlockSpec((CONV_W, di), lambda i: (0, 0)),
                  vec(di), vec(di), vec(di),
                  pl.BlockSpec((di, d), lambda i: (0, 0)),
                  vec(d)],
        out_specs=pl.BlockSpec((nb * tq, d), lambda i: (i, 0)),
        out_shape=jax.ShapeDtypeStruct((rows, d), F32),
        scratch_shapes=[pltpu.VMEM((nb, HALO + tq, di), F32), pltpu.VMEM((nb * tq, di), F32)],
        compiler_params=_cparams("parallel"),
        name="conv_sample",
    )(glu, state, sz, x, cw, cb.reshape(1, di), lg.reshape(1, di), lb.reshape(1, di), wo_bf, gp.reshape(1, d))


def _head_major(col_pair, head):
    return col_pair if head % 2 == 0 else _swap_halves(col_pair)


def _nsa_proj_body(x_ref, gkv_ref, gq_ref, wkv_ref, wq_ref, wz_ref, wg_ref,
                   cmp_ref, slc_ref, win_ref, qa_ref, sz_ref, gate_ref, *aug_refs, seq_len, tm):
    x = x_ref[...]
    xn = x * lax.rsqrt(jnp.mean(x * x, axis=-1, keepdims=True) + EPS)
    hkv = (xn * gkv_ref[...]).astype(BF16)
    hq = (xn * gq_ref[...]).astype(BF16)
    kv = jnp.dot(hkv, wkv_ref[...], preferred_element_type=F32)
    width = 2 * KV_HEADS * HEAD_DIM
    if aug_refs:
        kv_t = [kv[:, b * width:(b + 1) * width].T for b in range(N_BRANCH)]
        cmp_ref[0] = kv_t[0]
        slc_ref[0] = kv_t[1]
        win_ref[0] = kv_t[2]
    else:
        cmp_ref[...] = kv[:, :width]
        slc_ref[...] = kv[:, width:2 * width]
        win_ref[...] = kv[:, 2 * width:]
    lane = _lane((tm, LANES))
    low = lane < HEAD_DIM
    uq = jnp.dot(hq, wq_ref[...], preferred_element_type=F32)
    scale = HEAD_DIM ** -0.5
    for h in range(N_HEADS):
        qh = _head_major(uq[:, (h // 2) * LANES:(h // 2 + 1) * LANES], h)
        qa_ref[:, h * LANES:(h + 1) * LANES] = jnp.where(low, qh * scale, 0.0).astype(BF16)
    z = jnp.dot(hq, wz_ref[...], preferred_element_type=F32)
    sz_ref[...] = z * _sigmoid(z)
    gate_ref[...] = _sigmoid(jnp.dot(hq, wg_ref[...], preferred_element_type=F32))
    if aug_refs:
        ska_ref, sv_ref, wka_ref, wv_ref = aug_refs
        row = pl.program_id(0) * tm + lax.broadcasted_iota(jnp.int32, (tm, LANES), 0)
        blk = (row % seq_len) // L_SLC
        onehot = (lane - HEAD_DIM == blk).astype(F32)
        kw = KV_HEADS * HEAD_DIM
        for h in range(KV_HEADS):
            ks = _head_major(kv[:, width + (h // 2) * LANES:width + (h // 2 + 1) * LANES], h)
            ska_ref[:, h * LANES:(h + 1) * LANES] = jnp.where(low, ks, onehot).astype(BF16)
            kwn = _head_major(kv[:, 2 * width + (h // 2) * LANES:2 * width + (h // 2 + 1) * LANES], h)
            wka_ref[:, h * LANES:(h + 1) * LANES] = jnp.where(low, kwn, 0.0).astype(BF16)
        sv_ref[...] = kv_t[1][kw:].astype(BF16)
        wv_ref[...] = kv_t[2][kw:].astype(BF16)


def _nsa_proj(x, g_kv, g_q, wkv_bf, wq_bf, wz_bf, wg_bf, tm, seq_len=None):
    rows, d = x.shape
    width = 2 * KV_HEADS * HEAD_DIM
    row = lambda n: pl.BlockSpec((tm, n), lambda i: (i, 0))
    full = lambda a: pl.BlockSpec(a.shape, lambda i: (0, 0))
    out_specs = [row(width), row(width), row(width), row(N_HEADS * LANES), row(wz_bf.shape[1]), row(LANES)]
    out_shape = [jax.ShapeDtypeStruct((rows, width), F32)] * 3 + [
        jax.ShapeDtypeStruct((rows, N_HEADS * LANES), BF16),
        jax.ShapeDtypeStruct((rows, wz_bf.shape[1]), F32),
        jax.ShapeDtypeStruct((rows, LANES), F32)]
    if seq_len is not None:
        per_seq = seq_len // tm
        kv_t_spec = pl.BlockSpec((1, width, tm), lambda i: (i // per_seq, 0, i % per_seq))
        out_specs[:N_BRANCH] = [kv_t_spec] * N_BRANCH
        out_shape[:N_BRANCH] = [jax.ShapeDtypeStruct((rows // seq_len, width, seq_len), F32)] * N_BRANCH
        out_specs += [row(KV_HEADS * LANES), pl.BlockSpec((width // 2, tm), lambda i: (0, i))] * 2
        out_shape += [jax.ShapeDtypeStruct((rows, KV_HEADS * LANES), BF16),
                      jax.ShapeDtypeStruct((width // 2, rows), BF16)] * 2
    gkv = g_kv.reshape(1, d)
    gq = g_q.reshape(1, d)
    return pl.pallas_call(
        functools.partial(_nsa_proj_body, seq_len=seq_len, tm=tm),
        grid=(rows // tm,),
        in_specs=[row(d), full(gkv), full(gq), full(wkv_bf), full(wq_bf), full(wz_bf), full(wg_bf)],
        out_specs=out_specs,
        out_shape=out_shape,
        compiler_params=_cparams("parallel"),
        name="nsa_proj",
    )(x, gkv, gq, wkv_bf, wq_bf, wz_bf, wg_bf)


def _compress_body(pt_ref, *refs, n_pages):
    del pt_ref
    page_refs = refs[:n_pages]
    w1_ref, pe_ref, w2_ref, out_ref, a_ref = refs[n_pages:]
    width, page_rows = page_refs[0].shape[1:]
    n_col = width // LANES
    cpp = page_rows // CMP_STRIDE
    n_chunk = n_pages * cpp
    hid = w2_ref.shape[2]
    low = _lane((cpp, LANES)) < HEAD_DIM
    def assemble(c):
        for p in range(n_pages):
            col_t = page_refs[p][0, c * LANES:(c + 1) * LANES, :]
            by_s = pltpu.einshape("(ns)f->(sn)f", col_t.T, n=cpp)
            for s in range(0, CMP_STRIDE, 2):
                b0 = by_s[s * cpp:(s + 1) * cpp]
                b1 = by_s[(s + 1) * cpp:(s + 2) * cpp]
                dst = (slice(p * cpp, (p + 1) * cpp), slice((s // 2) * LANES, (s // 2 + 1) * LANES))
                a_ref[(2 * c,) + dst] = jnp.where(low, b0, _swap_halves(b1))
                a_ref[(2 * c + 1,) + dst] = jnp.where(low, _swap_halves(b0), b1)

    for kv in range(2):
        for c in range(kv * n_col // 2, (kv + 1) * n_col // 2):
            assemble(c)
        w1 = w1_ref[kv]
        pe = jnp.dot(pe_ref[kv].astype(BF16), w1, preferred_element_type=F32)
        pe_term = pe[0:1, :hid] + pe[1:2, hid:]
        a_kv = a_ref[kv * KV_HEADS:(kv + 1) * KV_HEADS].reshape(KV_HEADS * n_chunk, CMP_STRIDE * HEAD_DIM)
        parts = jnp.dot(a_kv.astype(BF16), w1, preferred_element_type=F32)
        for hp in range(KV_HEADS // 2):
            pair = jnp.zeros((n_chunk, LANES), F32)
            for par in range(2):
                part = parts[(2 * hp + par) * n_chunk:(2 * hp + par + 1) * n_chunk]
                pre = part[:, :hid] + pltpu.roll(part[:, hid:], n_chunk - 1, axis=0) + pe_term
                mid = pre * _sigmoid(pre)
                pair = pair + jnp.dot(mid.astype(BF16), w2_ref[kv, par], preferred_element_type=F32)
            col = kv * (KV_HEADS // 2) + hp
            out_ref[0, col * LANES:(col + 1) * LANES, :] = pair.T


def _compress(rows_t, page_table, w1cat_bf, pe8, w2_bf, n_pages, page_rows):
    width = rows_t.shape[1]
    n_chunk = n_pages * page_rows // CMP_STRIDE
    if page_table is None:
        n_seq = rows_t.shape[0]
        page_table = jnp.zeros((1,), jnp.int32)
        page_spec = lambda j: pl.BlockSpec((1, width, page_rows), lambda i, pt: (i, 0, j))
    else:
        n_seq = page_table.shape[0] // n_pages
        page_spec = lambda j: pl.BlockSpec((1, width, page_rows), lambda i, pt: (pt[i * n_pages + j], 0, 0))
    full = lambda a: pl.BlockSpec(a.shape, lambda i, pt: (0,) * a.ndim)
    grid_spec = pltpu.PrefetchScalarGridSpec(
        num_scalar_prefetch=1,
        grid=(n_seq,),
        in_specs=[page_spec(j) for j in range(n_pages)] + [full(w1cat_bf), full(pe8), full(w2_bf)],
        out_specs=pl.BlockSpec((1, width, n_chunk), lambda i, pt: (i, 0, 0)),
        scratch_shapes=[pltpu.VMEM((2 * KV_HEADS, n_chunk, CMP_STRIDE * HEAD_DIM), F32)],
    )
    return pl.pallas_call(
        functools.partial(_compress_body, n_pages=n_pages),
        grid_spec=grid_spec,
        out_shape=jax.ShapeDtypeStruct((n_seq, width, n_chunk), F32),
        compiler_params=_cparams("parallel"),
        name="compress",
    )(page_table, *([rows_t] * n_pages), w1cat_bf, pe8, w2_bf)


def _bias_tile_body(rb_ref, out_ref, *, row_step, lane_step, offset, hi):
    h = pl.program_id(0)
    shape = out_ref.shape[1:]
    d = row_step * lax.broadcasted_iota(jnp.int32, shape, 0) + lane_step * _lane(shape) + offset
    n = jnp.maximum(d, 0)
    nf = jnp.maximum(n, 1).astype(F32)
    large = MAX_EXACT + (jnp.log(nf / MAX_EXACT) / math.log(MAX_DISTANCE / MAX_EXACT)
                         * (NUM_BUCKETS - MAX_EXACT)).astype(jnp.int32)
    large = jnp.minimum(large, NUM_BUCKETS - 1)
    bucket = jnp.where(n < MAX_EXACT, n, large)
    far = rb_ref[NUM_BUCKETS - 1, h]
    val = jnp.zeros(shape, F32)
    for k in range(NUM_BUCKETS - 1):
        val = jnp.where(bucket == k, rb_ref[k, h] - far, val)
    out_ref[0] = jnp.where((d >= 0) & (d <= hi), val, NEG)


def _bias_tile(rel_bias, rows, width, lane_step, offset, hi=1 << 30, row_step=1):
    return pl.pallas_call(
        functools.partial(_bias_tile_body, row_step=row_step, lane_step=lane_step, offset=offset, hi=hi),
        grid=(N_HEADS,),
        in_specs=[pl.BlockSpec(memory_space=pltpu.SMEM)],
        out_specs=pl.BlockSpec((1, rows, width), lambda h: (h, 0, 0)),
        out_shape=jax.ShapeDtypeStruct((N_HEADS, rows, width), F32),
        compiler_params=_cparams("parallel"),
        name="bias_tile",
    )(rel_bias)


_NT = (((1,), (1,)), ((), ()))


def _softmax_rows(s):
    m = jnp.maximum(jnp.max(s, axis=-1, keepdims=True), M_FLOOR)
    p = jnp.exp(s - m)
    return p, jnp.sum(p, axis=-1, keepdims=True)


def _gate_col(gate, col):
    return jnp.sum(jnp.where(_lane(gate.shape) == col, gate, 0.0), axis=-1, keepdims=True)


def _pair_columns(o_even, o_odd, valid_half):
    lo = o_even if valid_half == 0 else _swap_halves(o_even)
    hi = o_odd if valid_half == 1 else _swap_halves(o_odd)
    return jnp.where(_lane(lo.shape) < HEAD_DIM, lo, hi)


def _select_blocks_t(imp_t, tb, n_sel):
    nb, nt = imp_t.shape
    sub = 8
    j = lax.broadcasted_iota(jnp.int32, (nb, nt), 0)
    valid = j <= tb
    forced = (j == 0) | (j == tb) | (j == tb - 1)
    score = jnp.where(valid, jnp.where(forced, BIG, imp_t), -BIG)
    groups = [score[lo:lo + sub] for lo in range(0, nb, sub)]
    jr = lax.broadcasted_iota(jnp.int32, (sub, nt), 0)
    counts = [jnp.zeros((sub, nt), F32) for _ in groups]
    for i in range(nb):
        row = score[i:i + 1]
        for r, grp in enumerate(groups):
            lo = r * sub
            if lo > i:
                one = jnp.where(row >= grp, 1.0, 0.0)
            elif lo + sub - 1 <= i:
                one = jnp.where(row > grp, 1.0, 0.0)
            else:
                one = jnp.where(jr + lo > i, jnp.where(row >= grp, 1.0, 0.0), jnp.where(row > grp, 1.0, 0.0))
            counts[r] = counts[r] + one
    rank = jnp.concatenate(counts, axis=0)
    return jnp.where((rank < n_sel) & valid, 0.0, NEG)


def _cmp_prompt_body(qa_ref, kc_ref, vc_ref, m_ref, cb_ref, gate_ref, oc_ref, qsel_ref, qflag_ref, *, n_key):
    qt = pl.program_id(1)
    start = pl.multiple_of(8 * qt + 8, 8)
    kwin = kc_ref[0, pl.ds(start, n_key), :]
    vwin = vc_ref[0, pl.ds(start, n_key), :]
    mwin_t = m_ref[pl.ds(start, n_key), :].T
    feat = lax.broadcasted_iota(jnp.int32, (LANES, Q_BLK), 0)
    pos = qt * Q_BLK + lax.broadcasted_iota(jnp.int32, (1, Q_BLK), 1)
    tb = pos // L_SLC
    gate = gate_ref[...]
    n_pair = GROUP // 2
    for kvh in range(KV_HEADS):
        ka = kwin[:, kvh * LANES:(kvh + 1) * LANES].astype(BF16)
        vpt = vwin[:, (kvh // 2) * LANES:(kvh // 2 + 1) * LANES].T
        vpt = vpt[(kvh % 2) * HEAD_DIM:(kvh % 2 + 1) * HEAD_DIM].astype(BF16)
        qts = [qa_ref[0, :, h * LANES:(h + 1) * LANES].astype(F32).T for h in range(kvh * GROUP, (kvh + 1) * GROUP)]
        pc_sum = jnp.zeros((n_key, Q_BLK), F32)
        for gp in range(n_pair):
            qa = jnp.concatenate([jnp.where(feat == HEAD_DIM, NEG, qts[2 * gp + e]) for e in range(2)], axis=1)
            qa = qa.astype(BF16)
            qflag_ref[0, kvh, :, 2 * gp * Q_BLK:(2 * gp + 2) * Q_BLK] = qa
            s = jnp.dot(ka, qa, preferred_element_type=F32)
            s = s + cb_ref[kvh, :, 2 * gp * Q_BLK:(2 * gp + 2) * Q_BLK]
            m = jnp.maximum(jnp.max(s, axis=0, keepdims=True), M_FLOOR)
            p = jnp.exp(s - m)
            inv = 1.0 / jnp.maximum(jnp.sum(p, axis=0, keepdims=True), 1e-30)
            pc = p * inv
            pc_sum = pc_sum + pc[:, :Q_BLK] + pc[:, Q_BLK:]
            o = jnp.dot(vpt, p.astype(BF16), preferred_element_type=F32) * inv
            _store_chain(oc_ref, o, gate, kvh * GROUP + 2 * gp, kvh * n_pair + gp)
        imp_t = jnp.dot(mwin_t, pc_sum, preferred_element_type=F32, precision=lax.Precision.HIGHEST)
        selneg_t = _select_blocks_t(imp_t[HEAD_DIM:], tb, N_SEL)
        for g in range(GROUP):
            qsel_ref[0, kvh, :, g * Q_BLK:(g + 1) * Q_BLK] = jnp.concatenate(
                [qts[g][:HEAD_DIM], selneg_t], axis=0).astype(BF16)


def _cmp_prompt(qa, kc_pad, vc_pad, m_pad, cb, gate, batch, seq):
    nq = seq // Q_BLK
    n_key = kc_pad.shape[1] // 2
    d_out = N_HEADS * HEAD_DIM
    return pl.pallas_call(
        functools.partial(_cmp_prompt_body, n_key=n_key),
        grid=(batch, nq),
        in_specs=[pl.BlockSpec((1, Q_BLK, N_HEADS * LANES), lambda b, t: (b, t, 0)),
                  pl.BlockSpec((1,) + kc_pad.shape[1:], lambda b, t: (b, 0, 0)),
                  pl.BlockSpec((1,) + vc_pad.shape[1:], lambda b, t: (b, 0, 0)),
                  pl.BlockSpec(m_pad.shape, lambda b, t: (0, 0)),
                  pl.BlockSpec(cb.shape, lambda b, t: (0, 0, 0)),
                  pl.BlockSpec((Q_BLK, LANES), lambda b, t: (b * nq + t, 0))],
        out_specs=[pl.BlockSpec((1, Q_BLK, d_out), lambda b, t: (b, t, 0))]
        + [pl.BlockSpec((1, KV_HEADS, LANES, GROUP * Q_BLK), lambda b, t: (b * nq + t, 0, 0, 0))] * 2,
        out_shape=[jax.ShapeDtypeStruct((batch, seq, d_out), F32)]
        + [jax.ShapeDtypeStruct((batch * nq, KV_HEADS, LANES, GROUP * Q_BLK), BF16)] * 2,
        compiler_params=_cparams("parallel", "parallel"),
        name="cmp_prompt",
    )(qa.reshape(batch, seq, -1), kc_pad, vc_pad, m_pad, cb, gate)


KEY_TILE = 256
COL_CHAIN = 256
WIN_TILES = 2
V_PAD = 16
V_ROWS = HEAD_DIM + V_PAD


def _with_ones_rows(v_t):
    n = v_t.shape[1]
    extra = jnp.zeros((KV_HEADS, V_PAD, n), v_t.dtype).at[:, 0].set(1)
    return jnp.concatenate([v_t.reshape(KV_HEADS, HEAD_DIM, n), extra], axis=1).reshape(KV_HEADS * V_ROWS, n)


def _store_chain(out_ref, o_t, gate, gate_col0, lane_col, row0=0):
    pair = jnp.concatenate([o_t[:, :Q_BLK], o_t[:, Q_BLK:]], axis=0).T
    g = jnp.where(_lane(pair.shape) < HEAD_DIM, _gate_col(gate, gate_col0), _gate_col(gate, gate_col0 + 1))
    out_ref[0, row0:row0 + Q_BLK, lane_col * LANES:(lane_col + 1) * LANES] = pair * g


_PAIR_CHAINS = [(e, c) for e in range(2) for c in range(GROUP * Q_BLK // COL_CHAIN)]


def _slc_prompt_body(q_ref, k_ref, v_ref, sb_ref, gate_ref, out_ref, m_ref, acc_ref, p_ref, s_ref):
    pr = pl.program_id(1)
    qt = pl.program_id(2)
    m_ref[...] = jnp.full(m_ref.shape, M_FLOOR, F32)
    acc_ref[...] = jnp.zeros(acc_ref.shape, F32)
    p_ref[...] = jnp.zeros(p_ref.shape, BF16)
    last = (qt * Q_BLK) // KEY_TILE
    cs = lambda c: slice(c * COL_CHAIN, (c + 1) * COL_CHAIN)

    def pv_of(kt):
        base = pl.multiple_of(kt * KEY_TILE, KEY_TILE)
        return [jnp.dot(v_ref[e * V_ROWS:(e + 1) * V_ROWS, pl.ds(base, KEY_TILE)], p_ref[i],
                        preferred_element_type=F32) for i, (e, c) in enumerate(_PAIR_CHAINS)]

    def scores_of(kt):
        base = pl.multiple_of(kt * KEY_TILE, KEY_TILE)
        ks = [k_ref[0, pl.ds(base, KEY_TILE), e * LANES:(e + 1) * LANES] for e in range(2)]
        return [jnp.dot(ks[e], q_ref[0, e, :, cs(c)], preferred_element_type=F32) for e, c in _PAIR_CHAINS]

    def make_step(with_bias):
        def step(kt, carry):
            s_next = scores_of(jnp.minimum(kt + 1, last))
            pvs = pv_of(jnp.maximum(kt - 1, 0))
            ss = [s_ref[i] for i in range(len(_PAIR_CHAINS))]
            if with_bias:
                case = qt % 2 + 2 * (last - kt)
                ss = [s + sb_ref[case, e, :, cs(c)] for s, (e, c) in zip(ss, _PAIR_CHAINS)]
            alphas = []
            for i, s in enumerate(ss):
                m_old = m_ref[i]
                m_new = jnp.maximum(m_old, jnp.max(s, axis=0, keepdims=True))
                alphas.append(jnp.exp(m_old - m_new))
                m_ref[i] = m_new
                p_ref[i] = jnp.exp(s - m_new).astype(BF16)
            for i in range(len(_PAIR_CHAINS)):
                acc_ref[i] = alphas[i] * (acc_ref[i] + pvs[i])
                s_ref[i] = s_next[i]
            return carry
        return step

    for i, s in enumerate(scores_of(0)):
        s_ref[i] = s
    n_far = jnp.maximum(last - 1, 0)
    lax.fori_loop(0, n_far, make_step(False), 0)
    lax.fori_loop(n_far, last + 1, make_step(True), 0)
    gate = gate_ref[...]
    pvs = pv_of(last)
    for i, (e, c) in enumerate(_PAIR_CHAINS):
        o = acc_ref[i] + pvs[i]
        o = o[:HEAD_DIM] * (1.0 / jnp.maximum(o[HEAD_DIM:HEAD_DIM + 1], 1e-30))
        _store_chain(out_ref, o, gate, N_HEADS + (2 * pr + e) * GROUP + 2 * c, i)


def _slc_prompt(qsel_t, ska, sv_t, sb_t, gate, batch, seq):
    nq = seq // Q_BLK
    cols = GROUP * Q_BLK
    d_out = N_HEADS * HEAD_DIM
    n_chain = len(_PAIR_CHAINS)
    return pl.pallas_call(
        _slc_prompt_body,
        grid=(batch, KV_HEADS // 2, nq),
        in_specs=[pl.BlockSpec((1, 2, LANES, cols), lambda b, p, t: (b * nq + t, p, 0, 0)),
                  pl.BlockSpec((1, seq, 2 * LANES), lambda b, p, t: (b, 0, p)),
                  pl.BlockSpec((2 * V_ROWS, seq), lambda b, p, t: (p, b)),
                  pl.BlockSpec((4, 2, KEY_TILE, cols), lambda b, p, t: (0, p, 0, 0)),
                  pl.BlockSpec((Q_BLK, LANES), lambda b, p, t: (b * nq + t, 0))],
        out_specs=pl.BlockSpec((1, Q_BLK, 2 * GROUP * HEAD_DIM), lambda b, p, t: (b, t, p)),
        out_shape=jax.ShapeDtypeStruct((batch, seq, d_out), F32),
        scratch_shapes=[pltpu.VMEM((n_chain, 1, COL_CHAIN), F32),
                        pltpu.VMEM((n_chain, V_ROWS, COL_CHAIN), F32),
                        pltpu.VMEM((n_chain, KEY_TILE, COL_CHAIN), BF16),
                        pltpu.VMEM((n_chain, KEY_TILE, COL_CHAIN), F32)],
        compiler_params=_cparams("parallel", "parallel", "arbitrary"),
        name="slc_prompt",
    )(qsel_t, ska.reshape(batch, seq, -1), sv_t, sb_t, gate)


def _win_prompt_body(q_ref, k_ref, v_ref, wb_ref, gate_ref, out_ref, *, n_key):
    pr = pl.program_id(1)
    t0 = pl.program_id(2) * WIN_TILES
    cs = lambda c: slice(c * COL_CHAIN, (c + 1) * COL_CHAIN)
    chains = [(u, e, c) for u in range(WIN_TILES) for e, c in _PAIR_CHAINS]
    bases = [pl.multiple_of((t0 + u) * Q_BLK, Q_BLK) for u in range(WIN_TILES)]
    ss = []
    for u in range(WIN_TILES):
        ks = [k_ref[0, pl.ds(bases[u], n_key), e * LANES:(e + 1) * LANES] for e in range(2)]
        ss += [jnp.dot(ks[e], q_ref[u, e, :, cs(c)], preferred_element_type=F32) + wb_ref[e, :, cs(c)]
               for e, c in _PAIR_CHAINS]
    ps = []
    for s in ss:
        m = jnp.maximum(jnp.max(s, axis=0, keepdims=True), M_FLOOR)
        ps.append(jnp.exp(s - m).astype(BF16))
    for i, (u, e, c) in enumerate(chains):
        o = jnp.dot(v_ref[e * V_ROWS:(e + 1) * V_ROWS, pl.ds(bases[u], n_key)], ps[i], preferred_element_type=F32)
        o = o[:HEAD_DIM] * (1.0 / jnp.maximum(o[HEAD_DIM:HEAD_DIM + 1], 1e-30))
        gate = gate_ref[u * Q_BLK:(u + 1) * Q_BLK, :]
        _store_chain(out_ref, o, gate, 2 * N_HEADS + (2 * pr + e) * GROUP + 2 * c, i % len(_PAIR_CHAINS), u * Q_BLK)


def _win_prompt(q_t, wka_pad, wv_t_pad, wb_t, gate, batch, seq):
    nq = seq // Q_BLK
    n_key = WINDOW + Q_BLK
    cols = GROUP * Q_BLK
    d_out = N_HEADS * HEAD_DIM
    padded = wka_pad.shape[1]
    return pl.pallas_call(
        functools.partial(_win_prompt_body, n_key=n_key),
        grid=(batch, KV_HEADS // 2, nq // WIN_TILES),
        in_specs=[pl.BlockSpec((WIN_TILES, 2, LANES, cols), lambda b, p, t: (b * (nq // WIN_TILES) + t, p, 0, 0)),
                  pl.BlockSpec((1, padded, 2 * LANES), lambda b, p, t: (b, 0, p)),
                  pl.BlockSpec((2 * V_ROWS, padded), lambda b, p, t: (p, b)),
                  pl.BlockSpec((2, n_key, cols), lambda b, p, t: (p, 0, 0)),
                  pl.BlockSpec((WIN_TILES * Q_BLK, LANES), lambda b, p, t: (b * (nq // WIN_TILES) + t, 0))],
        out_specs=pl.BlockSpec((1, WIN_TILES * Q_BLK, 2 * GROUP * HEAD_DIM), lambda b, p, t: (b, t, p)),
        out_shape=jax.ShapeDtypeStruct((batch, seq, d_out), F32),
        compiler_params=_cparams("parallel", "parallel", "parallel"),
        name="win_prompt",
    )(q_t, wka_pad, wv_t_pad, wb_t, gate)


def _nsa_out_body(oc_ref, os_ref, ow_ref, sz_ref, x_ref, wo_ref, gp_ref, out_ref):
    d = oc_ref.shape[-1]
    y = oc_ref[...] * sz_ref[:, :d] + os_ref[...] * sz_ref[:, d:2 * d] + ow_ref[...] * sz_ref[:, 2 * d:]
    o = jnp.dot(y.astype(BF16), wo_ref[...], preferred_element_type=F32)
    out_ref[...] = x_ref[...] + _rms(o, gp_ref[...])


def _nsa_out(oc, os_, ow, sz, x, wo_bf, gp, tm):
    rows, d = x.shape
    dq = oc.shape[-1]
    row = lambda n: pl.BlockSpec((tm, n), lambda i: (i, 0))
    return pl.pallas_call(
        _nsa_out_body,
        grid=(rows // tm,),
        in_specs=[row(dq), row(dq), row(dq), row(N_BRANCH * dq), row(d),
                  pl.BlockSpec((dq, d), lambda i: (0, 0)), pl.BlockSpec((1, d), lambda i: (0, 0))],
        out_specs=row(d),
        out_shape=jax.ShapeDtypeStruct((rows, d), F32),
        compiler_params=_cparams("parallel"),
        name="nsa_out",
    )(oc, os_, ow, sz, x, wo_bf, gp.reshape(1, d))


def _decode_attn_body(pt_ref, *refs, n_pages, tq, w_keep):
    del pt_ref
    page_refs = refs[:n_pages]
    (qa_ref, kvc_ref, snew_ref, wst_ref, wnew_ref, mt_ref, e_ref, cbs_ref, sbs_ref, wbs_ref, gate_ref,
     oc_ref, os_ref, ow_ref, kt_ref, vt_ref, kwt_ref, vwt_ref) = refs[n_pages:]
    n_past = n_pages * PAGE_SIZE
    kw = KV_HEADS * HEAD_DIM
    rows = N_HEADS * tq

    def new_tile(ref):
        return jnp.concatenate([ref[...], jnp.zeros((LANES - tq, 2 * kw), F32)], axis=0).T

    def fill(k_dst, v_dst, col0, tile):
        k_dst[:, col0:col0 + tile.shape[1]] = tile[:kw].astype(BF16)
        v_dst[:, col0:col0 + tile.shape[1]] = tile[kw:].astype(BF16)

    for p in range(n_pages):
        fill(kt_ref, vt_ref, p * PAGE_SIZE, page_refs[p][0])
    fill(kt_ref, vt_ref, n_past, new_tile(snew_ref))
    fill(kwt_ref, vwt_ref, 0, wst_ref[0])
    fill(kwt_ref, vwt_ref, w_keep, new_tile(wnew_ref))

    zero = jnp.zeros((tq, LANES), F32)
    tiles = [[], []]
    for h in range(N_HEADS):
        kvh = h // GROUP
        piece = qa_ref[:, h * LANES:(h + 1) * LANES]
        piece = piece if kvh % 2 == 0 else _swap_halves(piece)
        for ct in range(2):
            tiles[ct].append(piece if kvh // 2 == ct else zero)
    qbd = jnp.concatenate([jnp.concatenate(t, axis=0) for t in tiles], axis=1).astype(BF16)
    gate = gate_ref[...]

    def branch(k_t, v_t, bias, extra=None):
        s = jnp.dot(qbd, k_t, preferred_element_type=F32) + bias
        if extra is not None:
            s = s + extra
        p, l = _softmax_rows(s)
        inv = 1.0 / jnp.maximum(l, 1e-30)
        o_t = lax.dot_general(v_t, p.astype(BF16), _NT, preferred_element_type=F32)
        return p, inv, o_t.T

    def store(out_ref, o, scale):
        o = o * scale
        for kvh in range(KV_HEADS):
            tile = o[:, (kvh // 2) * LANES:(kvh // 2 + 1) * LANES]
            for gp in range(GROUP // 2):
                h0 = kvh * GROUP + 2 * gp
                col = kvh * (GROUP // 2) + gp
                out_ref[:, col * LANES:(col + 1) * LANES] = _pair_columns(
                    tile[h0 * tq:(h0 + 1) * tq], tile[(h0 + 1) * tq:(h0 + 2) * tq], kvh % 2)

    p, inv, o = branch(kvc_ref[0, :kw, :].astype(BF16), kvc_ref[0, kw:, :].astype(BF16), cbs_ref[...])
    store(oc_ref, o, inv * _gate_col(gate, 0))
    pc = p * inv
    pc_sum = jnp.concatenate(
        [sum(pc[(kvh * GROUP + g) * tq:(kvh * GROUP + g + 1) * tq] for g in range(GROUP)) for kvh in range(KV_HEADS)],
        axis=0)
    imp_t = lax.dot_general(mt_ref[...], pc_sum, _NT, preferred_element_type=F32, precision=lax.Precision.HIGHEST)
    n_col = KV_HEADS * tq
    nb_pad = -(-(-(-(n_past + tq) // L_SLC)) // 8) * 8
    tb = (n_past + lax.broadcasted_iota(jnp.int32, (1, n_col), 1) % tq) // L_SLC
    selneg_t = _select_blocks_t(imp_t[HEAD_DIM:HEAD_DIM + nb_pad], tb, N_SEL)
    sel_t = jnp.concatenate([jnp.zeros((HEAD_DIM, n_col), F32), selneg_t,
                             jnp.zeros((LANES - HEAD_DIM - nb_pad, n_col), F32)], axis=0)
    sel = jnp.concatenate([sel_t, jnp.zeros((LANES, LANES - n_col), F32)], axis=1).T
    sel = jnp.concatenate([sel[kvh * tq:(kvh + 1) * tq] for kvh in range(KV_HEADS) for _ in range(GROUP)], axis=0)
    block_mask = jnp.dot(sel.astype(BF16), e_ref[...], preferred_element_type=F32)
    _, inv, o = branch(kt_ref[...], vt_ref[...], sbs_ref[...], block_mask)
    store(os_ref, o, inv * _gate_col(gate, 1))
    _, inv, o = branch(kwt_ref[...], vwt_ref[...], wbs_ref[...])
    store(ow_ref, o, inv * _gate_col(gate, 2))


def _decode_attn(qa, kvc_t, pages_t, page_table, slc_new, win_state_t, win_new, m_t, e_s, cbs, sbs, wbs, gate_r,
                 n_seq, tq, n_pages):
    width = pages_t.shape[1]
    n_sk = sbs.shape[-1]
    n_wk = wbs.shape[-1]
    w_keep = win_state_t.shape[2]
    d_out = N_HEADS * HEAD_DIM
    rows = N_HEADS * tq
    page_spec = lambda j: pl.BlockSpec((1, width, PAGE_SIZE), lambda i, pt: (pt[i * n_pages + j], 0, 0))
    full = lambda a: pl.BlockSpec(a.shape, lambda i, pt: (0,) * a.ndim)
    row = lambda n: pl.BlockSpec((tq, n), lambda i, pt: (i, 0))
    grid_spec = pltpu.PrefetchScalarGridSpec(
        num_scalar_prefetch=1,
        grid=(n_seq,),
        in_specs=[page_spec(j) for j in range(n_pages)] + [
            row(N_HEADS * LANES),
            pl.BlockSpec((1,) + kvc_t.shape[1:], lambda i, pt: (i, 0, 0)),
            row(width),
            pl.BlockSpec((1, width, w_keep), lambda i, pt: (i, 0, 0)),
            row(width),
            full(m_t), full(e_s), full(cbs), full(sbs), full(wbs),
            pl.BlockSpec((rows, LANES), lambda i, pt: (i, 0))],
        out_specs=[row(d_out)] * 3,
        scratch_shapes=[pltpu.VMEM((width // 2, n_sk), BF16), pltpu.VMEM((width // 2, n_sk), BF16),
                        pltpu.VMEM((width // 2, n_wk), BF16), pltpu.VMEM((width // 2, n_wk), BF16)],
    )
    return pl.pallas_call(
        functools.partial(_decode_attn_body, n_pages=n_pages, tq=tq, w_keep=w_keep),
        grid_spec=grid_spec,
        out_shape=[jax.ShapeDtypeStruct((n_seq * tq, d_out), F32)] * 3,
        compiler_params=_cparams("parallel"),
        name="decode_attn",
    )(page_table, *([pages_t] * n_pages), qa, kvc_t, slc_new, win_state_t, win_new, m_t, e_s, cbs, sbs, wbs, gate_r)


def _overlap_matrix(n_rows, row0, n_cmp, n_blk):
    import numpy as np
    m = np.zeros((n_rows, LANES), np.float32)
    cs = np.arange(n_cmp)[:, None] * CMP_STRIDE
    js = np.arange(n_blk)[None, :] * L_SLC
    m[row0:row0 + n_cmp, HEAD_DIM:HEAD_DIM + n_blk] = (cs <= js + L_SLC - 1) & (cs + L_CMP - 1 >= js)
    return jnp.asarray(m)


def kernel(x_prompt, x_sample, state_conv, cache_cmp, cache_slc, state_win, page_table, rel_bias, a_norm_pre, a_w_in, a_conv_w, a_conv_b, a_ln_g, a_ln_b, a_w_out, a_norm_post, kv_norm, w_kv, cmp_pe, cmp_w1, cmp_w2, b_norm_pre, b_w_in, b_w_out, b_norm_post):
    import numpy as np
    bp, tp, d = x_prompt.shape
    bd, tq, _ = x_sample.shape
    n_pages = page_table.shape[1]
    past = n_pages * PAGE_SIZE
    w_keep = state_win.shape[1]
    width = 2 * KV_HEADS * HEAD_DIM
    kw = KV_HEADS * HEAD_DIM
    dq = N_HEADS * HEAD_DIM
    assert b_w_in.shape[0] == 1 and tp % (2 * KEY_TILE) == 0 and tp // L_SLC <= HEAD_DIM
    tm = 256
    tm_s = min(tm, bd * tq)

    xp = x_prompt
    xs = x_sample.reshape(bd * tq, d)
    conv_p, conv_s = [], []
    for l in range(a_w_in.shape[0]):
        w_in = a_w_in[l].astype(BF16)
        w_out = a_w_out[l].astype(BF16)
        di = a_w_out.shape[1]
        tail = (a_conv_w[l], a_conv_b[l], a_ln_g[l], a_ln_b[l], w_out, a_norm_post[l])
        glu, sz = _glu_proj(xp.reshape(bp * tp, d), a_norm_pre[l], w_in, tm)
        glu = glu.reshape(bp, tp, di)
        xp = _conv_prompt(glu, sz.reshape(bp, tp, di), xp, *tail, tm)
        conv_p.append(glu[:, -(CONV_W - 1):])
        glu, sz = _glu_proj(xs, a_norm_pre[l], w_in, tm_s)
        xs = _conv_sample(glu, state_conv[l], sz, xs, *tail, 16)
        conv_s.append(jnp.concatenate([state_conv[l], glu.reshape(bd, tq, di)], axis=1)[:, -(CONV_W - 1):])

    bw = b_w_in[0]
    n_gate = N_BRANCH * N_HEADS
    wq = bw[:, :dq].astype(BF16)
    wz = bw[:, dq:dq * (1 + N_BRANCH)].astype(BF16)
    wg = jnp.pad(bw[:, dq * (1 + N_BRANCH):], ((0, 0), (0, LANES - n_gate))).astype(BF16)
    wkv = w_kv.astype(BF16)
    xp2 = xp.reshape(bp * tp, d)
    (cmp_p, slc_p, win_p, qa_p, sz_p, gate_p, ska, sv, wka, wv) = _nsa_proj(
        xp2, kv_norm, b_norm_pre[0], wkv, wq, wz, wg, tm, seq_len=tp)
    cmp_s, slc_s, win_s, qa_s, sz_s, gate_s = _nsa_proj(xs, kv_norm, b_norm_pre[0], wkv, wq, wz, wg, tm_s)
    qa_s = qa_s.astype(F32)

    w1cat = jnp.concatenate([cmp_w1[:, :CMP_STRIDE * HEAD_DIM], cmp_w1[:, CMP_STRIDE * HEAD_DIM:]], axis=2).astype(BF16)
    pe8 = jnp.pad(cmp_pe.reshape(2, L_CMP // CMP_STRIDE, CMP_STRIDE * HEAD_DIM), ((0, 0), (0, 6), (0, 0)))
    zero = jnp.zeros_like(cmp_w2)
    w2h = jnp.stack([jnp.concatenate([cmp_w2, zero], axis=2), jnp.concatenate([zero, cmp_w2], axis=2)], axis=1).astype(BF16)
    prompt_page = min(1024, tp)
    pp = tp // prompt_page
    kvc_p = _compress(cmp_p, None, w1cat, pe8, w2h, pp, prompt_page).transpose(0, 2, 1)
    pt_flat = page_table.reshape(-1).astype(jnp.int32)
    cmp_pages_t = cache_cmp.transpose(0, 2, 3, 4, 1).reshape(-1, width, PAGE_SIZE)
    kvc_s = _compress(cmp_pages_t, pt_flat, w1cat, pe8, w2h, n_pages, PAGE_SIZE)

    n_chunk_p = kvc_p.shape[1]
    nc_p = n_chunk_p - L_CMP // CMP_STRIDE + 1
    real = (jnp.arange(n_chunk_p) < nc_p)[None, :, None]
    flag = jnp.zeros((LANES - HEAD_DIM,), F32).at[0].set(1.0)
    kc4 = jnp.where(real, kvc_p[:, :, :kw], 0.0).reshape(bp, n_chunk_p, KV_HEADS, HEAD_DIM)
    aug = jnp.where(real[..., None], 0.0, flag) * jnp.ones((bp, n_chunk_p, KV_HEADS, 1), F32)
    kc_real = jnp.concatenate([kc4, aug], axis=-1).reshape(bp, n_chunk_p, KV_HEADS * LANES)
    pad_row = jnp.concatenate([jnp.zeros((HEAD_DIM,), F32), flag])
    kc_front = jnp.broadcast_to(jnp.tile(pad_row, KV_HEADS), (bp, n_chunk_p, KV_HEADS * LANES))
    kc_pad = jnp.concatenate([kc_front, kc_real], axis=1)
    vc_pad = jnp.concatenate([jnp.zeros((bp, n_chunk_p, kw), F32), jnp.where(real, kvc_p[:, :, kw:], 0.0)], axis=1)
    m_pad = _overlap_matrix(2 * n_chunk_p, n_chunk_p, nc_p, tp // L_SLC)
    cb = _bias_tile(rel_bias, n_chunk_p, Q_BLK, 1, CMP_STRIDE * (n_chunk_p - Q_BLK // CMP_STRIDE) - (L_CMP - 1),
                    row_step=-CMP_STRIDE)
    cb = cb.reshape(KV_HEADS, GROUP, n_chunk_p, Q_BLK).transpose(0, 2, 1, 3).reshape(KV_HEADS, n_chunk_p, GROUP * Q_BLK)
    oc_p, qsel, qflag = _cmp_prompt(qa_p, kc_pad, vc_pad, m_pad, cb, gate_p, bp, tp)
    sb = jnp.stack([_bias_tile(rel_bias, KEY_TILE, Q_BLK, 1, off, row_step=-1)
                    for off in (0, Q_BLK, 2 * Q_BLK, 3 * Q_BLK)])
    sb = sb.reshape(4, KV_HEADS, GROUP, KEY_TILE, Q_BLK).transpose(0, 1, 3, 2, 4).reshape(4, KV_HEADS, KEY_TILE, GROUP * Q_BLK)
    os_p = _slc_prompt(qsel, ska, _with_ones_rows(sv), sb, gate_p, bp, tp)
    n_wkey = WINDOW + Q_BLK
    wb = _bias_tile(rel_bias, n_wkey, Q_BLK, 1, WINDOW, WINDOW, row_step=-1)
    wb = wb.reshape(KV_HEADS, GROUP, n_wkey, Q_BLK).transpose(0, 2, 1, 3).reshape(KV_HEADS, n_wkey, GROUP * Q_BLK)
    wka_front = jnp.broadcast_to(jnp.tile(pad_row, KV_HEADS).astype(BF16), (bp, WINDOW, KV_HEADS * LANES))
    wka_pad = jnp.concatenate([wka_front, wka.reshape(bp, tp, -1)], axis=1)
    wv_pad = jnp.concatenate([jnp.zeros((kw, bp, WINDOW), BF16), wv.reshape(kw, bp, tp)], axis=2)
    ow_p = _win_prompt(qflag, wka_pad, _with_ones_rows(wv_pad.reshape(kw, bp * (WINDOW + tp))), wb, gate_p, bp, tp)
    wo = b_w_out[0].astype(BF16)
    y_p = _nsa_out(oc_p.reshape(bp * tp, dq), os_p.reshape(bp * tp, dq), ow_p.reshape(bp * tp, dq), sz_p, xp2,
                   wo, b_norm_post[0], tm)

    n_chunk_s = kvc_s.shape[2]
    nc_s = n_chunk_s - L_CMP // CMP_STRIDE + 1
    nb_s = -(-(past + tq) // L_SLC)
    n_sk = past + LANES
    n_wk = w_keep + LANES
    m_t = _overlap_matrix(n_chunk_s, 0, nc_s, nb_s).T
    e_np = np.zeros((LANES, n_sk), np.float32)
    e_np[HEAD_DIM + np.arange(n_sk) // L_SLC, np.arange(n_sk)] = 1.0
    e_s = jnp.asarray(e_np, BF16)
    rows_s = N_HEADS * tq
    cbs = _bias_tile(rel_bias, tq, n_chunk_s, -CMP_STRIDE, past - (L_CMP - 1)).reshape(rows_s, n_chunk_s)
    sbs = _bias_tile(rel_bias, tq, n_sk, -1, past).reshape(rows_s, n_sk)
    wbs = _bias_tile(rel_bias, tq, n_wk, -1, w_keep, WINDOW).reshape(rows_s, n_wk)
    gate_r = gate_s[:, :n_gate].reshape(bd, tq, N_BRANCH, N_HEADS).transpose(0, 3, 1, 2).reshape(bd * rows_s, N_BRANCH)
    gate_r = jnp.pad(gate_r, ((0, 0), (0, LANES - N_BRANCH)))
    to_t = lambda a: a.transpose(0, 2, 3, 4, 1).reshape(a.shape[0], width, a.shape[1])
    oc_s, os_s, ow_s = _decode_attn(qa_s, kvc_s, to_t(cache_slc), pt_flat, slc_s, to_t(state_win), win_s,
                                    m_t, e_s, cbs, sbs, wbs, gate_r, bd, tq, n_pages)
    y_s = _nsa_out(oc_s, os_s, ow_s, sz_s, xs, wo, b_norm_post[0], tm_s)

    kv5 = lambda a, b, t: a.reshape(b, t, 2, KV_HEADS, HEAD_DIM)
    kv5_t = lambda a: a.reshape(bp, 2, KV_HEADS, HEAD_DIM, a.shape[2]).transpose(0, 4, 1, 2, 3)
    win_all = jnp.concatenate([state_win, kv5(win_s, bd, tq)], axis=1)
    return (y_p.reshape(bp, tp, d), y_s.reshape(bd, tq, d), jnp.stack(conv_p), jnp.stack(conv_s),
            kv5_t(cmp_p), kv5(cmp_s, bd, tq), kv5_t(slc_p), kv5(slc_s, bd, tq),
            kv5_t(win_p[:, :, -min(WINDOW, tp):]), win_all[:, -min(WINDOW, win_all.shape[1]):])
```

```python
import functools
import math

import jax
import jax.numpy as jnp
from jax import lax
from jax.experimental import pallas as pl
from jax.experimental.pallas import tpu as pltpu

F32 = jnp.float32
BF16 = jnp.bfloat16

EPS = 1e-6
NEG = -1e30
BIG = 1e9
M_FLOOR = -1e20

HEAD_DIM = 64
KV_HEADS = 4
N_HEADS = 16
GROUP = N_HEADS // KV_HEADS
N_BRANCH = 3
CONV_W = 31
L_CMP = 32
CMP_STRIDE = 16
L_SLC = 64
N_SEL = 16
WINDOW = 512
Q_BLK = 128
NUM_BUCKETS = 32
MAX_DISTANCE = 128
MAX_EXACT = NUM_BUCKETS // 2
PAGE_SIZE = 128
LANES = 128
HALO = 32
CONV_TAIL = 16

VMEM_LIMIT = 56 * 1024 * 1024


def _cparams(*sem):
    return pltpu.CompilerParams(dimension_semantics=sem, vmem_limit_bytes=VMEM_LIMIT)


def _sigmoid(x):
    return 1.0 / (1.0 + jnp.exp(-x))


def _rms(x, g):
    return x * lax.rsqrt(jnp.mean(x * x, axis=-1, keepdims=True) + EPS) * g


def _lane(shape):
    return lax.broadcasted_iota(jnp.int32, shape, len(shape) - 1)


def _swap_halves(x):
    return pltpu.roll(x, HEAD_DIM, axis=x.ndim - 1)


def _glu_proj_body(x_ref, g_ref, w_ref, glu_ref, sz_ref):
    di = glu_ref.shape[-1]
    h = _rms(x_ref[...], g_ref[...])
    u = jnp.dot(h.astype(BF16), w_ref[...], preferred_element_type=F32)
    z = u[:, 2 * di:]
    glu_ref[...] = u[:, :di] * _sigmoid(u[:, di:2 * di])
    sz_ref[...] = z * _sigmoid(z)


def _glu_proj(x, g, w_bf, tm):
    rows, d = x.shape
    di = w_bf.shape[1] // 3
    return pl.pallas_call(
        _glu_proj_body,
        grid=(rows // tm,),
        in_specs=[pl.BlockSpec((tm, d), lambda i: (i, 0)),
                  pl.BlockSpec((1, d), lambda i: (0, 0)),
                  pl.BlockSpec((d, 3 * di), lambda i: (0, 0))],
        out_specs=[pl.BlockSpec((tm, di), lambda i: (i, 0)),
                   pl.BlockSpec((tm, di), lambda i: (i, 0))],
        out_shape=[jax.ShapeDtypeStruct((rows, di), F32)] * 2,
        compiler_params=_cparams("parallel"),
        name="glu_proj",
    )(x, g.reshape(1, d), w_bf)


def _conv_tail(c, sz, x, lg_ref, lb_ref, wo_ref, gp_ref):
    mu = jnp.mean(c, axis=-1, keepdims=True)
    cc = c - mu
    var = jnp.mean(cc * cc, axis=-1, keepdims=True)
    y = cc * lax.rsqrt(var + EPS) * lg_ref[...] + lb_ref[...]
    y = y * _sigmoid(y) * sz
    o = jnp.dot(y.astype(BF16), wo_ref[...], preferred_element_type=F32)
    return x + _rms(o, gp_ref[...])


def _conv_prompt_body(glu_ref, prev_ref, sz_ref, x_ref, cw_ref, cb_ref, lg_ref, lb_ref, wo_ref, gp_ref,
                      out_ref, full_ref, c_ref, *, tm):
    t = pl.program_id(1)
    full_ref[0:HALO, :] = jnp.where(t > 0, prev_ref[0], 0.0)
    full_ref[HALO:HALO + tm, :] = glu_ref[0]
    full_ref[HALO + tm:, :] = jnp.zeros((CONV_TAIL, full_ref.shape[1]), F32)
    d = c_ref.shape[-1]
    first = HALO - (CONV_W - 1)
    sub = 8
    half = tm // 2
    for lc in range(d // LANES):
        ln = slice(lc * LANES, (lc + 1) * LANES)
        for r0 in range(0, tm, half):
            acc = jnp.zeros((half, LANES), F32)
            for b in range(sub):
                z = None
                for a in range(-(-(CONV_W - b) // sub)):
                    x = full_ref[r0 + sub * a:r0 + sub * a + half + 2 * sub, ln]
                    term = cw_ref[sub * a + b:sub * a + b + 1, ln] * x
                    z = term if z is None else z + term
                acc = acc + z[first + b:first + b + half]
            c_ref[r0:r0 + half, ln] = acc + cb_ref[:, ln]
    out_ref[0] = _conv_tail(c_ref[...], sz_ref[0], x_ref[0], lg_ref, lb_ref, wo_ref, gp_ref)


def _conv_prompt(glu, sz, x, cw, cb, lg, lb, wo_bf, gp, tm):
    b, t, d = x.shape
    di = glu.shape[-1]
    per = tm // HALO
    vec = lambda n: pl.BlockSpec((1, n), lambda i, j: (0, 0))
    return pl.pallas_call(
        functools.partial(_conv_prompt_body, tm=tm),
        grid=(b, t // tm),
        in_specs=[pl.BlockSpec((1, tm, di), lambda i, j: (i, j, 0)),
                  pl.BlockSpec((1, HALO, di), lambda i, j: (i, jnp.maximum(j * per - 1, 0), 0)),
                  pl.BlockSpec((1, tm, di), lambda i, j: (i, j, 0)),
                  pl.BlockSpec((1, tm, d), lambda i, j: (i, j, 0)),
                  pl.BlockSpec((CONV_W, di), lambda i, j: (0, 0)),
                  vec(di), vec(di), vec(di),
                  pl.BlockSpec((di, d), lambda i, j: (0, 0)),
                  vec(d)],
        out_specs=pl.BlockSpec((1, tm, d), lambda i, j: (i, j, 0)),
        out_shape=jax.ShapeDtypeStruct((b, t, d), F32),
        scratch_shapes=[pltpu.VMEM((HALO + tm + CONV_TAIL, di), F32), pltpu.VMEM((tm, di), F32)],
        compiler_params=_cparams("parallel", "arbitrary"),
        name="conv_prompt",
    )(glu, glu, sz, x, cw, cb.reshape(1, di), lg.reshape(1, di), lb.reshape(1, di), wo_bf, gp.reshape(1, d))


def _conv_sample_body(glu_ref, st_ref, sz_ref, x_ref, cw_ref, cb_ref, lg_ref, lb_ref, wo_ref, gp_ref,
                      out_ref, full_ref, c_ref, *, nb, tq):
    d = c_ref.shape[-1]
    first = HALO - (CONV_W - 1)
    full_ref[:, first:HALO, :] = st_ref[...]
    full_ref[:, HALO:HALO + tq, :] = glu_ref[...].reshape(nb, tq, d)
    for lc in range(d // LANES):
        ln = slice(lc * LANES, (lc + 1) * LANES)
        acc = jnp.zeros((nb, tq, LANES), F32)
        for j in range(CONV_W):
            acc = acc + cw_ref[j:j + 1, ln] * full_ref[:, first + j:first + j + tq, ln]
        c_ref[:, ln] = (acc + cb_ref[:, ln]).reshape(nb * tq, LANES)
    out_ref[...] = _conv_tail(c_ref[...], sz_ref[...], x_ref[...], lg_ref, lb_ref, wo_ref, gp_ref)


def _conv_sample(glu, state, sz, x, cw, cb, lg, lb, wo_bf, gp, nb):
    n_seq = state.shape[0]
    rows, d = x.shape
    di = glu.shape[-1]
    tq = rows // n_seq
    vec = lambda n: pl.BlockSpec((1, n), lambda i: (0, 0))
    return pl.pallas_call(
        functools.partial(_conv_sample_body, nb=nb, tq=tq),
        grid=(n_seq // nb,),
        in_specs=[pl.BlockSpec((nb * tq, di), lambda i: (i, 0)),
                  pl.BlockSpec((nb, CONV_W - 1, di), lambda i: (i, 0, 0)),
                  pl.BlockSpec((nb * tq, di), lambda i: (i, 0)),
                  pl.BlockSpec((nb * tq, d), lambda i: (i, 0)),
                  pl.BlockSpec((CONV_W, di), lambda i: (0, 0)),
                  vec(di), vec(di), vec(di),
                  pl.BlockSpec((di, d), lambda i: (0, 0)),
                  vec(d)],
        out_specs=pl.BlockSpec((nb * tq, d), lambda i: (i, 0)),
        out_shape=jax.ShapeDtypeStruct((rows, d), F32),
        scratch_shapes=[pltpu.VMEM((nb, HALO + tq, di), F32), pltpu.VMEM((nb * tq, di), F32)],
        compiler_params=_cparams("parallel"),
        name="conv_sample",
    )(glu, state, sz, x, cw, cb.reshape(1, di), lg.reshape(1, di), lb.reshape(1, di), wo_bf, gp.reshape(1, d))


def _head_major(col_pair, head):
    return col_pair if head % 2 == 0 else _swap_halves(col_pair)


def _nsa_proj_body(x_ref, gkv_ref, gq_ref, wkv_ref, wq_ref, wz_ref, wg_ref,
                   cmp_ref, slc_ref, win_ref, qa_ref, sz_ref, gate_ref, *aug_refs, seq_len, tm):
    x = x_ref[...]
    xn = x * lax.rsqrt(jnp.mean(x * x, axis=-1, keepdims=True) + EPS)
    hkv = (xn * gkv_ref[...]).astype(BF16)
    hq = (xn * gq_ref[...]).astype(BF16)
    kv = jnp.dot(hkv, wkv_ref[...], preferred_element_type=F32)
    width = 2 * KV_HEADS * HEAD_DIM
    if aug_refs:
        kv_t = [kv[:, b * width:(b + 1) * width].T for b in range(N_BRANCH)]
        cmp_ref[0] = kv_t[0]
        slc_ref[0] = kv_t[1]
        win_ref[0] = kv_t[2]
    else:
        cmp_ref[...] = kv[:, :width]
        slc_ref[...] = kv[:, width:2 * width]
        win_ref[...] = kv[:, 2 * width:]
    lane = _lane((tm, LANES))
    low = lane < HEAD_DIM
    uq = jnp.dot(hq, wq_ref[...], preferred_element_type=F32)
    scale = HEAD_DIM ** -0.5
    for h in range(N_HEADS):
        qh = _head_major(uq[:, (h // 2) * LANES:(h // 2 + 1) * LANES], h)
        qa_ref[:, h * LANES:(h + 1) * LANES] = jnp.where(low, qh * scale, 0.0).astype(BF16)
    z = jnp.dot(hq, wz_ref[...], preferred_element_type=F32)
    sz_ref[...] = z * _sigmoid(z)
    gate_ref[...] = _sigmoid(jnp.dot(hq, wg_ref[...], preferred_element_type=F32))
    if aug_refs:
        ska_ref, sv_ref, wka_ref, wv_ref = aug_refs
        row = pl.program_id(0) * tm + lax.broadcasted_iota(jnp.int32, (tm, LANES), 0)
        blk = (row % seq_len) // L_SLC
        onehot = (lane - HEAD_DIM == blk).astype(F32)
        kw = KV_HEADS * HEAD_DIM
        for h in range(KV_HEADS):
            ks = _head_major(kv[:, width + (h // 2) * LANES:width + (h // 2 + 1) * LANES], h)
            ska_ref[:, h * LANES:(h + 1) * LANES] = jnp.where(low, ks, onehot).astype(BF16)
            kwn = _head_major(kv[:, 2 * width + (h // 2) * LANES:2 * width + (h // 2 + 1) * LANES], h)
            wka_ref[:, h * LANES:(h + 1) * LANES] = jnp.where(low, kwn, 0.0).astype(BF16)
        extra = (lax.broadcasted_iota(jnp.int32, (V_PAD, tm), 0) == 0).astype(BF16)
        for dst, src in ((sv_ref, kv_t[1]), (wv_ref, kv_t[2])):
            for h in range(KV_HEADS):
                dst[h * V_ROWS:h * V_ROWS + HEAD_DIM, :] = src[kw + h * HEAD_DIM:kw + (h + 1) * HEAD_DIM].astype(BF16)
                dst[h * V_ROWS + HEAD_DIM:(h + 1) * V_ROWS, :] = extra


def _nsa_proj(x, g_kv, g_q, wkv_bf, wq_bf, wz_bf, wg_bf, tm, seq_len=None):
    rows, d = x.shape
    width = 2 * KV_HEADS * HEAD_DIM
    row = lambda n: pl.BlockSpec((tm, n), lambda i: (i, 0))
    full = lambda a: pl.BlockSpec(a.shape, lambda i: (0, 0))
    out_specs = [row(width), row(width), row(width), row(N_HEADS * LANES), row(wz_bf.shape[1]), row(LANES)]
    out_shape = [jax.ShapeDtypeStruct((rows, width), F32)] * 3 + [
        jax.ShapeDtypeStruct((rows, N_HEADS * LANES), BF16),
        jax.ShapeDtypeStruct((rows, wz_bf.shape[1]), F32),
        jax.ShapeDtypeStruct((rows, LANES), F32)]
    if seq_len is not None:
        per_seq = seq_len // tm
        kv_t_spec = pl.BlockSpec((1, width, tm), lambda i: (i // per_seq, 0, i % per_seq))
        out_specs[:N_BRANCH] = [kv_t_spec] * N_BRANCH
        out_shape[:N_BRANCH] = [jax.ShapeDtypeStruct((rows // seq_len, width, seq_len), F32)] * N_BRANCH
        out_specs += [row(KV_HEADS * LANES), pl.BlockSpec((KV_HEADS * V_ROWS, tm), lambda i: (0, i))] * 2
        out_shape += [jax.ShapeDtypeStruct((rows, KV_HEADS * LANES), BF16),
                      jax.ShapeDtypeStruct((KV_HEADS * V_ROWS, rows), BF16)] * 2
    gkv = g_kv.reshape(1, d)
    gq = g_q.reshape(1, d)
    return pl.pallas_call(
        functools.partial(_nsa_proj_body, seq_len=seq_len, tm=tm),
        grid=(rows // tm,),
        in_specs=[row(d), full(gkv), full(gq), full(wkv_bf), full(wq_bf), full(wz_bf), full(wg_bf)],
        out_specs=out_specs,
        out_shape=out_shape,
        compiler_params=_cparams("parallel"),
        name="nsa_proj",
    )(x, gkv, gq, wkv_bf, wq_bf, wz_bf, wg_bf)


def _compress_body(pt_ref, *refs, n_pages):
    del pt_ref
    page_refs = refs[:n_pages]
    w1_ref, pe_ref, w2_ref, out_ref, a_ref = refs[n_pages:]
    width, page_rows = page_refs[0].shape[1:]
    n_col = width // LANES
    cpp = page_rows // CMP_STRIDE
    n_chunk = n_pages * cpp
    hid = w2_ref.shape[2]
    low = _lane((cpp, LANES)) < HEAD_DIM
    def assemble(c):
        for p in range(n_pages):
            col_t = page_refs[p][0, c * LANES:(c + 1) * LANES, :]
            by_s = pltpu.einshape("(ns)f->(sn)f", col_t.T, n=cpp)
            for s in range(0, CMP_STRIDE, 2):
                b0 = by_s[s * cpp:(s + 1) * cpp]
                b1 = by_s[(s + 1) * cpp:(s + 2) * cpp]
                dst = (slice(p * cpp, (p + 1) * cpp), slice((s // 2) * LANES, (s // 2 + 1) * LANES))
                a_ref[(2 * c,) + dst] = jnp.where(low, b0, _swap_halves(b1))
                a_ref[(2 * c + 1,) + dst] = jnp.where(low, _swap_halves(b0), b1)

    for kv in range(2):
        for c in range(kv * n_col // 2, (kv + 1) * n_col // 2):
            assemble(c)
        w1 = w1_ref[kv]
        pe = jnp.dot(pe_ref[kv].astype(BF16), w1, preferred_element_type=F32)
        pe_term = pe[0:1, :hid] + pe[1:2, hid:]
        a_kv = a_ref[kv * KV_HEADS:(kv + 1) * KV_HEADS].reshape(KV_HEADS * n_chunk, CMP_STRIDE * HEAD_DIM)
        parts = jnp.dot(a_kv.astype(BF16), w1, preferred_element_type=F32)
        for hp in range(KV_HEADS // 2):
            pair = jnp.zeros((n_chunk, LANES), F32)
            for par in range(2):
                part = parts[(2 * hp + par) * n_chunk:(2 * hp + par + 1) * n_chunk]
                pre = part[:, :hid] + pltpu.roll(part[:, hid:], n_chunk - 1, axis=0) + pe_term
                mid = pre * _sigmoid(pre)
                pair = pair + jnp.dot(mid.astype(BF16), w2_ref[kv, par], preferred_element_type=F32)
            col = kv * (KV_HEADS // 2) + hp
            out_ref[0, col * LANES:(col + 1) * LANES, :] = pair.T


def _compress(rows_t, page_table, w1cat_bf, pe8, w2_bf, n_pages, page_rows):
    width = rows_t.shape[1]
    n_chunk = n_pages * page_rows // CMP_STRIDE
    if page_table is None:
        n_seq = rows_t.shape[0]
        page_table = jnp.zeros((1,), jnp.int32)
        page_spec = lambda j: pl.BlockSpec((1, width, page_rows), lambda i, pt: (i, 0, j))
    else:
        n_seq = page_table.shape[0] // n_pages
        page_spec = lambda j: pl.BlockSpec((1, width, page_rows), lambda i, pt: (pt[i * n_pages + j], 0, 0))
    full = lambda a: pl.BlockSpec(a.shape, lambda i, pt: (0,) * a.ndim)
    grid_spec = pltpu.PrefetchScalarGridSpec(
        num_scalar_prefetch=1,
        grid=(n_seq,),
        in_specs=[page_spec(j) for j in range(n_pages)] + [full(w1cat_bf), full(pe8), full(w2_bf)],
        out_specs=pl.BlockSpec((1, width, n_chunk), lambda i, pt: (i, 0, 0)),
        scratch_shapes=[pltpu.VMEM((2 * KV_HEADS, n_chunk, CMP_STRIDE * HEAD_DIM), F32)],
    )
    return pl.pallas_call(
        functools.partial(_compress_body, n_pages=n_pages),
        grid_spec=grid_spec,
        out_shape=jax.ShapeDtypeStruct((n_seq, width, n_chunk), F32),
        compiler_params=_cparams("parallel"),
        name="compress",
    )(page_table, *([rows_t] * n_pages), w1cat_bf, pe8, w2_bf)


def _bias_tile_body(rb_ref, out_ref, *, row_step, lane_step, offset, hi):
    h = pl.program_id(0)
    shape = out_ref.shape[1:]
    d = row_step * lax.broadcasted_iota(jnp.int32, shape, 0) + lane_step * _lane(shape) + offset
    n = jnp.maximum(d, 0)
    nf = jnp.maximum(n, 1).astype(F32)
    large = MAX_EXACT + (jnp.log(nf / MAX_EXACT) / math.log(MAX_DISTANCE / MAX_EXACT)
                         * (NUM_BUCKETS - MAX_EXACT)).astype(jnp.int32)
    large = jnp.minimum(large, NUM_BUCKETS - 1)
    bucket = jnp.where(n < MAX_EXACT, n, large)
    far = rb_ref[NUM_BUCKETS - 1, h]
    val = jnp.zeros(shape, F32)
    for k in range(NUM_BUCKETS - 1):
        val = jnp.where(bucket == k, rb_ref[k, h] - far, val)
    out_ref[0] = jnp.where((d >= 0) & (d <= hi), val, NEG)


def _bias_tile(rel_bias, rows, width, lane_step, offset, hi=1 << 30, row_step=1):
    return pl.pallas_call(
        functools.partial(_bias_tile_body, row_step=row_step, lane_step=lane_step, offset=offset, hi=hi),
        grid=(N_HEADS,),
        in_specs=[pl.BlockSpec(memory_space=pltpu.SMEM)],
        out_specs=pl.BlockSpec((1, rows, width), lambda h: (h, 0, 0)),
        out_shape=jax.ShapeDtypeStruct((N_HEADS, rows, width), F32),
        compiler_params=_cparams("parallel"),
        name="bias_tile",
    )(rel_bias)


_NT = (((1,), (1,)), ((), ()))


def _softmax_rows(s):
    m = jnp.maximum(jnp.max(s, axis=-1, keepdims=True), M_FLOOR)
    p = jnp.exp(s - m)
    return p, jnp.sum(p, axis=-1, keepdims=True)


def _gate_col(gate, col):
    return jnp.sum(jnp.where(_lane(gate.shape) == col, gate, 0.0), axis=-1, keepdims=True)


def _pair_columns(o_even, o_odd, valid_half):
    lo = o_even if valid_half == 0 else _swap_halves(o_even)
    hi = o_odd if valid_half == 1 else _swap_halves(o_odd)
    return jnp.where(_lane(lo.shape) < HEAD_DIM, lo, hi)


def _select_blocks_t(imp_t, tb, n_sel):
    nb, nt = imp_t.shape
    sub = 8
    j = lax.broadcasted_iota(jnp.int32, (nb, nt), 0)
    valid = j <= tb
    forced = (j == 0) | (j == tb) | (j == tb - 1)
    score = jnp.where(valid, jnp.where(forced, BIG, imp_t), -BIG)
    groups = [score[lo:lo + sub] for lo in range(0, nb, sub)]
    jr = lax.broadcasted_iota(jnp.int32, (sub, nt), 0)
    counts = [jnp.zeros((sub, nt), F32) for _ in groups]
    for i in range(nb):
        row = score[i:i + 1]
        for r, grp in enumerate(groups):
            lo = r * sub
            if lo > i:
                one = jnp.where(row >= grp, 1.0, 0.0)
            elif lo + sub - 1 <= i:
                one = jnp.where(row > grp, 1.0, 0.0)
            else:
                one = jnp.where(jr + lo > i, jnp.where(row >= grp, 1.0, 0.0), jnp.where(row > grp, 1.0, 0.0))
            counts[r] = counts[r] + one
    rank = jnp.concatenate(counts, axis=0)
    return jnp.where((rank < n_sel) & valid, 0.0, NEG)


def _cmp_prompt_body(qa_ref, kc_ref, vc_ref, m_ref, cb_ref, gate_ref, oc_ref, qsel_ref, qflag_ref, *, n_key):
    qt = pl.program_id(1)
    start = pl.multiple_of(8 * qt + 8, 8)
    kwin = kc_ref[0, pl.ds(start, n_key), :]
    vwin = vc_ref[0, pl.ds(start, n_key), :]
    mwin_t = m_ref[pl.ds(start, n_key), :].T
    feat = lax.broadcasted_iota(jnp.int32, (LANES, Q_BLK), 0)
    pos = qt * Q_BLK + lax.broadcasted_iota(jnp.int32, (1, Q_BLK), 1)
    tb = pos // L_SLC
    gate = gate_ref[...]
    n_pair = GROUP // 2
    imps, q_feats = [], []
    for kvh in range(KV_HEADS):
        ka = kwin[:, kvh * LANES:(kvh + 1) * LANES].astype(BF16)
        vpt = vwin[:, (kvh // 2) * LANES:(kvh // 2 + 1) * LANES].T
        vpt = vpt[(kvh % 2) * HEAD_DIM:(kvh % 2 + 1) * HEAD_DIM].astype(BF16)
        qts = [qa_ref[0, :, h * LANES:(h + 1) * LANES].astype(F32).T for h in range(kvh * GROUP, (kvh + 1) * GROUP)]
        pc_sum = jnp.zeros((n_key, Q_BLK), F32)
        for gp in range(n_pair):
            qa = jnp.concatenate([jnp.where(feat == HEAD_DIM, NEG, qts[2 * gp + e]) for e in range(2)], axis=1)
            qa = qa.astype(BF16)
            qflag_ref[0, kvh, :, 2 * gp * Q_BLK:(2 * gp + 2) * Q_BLK] = qa
            s = jnp.dot(ka, qa, preferred_element_type=F32)
            s = s + cb_ref[kvh, :, 2 * gp * Q_BLK:(2 * gp + 2) * Q_BLK]
            m = jnp.maximum(jnp.max(s, axis=0, keepdims=True), M_FLOOR)
            p = jnp.exp(s - m)
            inv = 1.0 / jnp.maximum(jnp.sum(p, axis=0, keepdims=True), 1e-30)
            pc = p * inv
            pc_sum = pc_sum + pc[:, :Q_BLK] + pc[:, Q_BLK:]
            o = jnp.dot(vpt, p.astype(BF16), preferred_element_type=F32) * inv
            _store_chain(oc_ref, o, gate, kvh * GROUP + 2 * gp, kvh * n_pair + gp)
        imps.append(jnp.dot(mwin_t, pc_sum, preferred_element_type=F32, precision=lax.Precision.HIGHEST)[HEAD_DIM:])
        q_feats.append([qt_[:HEAD_DIM] for qt_ in qts])
    n_blk = imps[0].shape[0]
    n_valid = (qt + 1) * (Q_BLK // L_SLC)
    step = n_blk // 4
    for n_rank in range(step, n_blk + 1, step):
        @pl.when((n_valid > n_rank - step) & (n_valid <= n_rank))
        def _():
            for kvh in range(KV_HEADS):
                selneg_t = _select_blocks_t(imps[kvh][:n_rank], tb, N_SEL)
                if n_rank < n_blk:
                    selneg_t = jnp.concatenate([selneg_t, jnp.full((n_blk - n_rank, Q_BLK), NEG, F32)], axis=0)
                for g in range(GROUP):
                    qsel_ref[0, kvh, :, g * Q_BLK:(g + 1) * Q_BLK] = jnp.concatenate(
                        [q_feats[kvh][g], selneg_t], axis=0).astype(BF16)


def _cmp_prompt(qa, kc_pad, vc_pad, m_pad, cb, gate, batch, seq):
    nq = seq // Q_BLK
    n_key = kc_pad.shape[1] // 2
    d_out = N_HEADS * HEAD_DIM
    return pl.pallas_call(
        functools.partial(_cmp_prompt_body, n_key=n_key),
        grid=(batch, nq),
        in_specs=[pl.BlockSpec((1, Q_BLK, N_HEADS * LANES), lambda b, t: (b, t, 0)),
                  pl.BlockSpec((1,) + kc_pad.shape[1:], lambda b, t: (b, 0, 0)),
                  pl.BlockSpec((1,) + vc_pad.shape[1:], lambda b, t: (b, 0, 0)),
                  pl.BlockSpec(m_pad.shape, lambda b, t: (0, 0)),
                  pl.BlockSpec(cb.shape, lambda b, t: (0, 0, 0)),
                  pl.BlockSpec((Q_BLK, LANES), lambda b, t: (b * nq + t, 0))],
        out_specs=[pl.BlockSpec((1, Q_BLK, d_out), lambda b, t: (b, t, 0))]
        + [pl.BlockSpec((1, KV_HEADS, LANES, GROUP * Q_BLK), lambda b, t: (b * nq + t, 0, 0, 0))] * 2,
        out_shape=[jax.ShapeDtypeStruct((batch, seq, d_out), F32)]
        + [jax.ShapeDtypeStruct((batch * nq, KV_HEADS, LANES, GROUP * Q_BLK), BF16)] * 2,
        compiler_params=_cparams("parallel", "parallel"),
        name="cmp_prompt",
    )(qa.reshape(batch, seq, -1), kc_pad, vc_pad, m_pad, cb, gate)


KEY_TILE = 256
COL_CHAIN = 256
WIN_TILES = 2
V_PAD = 16
V_ROWS = HEAD_DIM + V_PAD


def _store_chain(out_ref, o_t, gate, gate_col0, lane_col, row0=0):
    pair = jnp.concatenate([o_t[:, :Q_BLK], o_t[:, Q_BLK:]], axis=0).T
    g = jnp.where(_lane(pair.shape) < HEAD_DIM, _gate_col(gate, gate_col0), _gate_col(gate, gate_col0 + 1))
    out_ref[0, row0:row0 + Q_BLK, lane_col * LANES:(lane_col + 1) * LANES] = pair * g


_PAIR_CHAINS = [(e, c) for e in range(2) for c in range(GROUP * Q_BLK // COL_CHAIN)]


def _slc_prompt_body(q_ref, k_ref, v_ref, sb_ref, gate_ref, out_ref, m_ref, acc_ref, p_ref, s_ref):
    pr = pl.program_id(1)
    qt = pl.program_id(2)
    m_ref[...] = jnp.full(m_ref.shape, M_FLOOR, F32)
    acc_ref[...] = jnp.zeros(acc_ref.shape, F32)
    p_ref[...] = jnp.zeros(p_ref.shape, BF16)
    last = (qt * Q_BLK) // KEY_TILE
    cs = lambda c: slice(c * COL_CHAIN, (c + 1) * COL_CHAIN)

    def pv_of(kt):
        base = pl.multiple_of(kt * KEY_TILE, KEY_TILE)
        return [jnp.dot(v_ref[e * V_ROWS:(e + 1) * V_ROWS, pl.ds(base, KEY_TILE)], p_ref[i],
                        preferred_element_type=F32) for i, (e, c) in enumerate(_PAIR_CHAINS)]

    def scores_of(kt):
        base = pl.multiple_of(kt * KEY_TILE, KEY_TILE)
        ks = [k_ref[0, pl.ds(base, KEY_TILE), e * LANES:(e + 1) * LANES] for e in range(2)]
        return [jnp.dot(ks[e], q_ref[0, e, :, cs(c)], preferred_element_type=F32) for e, c in _PAIR_CHAINS]

    def make_step(with_bias):
        def step(kt, carry):
            s_next = scores_of(jnp.minimum(kt + 1, last))
            pvs = pv_of(jnp.maximum(kt - 1, 0))
            ss = [s_ref[i] for i in range(len(_PAIR_CHAINS))]
            if with_bias:
                case = qt % 2 + 2 * (last - kt)
                ss = [s + sb_ref[case, e, :, cs(c)] for s, (e, c) in zip(ss, _PAIR_CHAINS)]
            alphas = []
            for i, s in enumerate(ss):
                m_old = m_ref[i]
                m_new = jnp.maximum(m_old, jnp.max(s, axis=0, keepdims=True))
                alphas.append(jnp.exp(m_old - m_new))
                m_ref[i] = m_new
                p_ref[i] = jnp.exp(s - m_new).astype(BF16)
            for i in range(len(_PAIR_CHAINS)):
                acc_ref[i] = alphas[i] * (acc_ref[i] + pvs[i])
                s_ref[i] = s_next[i]
            return carry
        return step

    for i, s in enumerate(scores_of(0)):
        s_ref[i] = s
    n_far = jnp.maximum(last - 1, 0)
    lax.fori_loop(0, n_far, make_step(False), 0)
    lax.fori_loop(n_far, last + 1, make_step(True), 0)
    gate = gate_ref[...]
    pvs = pv_of(last)
    for i, (e, c) in enumerate(_PAIR_CHAINS):
        o = acc_ref[i] + pvs[i]
        o = o[:HEAD_DIM] * (1.0 / jnp.maximum(o[HEAD_DIM:HEAD_DIM + 1], 1e-30))
        _store_chain(out_ref, o, gate, N_HEADS + (2 * pr + e) * GROUP + 2 * c, i)


def _slc_prompt(qsel_t, ska, sv_t, sb_t, gate, batch, seq):
    nq = seq // Q_BLK
    cols = GROUP * Q_BLK
    d_out = N_HEADS * HEAD_DIM
    n_chain = len(_PAIR_CHAINS)
    return pl.pallas_call(
        _slc_prompt_body,
        grid=(batch, KV_HEADS // 2, nq),
        in_specs=[pl.BlockSpec((1, 2, LANES, cols), lambda b, p, t: (b * nq + t, p, 0, 0)),
                  pl.BlockSpec((1, seq, 2 * LANES), lambda b, p, t: (b, 0, p)),
                  pl.BlockSpec((2 * V_ROWS, seq), lambda b, p, t: (p, b)),
                  pl.BlockSpec((4, 2, KEY_TILE, cols), lambda b, p, t: (0, p, 0, 0)),
                  pl.BlockSpec((Q_BLK, LANES), lambda b, p, t: (b * nq + t, 0))],
        out_specs=pl.BlockSpec((1, Q_BLK, 2 * GROUP * HEAD_DIM), lambda b, p, t: (b, t, p)),
        out_shape=jax.ShapeDtypeStruct((batch, seq, d_out), F32),
        scratch_shapes=[pltpu.VMEM((n_chain, 1, COL_CHAIN), F32),
                        pltpu.VMEM((n_chain, V_ROWS, COL_CHAIN), F32),
                        pltpu.VMEM((n_chain, KEY_TILE, COL_CHAIN), BF16),
                        pltpu.VMEM((n_chain, KEY_TILE, COL_CHAIN), F32)],
        compiler_params=_cparams("parallel", "parallel", "arbitrary"),
        name="slc_prompt",
    )(qsel_t, ska.reshape(batch, seq, -1), sv_t, sb_t, gate)


def _win_prompt_body(q_ref, k_ref, v_ref, wb_ref, gate_ref, out_ref, *, n_key):
    pr = pl.program_id(1)
    t0 = pl.program_id(2) * WIN_TILES
    cs = lambda c: slice(c * COL_CHAIN, (c + 1) * COL_CHAIN)
    chains = [(u, e, c) for u in range(WIN_TILES) for e, c in _PAIR_CHAINS]
    bases = [pl.multiple_of((t0 + u) * Q_BLK, Q_BLK) for u in range(WIN_TILES)]
    ss = []
    for u in range(WIN_TILES):
        ks = [k_ref[0, pl.ds(bases[u], n_key), e * LANES:(e + 1) * LANES] for e in range(2)]
        ss += [jnp.dot(ks[e], q_ref[u, e, :, cs(c)], preferred_element_type=F32) + wb_ref[e, :, cs(c)]
               for e, c in _PAIR_CHAINS]
    ps = []
    for s in ss:
        m = jnp.maximum(jnp.max(s, axis=0, keepdims=True), M_FLOOR)
        ps.append(jnp.exp(s - m).astype(BF16))
    for i, (u, e, c) in enumerate(chains):
        o = jnp.dot(v_ref[e * V_ROWS:(e + 1) * V_ROWS, pl.ds(bases[u], n_key)], ps[i], preferred_element_type=F32)
        o = o[:HEAD_DIM] * (1.0 / jnp.maximum(o[HEAD_DIM:HEAD_DIM + 1], 1e-30))
        gate = gate_ref[u * Q_BLK:(u + 1) * Q_BLK, :]
        _store_chain(out_ref, o, gate, 2 * N_HEADS + (2 * pr + e) * GROUP + 2 * c, i % len(_PAIR_CHAINS), u * Q_BLK)


def _win_prompt(q_t, wka_pad, wv_t_pad, wb_t, gate, batch, seq):
    nq = seq // Q_BLK
    n_key = WINDOW + Q_BLK
    cols = GROUP * Q_BLK
    d_out = N_HEADS * HEAD_DIM
    padded = wka_pad.shape[1]
    return pl.pallas_call(
        functools.partial(_win_prompt_body, n_key=n_key),
        grid=(batch, KV_HEADS // 2, nq // WIN_TILES),
        in_specs=[pl.BlockSpec((WIN_TILES, 2, LANES, cols), lambda b, p, t: (b * (nq // WIN_TILES) + t, p, 0, 0)),
                  pl.BlockSpec((1, padded, 2 * LANES), lambda b, p, t: (b, 0, p)),
                  pl.BlockSpec((2 * V_ROWS, padded), lambda b, p, t: (p, b)),
                  pl.BlockSpec((2, n_key, cols), lambda b, p, t: (p, 0, 0)),
                  pl.BlockSpec((WIN_TILES * Q_BLK, LANES), lambda b, p, t: (b * (nq // WIN_TILES) + t, 0))],
        out_specs=pl.BlockSpec((1, WIN_TILES * Q_BLK, 2 * GROUP * HEAD_DIM), lambda b, p, t: (b, t, p)),
        out_shape=jax.ShapeDtypeStruct((batch, seq, d_out), F32),
        compiler_params=_cparams("parallel", "parallel", "parallel"),
        name="win_prompt",
    )(q_t, wka_pad, wv_t_pad, wb_t, gate)


def _nsa_out_body(oc_ref, os_ref, ow_ref, sz_ref, x_ref, wo_ref, gp_ref, out_ref):
    d = oc_ref.shape[-1]
    y = oc_ref[...] * sz_ref[:, :d] + os_ref[...] * sz_ref[:, d:2 * d] + ow_ref[...] * sz_ref[:, 2 * d:]
    o = jnp.dot(y.astype(BF16), wo_ref[...], preferred_element_type=F32)
    out_ref[...] = x_ref[...] + _rms(o, gp_ref[...])


def _nsa_out(oc, os_, ow, sz, x, wo_bf, gp, tm):
    rows, d = x.shape
    dq = oc.shape[-1]
    row = lambda n: pl.BlockSpec((tm, n), lambda i: (i, 0))
    return pl.pallas_call(
        _nsa_out_body,
        grid=(rows // tm,),
        in_specs=[row(dq), row(dq), row(dq), row(N_BRANCH * dq), row(d),
                  pl.BlockSpec((dq, d), lambda i: (0, 0)), pl.BlockSpec((1, d), lambda i: (0, 0))],
        out_specs=row(d),
        out_shape=jax.ShapeDtypeStruct((rows, d), F32),
        compiler_params=_cparams("parallel"),
        name="nsa_out",
    )(oc, os_, ow, sz, x, wo_bf, gp.reshape(1, d))


def _decode_attn_body(pt_ref, *refs, n_pages, tq, w_keep):
    del pt_ref
    page_refs = refs[:n_pages]
    (qa_ref, kvc_ref, snew_ref, wst_ref, wnew_ref, mt_ref, e_ref, cbs_ref, sbs_ref, wbs_ref, gate_ref,
     oc_ref, os_ref, ow_ref, kt_ref, vt_ref, kwt_ref, vwt_ref) = refs[n_pages:]
    n_past = n_pages * PAGE_SIZE
    kw = KV_HEADS * HEAD_DIM
    rows = N_HEADS * tq

    def new_tile(ref):
        return jnp.concatenate([ref[...], jnp.zeros((LANES - tq, 2 * kw), F32)], axis=0).T

    def fill(k_dst, v_dst, col0, tile):
        k_dst[:, col0:col0 + tile.shape[1]] = tile[:kw].astype(BF16)
        v_dst[:, col0:col0 + tile.shape[1]] = tile[kw:].astype(BF16)

    for p in range(n_pages):
        fill(kt_ref, vt_ref, p * PAGE_SIZE, page_refs[p][0])
    fill(kt_ref, vt_ref, n_past, new_tile(snew_ref))
    fill(kwt_ref, vwt_ref, 0, wst_ref[0])
    fill(kwt_ref, vwt_ref, w_keep, new_tile(wnew_ref))

    zero = jnp.zeros((tq, LANES), F32)
    tiles = [[], []]
    for h in range(N_HEADS):
        kvh = h // GROUP
        piece = qa_ref[:, h * LANES:(h + 1) * LANES]
        piece = piece if kvh % 2 == 0 else _swap_halves(piece)
        for ct in range(2):
            tiles[ct].append(piece if kvh // 2 == ct else zero)
    qbd = jnp.concatenate([jnp.concatenate(t, axis=0) for t in tiles], axis=1).astype(BF16)
    gate = gate_ref[...]

    def branch(k_t, v_t, bias, extra=None):
        s = jnp.dot(qbd, k_t, preferred_element_type=F32) + bias
        if extra is not None:
            s = s + extra
        p, l = _softmax_rows(s)
        inv = 1.0 / jnp.maximum(l, 1e-30)
        o_t = lax.dot_general(v_t, p.astype(BF16), _NT, preferred_element_type=F32)
        return p, inv, o_t.T

    def store(out_ref, o, scale):
        o = o * scale
        for kvh in range(KV_HEADS):
            tile = o[:, (kvh // 2) * LANES:(kvh // 2 + 1) * LANES]
            for gp in range(GROUP // 2):
                h0 = kvh * GROUP + 2 * gp
                col = kvh * (GROUP // 2) + gp
                out_ref[:, col * LANES:(col + 1) * LANES] = _pair_columns(
                    tile[h0 * tq:(h0 + 1) * tq], tile[(h0 + 1) * tq:(h0 + 2) * tq], kvh % 2)

    p, inv, o = branch(kvc_ref[0, :kw, :].astype(BF16), kvc_ref[0, kw:, :].astype(BF16), cbs_ref[...])
    store(oc_ref, o, inv * _gate_col(gate, 0))
    pc = p * inv
    pc_sum = jnp.concatenate(
        [sum(pc[(kvh * GROUP + g) * tq:(kvh * GROUP + g + 1) * tq] for g in range(GROUP)) for kvh in range(KV_HEADS)],
        axis=0)
    imp_t = lax.dot_general(mt_ref[...], pc_sum, _NT, preferred_element_type=F32, precision=lax.Precision.HIGHEST)
    n_col = KV_HEADS * tq
    nb_pad = -(-(-(-(n_past + tq) // L_SLC)) // 8) * 8
    tb = (n_past + lax.broadcasted_iota(jnp.int32, (1, n_col), 1) % tq) // L_SLC
    selneg_t = _select_blocks_t(imp_t[HEAD_DIM:HEAD_DIM + nb_pad], tb, N_SEL)
    sel_t = jnp.concatenate([jnp.zeros((HEAD_DIM, n_col), F32), selneg_t,
                             jnp.zeros((LANES - HEAD_DIM - nb_pad, n_col), F32)], axis=0)
    sel = jnp.concatenate([sel_t, jnp.zeros((LANES, LANES - n_col), F32)], axis=1).T
    sel = jnp.concatenate([sel[kvh * tq:(kvh + 1) * tq] for kvh in range(KV_HEADS) for _ in range(GROUP)], axis=0)
    block_mask = jnp.dot(sel.astype(BF16), e_ref[...], preferred_element_type=F32)
    _, inv, o = branch(kt_ref[...], vt_ref[...], sbs_ref[...], block_mask)
    store(os_ref, o, inv * _gate_col(gate, 1))
    _, inv, o = branch(kwt_ref[...], vwt_ref[...], wbs_ref[...])
    store(ow_ref, o, inv * _gate_col(gate, 2))


def _decode_attn(qa, kvc_t, pages_t, page_table, slc_new, win_state_t, win_new, m_t, e_s, cbs, sbs, wbs, gate_r,
                 n_seq, tq, n_pages):
    width = pages_t.shape[1]
    n_sk = sbs.shape[-1]
    n_wk = wbs.shape[-1]
    w_keep = win_state_t.shape[2]
    d_out = N_HEADS * HEAD_DIM
    rows = N_HEADS * tq
    page_spec = lambda j: pl.BlockSpec((1, width, PAGE_SIZE), lambda i, pt: (pt[i * n_pages + j], 0, 0))
    full = lambda a: pl.BlockSpec(a.shape, lambda i, pt: (0,) * a.ndim)
    row = lambda n: pl.BlockSpec((tq, n), lambda i, pt: (i, 0))
    grid_spec = pltpu.PrefetchScalarGridSpec(
        num_scalar_prefetch=1,
        grid=(n_seq,),
        in_specs=[page_spec(j) for j in range(n_pages)] + [
            row(N_HEADS * LANES),
            pl.BlockSpec((1,) + kvc_t.shape[1:], lambda i, pt: (i, 0, 0)),
            row(width),
            pl.BlockSpec((1, width, w_keep), lambda i, pt: (i, 0, 0)),
            row(width),
            full(m_t), full(e_s), full(cbs), full(sbs), full(wbs),
            pl.BlockSpec((rows, LANES), lambda i, pt: (i, 0))],
        out_specs=[row(d_out)] * 3,
        scratch_shapes=[pltpu.VMEM((width // 2, n_sk), BF16), pltpu.VMEM((width // 2, n_sk), BF16),
                        pltpu.VMEM((width // 2, n_wk), BF16), pltpu.VMEM((width // 2, n_wk), BF16)],
    )
    return pl.pallas_call(
        functools.partial(_decode_attn_body, n_pages=n_pages, tq=tq, w_keep=w_keep),
        grid_spec=grid_spec,
        out_shape=[jax.ShapeDtypeStruct((n_seq * tq, d_out), F32)] * 3,
        compiler_params=_cparams("parallel"),
        name="decode_attn",
    )(page_table, *([pages_t] * n_pages), qa, kvc_t, slc_new, win_state_t, win_new, m_t, e_s, cbs, sbs, wbs, gate_r)


def _overlap_matrix(n_rows, row0, n_cmp, n_blk):
    import numpy as np
    m = np.zeros((n_rows, LANES), np.float32)
    cs = np.arange(n_cmp)[:, None] * CMP_STRIDE
    js = np.arange(n_blk)[None, :] * L_SLC
    m[row0:row0 + n_cmp, HEAD_DIM:HEAD_DIM + n_blk] = (cs <= js + L_SLC - 1) & (cs + L_CMP - 1 >= js)
    return jnp.asarray(m)


def kernel(x_prompt, x_sample, state_conv, cache_cmp, cache_slc, state_win, page_table, rel_bias, a_norm_pre, a_w_in, a_conv_w, a_conv_b, a_ln_g, a_ln_b, a_w_out, a_norm_post, kv_norm, w_kv, cmp_pe, cmp_w1, cmp_w2, b_norm_pre, b_w_in, b_w_out, b_norm_post):
    import numpy as np
    bp, tp, d = x_prompt.shape
    bd, tq, _ = x_sample.shape
    n_pages = page_table.shape[1]
    past = n_pages * PAGE_SIZE
    w_keep = state_win.shape[1]
    width = 2 * KV_HEADS * HEAD_DIM
    kw = KV_HEADS * HEAD_DIM
    dq = N_HEADS * HEAD_DIM
    assert b_w_in.shape[0] == 1 and tp % (2 * KEY_TILE) == 0 and tp // L_SLC <= HEAD_DIM
    tm = 256
    tm_s = min(tm, bd * tq)

    xp = x_prompt
    xs = x_sample.reshape(bd * tq, d)
    conv_p, conv_s = [], []
    for l in range(a_w_in.shape[0]):
        w_in = a_w_in[l].astype(BF16)
        w_out = a_w_out[l].astype(BF16)
        di = a_w_out.shape[1]
        tail = (a_conv_w[l], a_conv_b[l], a_ln_g[l], a_ln_b[l], w_out, a_norm_post[l])
        glu, sz = _glu_proj(xp.reshape(bp * tp, d), a_norm_pre[l], w_in, tm)
        glu = glu.reshape(bp, tp, di)
        xp = _conv_prompt(glu, sz.reshape(bp, tp, di), xp, *tail, tm)
        conv_p.append(glu[:, -(CONV_W - 1):])
        glu, sz = _glu_proj(xs, a_norm_pre[l], w_in, tm_s)
        xs = _conv_sample(glu, state_conv[l], sz, xs, *tail, 16)
        conv_s.append(jnp.concatenate([state_conv[l], glu.reshape(bd, tq, di)], axis=1)[:, -(CONV_W - 1):])

    bw = b_w_in[0]
    n_gate = N_BRANCH * N_HEADS
    wq = bw[:, :dq].astype(BF16)
    wz = bw[:, dq:dq * (1 + N_BRANCH)].astype(BF16)
    wg = jnp.pad(bw[:, dq * (1 + N_BRANCH):], ((0, 0), (0, LANES - n_gate))).astype(BF16)
    wkv = w_kv.astype(BF16)
    xp2 = xp.reshape(bp * tp, d)
    (cmp_p, slc_p, win_p, qa_p, sz_p, gate_p, ska, sv, wka, wv) = _nsa_proj(
        xp2, kv_norm, b_norm_pre[0], wkv, wq, wz, wg, tm, seq_len=tp)
    cmp_s, slc_s, win_s, qa_s, sz_s, gate_s = _nsa_proj(xs, kv_norm, b_norm_pre[0], wkv, wq, wz, wg, tm_s)
    qa_s = qa_s.astype(F32)

    w1cat = jnp.concatenate([cmp_w1[:, :CMP_STRIDE * HEAD_DIM], cmp_w1[:, CMP_STRIDE * HEAD_DIM:]], axis=2).astype(BF16)
    pe8 = jnp.pad(cmp_pe.reshape(2, L_CMP // CMP_STRIDE, CMP_STRIDE * HEAD_DIM), ((0, 0), (0, 6), (0, 0)))
    zero = jnp.zeros_like(cmp_w2)
    w2h = jnp.stack([jnp.concatenate([cmp_w2, zero], axis=2), jnp.concatenate([zero, cmp_w2], axis=2)], axis=1).astype(BF16)
    prompt_page = min(1024, tp)
    pp = tp // prompt_page
    kvc_p = _compress(cmp_p, None, w1cat, pe8, w2h, pp, prompt_page).transpose(0, 2, 1)
    pt_flat = page_table.reshape(-1).astype(jnp.int32)
    cmp_pages_t = cache_cmp.transpose(0, 2, 3, 4, 1).reshape(-1, width, PAGE_SIZE)
    kvc_s = _compress(cmp_pages_t, pt_flat, w1cat, pe8, w2h, n_pages, PAGE_SIZE)

    n_chunk_p = kvc_p.shape[1]
    nc_p = n_chunk_p - L_CMP // CMP_STRIDE + 1
    real = (jnp.arange(n_chunk_p) < nc_p)[None, :, None]
    flag = jnp.zeros((LANES - HEAD_DIM,), F32).at[0].set(1.0)
    kc4 = jnp.where(real, kvc_p[:, :, :kw], 0.0).reshape(bp, n_chunk_p, KV_HEADS, HEAD_DIM)
    aug = jnp.where(real[..., None], 0.0, flag) * jnp.ones((bp, n_chunk_p, KV_HEADS, 1), F32)
    kc_real = jnp.concatenate([kc4, aug], axis=-1).reshape(bp, n_chunk_p, KV_HEADS * LANES)
    pad_row = jnp.concatenate([jnp.zeros((HEAD_DIM,), F32), flag])
    kc_front = jnp.broadcast_to(jnp.tile(pad_row, KV_HEADS), (bp, n_chunk_p, KV_HEADS * LANES))
    kc_pad = jnp.concatenate([kc_front, kc_real], axis=1)
    vc_pad = jnp.concatenate([jnp.zeros((bp, n_chunk_p, kw), F32), jnp.where(real, kvc_p[:, :, kw:], 0.0)], axis=1)
    m_pad = _overlap_matrix(2 * n_chunk_p, n_chunk_p, nc_p, tp // L_SLC)
    cb = _bias_tile(rel_bias, n_chunk_p, Q_BLK, 1, CMP_STRIDE * (n_chunk_p - Q_BLK // CMP_STRIDE) - (L_CMP - 1),
                    row_step=-CMP_STRIDE)
    cb = cb.reshape(KV_HEADS, GROUP, n_chunk_p, Q_BLK).transpose(0, 2, 1, 3).reshape(KV_HEADS, n_chunk_p, GROUP * Q_BLK)
    oc_p, qsel, qflag = _cmp_prompt(qa_p, kc_pad, vc_pad, m_pad, cb, gate_p, bp, tp)
    sb = jnp.stack([_bias_tile(rel_bias, KEY_TILE, Q_BLK, 1, off, row_step=-1)
                    for off in (0, Q_BLK, 2 * Q_BLK, 3 * Q_BLK)])
    sb = sb.reshape(4, KV_HEADS, GROUP, KEY_TILE, Q_BLK).transpose(0, 1, 3, 2, 4).reshape(4, KV_HEADS, KEY_TILE, GROUP * Q_BLK)
    os_p = _slc_prompt(qsel, ska, sv, sb, gate_p, bp, tp)
    n_wkey = WINDOW + Q_BLK
    wb = _bias_tile(rel_bias, n_wkey, Q_BLK, 1, WINDOW, WINDOW, row_step=-1)
    wb = wb.reshape(KV_HEADS, GROUP, n_wkey, Q_BLK).transpose(0, 2, 1, 3).reshape(KV_HEADS, n_wkey, GROUP * Q_BLK)
    wka_front = jnp.broadcast_to(jnp.tile(pad_row, KV_HEADS).astype(BF16), (bp, WINDOW, KV_HEADS * LANES))
    wka_pad = jnp.concatenate([wka_front, wka.reshape(bp, tp, -1)], axis=1)
    v_rows = wv.shape[0]
    wv_pad = jnp.concatenate([jnp.zeros((v_rows, bp, WINDOW), BF16), wv.reshape(v_rows, bp, tp)], axis=2)
    ow_p = _win_prompt(qflag, wka_pad, wv_pad.reshape(v_rows, bp * (WINDOW + tp)), wb, gate_p, bp, tp)
    wo = b_w_out[0].astype(BF16)
    y_p = _nsa_out(oc_p.reshape(bp * tp, dq), os_p.reshape(bp * tp, dq), ow_p.reshape(bp * tp, dq), sz_p, xp2,
                   wo, b_norm_post[0], tm)

    n_chunk_s = kvc_s.shape[2]
    nc_s = n_chunk_s - L_CMP // CMP_STRIDE + 1
    nb_s = -(-(past + tq) // L_SLC)
    n_sk = past + LANES
    n_wk = w_keep + LANES
    m_t = _overlap_matrix(n_chunk_s, 0, nc_s, nb_s).T
    e_np = np.zeros((LANES, n_sk), np.float32)
    e_np[HEAD_DIM + np.arange(n_sk) // L_SLC, np.arange(n_sk)] = 1.0
    e_s = jnp.asarray(e_np, BF16)
    rows_s = N_HEADS * tq
    cbs = _bias_tile(rel_bias, tq, n_chunk_s, -CMP_STRIDE, past - (L_CMP - 1)).reshape(rows_s, n_chunk_s)
    sbs = _bias_tile(rel_bias, tq, n_sk, -1, past).reshape(rows_s, n_sk)
    wbs = _bias_tile(rel_bias, tq, n_wk, -1, w_keep, WINDOW).reshape(rows_s, n_wk)
    gate_r = gate_s[:, :n_gate].reshape(bd, tq, N_BRANCH, N_HEADS).transpose(0, 3, 1, 2).reshape(bd * rows_s, N_BRANCH)
    gate_r = jnp.pad(gate_r, ((0, 0), (0, LANES - N_BRANCH)))
    to_t = lambda a: a.transpose(0, 2, 3, 4, 1).reshape(a.shape[0], width, a.shape[1])
    oc_s, os_s, ow_s = _decode_attn(qa_s, kvc_s, to_t(cache_slc), pt_flat, slc_s, to_t(state_win), win_s,
                                    m_t, e_s, cbs, sbs, wbs, gate_r, bd, tq, n_pages)
    y_s = _nsa_out(oc_s, os_s, ow_s, sz_s, xs, wo, b_norm_post[0], tm_s)

    kv5 = lambda a, b, t: a.reshape(b, t, 2, KV_HEADS, HEAD_DIM)
    kv5_t = lambda a: a.reshape(bp, 2, KV_HEADS, HEAD_DIM, a.shape[2]).transpose(0, 4, 1, 2, 3)
    win_all = jnp.concatenate([state_win, kv5(win_s, bd, tq)], axis=1)
    return (y_p.reshape(bp, tp, d), y_s.reshape(bd, tq, d), jnp.stack(conv_p), jnp.stack(conv_s),
            kv5_t(cmp_p), kv5(cmp_s, bd, tq), kv5_t(slc_p), kv5(slc_s, bd, tq),
            kv5_t(win_p[:, :, -min(WINDOW, tp):]), win_all[:, -min(WINDOW, win_all.shape[1]):])
```

```python
import functools
import math

import jax
import jax.numpy as jnp
from jax import lax
from jax.experimental import pallas as pl
from jax.experimental.pallas import tpu as pltpu

F32 = jnp.float32
BF16 = jnp.bfloat16

EPS = 1e-6
NEG = -1e30
BIG = 1e9
M_FLOOR = -1e20

HEAD_DIM = 64
KV_HEADS = 4
N_HEADS = 16
GROUP = N_HEADS // KV_HEADS
N_BRANCH = 3
CONV_W = 31
L_CMP = 32
CMP_STRIDE = 16
L_SLC = 64
N_SEL = 16
WINDOW = 512
Q_BLK = 128
NUM_BUCKETS = 32
MAX_DISTANCE = 128
MAX_EXACT = NUM_BUCKETS // 2
PAGE_SIZE = 128
LANES = 128
HALO = 32
CONV_TAIL = 16

VMEM_LIMIT = 56 * 1024 * 1024


def _cparams(*sem):
    return pltpu.CompilerParams(dimension_semantics=sem, vmem_limit_bytes=VMEM_LIMIT)


def _sigmoid(x):
    return 1.0 / (1.0 + jnp.exp(-x))


def _rms(x, g):
    return x * lax.rsqrt(jnp.mean(x * x, axis=-1, keepdims=True) + EPS) * g


def _lane(shape):
    return lax.broadcasted_iota(jnp.int32, shape, len(shape) - 1)


def _swap_halves(x):
    return pltpu.roll(x, HEAD_DIM, axis=x.ndim - 1)


def _glu_proj_body(x_ref, g_ref, w_ref, glu_ref, sz_ref):
    di = glu_ref.shape[-1]
    h = _rms(x_ref[...], g_ref[...])
    u = jnp.dot(h.astype(BF16), w_ref[...], preferred_element_type=F32)
    z = u[:, 2 * di:]
    glu_ref[...] = u[:, :di] * _sigmoid(u[:, di:2 * di])
    sz_ref[...] = z * _sigmoid(z)


def _glu_proj(x, g, w_bf, tm):
    rows, d = x.shape
    di = w_bf.shape[1] // 3
    return pl.pallas_call(
        _glu_proj_body,
        grid=(rows // tm,),
        in_specs=[pl.BlockSpec((tm, d), lambda i: (i, 0)),
                  pl.BlockSpec((1, d), lambda i: (0, 0)),
                  pl.BlockSpec((d, 3 * di), lambda i: (0, 0))],
        out_specs=[pl.BlockSpec((tm, di), lambda i: (i, 0)),
                   pl.BlockSpec((tm, di), lambda i: (i, 0))],
        out_shape=[jax.ShapeDtypeStruct((rows, di), F32)] * 2,
        compiler_params=_cparams("parallel"),
        name="glu_proj",
    )(x, g.reshape(1, d), w_bf)


def _conv_tail(c, sz, x, lg_ref, lb_ref, wo_ref, gp_ref):
    mu = jnp.mean(c, axis=-1, keepdims=True)
    cc = c - mu
    var = jnp.mean(cc * cc, axis=-1, keepdims=True)
    y = cc * lax.rsqrt(var + EPS) * lg_ref[...] + lb_ref[...]
    y = y * _sigmoid(y) * sz
    o = jnp.dot(y.astype(BF16), wo_ref[...], preferred_element_type=F32)
    return x + _rms(o, gp_ref[...])


def _conv_prompt_body(glu_ref, prev_ref, sz_ref, x_ref, cw_ref, cb_ref, lg_ref, lb_ref, wo_ref, gp_ref,
                      out_ref, full_ref, c_ref, *, tm):
    t = pl.program_id(1)
    full_ref[0:HALO, :] = jnp.where(t > 0, prev_ref[0], 0.0)
    full_ref[HALO:HALO + tm, :] = glu_ref[0]
    full_ref[HALO + tm:, :] = jnp.zeros((CONV_TAIL, full_ref.shape[1]), F32)
    d = c_ref.shape[-1]
    first = HALO - (CONV_W - 1)
    sub = 8
    half = tm // 2
    for lc in range(d // LANES):
        ln = slice(lc * LANES, (lc + 1) * LANES)
        for r0 in range(0, tm, half):
            acc = jnp.zeros((half, LANES), F32)
            for b in range(sub):
                z = None
                for a in range(-(-(CONV_W - b) // sub)):
                    x = full_ref[r0 + sub * a:r0 + sub * a + half + 2 * sub, ln]
                    term = cw_ref[sub * a + b:sub * a + b + 1, ln] * x
                    z = term if z is None else z + term
                acc = acc + z[first + b:first + b + half]
            c_ref[r0:r0 + half, ln] = acc + cb_ref[:, ln]
    out_ref[0] = _conv_tail(c_ref[...], sz_ref[0], x_ref[0], lg_ref, lb_ref, wo_ref, gp_ref)


def _conv_prompt(glu, sz, x, cw, cb, lg, lb, wo_bf, gp, tm):
    b, t, d = x.shape
    di = glu.shape[-1]
    per = tm // HALO
    vec = lambda n: pl.BlockSpec((1, n), lambda i, j: (0, 0))
    return pl.pallas_call(
        functools.partial(_conv_prompt_body, tm=tm),
        grid=(b, t // tm),
        in_specs=[pl.BlockSpec((1, tm, di), lambda i, j: (i, j, 0)),
                  pl.BlockSpec((1, HALO, di), lambda i, j: (i, jnp.maximum(j * per - 1, 0), 0)),
                  pl.BlockSpec((1, tm, di), lambda i, j: (i, j, 0)),
                  pl.BlockSpec((1, tm, d), lambda i, j: (i, j, 0)),
                  pl.BlockSpec((CONV_W, di), lambda i, j: (0, 0)),
                  vec(di), vec(di), vec(di),
                  pl.BlockSpec((di, d), lambda i, j: (0, 0)),
                  vec(d)],
        out_specs=pl.BlockSpec((1, tm, d), lambda i, j: (i, j, 0)),
        out_shape=jax.ShapeDtypeStruct((b, t, d), F32),
        scratch_shapes=[pltpu.VMEM((HALO + tm + CONV_TAIL, di), F32), pltpu.VMEM((tm, di), F32)],
        compiler_params=_cparams("parallel", "arbitrary"),
        name="conv_prompt",
    )(glu, glu, sz, x, cw, cb.reshape(1, di), lg.reshape(1, di), lb.reshape(1, di), wo_bf, gp.reshape(1, d))


def _conv_sample_body(glu_ref, st_ref, sz_ref, x_ref, cw_ref, cb_ref, lg_ref, lb_ref, wo_ref, gp_ref,
                      out_ref, full_ref, c_ref, *, nb, tq):
    d = c_ref.shape[-1]
    first = HALO - (CONV_W - 1)
    full_ref[:, first:HALO, :] = st_ref[...]
    full_ref[:, HALO:HALO + tq, :] = glu_ref[...].reshape(nb, tq, d)
    for lc in range(d // LANES):
        ln = slice(lc * LANES, (lc + 1) * LANES)
        acc = jnp.zeros((nb, tq, LANES), F32)
        for j in range(CONV_W):
            acc = acc + cw_ref[j:j + 1, ln] * full_ref[:, first + j:first + j + tq, ln]
        c_ref[:, ln] = (acc + cb_ref[:, ln]).reshape(nb * tq, LANES)
    out_ref[...] = _conv_tail(c_ref[...], sz_ref[...], x_ref[...], lg_ref, lb_ref, wo_ref, gp_ref)


def _conv_sample(glu, state, sz, x, cw, cb, lg, lb, wo_bf, gp, nb):
    n_seq = state.shape[0]
    rows, d = x.shape
    di = glu.shape[-1]
    tq = rows // n_seq
    vec = lambda n: pl.BlockSpec((1, n), lambda i: (0, 0))
    return pl.pallas_call(
        functools.partial(_conv_sample_body, nb=nb, tq=tq),
        grid=(n_seq // nb,),
        in_specs=[pl.BlockSpec((nb * tq, di), lambda i: (i, 0)),
                  pl.BlockSpec((nb, CONV_W - 1, di), lambda i: (i, 0, 0)),
                  pl.BlockSpec((nb * tq, di), lambda i: (i, 0)),
                  pl.BlockSpec((nb * tq, d), lambda i: (i, 0)),
                  pl.BlockSpec((CONV_W, di), lambda i: (0, 0)),
                  vec(di), vec(di), vec(di),
                  pl.BlockSpec((di, d), lambda i: (0, 0)),
                  vec(d)],
        out_specs=pl.BlockSpec((nb * tq, d), lambda i: (i, 0)),
        out_shape=jax.ShapeDtypeStruct((rows, d), F32),
        scratch_shapes=[pltpu.VMEM((nb, HALO + tq, di), F32), pltpu.VMEM((nb * tq, di), F32)],
        compiler_params=_cparams("parallel"),
        name="conv_sample",
    )(glu, state, sz, x, cw, cb.reshape(1, di), lg.reshape(1, di), lb.reshape(1, di), wo_bf, gp.reshape(1, d))


def _head_major(col_pair, head):
    return col_pair if head % 2 == 0 else _swap_halves(col_pair)


def _nsa_proj_body(x_ref, gkv_ref, gq_ref, wkv_ref, wq_ref, wz_ref, wg_ref,
                   cmp_ref, slc_ref, win_ref, qa_ref, sz_ref, gate_ref, *aug_refs, seq_len, tm):
    x = x_ref[...]
    xn = x * lax.rsqrt(jnp.mean(x * x, axis=-1, keepdims=True) + EPS)
    hkv = (xn * gkv_ref[...]).astype(BF16)
    hq = (xn * gq_ref[...]).astype(BF16)
    kv = jnp.dot(hkv, wkv_ref[...], preferred_element_type=F32)
    width = 2 * KV_HEADS * HEAD_DIM
    if aug_refs:
        kv_t = [kv[:, b * width:(b + 1) * width].T for b in range(N_BRANCH)]
        cmp_ref[0] = kv_t[0]
        slc_ref[0] = kv_t[1]
        win_ref[0] = kv_t[2]
    else:
        cmp_ref[...] = kv[:, :width]
        slc_ref[...] = kv[:, width:2 * width]
        win_ref[...] = kv[:, 2 * width:]
    lane = _lane((tm, LANES))
    low = lane < HEAD_DIM
    uq = jnp.dot(hq, wq_ref[...], preferred_element_type=F32)
    scale = HEAD_DIM ** -0.5
    for h in range(N_HEADS):
        qh = _head_major(uq[:, (h // 2) * LANES:(h // 2 + 1) * LANES], h)
        qa_ref[:, h * LANES:(h + 1) * LANES] = jnp.where(low, qh * scale, 0.0).astype(BF16)
    z = jnp.dot(hq, wz_ref[...], preferred_element_type=F32)
    sz_ref[...] = z * _sigmoid(z)
    gate_ref[...] = _sigmoid(jnp.dot(hq, wg_ref[...], preferred_element_type=F32))
    if aug_refs:
        ska_ref, sv_ref, wka_ref, wv_ref = aug_refs
        row = pl.program_id(0) * tm + lax.broadcasted_iota(jnp.int32, (tm, LANES), 0)
        blk = (row % seq_len) // L_SLC
        onehot = (lane - HEAD_DIM == blk).astype(F32)
        kw = KV_HEADS * HEAD_DIM
        for h in range(KV_HEADS):
            ks = _head_major(kv[:, width + (h // 2) * LANES:width + (h // 2 + 1) * LANES], h)
            ska_ref[:, h * LANES:(h + 1) * LANES] = jnp.where(low, ks, onehot).astype(BF16)
            kwn = _head_major(kv[:, 2 * width + (h // 2) * LANES:2 * width + (h // 2 + 1) * LANES], h)
            wka_ref[:, h * LANES:(h + 1) * LANES] = jnp.where(low, kwn, 0.0).astype(BF16)
        extra = (lax.broadcasted_iota(jnp.int32, (V_PAD, tm), 0) == 0).astype(BF16)
        for dst, src in ((sv_ref, kv_t[1]), (wv_ref, kv_t[2])):
            for h in range(KV_HEADS):
                dst[h * V_ROWS:h * V_ROWS + HEAD_DIM, :] = src[kw + h * HEAD_DIM:kw + (h + 1) * HEAD_DIM].astype(BF16)
                dst[h * V_ROWS + HEAD_DIM:(h + 1) * V_ROWS, :] = extra


def _nsa_proj(x, g_kv, g_q, wkv_bf, wq_bf, wz_bf, wg_bf, tm, seq_len=None):
    rows, d = x.shape
    width = 2 * KV_HEADS * HEAD_DIM
    row = lambda n: pl.BlockSpec((tm, n), lambda i: (i, 0))
    full = lambda a: pl.BlockSpec(a.shape, lambda i: (0, 0))
    out_specs = [row(width), row(width), row(width), row(N_HEADS * LANES), row(wz_bf.shape[1]), row(LANES)]
    out_shape = [jax.ShapeDtypeStruct((rows, width), F32)] * 3 + [
        jax.ShapeDtypeStruct((rows, N_HEADS * LANES), BF16),
        jax.ShapeDtypeStruct((rows, wz_bf.shape[1]), F32),
        jax.ShapeDtypeStruct((rows, LANES), F32)]
    if seq_len is not None:
        per_seq = seq_len // tm
        kv_t_spec = pl.BlockSpec((1, width, tm), lambda i: (i // per_seq, 0, i % per_seq))
        out_specs[:N_BRANCH] = [kv_t_spec] * N_BRANCH
        out_shape[:N_BRANCH] = [jax.ShapeDtypeStruct((rows // seq_len, width, seq_len), F32)] * N_BRANCH
        out_specs += [row(KV_HEADS * LANES), pl.BlockSpec((KV_HEADS * V_ROWS, tm), lambda i: (0, i))] * 2
        out_shape += [jax.ShapeDtypeStruct((rows, KV_HEADS * LANES), BF16),
                      jax.ShapeDtypeStruct((KV_HEADS * V_ROWS, rows), BF16)] * 2
    gkv = g_kv.reshape(1, d)
    gq = g_q.reshape(1, d)
    return pl.pallas_call(
        functools.partial(_nsa_proj_body, seq_len=seq_len, tm=tm),
        grid=(rows // tm,),
        in_specs=[row(d), full(gkv), full(gq), full(wkv_bf), full(wq_bf), full(wz_bf), full(wg_bf)],
        out_specs=out_specs,
        out_shape=out_shape,
        compiler_params=_cparams("parallel"),
        name="nsa_proj",
    )(x, gkv, gq, wkv_bf, wq_bf, wz_bf, wg_bf)


def _compress_body(pt_ref, *refs, n_pages):
    del pt_ref
    page_refs = refs[:n_pages]
    w1_ref, pe_ref, w2_ref, out_ref, a_ref = refs[n_pages:]
    width, page_rows = page_refs[0].shape[1:]
    n_col = width // LANES
    cpp = page_rows // CMP_STRIDE
    n_chunk = n_pages * cpp
    hid = w2_ref.shape[2]
    low = _lane((cpp, LANES)) < HEAD_DIM
    def assemble(c):
        for p in range(n_pages):
            col_t = page_refs[p][0, c * LANES:(c + 1) * LANES, :]
            by_s = pltpu.einshape("(ns)f->(sn)f", col_t.T, n=cpp)
            for s in range(0, CMP_STRIDE, 2):
                b0 = by_s[s * cpp:(s + 1) * cpp]
                b1 = by_s[(s + 1) * cpp:(s + 2) * cpp]
                dst = (slice(p * cpp, (p + 1) * cpp), slice((s // 2) * LANES, (s // 2 + 1) * LANES))
                a_ref[(2 * c,) + dst] = jnp.where(low, b0, _swap_halves(b1))
                a_ref[(2 * c + 1,) + dst] = jnp.where(low, _swap_halves(b0), b1)

    for kv in range(2):
        for c in range(kv * n_col // 2, (kv + 1) * n_col // 2):
            assemble(c)
        w1 = w1_ref[kv]
        pe = jnp.dot(pe_ref[kv].astype(BF16), w1, preferred_element_type=F32)
        pe_term = pe[0:1, :hid] + pe[1:2, hid:]
        a_kv = a_ref[kv * KV_HEADS:(kv + 1) * KV_HEADS].reshape(KV_HEADS * n_chunk, CMP_STRIDE * HEAD_DIM)
        parts = jnp.dot(a_kv.astype(BF16), w1, preferred_element_type=F32)
        for hp in range(KV_HEADS // 2):
            pair = jnp.zeros((n_chunk, LANES), F32)
            for par in range(2):
                part = parts[(2 * hp + par) * n_chunk:(2 * hp + par + 1) * n_chunk]
                pre = part[:, :hid] + pltpu.roll(part[:, hid:], n_chunk - 1, axis=0) + pe_term
                mid = pre * _sigmoid(pre)
                pair = pair + jnp.dot(mid.astype(BF16), w2_ref[kv, par], preferred_element_type=F32)
            col = kv * (KV_HEADS // 2) + hp
            out_ref[0, col * LANES:(col + 1) * LANES, :] = pair.T


def _compress(rows_t, page_table, w1cat_bf, pe8, w2_bf, n_pages, page_rows):
    width = rows_t.shape[1]
    n_chunk = n_pages * page_rows // CMP_STRIDE
    if page_table is None:
        n_seq = rows_t.shape[0]
        page_table = jnp.zeros((1,), jnp.int32)
        page_spec = lambda j: pl.BlockSpec((1, width, page_rows), lambda i, pt: (i, 0, j))
    else:
        n_seq = page_table.shape[0] // n_pages
        page_spec = lambda j: pl.BlockSpec((1, width, page_rows), lambda i, pt: (pt[i * n_pages + j], 0, 0))
    full = lambda a: pl.BlockSpec(a.shape, lambda i, pt: (0,) * a.ndim)
    grid_spec = pltpu.PrefetchScalarGridSpec(
        num_scalar_prefetch=1,
        grid=(n_seq,),
        in_specs=[page_spec(j) for j in range(n_pages)] + [full(w1cat_bf), full(pe8), full(w2_bf)],
        out_specs=pl.BlockSpec((1, width, n_chunk), lambda i, pt: (i, 0, 0)),
        scratch_shapes=[pltpu.VMEM((2 * KV_HEADS, n_chunk, CMP_STRIDE * HEAD_DIM), F32)],
    )
    return pl.pallas_call(
        functools.partial(_compress_body, n_pages=n_pages),
        grid_spec=grid_spec,
        out_shape=jax.ShapeDtypeStruct((n_seq, width, n_chunk), F32),
        compiler_params=_cparams("parallel"),
        name="compress",
    )(page_table, *([rows_t] * n_pages), w1cat_bf, pe8, w2_bf)


def _bias_tile_body(rb_ref, out_ref, *, row_step, lane_step, offset, hi):
    h = pl.program_id(0)
    shape = out_ref.shape[1:]
    d = row_step * lax.broadcasted_iota(jnp.int32, shape, 0) + lane_step * _lane(shape) + offset
    n = jnp.maximum(d, 0)
    nf = jnp.maximum(n, 1).astype(F32)
    large = MAX_EXACT + (jnp.log(nf / MAX_EXACT) / math.log(MAX_DISTANCE / MAX_EXACT)
                         * (NUM_BUCKETS - MAX_EXACT)).astype(jnp.int32)
    large = jnp.minimum(large, NUM_BUCKETS - 1)
    bucket = jnp.where(n < MAX_EXACT, n, large)
    far = rb_ref[NUM_BUCKETS - 1, h]
    val = jnp.zeros(shape, F32)
    for k in range(NUM_BUCKETS - 1):
        val = jnp.where(bucket == k, rb_ref[k, h] - far, val)
    out_ref[0] = jnp.where((d >= 0) & (d <= hi), val, NEG)


def _bias_tile(rel_bias, rows, width, lane_step, offset, hi=1 << 30, row_step=1):
    return pl.pallas_call(
        functools.partial(_bias_tile_body, row_step=row_step, lane_step=lane_step, offset=offset, hi=hi),
        grid=(N_HEADS,),
        in_specs=[pl.BlockSpec(memory_space=pltpu.SMEM)],
        out_specs=pl.BlockSpec((1, rows, width), lambda h: (h, 0, 0)),
        out_shape=jax.ShapeDtypeStruct((N_HEADS, rows, width), F32),
        compiler_params=_cparams("parallel"),
        name="bias_tile",
    )(rel_bias)


_NT = (((1,), (1,)), ((), ()))


def _softmax_rows(s):
    m = jnp.maximum(jnp.max(s, axis=-1, keepdims=True), M_FLOOR)
    p = jnp.exp(s - m)
    return p, jnp.sum(p, axis=-1, keepdims=True)


def _gate_col(gate, col):
    return jnp.sum(jnp.where(_lane(gate.shape) == col, gate, 0.0), axis=-1, keepdims=True)


def _pair_columns(o_even, o_odd, valid_half):
    lo = o_even if valid_half == 0 else _swap_halves(o_even)
    hi = o_odd if valid_half == 1 else _swap_halves(o_odd)
    return jnp.where(_lane(lo.shape) < HEAD_DIM, lo, hi)


def _select_blocks_t(imp_t, tb, n_sel):
    nb, nt = imp_t.shape
    sub = 8
    j = lax.broadcasted_iota(jnp.int32, (nb, nt), 0)
    valid = j <= tb
    forced = (j == 0) | (j == tb) | (j == tb - 1)
    score = jnp.where(valid, jnp.where(forced, BIG, imp_t), -BIG)
    groups = [score[lo:lo + sub] for lo in range(0, nb, sub)]
    jr = lax.broadcasted_iota(jnp.int32, (sub, nt), 0)
    counts = [jnp.zeros((sub, nt), F32) for _ in groups]
    for i in range(nb):
        row = score[i:i + 1]
        for r, grp in enumerate(groups):
            lo = r * sub
            if lo > i:
                one = jnp.where(row >= grp, 1.0, 0.0)
            elif lo + sub - 1 <= i:
                one = jnp.where(row > grp, 1.0, 0.0)
            else:
                one = jnp.where(jr + lo > i, jnp.where(row >= grp, 1.0, 0.0), jnp.where(row > grp, 1.0, 0.0))
            counts[r] = counts[r] + one
    rank = jnp.concatenate(counts, axis=0)
    return jnp.where((rank < n_sel) & valid, 0.0, NEG)


def _cmp_prompt_body(qa_ref, kc_ref, vc_ref, m_ref, cb_ref, gate_ref, oc_ref, qsel_ref, qflag_ref, *, n_key):
    qt = pl.program_id(1)
    start = pl.multiple_of(8 * qt + 8, 8)
    kwin = kc_ref[0, pl.ds(start, n_key), :]
    vwin = vc_ref[0, pl.ds(start, n_key), :]
    mwin_t = m_ref[pl.ds(start, n_key), :].T
    feat = lax.broadcasted_iota(jnp.int32, (LANES, Q_BLK), 0)
    pos = qt * Q_BLK + lax.broadcasted_iota(jnp.int32, (1, Q_BLK), 1)
    tb = pos // L_SLC
    gate = gate_ref[...]
    n_pair = GROUP // 2
    imps, q_feats = [], []
    for kvh in range(KV_HEADS):
        ka = kwin[:, kvh * LANES:(kvh + 1) * LANES].astype(BF16)
        vpt = vwin[:, (kvh // 2) * LANES:(kvh // 2 + 1) * LANES].T
        vpt = vpt[(kvh % 2) * HEAD_DIM:(kvh % 2 + 1) * HEAD_DIM].astype(BF16)
        qts = [qa_ref[0, :, h * LANES:(h + 1) * LANES].astype(F32).T for h in range(kvh * GROUP, (kvh + 1) * GROUP)]
        pc_sum = jnp.zeros((n_key, Q_BLK), F32)
        for gp in range(n_pair):
            qa = jnp.concatenate([jnp.where(feat == HEAD_DIM, NEG, qts[2 * gp + e]) for e in range(2)], axis=1)
            qa = qa.astype(BF16)
            qflag_ref[0, kvh, :, 2 * gp * Q_BLK:(2 * gp + 2) * Q_BLK] = qa
            s = jnp.dot(ka, qa, preferred_element_type=F32)
            s = s + cb_ref[kvh, :, 2 * gp * Q_BLK:(2 * gp + 2) * Q_BLK]
            m = jnp.maximum(jnp.max(s, axis=0, keepdims=True), M_FLOOR)
            p = jnp.exp(s - m)
            inv = 1.0 / jnp.maximum(jnp.sum(p, axis=0, keepdims=True), 1e-30)
            pc = p * inv
            pc_sum = pc_sum + pc[:, :Q_BLK] + pc[:, Q_BLK:]
            o = jnp.dot(vpt, p.astype(BF16), preferred_element_type=F32) * inv
            _store_chain(oc_ref, o, gate, kvh * GROUP + 2 * gp, kvh * n_pair + gp)
        imps.append(jnp.dot(mwin_t, pc_sum, preferred_element_type=F32, precision=lax.Precision.HIGHEST)[HEAD_DIM:])
        q_feats.append([qt_[:HEAD_DIM] for qt_ in qts])
    n_blk = imps[0].shape[0]
    n_valid = (qt + 1) * (Q_BLK // L_SLC)
    step = n_blk // 4
    for n_rank in range(step, n_blk + 1, step):
        @pl.when((n_valid > n_rank - step) & (n_valid <= n_rank))
        def _():
            for kvh in range(KV_HEADS):
                selneg_t = _select_blocks_t(imps[kvh][:n_rank], tb, N_SEL)
                if n_rank < n_blk:
                    selneg_t = jnp.concatenate([selneg_t, jnp.full((n_blk - n_rank, Q_BLK), NEG, F32)], axis=0)
                for g in range(GROUP):
                    qsel_ref[0, kvh, :, g * Q_BLK:(g + 1) * Q_BLK] = jnp.concatenate(
                        [q_feats[kvh][g], selneg_t], axis=0).astype(BF16)


def _cmp_prompt(qa, kc_pad, vc_pad, m_pad, cb, gate, batch, seq):
    nq = seq // Q_BLK
    n_key = kc_pad.shape[1] // 2
    d_out = N_HEADS * HEAD_DIM
    return pl.pallas_call(
        functools.partial(_cmp_prompt_body, n_key=n_key),
        grid=(batch, nq),
        in_specs=[pl.BlockSpec((1, Q_BLK, N_HEADS * LANES), lambda b, t: (b, t, 0)),
                  pl.BlockSpec((1,) + kc_pad.shape[1:], lambda b, t: (b, 0, 0)),
                  pl.BlockSpec((1,) + vc_pad.shape[1:], lambda b, t: (b, 0, 0)),
                  pl.BlockSpec(m_pad.shape, lambda b, t: (0, 0)),
                  pl.BlockSpec(cb.shape, lambda b, t: (0, 0, 0)),
                  pl.BlockSpec((Q_BLK, LANES), lambda b, t: (b * nq + t, 0))],
        out_specs=[pl.BlockSpec((1, Q_BLK, d_out), lambda b, t: (b, t, 0))]
        + [pl.BlockSpec((1, KV_HEADS, LANES, GROUP * Q_BLK), lambda b, t: (b * nq + t, 0, 0, 0))] * 2,
        out_shape=[jax.ShapeDtypeStruct((batch, seq, d_out), F32)]
        + [jax.ShapeDtypeStruct((batch * nq, KV_HEADS, LANES, GROUP * Q_BLK), BF16)] * 2,
        compiler_params=_cparams("parallel", "parallel"),
        name="cmp_prompt",
    )(qa.reshape(batch, seq, -1), kc_pad, vc_pad, m_pad, cb, gate)


KEY_TILE = 256
COL_CHAIN = 256
WIN_TILES = 2
V_PAD = 16
V_ROWS = HEAD_DIM + V_PAD


def _store_chain(out_ref, o_t, gate, gate_col0, lane_col, row0=0):
    pair = jnp.concatenate([o_t[:, :Q_BLK], o_t[:, Q_BLK:]], axis=0).T
    g = jnp.where(_lane(pair.shape) < HEAD_DIM, _gate_col(gate, gate_col0), _gate_col(gate, gate_col0 + 1))
    out_ref[0, row0:row0 + Q_BLK, lane_col * LANES:(lane_col + 1) * LANES] = pair * g


_PAIR_CHAINS = [(e, c) for e in range(2) for c in range(GROUP * Q_BLK // COL_CHAIN)]


def _slc_prompt_body(q_ref, k_ref, v_ref, sb_ref, gate_ref, out_ref, m_ref, acc_ref, p_ref, s_ref):
    pr = pl.program_id(1)
    qt = pl.program_id(2)
    m_ref[...] = jnp.full(m_ref.shape, M_FLOOR, F32)
    acc_ref[...] = jnp.zeros(acc_ref.shape, F32)
    p_ref[...] = jnp.zeros(p_ref.shape, BF16)
    last = (qt * Q_BLK) // KEY_TILE
    cs = lambda c: slice(c * COL_CHAIN, (c + 1) * COL_CHAIN)

    def pv_of(kt):
        base = pl.multiple_of(kt * KEY_TILE, KEY_TILE)
        return [jnp.dot(v_ref[e * V_ROWS:(e + 1) * V_ROWS, pl.ds(base, KEY_TILE)], p_ref[i],
                        preferred_element_type=F32) for i, (e, c) in enumerate(_PAIR_CHAINS)]

    def scores_of(kt):
        base = pl.multiple_of(kt * KEY_TILE, KEY_TILE)
        ks = [k_ref[0, pl.ds(base, KEY_TILE), e * LANES:(e + 1) * LANES] for e in range(2)]
        return [jnp.dot(ks[e], q_ref[0, e, :, cs(c)], preferred_element_type=F32) for e, c in _PAIR_CHAINS]

    def make_step(with_bias):
        def step(kt, carry):
            s_next = scores_of(jnp.minimum(kt + 1, last))
            pvs = pv_of(jnp.maximum(kt - 1, 0))
            ss = [s_ref[i] for i in range(len(_PAIR_CHAINS))]
            if with_bias:
                case = qt % 2 + 2 * (last - kt)
                ss = [s + sb_ref[case, 2 * pr + e, :, cs(c)] for s, (e, c) in zip(ss, _PAIR_CHAINS)]
            alphas = []
            for i, s in enumerate(ss):
                m_old = m_ref[i]
                m_new = jnp.maximum(m_old, jnp.max(s, axis=0, keepdims=True))
                alphas.append(jnp.exp(m_old - m_new))
                m_ref[i] = m_new
                p_ref[i] = jnp.exp(s - m_new).astype(BF16)
            for i in range(len(_PAIR_CHAINS)):
                acc_ref[i] = alphas[i] * (acc_ref[i] + pvs[i])
                s_ref[i] = s_next[i]
            return carry
        return step

    for i, s in enumerate(scores_of(0)):
        s_ref[i] = s
    n_far = jnp.maximum(last - 1, 0)
    lax.fori_loop(0, n_far, make_step(False), 0)
    lax.fori_loop(n_far, last + 1, make_step(True), 0)
    gate = gate_ref[...]
    pvs = pv_of(last)
    for i, (e, c) in enumerate(_PAIR_CHAINS):
        o = acc_ref[i] + pvs[i]
        o = o[:HEAD_DIM] * (1.0 / jnp.maximum(o[HEAD_DIM:HEAD_DIM + 1], 1e-30))
        _store_chain(out_ref, o, gate, N_HEADS + (2 * pr + e) * GROUP + 2 * c, i)


def _slc_prompt(qsel_t, ska, sv_t, sb_t, gate, batch, seq):
    nq = seq // Q_BLK
    cols = GROUP * Q_BLK
    d_out = N_HEADS * HEAD_DIM
    n_chain = len(_PAIR_CHAINS)
    return pl.pallas_call(
        _slc_prompt_body,
        grid=(batch, KV_HEADS // 2, nq),
        in_specs=[pl.BlockSpec((1, 2, LANES, cols), lambda b, p, t: (b * nq + t, p, 0, 0)),
                  pl.BlockSpec((1, seq, 2 * LANES), lambda b, p, t: (b, 0, p)),
                  pl.BlockSpec((2 * V_ROWS, seq), lambda b, p, t: (p, b)),
                  pl.BlockSpec(sb_t.shape, lambda b, p, t: (0, 0, 0, 0)),
                  pl.BlockSpec((Q_BLK, LANES), lambda b, p, t: (b * nq + t, 0))],
        out_specs=pl.BlockSpec((1, Q_BLK, 2 * GROUP * HEAD_DIM), lambda b, p, t: (b, t, p)),
        out_shape=jax.ShapeDtypeStruct((batch, seq, d_out), F32),
        scratch_shapes=[pltpu.VMEM((n_chain, 1, COL_CHAIN), F32),
                        pltpu.VMEM((n_chain, V_ROWS, COL_CHAIN), F32),
                        pltpu.VMEM((n_chain, KEY_TILE, COL_CHAIN), BF16),
                        pltpu.VMEM((n_chain, KEY_TILE, COL_CHAIN), F32)],
        compiler_params=_cparams("parallel", "parallel", "arbitrary"),
        name="slc_prompt",
    )(qsel_t, ska.reshape(batch, seq, -1), sv_t, sb_t, gate)


def _win_prompt_body(q_ref, k_ref, v_ref, wb_ref, gate_ref, out_ref, *, n_key):
    pr = pl.program_id(1)
    t0 = pl.program_id(2) * WIN_TILES
    cs = lambda c: slice(c * COL_CHAIN, (c + 1) * COL_CHAIN)
    chains = [(u, e, c) for u in range(WIN_TILES) for e, c in _PAIR_CHAINS]
    bases = [pl.multiple_of((t0 + u) * Q_BLK, Q_BLK) for u in range(WIN_TILES)]
    ss = []
    for u in range(WIN_TILES):
        ks = [k_ref[0, pl.ds(bases[u], n_key), e * LANES:(e + 1) * LANES] for e in range(2)]
        ss += [jnp.dot(ks[e], q_ref[u, e, :, cs(c)], preferred_element_type=F32) + wb_ref[2 * pr + e, :, cs(c)]
               for e, c in _PAIR_CHAINS]
    ps = []
    for s in ss:
        m = jnp.maximum(jnp.max(s, axis=0, keepdims=True), M_FLOOR)
        ps.append(jnp.exp(s - m).astype(BF16))
    for i, (u, e, c) in enumerate(chains):
        o = jnp.dot(v_ref[e * V_ROWS:(e + 1) * V_ROWS, pl.ds(bases[u], n_key)], ps[i], preferred_element_type=F32)
        o = o[:HEAD_DIM] * (1.0 / jnp.maximum(o[HEAD_DIM:HEAD_DIM + 1], 1e-30))
        gate = gate_ref[u * Q_BLK:(u + 1) * Q_BLK, :]
        _store_chain(out_ref, o, gate, 2 * N_HEADS + (2 * pr + e) * GROUP + 2 * c, i % len(_PAIR_CHAINS), u * Q_BLK)


def _win_prompt(q_t, wka_pad, wv_t_pad, wb_t, gate, batch, seq):
    nq = seq // Q_BLK
    n_key = WINDOW + Q_BLK
    cols = GROUP * Q_BLK
    d_out = N_HEADS * HEAD_DIM
    padded = wka_pad.shape[1]
    return pl.pallas_call(
        functools.partial(_win_prompt_body, n_key=n_key),
        grid=(batch, KV_HEADS // 2, nq // WIN_TILES),
        in_specs=[pl.BlockSpec((WIN_TILES, 2, LANES, cols), lambda b, p, t: (b * (nq // WIN_TILES) + t, p, 0, 0)),
                  pl.BlockSpec((1, padded, 2 * LANES), lambda b, p, t: (b, 0, p)),
                  pl.BlockSpec((2 * V_ROWS, padded), lambda b, p, t: (p, b)),
                  pl.BlockSpec(wb_t.shape, lambda b, p, t: (0, 0, 0)),
                  pl.BlockSpec((WIN_TILES * Q_BLK, LANES), lambda b, p, t: (b * (nq // WIN_TILES) + t, 0))],
        out_specs=pl.BlockSpec((1, WIN_TILES * Q_BLK, 2 * GROUP * HEAD_DIM), lambda b, p, t: (b, t, p)),
        out_shape=jax.ShapeDtypeStruct((batch, seq, d_out), F32),
        compiler_params=_cparams("parallel", "parallel", "parallel"),
        name="win_prompt",
    )(q_t, wka_pad, wv_t_pad, wb_t, gate)


def _nsa_out_body(oc_ref, os_ref, ow_ref, sz_ref, x_ref, wo_ref, gp_ref, out_ref):
    d = oc_ref.shape[-1]
    y = oc_ref[...] * sz_ref[:, :d] + os_ref[...] * sz_ref[:, d:2 * d] + ow_ref[...] * sz_ref[:, 2 * d:]
    o = jnp.dot(y.astype(BF16), wo_ref[...], preferred_element_type=F32)
    out_ref[...] = x_ref[...] + _rms(o, gp_ref[...])


def _nsa_out(oc, os_, ow, sz, x, wo_bf, gp, tm):
    rows, d = x.shape
    dq = oc.shape[-1]
    row = lambda n: pl.BlockSpec((tm, n), lambda i: (i, 0))
    return pl.pallas_call(
        _nsa_out_body,
        grid=(rows // tm,),
        in_specs=[row(dq), row(dq), row(dq), row(N_BRANCH * dq), row(d),
                  pl.BlockSpec((dq, d), lambda i: (0, 0)), pl.BlockSpec((1, d), lambda i: (0, 0))],
        out_specs=row(d),
        out_shape=jax.ShapeDtypeStruct((rows, d), F32),
        compiler_params=_cparams("parallel"),
        name="nsa_out",
    )(oc, os_, ow, sz, x, wo_bf, gp.reshape(1, d))


def _decode_attn_body(pt_ref, *refs, n_pages, tq, w_keep):
    del pt_ref
    page_refs = refs[:n_pages]
    (qa_ref, kvc_ref, snew_ref, wst_ref, wnew_ref, mt_ref, e_ref, cbs_ref, sbs_ref, wbs_ref, gate_ref,
     oc_ref, os_ref, ow_ref, kt_ref, vt_ref, kwt_ref, vwt_ref) = refs[n_pages:]
    n_past = n_pages * PAGE_SIZE
    kw = KV_HEADS * HEAD_DIM
    rows = N_HEADS * tq

    def new_tile(ref):
        return jnp.concatenate([ref[...], jnp.zeros((LANES - tq, 2 * kw), F32)], axis=0).T

    def fill(k_dst, v_dst, col0, tile):
        k_dst[:, col0:col0 + tile.shape[1]] = tile[:kw].astype(BF16)
        v_dst[:, col0:col0 + tile.shape[1]] = tile[kw:].astype(BF16)

    for p in range(n_pages):
        fill(kt_ref, vt_ref, p * PAGE_SIZE, page_refs[p][0])
    fill(kt_ref, vt_ref, n_past, new_tile(snew_ref))
    fill(kwt_ref, vwt_ref, 0, wst_ref[0])
    fill(kwt_ref, vwt_ref, w_keep, new_tile(wnew_ref))

    zero = jnp.zeros((tq, LANES), F32)
    tiles = [[], []]
    for h in range(N_HEADS):
        kvh = h // GROUP
        piece = qa_ref[:, h * LANES:(h + 1) * LANES]
        piece = piece if kvh % 2 == 0 else _swap_halves(piece)
        for ct in range(2):
            tiles[ct].append(piece if kvh // 2 == ct else zero)
    qbd = jnp.concatenate([jnp.concatenate(t, axis=0) for t in tiles], axis=1).astype(BF16)
    gate = gate_ref[...]

    def branch(k_t, v_t, bias, extra=None):
        s = jnp.dot(qbd, k_t, preferred_element_type=F32) + bias
        if extra is not None:
            s = s + extra
        p, l = _softmax_rows(s)
        inv = 1.0 / jnp.maximum(l, 1e-30)
        o_t = lax.dot_general(v_t, p.astype(BF16), _NT, preferred_element_type=F32)
        return p, inv, o_t.T

    def store(out_ref, o, scale):
        o = o * scale
        for kvh in range(KV_HEADS):
            tile = o[:, (kvh // 2) * LANES:(kvh // 2 + 1) * LANES]
            for gp in range(GROUP // 2):
                h0 = kvh * GROUP + 2 * gp
                col = kvh * (GROUP // 2) + gp
                out_ref[:, col * LANES:(col + 1) * LANES] = _pair_columns(
                    tile[h0 * tq:(h0 + 1) * tq], tile[(h0 + 1) * tq:(h0 + 2) * tq], kvh % 2)

    p, inv, o = branch(kvc_ref[0, :kw, :].astype(BF16), kvc_ref[0, kw:, :].astype(BF16), cbs_ref[...])
    store(oc_ref, o, inv * _gate_col(gate, 0))
    pc = p * inv
    pc_sum = jnp.concatenate(
        [sum(pc[(kvh * GROUP + g) * tq:(kvh * GROUP + g + 1) * tq] for g in range(GROUP)) for kvh in range(KV_HEADS)],
        axis=0)
    imp_t = lax.dot_general(mt_ref[...], pc_sum, _NT, preferred_element_type=F32, precision=lax.Precision.HIGHEST)
    n_col = KV_HEADS * tq
    nb_pad = -(-(-(-(n_past + tq) // L_SLC)) // 8) * 8
    tb = (n_past + lax.broadcasted_iota(jnp.int32, (1, n_col), 1) % tq) // L_SLC
    selneg_t = _select_blocks_t(imp_t[HEAD_DIM:HEAD_DIM + nb_pad], tb, N_SEL)
    sel_t = jnp.concatenate([jnp.zeros((HEAD_DIM, n_col), F32), selneg_t,
                             jnp.zeros((LANES - HEAD_DIM - nb_pad, n_col), F32)], axis=0)
    sel = jnp.concatenate([sel_t, jnp.zeros((LANES, LANES - n_col), F32)], axis=1).T
    sel = jnp.concatenate([sel[kvh * tq:(kvh + 1) * tq] for kvh in range(KV_HEADS) for _ in range(GROUP)], axis=0)
    block_mask = jnp.dot(sel.astype(BF16), e_ref[...], preferred_element_type=F32)
    _, inv, o = branch(kt_ref[...], vt_ref[...], sbs_ref[...], block_mask)
    store(os_ref, o, inv * _gate_col(gate, 1))
    _, inv, o = branch(kwt_ref[...], vwt_ref[...], wbs_ref[...])
    store(ow_ref, o, inv * _gate_col(gate, 2))


def _decode_attn(qa, kvc_t, pages_t, page_table, slc_new, win_state_t, win_new, m_t, e_s, cbs, sbs, wbs, gate_r,
                 n_seq, tq, n_pages):
    width = pages_t.shape[1]
    n_sk = sbs.shape[-1]
    n_wk = wbs.shape[-1]
    w_keep = win_state_t.shape[2]
    d_out = N_HEADS * HEAD_DIM
    rows = N_HEADS * tq
    page_spec = lambda j: pl.BlockSpec((1, width, PAGE_SIZE), lambda i, pt: (pt[i * n_pages + j], 0, 0))
    full = lambda a: pl.BlockSpec(a.shape, lambda i, pt: (0,) * a.ndim)
    row = lambda n: pl.BlockSpec((tq, n), lambda i, pt: (i, 0))
    grid_spec = pltpu.PrefetchScalarGridSpec(
        num_scalar_prefetch=1,
        grid=(n_seq,),
        in_specs=[page_spec(j) for j in range(n_pages)] + [
            row(N_HEADS * LANES),
            pl.BlockSpec((1,) + kvc_t.shape[1:], lambda i, pt: (i, 0, 0)),
            row(width),
            pl.BlockSpec((1, width, w_keep), lambda i, pt: (i, 0, 0)),
            row(width),
            full(m_t), full(e_s), full(cbs), full(sbs), full(wbs),
            pl.BlockSpec((rows, LANES), lambda i, pt: (i, 0))],
        out_specs=[row(d_out)] * 3,
        scratch_shapes=[pltpu.VMEM((width // 2, n_sk), BF16), pltpu.VMEM((width // 2, n_sk), BF16),
                        pltpu.VMEM((width // 2, n_wk), BF16), pltpu.VMEM((width // 2, n_wk), BF16)],
    )
    return pl.pallas_call(
        functools.partial(_decode_attn_body, n_pages=n_pages, tq=tq, w_keep=w_keep),
        grid_spec=grid_spec,
        out_shape=[jax.ShapeDtypeStruct((n_seq * tq, d_out), F32)] * 3,
        compiler_params=_cparams("parallel"),
        name="decode_attn",
    )(page_table, *([pages_t] * n_pages), qa, kvc_t, slc_new, win_state_t, win_new, m_t, e_s, cbs, sbs, wbs, gate_r)


def _overlap_matrix(n_rows, row0, n_cmp, n_blk):
    import numpy as np
    m = np.zeros((n_rows, LANES), np.float32)
    cs = np.arange(n_cmp)[:, None] * CMP_STRIDE
    js = np.arange(n_blk)[None, :] * L_SLC
    m[row0:row0 + n_cmp, HEAD_DIM:HEAD_DIM + n_blk] = (cs <= js + L_SLC - 1) & (cs + L_CMP - 1 >= js)
    return jnp.asarray(m)


def kernel(x_prompt, x_sample, state_conv, cache_cmp, cache_slc, state_win, page_table, rel_bias, a_norm_pre, a_w_in, a_conv_w, a_conv_b, a_ln_g, a_ln_b, a_w_out, a_norm_post, kv_norm, w_kv, cmp_pe, cmp_w1, cmp_w2, b_norm_pre, b_w_in, b_w_out, b_norm_post):
    import numpy as np
    bp, tp, d = x_prompt.shape
    bd, tq, _ = x_sample.shape
    n_pages = page_table.shape[1]
    past = n_pages * PAGE_SIZE
    w_keep = state_win.shape[1]
    width = 2 * KV_HEADS * HEAD_DIM
    kw = KV_HEADS * HEAD_DIM
    dq = N_HEADS * HEAD_DIM
    assert b_w_in.shape[0] == 1 and tp % (2 * KEY_TILE) == 0 and tp // L_SLC <= HEAD_DIM
    tm = 256
    tm_s = min(tm, bd * tq)

    xp = x_prompt
    xs = x_sample.reshape(bd * tq, d)
    conv_p, conv_s = [], []
    for l in range(a_w_in.shape[0]):
        w_in = a_w_in[l].astype(BF16)
        w_out = a_w_out[l].astype(BF16)
        di = a_w_out.shape[1]
        tail = (a_conv_w[l], a_conv_b[l], a_ln_g[l], a_ln_b[l], w_out, a_norm_post[l])
        glu, sz = _glu_proj(xp.reshape(bp * tp, d), a_norm_pre[l], w_in, tm)
        glu = glu.reshape(bp, tp, di)
        xp = _conv_prompt(glu, sz.reshape(bp, tp, di), xp, *tail, tm)
        conv_p.append(glu[:, -(CONV_W - 1):])
        glu, sz = _glu_proj(xs, a_norm_pre[l], w_in, tm_s)
        xs = _conv_sample(glu, state_conv[l], sz, xs, *tail, 16)
        conv_s.append(jnp.concatenate([state_conv[l], glu.reshape(bd, tq, di)], axis=1)[:, -(CONV_W - 1):])

    bw = b_w_in[0]
    n_gate = N_BRANCH * N_HEADS
    wq = bw[:, :dq].astype(BF16)
    wz = bw[:, dq:dq * (1 + N_BRANCH)].astype(BF16)
    wg = jnp.pad(bw[:, dq * (1 + N_BRANCH):], ((0, 0), (0, LANES - n_gate))).astype(BF16)
    wkv = w_kv.astype(BF16)
    xp2 = xp.reshape(bp * tp, d)
    (cmp_p, slc_p, win_p, qa_p, sz_p, gate_p, ska, sv, wka, wv) = _nsa_proj(
        xp2, kv_norm, b_norm_pre[0], wkv, wq, wz, wg, tm, seq_len=tp)
    cmp_s, slc_s, win_s, qa_s, sz_s, gate_s = _nsa_proj(xs, kv_norm, b_norm_pre[0], wkv, wq, wz, wg, tm_s)
    qa_s = qa_s.astype(F32)

    w1cat = jnp.concatenate([cmp_w1[:, :CMP_STRIDE * HEAD_DIM], cmp_w1[:, CMP_STRIDE * HEAD_DIM:]], axis=2).astype(BF16)
    pe8 = jnp.pad(cmp_pe.reshape(2, L_CMP // CMP_STRIDE, CMP_STRIDE * HEAD_DIM), ((0, 0), (0, 6), (0, 0)))
    zero = jnp.zeros_like(cmp_w2)
    w2h = jnp.stack([jnp.concatenate([cmp_w2, zero], axis=2), jnp.concatenate([zero, cmp_w2], axis=2)], axis=1).astype(BF16)
    prompt_page = min(1024, tp)
    pp = tp // prompt_page
    kvc_p = _compress(cmp_p, None, w1cat, pe8, w2h, pp, prompt_page).transpose(0, 2, 1)
    pt_flat = page_table.reshape(-1).astype(jnp.int32)
    cmp_pages_t = cache_cmp.transpose(0, 2, 3, 4, 1).reshape(-1, width, PAGE_SIZE)
    kvc_s = _compress(cmp_pages_t, pt_flat, w1cat, pe8, w2h, n_pages, PAGE_SIZE)

    n_chunk_p = kvc_p.shape[1]
    nc_p = n_chunk_p - L_CMP // CMP_STRIDE + 1
    real = (jnp.arange(n_chunk_p) < nc_p)[None, :, None]
    flag = jnp.zeros((LANES - HEAD_DIM,), F32).at[0].set(1.0)
    kc4 = jnp.where(real, kvc_p[:, :, :kw], 0.0).reshape(bp, n_chunk_p, KV_HEADS, HEAD_DIM)
    aug = jnp.where(real[..., None], 0.0, flag) * jnp.ones((bp, n_chunk_p, KV_HEADS, 1), F32)
    kc_real = jnp.concatenate([kc4, aug], axis=-1).reshape(bp, n_chunk_p, KV_HEADS * LANES)
    pad_row = jnp.concatenate([jnp.zeros((HEAD_DIM,), F32), flag])
    kc_front = jnp.broadcast_to(jnp.tile(pad_row, KV_HEADS), (bp, n_chunk_p, KV_HEADS * LANES))
    kc_pad = jnp.concatenate([kc_front, kc_real], axis=1)
    vc_pad = jnp.concatenate([jnp.zeros((bp, n_chunk_p, kw), F32), jnp.where(real, kvc_p[:, :, kw:], 0.0)], axis=1)
    m_pad = _overlap_matrix(2 * n_chunk_p, n_chunk_p, nc_p, tp // L_SLC)
    cb = _bias_tile(rel_bias, n_chunk_p, Q_BLK, 1, CMP_STRIDE * (n_chunk_p - Q_BLK // CMP_STRIDE) - (L_CMP - 1),
                    row_step=-CMP_STRIDE)
    cb = cb.reshape(KV_HEADS, GROUP, n_chunk_p, Q_BLK).transpose(0, 2, 1, 3).reshape(KV_HEADS, n_chunk_p, GROUP * Q_BLK)
    oc_p, qsel, qflag = _cmp_prompt(qa_p, kc_pad, vc_pad, m_pad, cb, gate_p, bp, tp)
    sb = jnp.stack([_bias_tile(rel_bias, KEY_TILE, Q_BLK, 1, off, row_step=-1)
                    for off in (0, Q_BLK, 2 * Q_BLK, 3 * Q_BLK)])
    sb = sb.reshape(4, KV_HEADS, GROUP, KEY_TILE, Q_BLK).transpose(0, 1, 3, 2, 4).reshape(4, KV_HEADS, KEY_TILE, GROUP * Q_BLK)
    os_p = _slc_prompt(qsel, ska, sv, sb, gate_p, bp, tp)
    n_wkey = WINDOW + Q_BLK
    wb = _bias_tile(rel_bias, n_wkey, Q_BLK, 1, WINDOW, WINDOW, row_step=-1)
    wb = wb.reshape(KV_HEADS, GROUP, n_wkey, Q_BLK).transpose(0, 2, 1, 3).reshape(KV_HEADS, n_wkey, GROUP * Q_BLK)
    wka_front = jnp.broadcast_to(jnp.tile(pad_row, KV_HEADS).astype(BF16), (bp, WINDOW, KV_HEADS * LANES))
    wka_pad = jnp.concatenate([wka_front, wka.reshape(bp, tp, -1)], axis=1)
    v_rows = wv.shape[0]
    wv_pad = jnp.concatenate([jnp.zeros((v_rows, bp, WINDOW), BF16), wv.reshape(v_rows, bp, tp)], axis=2)
    ow_p = _win_prompt(qflag, wka_pad, wv_pad.reshape(v_rows, bp * (WINDOW + tp)), wb, gate_p, bp, tp)
    wo = b_w_out[0].astype(BF16)
    y_p = _nsa_out(oc_p.reshape(bp * tp, dq), os_p.reshape(bp * tp, dq), ow_p.reshape(bp * tp, dq), sz_p, xp2,
                   wo, b_norm_post[0], tm)

    n_chunk_s = kvc_s.shape[2]
    nc_s = n_chunk_s - L_CMP // CMP_STRIDE + 1
    nb_s = -(-(past + tq) // L_SLC)
    n_sk = past + LANES
    n_wk = w_keep + LANES
    m_t = _overlap_matrix(n_chunk_s, 0, nc_s, nb_s).T
    e_np = np.zeros((LANES, n_sk), np.float32)
    e_np[HEAD_DIM + np.arange(n_sk) // L_SLC, np.arange(n_sk)] = 1.0
    e_s = jnp.asarray(e_np, BF16)
    rows_s = N_HEADS * tq
    cbs = _bias_tile(rel_bias, tq, n_chunk_s, -CMP_STRIDE, past - (L_CMP - 1)).reshape(rows_s, n_chunk_s)
    sbs = _bias_tile(rel_bias, tq, n_sk, -1, past).reshape(rows_s, n_sk)
    wbs = _bias_tile(rel_bias, tq, n_wk, -1, w_keep, WINDOW).reshape(rows_s, n_wk)
    gate_r = gate_s[:, :n_gate].reshape(bd, tq, N_BRANCH, N_HEADS).transpose(0, 3, 1, 2).reshape(bd * rows_s, N_BRANCH)
    gate_r = jnp.pad(gate_r, ((0, 0), (0, LANES - N_BRANCH)))
    to_t = lambda a: a.transpose(0, 2, 3, 4, 1).reshape(a.shape[0], width, a.shape[1])
    oc_s, os_s, ow_s = _decode_attn(qa_s, kvc_s, to_t(cache_slc), pt_flat, slc_s, to_t(state_win), win_s,
                                    m_t, e_s, cbs, sbs, wbs, gate_r, bd, tq, n_pages)
    y_s = _nsa_out(oc_s, os_s, ow_s, sz_s, xs, wo, b_norm_post[0], tm_s)

    kv5 = lambda a, b, t: a.reshape(b, t, 2, KV_HEADS, HEAD_DIM)
    kv5_t = lambda a: a.reshape(bp, 2, KV_HEADS, HEAD_DIM, a.shape[2]).transpose(0, 4, 1, 2, 3)
    win_all = jnp.concatenate([state_win, kv5(win_s, bd, tq)], axis=1)
    return (y_p.reshape(bp, tp, d), y_s.reshape(bd, tq, d), jnp.stack(conv_p), jnp.stack(conv_s),
            kv5_t(cmp_p), kv5(cmp_s, bd, tq), kv5_t(slc_p), kv5(slc_s, bd, tq),
            kv5_t(win_p[:, :, -min(WINDOW, tp):]), win_all[:, -min(WINDOW, win_all.shape[1]):])
```

```python
import functools
import math

import jax
import jax.numpy as jnp
from jax import lax
from jax.experimental import pallas as pl
from jax.experimental.pallas import tpu as pltpu

F32 = jnp.float32
BF16 = jnp.bfloat16

EPS = 1e-6
NEG = -1e30
BIG = 1e9
M_FLOOR = -1e20

HEAD_DIM = 64
KV_HEADS = 4
N_HEADS = 16
GROUP = N_HEADS // KV_HEADS
N_BRANCH = 3
CONV_W = 31
L_CMP = 32
CMP_STRIDE = 16
L_SLC = 64
N_SEL = 16
WINDOW = 512
Q_BLK = 128
NUM_BUCKETS = 32
MAX_DISTANCE = 128
MAX_EXACT = NUM_BUCKETS // 2
PAGE_SIZE = 128
LANES = 128
HALO = 32
CONV_TAIL = 16

VMEM_LIMIT = 56 * 1024 * 1024


def _cparams(*sem):
    return pltpu.CompilerParams(dimension_semantics=sem, vmem_limit_bytes=VMEM_LIMIT)


def _sigmoid(x):
    return 1.0 / (1.0 + jnp.exp(-x))


def _rms(x, g):
    return x * lax.rsqrt(jnp.mean(x * x, axis=-1, keepdims=True) + EPS) * g


def _lane(shape):
    return lax.broadcasted_iota(jnp.int32, shape, len(shape) - 1)


def _swap_halves(x):
    return pltpu.roll(x, HEAD_DIM, axis=x.ndim - 1)


def _glu_proj_body(x_ref, g_ref, w_ref, glu_ref, sz_ref):
    di = glu_ref.shape[-1]
    h = _rms(x_ref[...], g_ref[...])
    u = jnp.dot(h.astype(BF16), w_ref[...], preferred_element_type=F32)
    z = u[:, 2 * di:]
    glu_ref[...] = u[:, :di] * _sigmoid(u[:, di:2 * di])
    sz_ref[...] = z * _sigmoid(z)


def _glu_proj(x, g, w_bf, tm):
    rows, d = x.shape
    di = w_bf.shape[1] // 3
    return pl.pallas_call(
        _glu_proj_body,
        grid=(rows // tm,),
        in_specs=[pl.BlockSpec((tm, d), lambda i: (i, 0)),
                  pl.BlockSpec((1, d), lambda i: (0, 0)),
                  pl.BlockSpec((d, 3 * di), lambda i: (0, 0))],
        out_specs=[pl.BlockSpec((tm, di), lambda i: (i, 0)),
                   pl.BlockSpec((tm, di), lambda i: (i, 0))],
        out_shape=[jax.ShapeDtypeStruct((rows, di), F32)] * 2,
        compiler_params=_cparams("parallel"),
        name="glu_proj",
    )(x, g.reshape(1, d), w_bf)


def _conv_tail(c, sz, x, lg_ref, lb_ref, wo_ref, gp_ref):
    mu = jnp.mean(c, axis=-1, keepdims=True)
    cc = c - mu
    var = jnp.mean(cc * cc, axis=-1, keepdims=True)
    y = cc * lax.rsqrt(var + EPS) * lg_ref[...] + lb_ref[...]
    y = y * _sigmoid(y) * sz
    o = jnp.dot(y.astype(BF16), wo_ref[...], preferred_element_type=F32)
    return x + _rms(o, gp_ref[...])


def _conv_prompt_body(glu_ref, prev_ref, sz_ref, x_ref, cw_ref, cb_ref, lg_ref, lb_ref, wo_ref, gp_ref,
                      out_ref, full_ref, c_ref, *, tm):
    t = pl.program_id(1)
    full_ref[0:HALO, :] = jnp.where(t > 0, prev_ref[0], 0.0)
    full_ref[HALO:HALO + tm, :] = glu_ref[0]
    full_ref[HALO + tm:, :] = jnp.zeros((CONV_TAIL, full_ref.shape[1]), F32)
    d = c_ref.shape[-1]
    first = HALO - (CONV_W - 1)
    sub = 8
    half = tm // 2
    for lc in range(d // LANES):
        ln = slice(lc * LANES, (lc + 1) * LANES)
        for r0 in range(0, tm, half):
            acc = jnp.zeros((half, LANES), F32)
            for b in range(sub):
                z = None
                for a in range(-(-(CONV_W - b) // sub)):
                    x = full_ref[r0 + sub * a:r0 + sub * a + half + 2 * sub, ln]
                    term = cw_ref[sub * a + b:sub * a + b + 1, ln] * x
                    z = term if z is None else z + term
                acc = acc + z[first + b:first + b + half]
            c_ref[r0:r0 + half, ln] = acc + cb_ref[:, ln]
    out_ref[0] = _conv_tail(c_ref[...], sz_ref[0], x_ref[0], lg_ref, lb_ref, wo_ref, gp_ref)


def _conv_prompt(glu, sz, x, cw, cb, lg, lb, wo_bf, gp, tm):
    b, t, d = x.shape
    di = glu.shape[-1]
    per = tm // HALO
    vec = lambda n: pl.BlockSpec((1, n), lambda i, j: (0, 0))
    return pl.pallas_call(
        functools.partial(_conv_prompt_body, tm=tm),
        grid=(b, t // tm),
        in_specs=[pl.BlockSpec((1, tm, di), lambda i, j: (i, j, 0)),
                  pl.BlockSpec((1, HALO, di), lambda i, j: (i, jnp.maximum(j * per - 1, 0), 0)),
                  pl.BlockSpec((1, tm, di), lambda i, j: (i, j, 0)),
                  pl.BlockSpec((1, tm, d), lambda i, j: (i, j, 0)),
                  pl.BlockSpec((CONV_W, di), lambda i, j: (0, 0)),
                  vec(di), vec(di), vec(di),
                  pl.BlockSpec((di, d), lambda i, j: (0, 0)),
                  vec(d)],
        out_specs=pl.BlockSpec((1, tm, d), lambda i, j: (i, j, 0)),
        out_shape=jax.ShapeDtypeStruct((b, t, d), F32),
        scratch_shapes=[pltpu.VMEM((HALO + tm + CONV_TAIL, di), F32), pltpu.VMEM((tm, di), F32)],
        compiler_params=_cparams("parallel", "arbitrary"),
        name="conv_prompt",
    )(glu, glu, sz, x, cw, cb.reshape(1, di), lg.reshape(1, di), lb.reshape(1, di), wo_bf, gp.reshape(1, d))


def _conv_sample_body(glu_ref, st_ref, sz_ref, x_ref, cw_ref, cb_ref, lg_ref, lb_ref, wo_ref, gp_ref,
                      out_ref, full_ref, c_ref, *, nb, tq):
    d = c_ref.shape[-1]
    first = HALO - (CONV_W - 1)
    full_ref[:, first:HALO, :] = st_ref[...]
    full_ref[:, HALO:HALO + tq, :] = glu_ref[...].reshape(nb, tq, d)
    for lc in range(d // LANES):
        ln = slice(lc * LANES, (lc + 1) * LANES)
        acc = jnp.zeros((nb, tq, LANES), F32)
        for j in range(CONV_W):
            acc = acc + cw_ref[j:j + 1, ln] * full_ref[:, first + j:first + j + tq, ln]
        c_ref[:, ln] = (acc + cb_ref[:, ln]).reshape(nb * tq, LANES)
    out_ref[...] = _conv_tail(c_ref[...], sz_ref[...], x_ref[...], lg_ref, lb_ref, wo_ref, gp_ref)


def _conv_sample(glu, state, sz, x, cw, cb, lg, lb, wo_bf, gp, nb):
    n_seq = state.shape[0]
    rows, d = x.shape
    di = glu.shape[-1]
    tq = rows // n_seq
    vec = lambda n: pl.BlockSpec((1, n), lambda i: (0, 0))
    return pl.pallas_call(
        functools.partial(_conv_sample_body, nb=nb, tq=tq),
        grid=(n_seq // nb,),
        in_specs=[pl.BlockSpec((nb * tq, di), lambda i: (i, 0)),
                  pl.BlockSpec((nb, CONV_W - 1, di), lambda i: (i, 0, 0)),
                  pl.BlockSpec((nb * tq, di), lambda i: (i, 0)),
                  pl.BlockSpec((nb * tq, d), lambda i: (i, 0)),
                  pl.BlockSpec((CONV_W, di), lambda i: (0, 0)),
                  vec(di), vec(di), vec(di),
                  pl.BlockSpec((di, d), lambda i: (0, 0)),
                  vec(d)],
        out_specs=pl.BlockSpec((nb * tq, d), lambda i: (i, 0)),
        out_shape=jax.ShapeDtypeStruct((rows, d), F32),
        scratch_shapes=[pltpu.VMEM((nb, HALO + tq, di), F32), pltpu.VMEM((nb * tq, di), F32)],
        compiler_params=_cparams("parallel"),
        name="conv_sample",
    )(glu, state, sz, x, cw, cb.reshape(1, di), lg.reshape(1, di), lb.reshape(1, di), wo_bf, gp.reshape(1, d))


def _head_major(col_pair, head):
    return col_pair if head % 2 == 0 else _swap_halves(col_pair)


def _nsa_proj_body(x_ref, gkv_ref, gq_ref, wkv_ref, wq_ref, wz_ref, wg_ref,
                   cmp_ref, slc_ref, win_ref, qa_ref, sz_ref, gate_ref, *aug_refs, seq_len, tm):
    x = x_ref[...]
    xn = x * lax.rsqrt(jnp.mean(x * x, axis=-1, keepdims=True) + EPS)
    hkv = (xn * gkv_ref[...]).astype(BF16)
    hq = (xn * gq_ref[...]).astype(BF16)
    kv = jnp.dot(hkv, wkv_ref[...], preferred_element_type=F32)
    width = 2 * KV_HEADS * HEAD_DIM
    if aug_refs:
        kv_t = [kv[:, b * width:(b + 1) * width].T for b in range(N_BRANCH)]
        cmp_ref[0] = kv_t[0]
        slc_ref[0] = kv_t[1]
        win_ref[0] = kv_t[2]
    else:
        cmp_ref[...] = kv[:, :width]
        slc_ref[...] = kv[:, width:2 * width]
        win_ref[...] = kv[:, 2 * width:]
    lane = _lane((tm, LANES))
    low = lane < HEAD_DIM
    uq = jnp.dot(hq, wq_ref[...], preferred_element_type=F32)
    scale = HEAD_DIM ** -0.5
    for h in range(N_HEADS):
        qh = _head_major(uq[:, (h // 2) * LANES:(h // 2 + 1) * LANES], h)
        qa_ref[:, h * LANES:(h + 1) * LANES] = jnp.where(low, qh * scale, 0.0).astype(BF16)
    z = jnp.dot(hq, wz_ref[...], preferred_element_type=F32)
    sz_ref[...] = z * _sigmoid(z)
    gate_ref[...] = _sigmoid(jnp.dot(hq, wg_ref[...], preferred_element_type=F32))
    if aug_refs:
        ska_ref, sv_ref, wka_ref, wv_ref = aug_refs
        row = pl.program_id(0) * tm + lax.broadcasted_iota(jnp.int32, (tm, LANES), 0)
        blk = (row % seq_len) // L_SLC
        onehot = (lane - HEAD_DIM == blk).astype(F32)
        kw = KV_HEADS * HEAD_DIM
        for h in range(KV_HEADS):
            ks = _head_major(kv[:, width + (h // 2) * LANES:width + (h // 2 + 1) * LANES], h)
            ska_ref[:, h * LANES:(h + 1) * LANES] = jnp.where(low, ks, onehot).astype(BF16)
            kwn = _head_major(kv[:, 2 * width + (h // 2) * LANES:2 * width + (h // 2 + 1) * LANES], h)
            wka_ref[:, h * LANES:(h + 1) * LANES] = jnp.where(low, kwn, 0.0).astype(BF16)
        extra = (lax.broadcasted_iota(jnp.int32, (V_PAD, tm), 0) == 0).astype(BF16)
        for dst, src in ((sv_ref, kv_t[1]), (wv_ref, kv_t[2])):
            for h in range(KV_HEADS):
                dst[h * V_ROWS:h * V_ROWS + HEAD_DIM, :] = src[kw + h * HEAD_DIM:kw + (h + 1) * HEAD_DIM].astype(BF16)
                dst[h * V_ROWS + HEAD_DIM:(h + 1) * V_ROWS, :] = extra


def _nsa_proj(x, g_kv, g_q, wkv_bf, wq_bf, wz_bf, wg_bf, tm, seq_len=None):
    rows, d = x.shape
    width = 2 * KV_HEADS * HEAD_DIM
    row = lambda n: pl.BlockSpec((tm, n), lambda i: (i, 0))
    full = lambda a: pl.BlockSpec(a.shape, lambda i: (0, 0))
    out_specs = [row(width), row(width), row(width), row(N_HEADS * LANES), row(wz_bf.shape[1]), row(LANES)]
    out_shape = [jax.ShapeDtypeStruct((rows, width), F32)] * 3 + [
        jax.ShapeDtypeStruct((rows, N_HEADS * LANES), BF16),
        jax.ShapeDtypeStruct((rows, wz_bf.shape[1]), F32),
        jax.ShapeDtypeStruct((rows, LANES), F32)]
    if seq_len is not None:
        per_seq = seq_len // tm
        kv_t_spec = pl.BlockSpec((1, width, tm), lambda i: (i // per_seq, 0, i % per_seq))
        out_specs[:N_BRANCH] = [kv_t_spec] * N_BRANCH
        out_shape[:N_BRANCH] = [jax.ShapeDtypeStruct((rows // seq_len, width, seq_len), F32)] * N_BRANCH
        out_specs += [row(KV_HEADS * LANES), pl.BlockSpec((KV_HEADS * V_ROWS, tm), lambda i: (0, i))] * 2
        out_shape += [jax.ShapeDtypeStruct((rows, KV_HEADS * LANES), BF16),
                      jax.ShapeDtypeStruct((KV_HEADS * V_ROWS, rows), BF16)] * 2
    gkv = g_kv.reshape(1, d)
    gq = g_q.reshape(1, d)
    return pl.pallas_call(
        functools.partial(_nsa_proj_body, seq_len=seq_len, tm=tm),
        grid=(rows // tm,),
        in_specs=[row(d), full(gkv), full(gq), full(wkv_bf), full(wq_bf), full(wz_bf), full(wg_bf)],
        out_specs=out_specs,
        out_shape=out_shape,
        compiler_params=_cparams("parallel"),
        name="nsa_proj",
    )(x, gkv, gq, wkv_bf, wq_bf, wz_bf, wg_bf)


def _compress_body(pt_ref, *refs, n_pages):
    del pt_ref
    page_refs = refs[:n_pages]
    w1_ref, pe_ref, w2_ref, out_ref, a_ref = refs[n_pages:]
    width, page_rows = page_refs[0].shape[1:]
    n_col = width // LANES
    cpp = page_rows // CMP_STRIDE
    n_chunk = n_pages * cpp
    hid = w2_ref.shape[2]
    low = _lane((cpp, LANES)) < HEAD_DIM
    def assemble(c):
        for p in range(n_pages):
            col_t = page_refs[p][0, c * LANES:(c + 1) * LANES, :]
            by_s = pltpu.einshape("(ns)f->(sn)f", col_t.T, n=cpp)
            for s in range(0, CMP_STRIDE, 2):
                b0 = by_s[s * cpp:(s + 1) * cpp]
                b1 = by_s[(s + 1) * cpp:(s + 2) * cpp]
                dst = (slice(p * cpp, (p + 1) * cpp), slice((s // 2) * LANES, (s // 2 + 1) * LANES))
                a_ref[(2 * c,) + dst] = jnp.where(low, b0, _swap_halves(b1))
                a_ref[(2 * c + 1,) + dst] = jnp.where(low, _swap_halves(b0), b1)

    for kv in range(2):
        for c in range(kv * n_col // 2, (kv + 1) * n_col // 2):
            assemble(c)
        w1 = w1_ref[kv]
        pe = jnp.dot(pe_ref[kv].astype(BF16), w1, preferred_element_type=F32)
        pe_term = pe[0:1, :hid] + pe[1:2, hid:]
        a_kv = a_ref[kv * KV_HEADS:(kv + 1) * KV_HEADS].reshape(KV_HEADS * n_chunk, CMP_STRIDE * HEAD_DIM)
        parts = jnp.dot(a_kv.astype(BF16), w1, preferred_element_type=F32)
        for hp in range(KV_HEADS // 2):
            pair = jnp.zeros((n_chunk, LANES), F32)
            for par in range(2):
                part = parts[(2 * hp + par) * n_chunk:(2 * hp + par + 1) * n_chunk]
                pre = part[:, :hid] + pltpu.roll(part[:, hid:], n_chunk - 1, axis=0) + pe_term
                mid = pre * _sigmoid(pre)
                pair = pair + jnp.dot(mid.astype(BF16), w2_ref[kv, par], preferred_element_type=F32)
            col = kv * (KV_HEADS // 2) + hp
            out_ref[0, col * LANES:(col + 1) * LANES, :] = pair.T


def _compress(rows_t, page_table, w1cat_bf, pe8, w2_bf, n_pages, page_rows):
    width = rows_t.shape[1]
    n_chunk = n_pages * page_rows // CMP_STRIDE
    if page_table is None:
        n_seq = rows_t.shape[0]
        page_table = jnp.zeros((1,), jnp.int32)
        page_spec = lambda j: pl.BlockSpec((1, width, page_rows), lambda i, pt: (i, 0, j))
    else:
        n_seq = page_table.shape[0] // n_pages
        page_spec = lambda j: pl.BlockSpec((1, width, page_rows), lambda i, pt: (pt[i * n_pages + j], 0, 0))
    full = lambda a: pl.BlockSpec(a.shape, lambda i, pt: (0,) * a.ndim)
    grid_spec = pltpu.PrefetchScalarGridSpec(
        num_scalar_prefetch=1,
        grid=(n_seq,),
        in_specs=[page_spec(j) for j in range(n_pages)] + [full(w1cat_bf), full(pe8), full(w2_bf)],
        out_specs=pl.BlockSpec((1, width, n_chunk), lambda i, pt: (i, 0, 0)),
        scratch_shapes=[pltpu.VMEM((2 * KV_HEADS, n_chunk, CMP_STRIDE * HEAD_DIM), F32)],
    )
    return pl.pallas_call(
        functools.partial(_compress_body, n_pages=n_pages),
        grid_spec=grid_spec,
        out_shape=jax.ShapeDtypeStruct((n_seq, width, n_chunk), F32),
        compiler_params=_cparams("parallel"),
        name="compress",
    )(page_table, *([rows_t] * n_pages), w1cat_bf, pe8, w2_bf)


def _tile_spec(rows, width, lane_step, offset, hi=1 << 30, row_step=1):
    return (rows, width, row_step, lane_step, offset, hi)


def _bias_tiles_body(rb_ref, *out_refs, specs):
    h = pl.program_id(0)
    far = rb_ref[NUM_BUCKETS - 1, h]
    for out_ref, (rows, width, row_step, lane_step, offset, hi) in zip(out_refs, specs):
        shape = (rows, width)
        d = row_step * lax.broadcasted_iota(jnp.int32, shape, 0) + lane_step * _lane(shape) + offset
        n = jnp.maximum(d, 0)
        nf = jnp.maximum(n, 1).astype(F32)
        large = MAX_EXACT + (jnp.log(nf / MAX_EXACT) / math.log(MAX_DISTANCE / MAX_EXACT)
                             * (NUM_BUCKETS - MAX_EXACT)).astype(jnp.int32)
        large = jnp.minimum(large, NUM_BUCKETS - 1)
        bucket = jnp.where(n < MAX_EXACT, n, large)
        val = jnp.zeros(shape, F32)
        for k in range(NUM_BUCKETS - 1):
            val = jnp.where(bucket == k, rb_ref[k, h] - far, val)
        out_ref[0] = jnp.where((d >= 0) & (d <= hi), val, NEG)


def _bias_tiles(rel_bias, specs):
    return pl.pallas_call(
        functools.partial(_bias_tiles_body, specs=specs),
        grid=(N_HEADS,),
        in_specs=[pl.BlockSpec(memory_space=pltpu.SMEM)],
        out_specs=[pl.BlockSpec((1, s[0], s[1]), lambda h: (h, 0, 0)) for s in specs],
        out_shape=[jax.ShapeDtypeStruct((N_HEADS, s[0], s[1]), F32) for s in specs],
        compiler_params=_cparams("parallel"),
        name="bias_tiles",
    )(rel_bias)


_NT = (((1,), (1,)), ((), ()))


def _softmax_rows(s):
    m = jnp.maximum(jnp.max(s, axis=-1, keepdims=True), M_FLOOR)
    p = jnp.exp(s - m)
    return p, jnp.sum(p, axis=-1, keepdims=True)


def _gate_col(gate, col):
    return jnp.sum(jnp.where(_lane(gate.shape) == col, gate, 0.0), axis=-1, keepdims=True)


def _pair_columns(o_even, o_odd, valid_half):
    lo = o_even if valid_half == 0 else _swap_halves(o_even)
    hi = o_odd if valid_half == 1 else _swap_halves(o_odd)
    return jnp.where(_lane(lo.shape) < HEAD_DIM, lo, hi)


def _select_blocks_t(imp_t, tb, n_sel):
    nb, nt = imp_t.shape
    sub = 8
    j = lax.broadcasted_iota(jnp.int32, (nb, nt), 0)
    valid = j <= tb
    forced = (j == 0) | (j == tb) | (j == tb - 1)
    score = jnp.where(valid, jnp.where(forced, BIG, imp_t), -BIG)
    groups = [score[lo:lo + sub] for lo in range(0, nb, sub)]
    jr = lax.broadcasted_iota(jnp.int32, (sub, nt), 0)
    counts = [jnp.zeros((sub, nt), F32) for _ in groups]
    for i in range(nb):
        row = score[i:i + 1]
        for r, grp in enumerate(groups):
            lo = r * sub
            if lo > i:
                one = jnp.where(row >= grp, 1.0, 0.0)
            elif lo + sub - 1 <= i:
                one = jnp.where(row > grp, 1.0, 0.0)
            else:
                one = jnp.where(jr + lo > i, jnp.where(row >= grp, 1.0, 0.0), jnp.where(row > grp, 1.0, 0.0))
            counts[r] = counts[r] + one
    rank = jnp.concatenate(counts, axis=0)
    return jnp.where((rank < n_sel) & valid, 0.0, NEG)


def _cmp_prompt_body(qa_ref, kc_ref, vc_ref, m_ref, cb_ref, gate_ref, oc_ref, qsel_ref, qflag_ref, *, n_key):
    qt = pl.program_id(1)
    start = pl.multiple_of(8 * qt + 8, 8)
    kwin = kc_ref[0, pl.ds(start, n_key), :]
    vwin = vc_ref[0, pl.ds(start, n_key), :]
    mwin_t = m_ref[pl.ds(start, n_key), :].T
    feat = lax.broadcasted_iota(jnp.int32, (LANES, Q_BLK), 0)
    pos = qt * Q_BLK + lax.broadcasted_iota(jnp.int32, (1, Q_BLK), 1)
    tb = pos // L_SLC
    gate = gate_ref[...]
    n_pair = GROUP // 2
    imps, q_feats = [], []
    for kvh in range(KV_HEADS):
        ka = kwin[:, kvh * LANES:(kvh + 1) * LANES].astype(BF16)
        vpt = vwin[:, (kvh // 2) * LANES:(kvh // 2 + 1) * LANES].T
        vpt = vpt[(kvh % 2) * HEAD_DIM:(kvh % 2 + 1) * HEAD_DIM].astype(BF16)
        qts = [qa_ref[0, :, h * LANES:(h + 1) * LANES].astype(F32).T for h in range(kvh * GROUP, (kvh + 1) * GROUP)]
        pc_sum = jnp.zeros((n_key, Q_BLK), F32)
        for gp in range(n_pair):
            qa = jnp.concatenate([jnp.where(feat == HEAD_DIM, NEG, qts[2 * gp + e]) for e in range(2)], axis=1)
            qa = qa.astype(BF16)
            qflag_ref[0, kvh, :, 2 * gp * Q_BLK:(2 * gp + 2) * Q_BLK] = qa
            s = jnp.dot(ka, qa, preferred_element_type=F32)
            s = s + cb_ref[kvh, :, 2 * gp * Q_BLK:(2 * gp + 2) * Q_BLK]
            m = jnp.maximum(jnp.max(s, axis=0, keepdims=True), M_FLOOR)
            p = jnp.exp(s - m)
            inv = 1.0 / jnp.maximum(jnp.sum(p, axis=0, keepdims=True), 1e-30)
            pc = p * inv
            pc_sum = pc_sum + pc[:, :Q_BLK] + pc[:, Q_BLK:]
            o = jnp.dot(vpt, p.astype(BF16), preferred_element_type=F32) * inv
            _store_chain(oc_ref, o, gate, kvh * GROUP + 2 * gp, kvh * n_pair + gp)
        imps.append(jnp.dot(mwin_t, pc_sum, preferred_element_type=F32, precision=lax.Precision.HIGHEST)[HEAD_DIM:])
        q_feats.append([qt_[:HEAD_DIM] for qt_ in qts])
    n_blk = imps[0].shape[0]
    n_valid = (qt + 1) * (Q_BLK // L_SLC)
    step = n_blk // 4
    for n_rank in range(step, n_blk + 1, step):
        @pl.when((n_valid > n_rank - step) & (n_valid <= n_rank))
        def _():
            for kvh in range(KV_HEADS):
                selneg_t = _select_blocks_t(imps[kvh][:n_rank], tb, N_SEL)
                if n_rank < n_blk:
                    selneg_t = jnp.concatenate([selneg_t, jnp.full((n_blk - n_rank, Q_BLK), NEG, F32)], axis=0)
                for g in range(GROUP):
                    qsel_ref[0, kvh, :, g * Q_BLK:(g + 1) * Q_BLK] = jnp.concatenate(
                        [q_feats[kvh][g], selneg_t], axis=0).astype(BF16)


def _cmp_prompt(qa, kc_pad, vc_pad, m_pad, cb, gate, batch, seq):
    nq = seq // Q_BLK
    n_key = kc_pad.shape[1] // 2
    d_out = N_HEADS * HEAD_DIM
    return pl.pallas_call(
        functools.partial(_cmp_prompt_body, n_key=n_key),
        grid=(batch, nq),
        in_specs=[pl.BlockSpec((1, Q_BLK, N_HEADS * LANES), lambda b, t: (b, t, 0)),
                  pl.BlockSpec((1,) + kc_pad.shape[1:], lambda b, t: (b, 0, 0)),
                  pl.BlockSpec((1,) + vc_pad.shape[1:], lambda b, t: (b, 0, 0)),
                  pl.BlockSpec(m_pad.shape, lambda b, t: (0, 0)),
                  pl.BlockSpec(cb.shape, lambda b, t: (0, 0, 0)),
                  pl.BlockSpec((Q_BLK, LANES), lambda b, t: (b * nq + t, 0))],
        out_specs=[pl.BlockSpec((1, Q_BLK, d_out), lambda b, t: (b, t, 0))]
        + [pl.BlockSpec((1, KV_HEADS, LANES, GROUP * Q_BLK), lambda b, t: (b * nq + t, 0, 0, 0))] * 2,
        out_shape=[jax.ShapeDtypeStruct((batch, seq, d_out), F32)]
        + [jax.ShapeDtypeStruct((batch * nq, KV_HEADS, LANES, GROUP * Q_BLK), BF16)] * 2,
        compiler_params=_cparams("parallel", "parallel"),
        name="cmp_prompt",
    )(qa.reshape(batch, seq, -1), kc_pad, vc_pad, m_pad, cb, gate)


KEY_TILE = 256
COL_CHAIN = 256
WIN_TILES = 2
V_PAD = 16
V_ROWS = HEAD_DIM + V_PAD


def _store_chain(out_ref, o_t, gate, gate_col0, lane_col, row0=0):
    pair = jnp.concatenate([o_t[:, :Q_BLK], o_t[:, Q_BLK:]], axis=0).T
    g = jnp.where(_lane(pair.shape) < HEAD_DIM, _gate_col(gate, gate_col0), _gate_col(gate, gate_col0 + 1))
    out_ref[0, row0:row0 + Q_BLK, lane_col * LANES:(lane_col + 1) * LANES] = pair * g


_PAIR_CHAINS = [(e, c) for e in range(2) for c in range(GROUP * Q_BLK // COL_CHAIN)]


def _slc_prompt_body(q_ref, k_ref, v_ref, sb_ref, gate_ref, out_ref, m_ref, acc_ref, p_ref, s_ref):
    pr = pl.program_id(1)
    qt = pl.program_id(2)
    m_ref[...] = jnp.full(m_ref.shape, M_FLOOR, F32)
    acc_ref[...] = jnp.zeros(acc_ref.shape, F32)
    p_ref[...] = jnp.zeros(p_ref.shape, BF16)
    last = (qt * Q_BLK) // KEY_TILE
    cs = lambda c: slice(c * COL_CHAIN, (c + 1) * COL_CHAIN)

    def pv_of(kt):
        base = pl.multiple_of(kt * KEY_TILE, KEY_TILE)
        return [jnp.dot(v_ref[e * V_ROWS:(e + 1) * V_ROWS, pl.ds(base, KEY_TILE)], p_ref[i],
                        preferred_element_type=F32) for i, (e, c) in enumerate(_PAIR_CHAINS)]

    def scores_of(kt):
        base = pl.multiple_of(kt * KEY_TILE, KEY_TILE)
        ks = [k_ref[0, pl.ds(base, KEY_TILE), e * LANES:(e + 1) * LANES] for e in range(2)]
        return [jnp.dot(ks[e], q_ref[0, e, :, cs(c)], preferred_element_type=F32) for e, c in _PAIR_CHAINS]

    def make_step(with_bias):
        def step(kt, carry):
            s_next = scores_of(jnp.minimum(kt + 1, last))
            pvs = pv_of(jnp.maximum(kt - 1, 0))
            ss = [s_ref[i] for i in range(len(_PAIR_CHAINS))]
            if with_bias:
                case = qt % 2 + 2 * (last - kt)
                ss = [s + sb_ref[case, 2 * pr + e, :, cs(c)] for s, (e, c) in zip(ss, _PAIR_CHAINS)]
            alphas = []
            for i, s in enumerate(ss):
                m_old = m_ref[i]
                m_new = jnp.maximum(m_old, jnp.max(s, axis=0, keepdims=True))
                alphas.append(jnp.exp(m_old - m_new))
                m_ref[i] = m_new
                p_ref[i] = jnp.exp(s - m_new).astype(BF16)
            for i in range(len(_PAIR_CHAINS)):
                acc_ref[i] = alphas[i] * (acc_ref[i] + pvs[i])
                s_ref[i] = s_next[i]
            return carry
        return step

    for i, s in enumerate(scores_of(0)):
        s_ref[i] = s
    n_far = jnp.maximum(last - 1, 0)
    lax.fori_loop(0, n_far, make_step(False), 0)
    lax.fori_loop(n_far, last + 1, make_step(True), 0)
    gate = gate_ref[...]
    pvs = pv_of(last)
    for i, (e, c) in enumerate(_PAIR_CHAINS):
        o = acc_ref[i] + pvs[i]
        o = o[:HEAD_DIM] * (1.0 / jnp.maximum(o[HEAD_DIM:HEAD_DIM + 1], 1e-30))
        _store_chain(out_ref, o, gate, N_HEADS + (2 * pr + e) * GROUP + 2 * c, i)


def _slc_prompt(qsel_t, ska, sv_t, sb_t, gate, batch, seq):
    nq = seq // Q_BLK
    cols = GROUP * Q_BLK
    d_out = N_HEADS * HEAD_DIM
    n_chain = len(_PAIR_CHAINS)
    return pl.pallas_call(
        _slc_prompt_body,
        grid=(batch, KV_HEADS // 2, nq),
        in_specs=[pl.BlockSpec((1, 2, LANES, cols), lambda b, p, t: (b * nq + t, p, 0, 0)),
                  pl.BlockSpec((1, seq, 2 * LANES), lambda b, p, t: (b, 0, p)),
                  pl.BlockSpec((2 * V_ROWS, seq), lambda b, p, t: (p, b)),
                  pl.BlockSpec(sb_t.shape, lambda b, p, t: (0, 0, 0, 0)),
                  pl.BlockSpec((Q_BLK, LANES), lambda b, p, t: (b * nq + t, 0))],
        out_specs=pl.BlockSpec((1, Q_BLK, 2 * GROUP * HEAD_DIM), lambda b, p, t: (b, t, p)),
        out_shape=jax.ShapeDtypeStruct((batch, seq, d_out), F32),
        scratch_shapes=[pltpu.VMEM((n_chain, 1, COL_CHAIN), F32),
                        pltpu.VMEM((n_chain, V_ROWS, COL_CHAIN), F32),
                        pltpu.VMEM((n_chain, KEY_TILE, COL_CHAIN), BF16),
                        pltpu.VMEM((n_chain, KEY_TILE, COL_CHAIN), F32)],
        compiler_params=_cparams("parallel", "parallel", "arbitrary"),
        name="slc_prompt",
    )(qsel_t, ska.reshape(batch, seq, -1), sv_t, sb_t, gate)


def _win_prompt_body(q_ref, k_ref, v_ref, wb_ref, gate_ref, out_ref, *, n_key):
    pr = pl.program_id(1)
    t0 = pl.program_id(2) * WIN_TILES
    cs = lambda c: slice(c * COL_CHAIN, (c + 1) * COL_CHAIN)
    chains = [(u, e, c) for u in range(WIN_TILES) for e, c in _PAIR_CHAINS]
    bases = [pl.multiple_of((t0 + u) * Q_BLK, Q_BLK) for u in range(WIN_TILES)]
    ss = []
    for u in range(WIN_TILES):
        ks = [k_ref[0, pl.ds(bases[u], n_key), e * LANES:(e + 1) * LANES] for e in range(2)]
        ss += [jnp.dot(ks[e], q_ref[u, e, :, cs(c)], preferred_element_type=F32) + wb_ref[2 * pr + e, :, cs(c)]
               for e, c in _PAIR_CHAINS]
    ps = []
    for s in ss:
        m = jnp.maximum(jnp.max(s, axis=0, keepdims=True), M_FLOOR)
        ps.append(jnp.exp(s - m).astype(BF16))
    for i, (u, e, c) in enumerate(chains):
        o = jnp.dot(v_ref[e * V_ROWS:(e + 1) * V_ROWS, pl.ds(bases[u], n_key)], ps[i], preferred_element_type=F32)
        o = o[:HEAD_DIM] * (1.0 / jnp.maximum(o[HEAD_DIM:HEAD_DIM + 1], 1e-30))
        gate = gate_ref[u * Q_BLK:(u + 1) * Q_BLK, :]
        _store_chain(out_ref, o, gate, 2 * N_HEADS + (2 * pr + e) * GROUP + 2 * c, i % len(_PAIR_CHAINS), u * Q_BLK)


def _win_prompt(q_t, wka_pad, wv_t_pad, wb_t, gate, batch, seq):
    nq = seq // Q_BLK
    n_key = WINDOW + Q_BLK
    cols = GROUP * Q_BLK
    d_out = N_HEADS * HEAD_DIM
    padded = wka_pad.shape[1]
    return pl.pallas_call(
        functools.partial(_win_prompt_body, n_key=n_key),
        grid=(batch, KV_HEADS // 2, nq // WIN_TILES),
        in_specs=[pl.BlockSpec((WIN_TILES, 2, LANES, cols), lambda b, p, t: (b * (nq // WIN_TILES) + t, p, 0, 0)),
                  pl.BlockSpec((1, padded, 2 * LANES), lambda b, p, t: (b, 0, p)),
                  pl.BlockSpec((2 * V_ROWS, padded), lambda b, p, t: (p, b)),
                  pl.BlockSpec(wb_t.shape, lambda b, p, t: (0, 0, 0)),
                  pl.BlockSpec((WIN_TILES * Q_BLK, LANES), lambda b, p, t: (b * (nq // WIN_TILES) + t, 0))],
        out_specs=pl.BlockSpec((1, WIN_TILES * Q_BLK, 2 * GROUP * HEAD_DIM), lambda b, p, t: (b, t, p)),
        out_shape=jax.ShapeDtypeStruct((batch, seq, d_out), F32),
        compiler_params=_cparams("parallel", "parallel", "parallel"),
        name="win_prompt",
    )(q_t, wka_pad, wv_t_pad, wb_t, gate)


def _nsa_out_body(oc_ref, os_ref, ow_ref, sz_ref, x_ref, wo_ref, gp_ref, out_ref):
    d = oc_ref.shape[-1]
    y = oc_ref[...] * sz_ref[:, :d] + os_ref[...] * sz_ref[:, d:2 * d] + ow_ref[...] * sz_ref[:, 2 * d:]
    o = jnp.dot(y.astype(BF16), wo_ref[...], preferred_element_type=F32)
    out_ref[...] = x_ref[...] + _rms(o, gp_ref[...])


def _nsa_out(oc, os_, ow, sz, x, wo_bf, gp, tm):
    rows, d = x.shape
    dq = oc.shape[-1]
    row = lambda n: pl.BlockSpec((tm, n), lambda i: (i, 0))
    return pl.pallas_call(
        _nsa_out_body,
        grid=(rows // tm,),
        in_specs=[row(dq), row(dq), row(dq), row(N_BRANCH * dq), row(d),
                  pl.BlockSpec((dq, d), lambda i: (0, 0)), pl.BlockSpec((1, d), lambda i: (0, 0))],
        out_specs=row(d),
        out_shape=jax.ShapeDtypeStruct((rows, d), F32),
        compiler_params=_cparams("parallel"),
        name="nsa_out",
    )(oc, os_, ow, sz, x, wo_bf, gp.reshape(1, d))


def _decode_attn_body(pt_ref, *refs, n_pages, tq, w_keep):
    del pt_ref
    page_refs = refs[:n_pages]
    (qa_ref, kvc_ref, snew_ref, wst_ref, wnew_ref, mt_ref, e_ref, cbs_ref, sbs_ref, wbs_ref, gate_ref,
     oc_ref, os_ref, ow_ref, kt_ref, vt_ref, kwt_ref, vwt_ref) = refs[n_pages:]
    n_past = n_pages * PAGE_SIZE
    kw = KV_HEADS * HEAD_DIM
    rows = N_HEADS * tq

    def new_tile(ref):
        return jnp.concatenate([ref[...], jnp.zeros((LANES - tq, 2 * kw), F32)], axis=0).T

    def fill(k_dst, v_dst, col0, tile):
        k_dst[:, col0:col0 + tile.shape[1]] = tile[:kw].astype(BF16)
        v_dst[:, col0:col0 + tile.shape[1]] = tile[kw:].astype(BF16)

    for p in range(n_pages):
        fill(kt_ref, vt_ref, p * PAGE_SIZE, page_refs[p][0])
    fill(kt_ref, vt_ref, n_past, new_tile(snew_ref))
    fill(kwt_ref, vwt_ref, 0, wst_ref[0])
    fill(kwt_ref, vwt_ref, w_keep, new_tile(wnew_ref))

    zero = jnp.zeros((tq, LANES), F32)
    tiles = [[], []]
    for h in range(N_HEADS):
        kvh = h // GROUP
        piece = qa_ref[:, h * LANES:(h + 1) * LANES]
        piece = piece if kvh % 2 == 0 else _swap_halves(piece)
        for ct in range(2):
            tiles[ct].append(piece if kvh // 2 == ct else zero)
    qbd = jnp.concatenate([jnp.concatenate(t, axis=0) for t in tiles], axis=1).astype(BF16)
    gate = gate_ref[...]

    def branch(k_t, v_t, bias, extra=None):
        s = jnp.dot(qbd, k_t, preferred_element_type=F32) + bias
        if extra is not None:
            s = s + extra
        p, l = _softmax_rows(s)
        inv = 1.0 / jnp.maximum(l, 1e-30)
        o_t = lax.dot_general(v_t, p.astype(BF16), _NT, preferred_element_type=F32)
        return p, inv, o_t.T

    def store(out_ref, o, scale):
        o = o * scale
        for kvh in range(KV_HEADS):
            tile = o[:, (kvh // 2) * LANES:(kvh // 2 + 1) * LANES]
            for gp in range(GROUP // 2):
                h0 = kvh * GROUP + 2 * gp
                col = kvh * (GROUP // 2) + gp
                out_ref[:, col * LANES:(col + 1) * LANES] = _pair_columns(
                    tile[h0 * tq:(h0 + 1) * tq], tile[(h0 + 1) * tq:(h0 + 2) * tq], kvh % 2)

    p, inv, o = branch(kvc_ref[0, :kw, :].astype(BF16), kvc_ref[0, kw:, :].astype(BF16), cbs_ref[...])
    store(oc_ref, o, inv * _gate_col(gate, 0))
    pc = p * inv
    pc_sum = jnp.concatenate(
        [sum(pc[(kvh * GROUP + g) * tq:(kvh * GROUP + g + 1) * tq] for g in range(GROUP)) for kvh in range(KV_HEADS)],
        axis=0)
    imp_t = lax.dot_general(mt_ref[...], pc_sum, _NT, preferred_element_type=F32, precision=lax.Precision.HIGHEST)
    n_col = KV_HEADS * tq
    nb_pad = -(-(-(-(n_past + tq) // L_SLC)) // 8) * 8
    tb = (n_past + lax.broadcasted_iota(jnp.int32, (1, n_col), 1) % tq) // L_SLC
    selneg_t = _select_blocks_t(imp_t[HEAD_DIM:HEAD_DIM + nb_pad], tb, N_SEL)
    sel_t = jnp.concatenate([jnp.zeros((HEAD_DIM, n_col), F32), selneg_t,
                             jnp.zeros((LANES - HEAD_DIM - nb_pad, n_col), F32)], axis=0)
    sel = jnp.concatenate([sel_t, jnp.zeros((LANES, LANES - n_col), F32)], axis=1).T
    sel = jnp.concatenate([sel[kvh * tq:(kvh + 1) * tq] for kvh in range(KV_HEADS) for _ in range(GROUP)], axis=0)
    block_mask = jnp.dot(sel.astype(BF16), e_ref[...], preferred_element_type=F32)
    _, inv, o = branch(kt_ref[...], vt_ref[...], sbs_ref[...], block_mask)
    store(os_ref, o, inv * _gate_col(gate, 1))
    _, inv, o = branch(kwt_ref[...], vwt_ref[...], wbs_ref[...])
    store(ow_ref, o, inv * _gate_col(gate, 2))


def _decode_attn(qa, kvc_t, pages_t, page_table, slc_new, win_state_t, win_new, m_t, e_s, cbs, sbs, wbs, gate_r,
                 n_seq, tq, n_pages):
    width = pages_t.shape[1]
    n_sk = sbs.shape[-1]
    n_wk = wbs.shape[-1]
    w_keep = win_state_t.shape[2]
    d_out = N_HEADS * HEAD_DIM
    rows = N_HEADS * tq
    page_spec = lambda j: pl.BlockSpec((1, width, PAGE_SIZE), lambda i, pt: (pt[i * n_pages + j], 0, 0))
    full = lambda a: pl.BlockSpec(a.shape, lambda i, pt: (0,) * a.ndim)
    row = lambda n: pl.BlockSpec((tq, n), lambda i, pt: (i, 0))
    grid_spec = pltpu.PrefetchScalarGridSpec(
        num_scalar_prefetch=1,
        grid=(n_seq,),
        in_specs=[page_spec(j) for j in range(n_pages)] + [
            row(N_HEADS * LANES),
            pl.BlockSpec((1,) + kvc_t.shape[1:], lambda i, pt: (i, 0, 0)),
            row(width),
            pl.BlockSpec((1, width, w_keep), lambda i, pt: (i, 0, 0)),
            row(width),
            full(m_t), full(e_s), full(cbs), full(sbs), full(wbs),
            pl.BlockSpec((rows, LANES), lambda i, pt: (i, 0))],
        out_specs=[row(d_out)] * 3,
        scratch_shapes=[pltpu.VMEM((width // 2, n_sk), BF16), pltpu.VMEM((width // 2, n_sk), BF16),
                        pltpu.VMEM((width // 2, n_wk), BF16), pltpu.VMEM((width // 2, n_wk), BF16)],
    )
    return pl.pallas_call(
        functools.partial(_decode_attn_body, n_pages=n_pages, tq=tq, w_keep=w_keep),
        grid_spec=grid_spec,
        out_shape=[jax.ShapeDtypeStruct((n_seq * tq, d_out), F32)] * 3,
        compiler_params=_cparams("parallel"),
        name="decode_attn",
    )(page_table, *([pages_t] * n_pages), qa, kvc_t, slc_new, win_state_t, win_new, m_t, e_s, cbs, sbs, wbs, gate_r)


def _overlap_matrix(n_rows, row0, n_cmp, n_blk):
    import numpy as np
    m = np.zeros((n_rows, LANES), np.float32)
    cs = np.arange(n_cmp)[:, None] * CMP_STRIDE
    js = np.arange(n_blk)[None, :] * L_SLC
    m[row0:row0 + n_cmp, HEAD_DIM:HEAD_DIM + n_blk] = (cs <= js + L_SLC - 1) & (cs + L_CMP - 1 >= js)
    return jnp.asarray(m)


def kernel(x_prompt, x_sample, state_conv, cache_cmp, cache_slc, state_win, page_table, rel_bias, a_norm_pre, a_w_in, a_conv_w, a_conv_b, a_ln_g, a_ln_b, a_w_out, a_norm_post, kv_norm, w_kv, cmp_pe, cmp_w1, cmp_w2, b_norm_pre, b_w_in, b_w_out, b_norm_post):
    import numpy as np
    bp, tp, d = x_prompt.shape
    bd, tq, _ = x_sample.shape
    n_pages = page_table.shape[1]
    past = n_pages * PAGE_SIZE
    w_keep = state_win.shape[1]
    width = 2 * KV_HEADS * HEAD_DIM
    kw = KV_HEADS * HEAD_DIM
    dq = N_HEADS * HEAD_DIM
    assert b_w_in.shape[0] == 1 and tp % (2 * KEY_TILE) == 0 and tp // L_SLC <= HEAD_DIM
    tm = 256
    tm_s = min(tm, bd * tq)

    xp = x_prompt
    xs = x_sample.reshape(bd * tq, d)
    conv_p, conv_s = [], []
    for l in range(a_w_in.shape[0]):
        w_in = a_w_in[l].astype(BF16)
        w_out = a_w_out[l].astype(BF16)
        di = a_w_out.shape[1]
        tail = (a_conv_w[l], a_conv_b[l], a_ln_g[l], a_ln_b[l], w_out, a_norm_post[l])
        glu, sz = _glu_proj(xp.reshape(bp * tp, d), a_norm_pre[l], w_in, tm)
        glu = glu.reshape(bp, tp, di)
        xp = _conv_prompt(glu, sz.reshape(bp, tp, di), xp, *tail, tm)
        conv_p.append(glu[:, -(CONV_W - 1):])
        glu, sz = _glu_proj(xs, a_norm_pre[l], w_in, tm_s)
        xs = _conv_sample(glu, state_conv[l], sz, xs, *tail, 16)
        conv_s.append(jnp.concatenate([state_conv[l], glu.reshape(bd, tq, di)], axis=1)[:, -(CONV_W - 1):])

    bw = b_w_in[0]
    n_gate = N_BRANCH * N_HEADS
    wq = bw[:, :dq].astype(BF16)
    wz = bw[:, dq:dq * (1 + N_BRANCH)].astype(BF16)
    wg = jnp.pad(bw[:, dq * (1 + N_BRANCH):], ((0, 0), (0, LANES - n_gate))).astype(BF16)
    wkv = w_kv.astype(BF16)
    xp2 = xp.reshape(bp * tp, d)
    (cmp_p, slc_p, win_p, qa_p, sz_p, gate_p, ska, sv, wka, wv) = _nsa_proj(
        xp2, kv_norm, b_norm_pre[0], wkv, wq, wz, wg, tm, seq_len=tp)
    cmp_s, slc_s, win_s, qa_s, sz_s, gate_s = _nsa_proj(xs, kv_norm, b_norm_pre[0], wkv, wq, wz, wg, tm_s)
    qa_s = qa_s.astype(F32)

    w1cat = jnp.concatenate([cmp_w1[:, :CMP_STRIDE * HEAD_DIM], cmp_w1[:, CMP_STRIDE * HEAD_DIM:]], axis=2).astype(BF16)
    pe8 = jnp.pad(cmp_pe.reshape(2, L_CMP // CMP_STRIDE, CMP_STRIDE * HEAD_DIM), ((0, 0), (0, 6), (0, 0)))
    zero = jnp.zeros_like(cmp_w2)
    w2h = jnp.stack([jnp.concatenate([cmp_w2, zero], axis=2), jnp.concatenate([zero, cmp_w2], axis=2)], axis=1).astype(BF16)
    prompt_page = min(1024, tp)
    pp = tp // prompt_page
    kvc_p = _compress(cmp_p, None, w1cat, pe8, w2h, pp, prompt_page).transpose(0, 2, 1)
    pt_flat = page_table.reshape(-1).astype(jnp.int32)
    cmp_pages_t = cache_cmp.transpose(0, 2, 3, 4, 1).reshape(-1, width, PAGE_SIZE)
    kvc_s = _compress(cmp_pages_t, pt_flat, w1cat, pe8, w2h, n_pages, PAGE_SIZE)

    n_chunk_p = kvc_p.shape[1]
    nc_p = n_chunk_p - L_CMP // CMP_STRIDE + 1
    real = (jnp.arange(n_chunk_p) < nc_p)[None, :, None]
    flag = jnp.zeros((LANES - HEAD_DIM,), F32).at[0].set(1.0)
    kc4 = jnp.where(real, kvc_p[:, :, :kw], 0.0).reshape(bp, n_chunk_p, KV_HEADS, HEAD_DIM)
    aug = jnp.where(real[..., None], 0.0, flag) * jnp.ones((bp, n_chunk_p, KV_HEADS, 1), F32)
    kc_real = jnp.concatenate([kc4, aug], axis=-1).reshape(bp, n_chunk_p, KV_HEADS * LANES)
    pad_row = jnp.concatenate([jnp.zeros((HEAD_DIM,), F32), flag])
    kc_front = jnp.broadcast_to(jnp.tile(pad_row, KV_HEADS), (bp, n_chunk_p, KV_HEADS * LANES))
    kc_pad = jnp.concatenate([kc_front, kc_real], axis=1)
    vc_pad = jnp.concatenate([jnp.zeros((bp, n_chunk_p, kw), F32), jnp.where(real, kvc_p[:, :, kw:], 0.0)], axis=1)
    m_pad = _overlap_matrix(2 * n_chunk_p, n_chunk_p, nc_p, tp // L_SLC)
    n_wkey = WINDOW + Q_BLK
    cb, wb, *sb = _bias_tiles(rel_bias, [
        _tile_spec(n_chunk_p, Q_BLK, 1, CMP_STRIDE * (n_chunk_p - Q_BLK // CMP_STRIDE) - (L_CMP - 1),
                   row_step=-CMP_STRIDE),
        _tile_spec(n_wkey, Q_BLK, 1, WINDOW, WINDOW, row_step=-1)]
        + [_tile_spec(KEY_TILE, Q_BLK, 1, off, row_step=-1) for off in (0, Q_BLK, 2 * Q_BLK, 3 * Q_BLK)])
    cb = cb.reshape(KV_HEADS, GROUP, n_chunk_p, Q_BLK).transpose(0, 2, 1, 3).reshape(KV_HEADS, n_chunk_p, GROUP * Q_BLK)
    oc_p, qsel, qflag = _cmp_prompt(qa_p, kc_pad, vc_pad, m_pad, cb, gate_p, bp, tp)
    sb = jnp.stack(sb)
    sb = sb.reshape(4, KV_HEADS, GROUP, KEY_TILE, Q_BLK).transpose(0, 1, 3, 2, 4).reshape(4, KV_HEADS, KEY_TILE, GROUP * Q_BLK)
    os_p = _slc_prompt(qsel, ska, sv, sb, gate_p, bp, tp)
    wb = wb.reshape(KV_HEADS, GROUP, n_wkey, Q_BLK).transpose(0, 2, 1, 3).reshape(KV_HEADS, n_wkey, GROUP * Q_BLK)
    wka_front = jnp.broadcast_to(jnp.tile(pad_row, KV_HEADS).astype(BF16), (bp, WINDOW, KV_HEADS * LANES))
    wka_pad = jnp.concatenate([wka_front, wka.reshape(bp, tp, -1)], axis=1)
    v_rows = wv.shape[0]
    wv_pad = jnp.concatenate([jnp.zeros((v_rows, bp, WINDOW), BF16), wv.reshape(v_rows, bp, tp)], axis=2)
    ow_p = _win_prompt(qflag, wka_pad, wv_pad.reshape(v_rows, bp * (WINDOW + tp)), wb, gate_p, bp, tp)
    wo = b_w_out[0].astype(BF16)
    y_p = _nsa_out(oc_p.reshape(bp * tp, dq), os_p.reshape(bp * tp, dq), ow_p.reshape(bp * tp, dq), sz_p, xp2,
                   wo, b_norm_post[0], tm)

    n_chunk_s = kvc_s.shape[2]
    nc_s = n_chunk_s - L_CMP // CMP_STRIDE + 1
    nb_s = -(-(past + tq) // L_SLC)
    n_sk = past + LANES
    n_wk = w_keep + LANES
    m_t = _overlap_matrix(n_chunk_s, 0, nc_s, nb_s).T
    e_np = np.zeros((LANES, n_sk), np.float32)
    e_np[HEAD_DIM + np.arange(n_sk) // L_SLC, np.arange(n_sk)] = 1.0
    e_s = jnp.asarray(e_np, BF16)
    rows_s = N_HEADS * tq
    cbs, sbs, wbs = [t.reshape(rows_s, t.shape[-1]) for t in _bias_tiles(rel_bias, [
        _tile_spec(tq, n_chunk_s, -CMP_STRIDE, past - (L_CMP - 1)),
        _tile_spec(tq, n_sk, -1, past),
        _tile_spec(tq, n_wk, -1, w_keep, WINDOW)])]
    gate_r = gate_s[:, :n_gate].reshape(bd, tq, N_BRANCH, N_HEADS).transpose(0, 3, 1, 2).reshape(bd * rows_s, N_BRANCH)
    gate_r = jnp.pad(gate_r, ((0, 0), (0, LANES - N_BRANCH)))
    to_t = lambda a: a.transpose(0, 2, 3, 4, 1).reshape(a.shape[0], width, a.shape[1])
    oc_s, os_s, ow_s = _decode_attn(qa_s, kvc_s, to_t(cache_slc), pt_flat, slc_s, to_t(state_win), win_s,
                                    m_t, e_s, cbs, sbs, wbs, gate_r, bd, tq, n_pages)
    y_s = _nsa_out(oc_s, os_s, ow_s, sz_s, xs, wo, b_norm_post[0], tm_s)

    kv5 = lambda a, b, t: a.reshape(b, t, 2, KV_HEADS, HEAD_DIM)
    kv5_t = lambda a: a.reshape(bp, 2, KV_HEADS, HEAD_DIM, a.shape[2]).transpose(0, 4, 1, 2, 3)
    win_all = jnp.concatenate([state_win, kv5(win_s, bd, tq)], axis=1)
    return (y_p.reshape(bp, tp, d), y_s.reshape(bd, tq, d), jnp.stack(conv_p), jnp.stack(conv_s),
            kv5_t(cmp_p), kv5(cmp_s, bd, tq), kv5_t(slc_p), kv5(slc_s, bd, tq),
            kv5_t(win_p[:, :, -min(WINDOW, tp):]), win_all[:, -min(WINDOW, win_all.shape[1]):])
```

```python
import functools
import math

import jax
import jax.numpy as jnp
from jax import lax
from jax.experimental import pallas as pl
from jax.experimental.pallas import tpu as pltpu

F32 = jnp.float32
BF16 = jnp.bfloat16

EPS = 1e-6
NEG = -1e30
BIG = 1e9
M_FLOOR = -1e20

HEAD_DIM = 64
KV_HEADS = 4
N_HEADS = 16
GROUP = N_HEADS // KV_HEADS
N_BRANCH = 3
CONV_W = 31
L_CMP = 32
CMP_STRIDE = 16
L_SLC = 64
N_SEL = 16
WINDOW = 512
Q_BLK = 128
NUM_BUCKETS = 32
MAX_DISTANCE = 128
MAX_EXACT = NUM_BUCKETS // 2
PAGE_SIZE = 128
LANES = 128
HALO = 32
CONV_TAIL = 16

VMEM_LIMIT = 56 * 1024 * 1024


def _cparams(*sem):
    return pltpu.CompilerParams(dimension_semantics=sem, vmem_limit_bytes=VMEM_LIMIT)


def _sigmoid(x):
    return 1.0 / (1.0 + jnp.exp(-x))


def _rms(x, g):
    return x * lax.rsqrt(jnp.mean(x * x, axis=-1, keepdims=True) + EPS) * g


def _lane(shape):
    return lax.broadcasted_iota(jnp.int32, shape, len(shape) - 1)


def _swap_halves(x):
    return pltpu.roll(x, HEAD_DIM, axis=x.ndim - 1)


def _glu_proj_body(x_ref, g_ref, w_ref, glu_ref, sz_ref):
    di = glu_ref.shape[-1]
    h = _rms(x_ref[...], g_ref[...])
    u = jnp.dot(h.astype(BF16), w_ref[...], preferred_element_type=F32)
    z = u[:, 2 * di:]
    glu_ref[...] = u[:, :di] * _sigmoid(u[:, di:2 * di])
    sz_ref[...] = z * _sigmoid(z)


def _glu_proj(x, g, w_bf, tm):
    rows, d = x.shape
    di = w_bf.shape[1] // 3
    return pl.pallas_call(
        _glu_proj_body,
        grid=(rows // tm,),
        in_specs=[pl.BlockSpec((tm, d), lambda i: (i, 0)),
                  pl.BlockSpec((1, d), lambda i: (0, 0)),
                  pl.BlockSpec((d, 3 * di), lambda i: (0, 0))],
        out_specs=[pl.BlockSpec((tm, di), lambda i: (i, 0)),
                   pl.BlockSpec((tm, di), lambda i: (i, 0))],
        out_shape=[jax.ShapeDtypeStruct((rows, di), F32)] * 2,
        compiler_params=_cparams("parallel"),
        name="glu_proj",
    )(x, g.reshape(1, d), w_bf)


def _conv_tail(c, sz, x, lg_ref, lb_ref, wo_ref, gp_ref):
    mu = jnp.mean(c, axis=-1, keepdims=True)
    cc = c - mu
    var = jnp.mean(cc * cc, axis=-1, keepdims=True)
    y = cc * lax.rsqrt(var + EPS) * lg_ref[...] + lb_ref[...]
    y = y * _sigmoid(y) * sz
    o = jnp.dot(y.astype(BF16), wo_ref[...], preferred_element_type=F32)
    return x + _rms(o, gp_ref[...])


def _conv_prompt_body(glu_ref, prev_ref, sz_ref, x_ref, cw_ref, cb_ref, lg_ref, lb_ref, wo_ref, gp_ref,
                      out_ref, full_ref, c_ref, *, tm):
    t = pl.program_id(1)
    full_ref[0:HALO, :] = jnp.where(t > 0, prev_ref[0], 0.0)
    full_ref[HALO:HALO + tm, :] = glu_ref[0]
    full_ref[HALO + tm:, :] = jnp.zeros((CONV_TAIL, full_ref.shape[1]), F32)
    d = c_ref.shape[-1]
    first = HALO - (CONV_W - 1)
    sub = 8
    half = tm // 2
    for lc in range(d // LANES):
        ln = slice(lc * LANES, (lc + 1) * LANES)
        for r0 in range(0, tm, half):
            acc = jnp.zeros((half, LANES), F32)
            for b in range(sub):
                z = None
                for a in range(-(-(CONV_W - b) // sub)):
                    x = full_ref[r0 + sub * a:r0 + sub * a + half + 2 * sub, ln]
                    term = cw_ref[sub * a + b:sub * a + b + 1, ln] * x
                    z = term if z is None else z + term
                acc = acc + z[first + b:first + b + half]
            c_ref[r0:r0 + half, ln] = acc + cb_ref[:, ln]
    out_ref[0] = _conv_tail(c_ref[...], sz_ref[0], x_ref[0], lg_ref, lb_ref, wo_ref, gp_ref)


def _conv_prompt(glu, sz, x, cw, cb, lg, lb, wo_bf, gp, tm):
    b, t, d = x.shape
    di = glu.shape[-1]
    per = tm // HALO
    vec = lambda n: pl.BlockSpec((1, n), lambda i, j: (0, 0))
    return pl.pallas_call(
        functools.partial(_conv_prompt_body, tm=tm),
        grid=(b, t // tm),
        in_specs=[pl.BlockSpec((1, tm, di), lambda i, j: (i, j, 0)),
                  pl.BlockSpec((1, HALO, di), lambda i, j: (i, jnp.maximum(j * per - 1, 0), 0)),
                  pl.BlockSpec((1, tm, di), lambda i, j: (i, j, 0)),
                  pl.BlockSpec((1, tm, d), lambda i, j: (i, j, 0)),
                  pl.BlockSpec((CONV_W, di), lambda i, j: (0, 0)),
                  vec(di), vec(di), vec(di),
                  pl.BlockSpec((di, d), lambda i, j: (0, 0)),
                  vec(d)],
        out_specs=pl.BlockSpec((1, tm, d), lambda i, j: (i, j, 0)),
        out_shape=jax.ShapeDtypeStruct((b, t, d), F32),
        scratch_shapes=[pltpu.VMEM((HALO + tm + CONV_TAIL, di), F32), pltpu.VMEM((tm, di), F32)],
        compiler_params=_cparams("parallel", "arbitrary"),
        name="conv_prompt",
    )(glu, glu, sz, x, cw, cb.reshape(1, di), lg.reshape(1, di), lb.reshape(1, di), wo_bf, gp.reshape(1, d))


def _conv_sample_body(glu_ref, st_ref, sz_ref, x_ref, cw_ref, cb_ref, lg_ref, lb_ref, wo_ref, gp_ref,
                      out_ref, full_ref, c_ref, *, nb, tq):
    d = c_ref.shape[-1]
    first = HALO - (CONV_W - 1)
    full_ref[:, first:HALO, :] = st_ref[...]
    full_ref[:, HALO:HALO + tq, :] = glu_ref[...].reshape(nb, tq, d)
    for lc in range(d // LANES):
        ln = slice(lc * LANES, (lc + 1) * LANES)
        acc = jnp.zeros((nb, tq, LANES), F32)
        for j in range(CONV_W):
            acc = acc + cw_ref[j:j + 1, ln] * full_ref[:, first + j:first + j + tq, ln]
        c_ref[:, ln] = (acc + cb_ref[:, ln]).reshape(nb * tq, LANES)
    out_ref[...] = _conv_tail(c_ref[...], sz_ref[...], x_ref[...], lg_ref, lb_ref, wo_ref, gp_ref)


def _conv_sample(glu, state, sz, x, cw, cb, lg, lb, wo_bf, gp, nb):
    n_seq = state.shape[0]
    rows, d = x.shape
    di = glu.shape[-1]
    tq = rows // n_seq
    vec = lambda n: pl.BlockSpec((1, n), lambda i: (0, 0))
    return pl.pallas_call(
        functools.partial(_conv_sample_body, nb=nb, tq=tq),
        grid=(n_seq // nb,),
        in_specs=[pl.BlockSpec((nb * tq, di), lambda i: (i, 0)),
                  pl.BlockSpec((nb, CONV_W - 1, di), lambda i: (i, 0, 0)),
                  pl.BlockSpec((nb * tq, di), lambda i: (i, 0)),
                  pl.BlockSpec((nb * tq, d), lambda i: (i, 0)),
                  pl.BlockSpec((CONV_W, di), lambda i: (0, 0)),
                  vec(di), vec(di), vec(di),
                  pl.BlockSpec((di, d), lambda i: (0, 0)),
                  vec(d)],
        out_specs=pl.BlockSpec((nb * tq, d), lambda i: (i, 0)),
        out_shape=jax.ShapeDtypeStruct((rows, d), F32),
        scratch_shapes=[pltpu.VMEM((nb, HALO + tq, di), F32), pltpu.VMEM((nb * tq, di), F32)],
        compiler_params=_cparams("parallel"),
        name="conv_sample",
    )(glu, state, sz, x, cw, cb.reshape(1, di), lg.reshape(1, di), lb.reshape(1, di), wo_bf, gp.reshape(1, d))


def _head_major(col_pair, head):
    return col_pair if head % 2 == 0 else _swap_halves(col_pair)


def _nsa_proj_body(x_ref, gkv_ref, gq_ref, wkv_ref, wq_ref, wz_ref, wg_ref,
                   cmp_ref, slc_ref, win_ref, qa_ref, sz_ref, gate_ref, *aug_refs, seq_len, tm):
    x = x_ref[...]
    xn = x * lax.rsqrt(jnp.mean(x * x, axis=-1, keepdims=True) + EPS)
    hkv = (xn * gkv_ref[...]).astype(BF16)
    hq = (xn * gq_ref[...]).astype(BF16)
    kv = jnp.dot(hkv, wkv_ref[...], preferred_element_type=F32)
    width = 2 * KV_HEADS * HEAD_DIM
    if aug_refs:
        kv_t = [kv[:, b * width:(b + 1) * width].T for b in range(N_BRANCH)]
        cmp_ref[0] = kv_t[0]
        slc_ref[0] = kv_t[1]
        win_ref[0] = kv_t[2]
    else:
        cmp_ref[...] = kv[:, :width]
        slc_ref[...] = kv[:, width:2 * width]
        win_ref[...] = kv[:, 2 * width:]
    lane = _lane((tm, LANES))
    low = lane < HEAD_DIM
    uq = jnp.dot(hq, wq_ref[...], preferred_element_type=F32)
    scale = HEAD_DIM ** -0.5
    for h in range(N_HEADS):
        qh = _head_major(uq[:, (h // 2) * LANES:(h // 2 + 1) * LANES], h)
        qa_ref[:, h * LANES:(h + 1) * LANES] = jnp.where(low, qh * scale, 0.0).astype(BF16)
    z = jnp.dot(hq, wz_ref[...], preferred_element_type=F32)
    sz_ref[...] = z * _sigmoid(z)
    gate_ref[...] = _sigmoid(jnp.dot(hq, wg_ref[...], preferred_element_type=F32))
    if aug_refs:
        ska_ref, sv_ref, wka_ref, wv_ref = aug_refs
        row = pl.program_id(0) * tm + lax.broadcasted_iota(jnp.int32, (tm, LANES), 0)
        blk = (row % seq_len) // L_SLC
        onehot = (lane - HEAD_DIM == blk).astype(F32)
        kw = KV_HEADS * HEAD_DIM
        for h in range(KV_HEADS):
            ks = _head_major(kv[:, width + (h // 2) * LANES:width + (h // 2 + 1) * LANES], h)
            ska_ref[:, h * LANES:(h + 1) * LANES] = jnp.where(low, ks, onehot).astype(BF16)
            kwn = _head_major(kv[:, 2 * width + (h // 2) * LANES:2 * width + (h // 2 + 1) * LANES], h)
            wka_ref[:, h * LANES:(h + 1) * LANES] = jnp.where(low, kwn, 0.0).astype(BF16)
        extra = (lax.broadcasted_iota(jnp.int32, (V_PAD, tm), 0) == 0).astype(BF16)
        for dst, src in ((sv_ref, kv_t[1]), (wv_ref, kv_t[2])):
            for h in range(KV_HEADS):
                dst[h * V_ROWS:h * V_ROWS + HEAD_DIM, :] = src[kw + h * HEAD_DIM:kw + (h + 1) * HEAD_DIM].astype(BF16)
                dst[h * V_ROWS + HEAD_DIM:(h + 1) * V_ROWS, :] = extra


def _nsa_proj(x, g_kv, g_q, wkv_bf, wq_bf, wz_bf, wg_bf, tm, seq_len=None):
    rows, d = x.shape
    width = 2 * KV_HEADS * HEAD_DIM
    row = lambda n: pl.BlockSpec((tm, n), lambda i: (i, 0))
    full = lambda a: pl.BlockSpec(a.shape, lambda i: (0, 0))
    out_specs = [row(width), row(width), row(width), row(N_HEADS * LANES), row(wz_bf.shape[1]), row(LANES)]
    out_shape = [jax.ShapeDtypeStruct((rows, width), F32)] * 3 + [
        jax.ShapeDtypeStruct((rows, N_HEADS * LANES), BF16),
        jax.ShapeDtypeStruct((rows, wz_bf.shape[1]), F32),
        jax.ShapeDtypeStruct((rows, LANES), F32)]
    if seq_len is not None:
        per_seq = seq_len // tm
        kv_t_spec = pl.BlockSpec((1, width, tm), lambda i: (i // per_seq, 0, i % per_seq))
        out_specs[:N_BRANCH] = [kv_t_spec] * N_BRANCH
        out_shape[:N_BRANCH] = [jax.ShapeDtypeStruct((rows // seq_len, width, seq_len), F32)] * N_BRANCH
        out_specs += [row(KV_HEADS * LANES), pl.BlockSpec((KV_HEADS * V_ROWS, tm), lambda i: (0, i))] * 2
        out_shape += [jax.ShapeDtypeStruct((rows, KV_HEADS * LANES), BF16),
                      jax.ShapeDtypeStruct((KV_HEADS * V_ROWS, rows), BF16)] * 2
    gkv = g_kv.reshape(1, d)
    gq = g_q.reshape(1, d)
    return pl.pallas_call(
        functools.partial(_nsa_proj_body, seq_len=seq_len, tm=tm),
        grid=(rows // tm,),
        in_specs=[row(d), full(gkv), full(gq), full(wkv_bf), full(wq_bf), full(wz_bf), full(wg_bf)],
        out_specs=out_specs,
        out_shape=out_shape,
        compiler_params=_cparams("parallel"),
        name="nsa_proj",
    )(x, gkv, gq, wkv_bf, wq_bf, wz_bf, wg_bf)


def _compress_body(pt_ref, *refs, n_pages):
    del pt_ref
    page_refs = refs[:n_pages]
    w1_ref, pe_ref, w2_ref, out_ref, a_ref = refs[n_pages:]
    width, page_rows = page_refs[0].shape[1:]
    n_col = width // LANES
    cpp = page_rows // CMP_STRIDE
    n_chunk = n_pages * cpp
    hid = w2_ref.shape[2]
    low = _lane((cpp, LANES)) < HEAD_DIM
    def assemble(c):
        for p in range(n_pages):
            col_t = page_refs[p][0, c * LANES:(c + 1) * LANES, :]
            by_s = pltpu.einshape("(ns)f->(sn)f", col_t.T, n=cpp)
            for s in range(0, CMP_STRIDE, 2):
                b0 = by_s[s * cpp:(s + 1) * cpp]
                b1 = by_s[(s + 1) * cpp:(s + 2) * cpp]
                dst = (slice(p * cpp, (p + 1) * cpp), slice((s // 2) * LANES, (s // 2 + 1) * LANES))
                a_ref[(2 * c,) + dst] = jnp.where(low, b0, _swap_halves(b1))
                a_ref[(2 * c + 1,) + dst] = jnp.where(low, _swap_halves(b0), b1)

    for kv in range(2):
        for c in range(kv * n_col // 2, (kv + 1) * n_col // 2):
            assemble(c)
        w1 = w1_ref[kv]
        pe = jnp.dot(pe_ref[kv].astype(BF16), w1, preferred_element_type=F32)
        pe_term = pe[0:1, :hid] + pe[1:2, hid:]
        a_kv = a_ref[kv * KV_HEADS:(kv + 1) * KV_HEADS].reshape(KV_HEADS * n_chunk, CMP_STRIDE * HEAD_DIM)
        parts = jnp.dot(a_kv.astype(BF16), w1, preferred_element_type=F32)
        for hp in range(KV_HEADS // 2):
            pair = jnp.zeros((n_chunk, LANES), F32)
            for par in range(2):
                part = parts[(2 * hp + par) * n_chunk:(2 * hp + par + 1) * n_chunk]
                pre = part[:, :hid] + pltpu.roll(part[:, hid:], n_chunk - 1, axis=0) + pe_term
                mid = pre * _sigmoid(pre)
                pair = pair + jnp.dot(mid.astype(BF16), w2_ref[kv, par], preferred_element_type=F32)
            col = kv * (KV_HEADS // 2) + hp
            out_ref[0, col * LANES:(col + 1) * LANES, :] = pair.T


def _compress(rows_t, page_table, w1cat_bf, pe8, w2_bf, n_pages, page_rows):
    width = rows_t.shape[1]
    n_chunk = n_pages * page_rows // CMP_STRIDE
    if page_table is None:
        n_seq = rows_t.shape[0]
        page_table = jnp.zeros((1,), jnp.int32)
        page_spec = lambda j: pl.BlockSpec((1, width, page_rows), lambda i, pt: (i, 0, j))
    else:
        n_seq = page_table.shape[0] // n_pages
        page_spec = lambda j: pl.BlockSpec((1, width, page_rows), lambda i, pt: (pt[i * n_pages + j], 0, 0))
    full = lambda a: pl.BlockSpec(a.shape, lambda i, pt: (0,) * a.ndim)
    grid_spec = pltpu.PrefetchScalarGridSpec(
        num_scalar_prefetch=1,
        grid=(n_seq,),
        in_specs=[page_spec(j) for j in range(n_pages)] + [full(w1cat_bf), full(pe8), full(w2_bf)],
        out_specs=pl.BlockSpec((1, width, n_chunk), lambda i, pt: (i, 0, 0)),
        scratch_shapes=[pltpu.VMEM((2 * KV_HEADS, n_chunk, CMP_STRIDE * HEAD_DIM), F32)],
    )
    return pl.pallas_call(
        functools.partial(_compress_body, n_pages=n_pages),
        grid_spec=grid_spec,
        out_shape=jax.ShapeDtypeStruct((n_seq, width, n_chunk), F32),
        compiler_params=_cparams("parallel"),
        name="compress",
    )(page_table, *([rows_t] * n_pages), w1cat_bf, pe8, w2_bf)


def _tile_spec(rows, width, lane_step, offset, hi=1 << 30, row_step=1):
    return (rows, width, row_step, lane_step, offset, hi)


def _bias_tiles_body(rb_ref, *out_refs, specs):
    h = pl.program_id(0)
    far = rb_ref[NUM_BUCKETS - 1, h]
    for out_ref, (rows, width, row_step, lane_step, offset, hi) in zip(out_refs, specs):
        shape = (rows, width)
        d = row_step * lax.broadcasted_iota(jnp.int32, shape, 0) + lane_step * _lane(shape) + offset
        n = jnp.maximum(d, 0)
        nf = jnp.maximum(n, 1).astype(F32)
        large = MAX_EXACT + (jnp.log(nf / MAX_EXACT) / math.log(MAX_DISTANCE / MAX_EXACT)
                             * (NUM_BUCKETS - MAX_EXACT)).astype(jnp.int32)
        large = jnp.minimum(large, NUM_BUCKETS - 1)
        bucket = jnp.where(n < MAX_EXACT, n, large)
        val = jnp.zeros(shape, F32)
        for k in range(NUM_BUCKETS - 1):
            val = jnp.where(bucket == k, rb_ref[k, h] - far, val)
        out_ref[0] = jnp.where((d >= 0) & (d <= hi), val, NEG)


def _bias_tiles(rel_bias, specs):
    return pl.pallas_call(
        functools.partial(_bias_tiles_body, specs=specs),
        grid=(N_HEADS,),
        in_specs=[pl.BlockSpec(memory_space=pltpu.SMEM)],
        out_specs=[pl.BlockSpec((1, s[0], s[1]), lambda h: (h, 0, 0)) for s in specs],
        out_shape=[jax.ShapeDtypeStruct((N_HEADS, s[0], s[1]), F32) for s in specs],
        compiler_params=_cparams("parallel"),
        name="bias_tiles",
    )(rel_bias)


_NT = (((1,), (1,)), ((), ()))


def _softmax_rows(s):
    m = jnp.maximum(jnp.max(s, axis=-1, keepdims=True), M_FLOOR)
    p = jnp.exp(s - m)
    return p, jnp.sum(p, axis=-1, keepdims=True)


def _gate_col(gate, col):
    return jnp.sum(jnp.where(_lane(gate.shape) == col, gate, 0.0), axis=-1, keepdims=True)


def _pair_columns(o_even, o_odd, valid_half):
    lo = o_even if valid_half == 0 else _swap_halves(o_even)
    hi = o_odd if valid_half == 1 else _swap_halves(o_odd)
    return jnp.where(_lane(lo.shape) < HEAD_DIM, lo, hi)


def _select_blocks_t(imp_t, tb, n_sel):
    nb, nt = imp_t.shape
    sub = 8
    j = lax.broadcasted_iota(jnp.int32, (nb, nt), 0)
    valid = j <= tb
    forced = (j == 0) | (j == tb) | (j == tb - 1)
    score = jnp.where(valid, jnp.where(forced, BIG, imp_t), -BIG)
    groups = [score[lo:lo + sub] for lo in range(0, nb, sub)]
    jr = lax.broadcasted_iota(jnp.int32, (sub, nt), 0)
    counts = [jnp.zeros((sub, nt), F32) for _ in groups]
    for i in range(nb):
        row = score[i:i + 1]
        for r, grp in enumerate(groups):
            lo = r * sub
            if lo > i:
                one = jnp.where(row >= grp, 1.0, 0.0)
            elif lo + sub - 1 <= i:
                one = jnp.where(row > grp, 1.0, 0.0)
            else:
                one = jnp.where(jr + lo > i, jnp.where(row >= grp, 1.0, 0.0), jnp.where(row > grp, 1.0, 0.0))
            counts[r] = counts[r] + one
    rank = jnp.concatenate(counts, axis=0)
    return jnp.where((rank < n_sel) & valid, 0.0, NEG)


def _cmp_prompt_body(qa_ref, kc_ref, vc_ref, m_ref, cb_ref, gate_ref, sz_ref, oc_ref, qsel_ref, qflag_ref, *, n_key):
    qt = pl.program_id(1)
    start = pl.multiple_of(8 * qt + 8, 8)
    kwin = kc_ref[0, pl.ds(start, n_key), :]
    vwin = vc_ref[0, pl.ds(start, n_key), :]
    mwin_t = m_ref[pl.ds(start, n_key), :].T
    feat = lax.broadcasted_iota(jnp.int32, (LANES, Q_BLK), 0)
    pos = qt * Q_BLK + lax.broadcasted_iota(jnp.int32, (1, Q_BLK), 1)
    tb = pos // L_SLC
    gate = gate_ref[...]
    n_pair = GROUP // 2
    imps, q_feats = [], []
    for kvh in range(KV_HEADS):
        ka = kwin[:, kvh * LANES:(kvh + 1) * LANES].astype(BF16)
        vpt = vwin[:, (kvh // 2) * LANES:(kvh // 2 + 1) * LANES].T
        vpt = vpt[(kvh % 2) * HEAD_DIM:(kvh % 2 + 1) * HEAD_DIM].astype(BF16)
        qts = [qa_ref[0, :, h * LANES:(h + 1) * LANES].astype(F32).T for h in range(kvh * GROUP, (kvh + 1) * GROUP)]
        pc_sum = jnp.zeros((n_key, Q_BLK), F32)
        for gp in range(n_pair):
            qa = jnp.concatenate([jnp.where(feat == HEAD_DIM, NEG, qts[2 * gp + e]) for e in range(2)], axis=1)
            qa = qa.astype(BF16)
            qflag_ref[0, kvh, :, 2 * gp * Q_BLK:(2 * gp + 2) * Q_BLK] = qa
            s = jnp.dot(ka, qa, preferred_element_type=F32)
            s = s + cb_ref[kvh, :, 2 * gp * Q_BLK:(2 * gp + 2) * Q_BLK]
            m = jnp.maximum(jnp.max(s, axis=0, keepdims=True), M_FLOOR)
            p = jnp.exp(s - m)
            inv = 1.0 / jnp.maximum(jnp.sum(p, axis=0, keepdims=True), 1e-30)
            pc = p * inv
            pc_sum = pc_sum + pc[:, :Q_BLK] + pc[:, Q_BLK:]
            o = jnp.dot(vpt, p.astype(BF16), preferred_element_type=F32) * inv
            col = kvh * n_pair + gp
            _store_chain(oc_ref, o, gate, kvh * GROUP + 2 * gp, col, sz_ref[:, col * LANES:(col + 1) * LANES])
        imps.append(jnp.dot(mwin_t, pc_sum, preferred_element_type=F32, precision=lax.Precision.HIGHEST)[HEAD_DIM:])
        q_feats.append([qt_[:HEAD_DIM] for qt_ in qts])
    n_blk = imps[0].shape[0]
    n_valid = (qt + 1) * (Q_BLK // L_SLC)
    step = n_blk // 4
    for n_rank in range(step, n_blk + 1, step):
        @pl.when((n_valid > n_rank - step) & (n_valid <= n_rank))
        def _():
            for kvh in range(KV_HEADS):
                selneg_t = _select_blocks_t(imps[kvh][:n_rank], tb, N_SEL)
                if n_rank < n_blk:
                    selneg_t = jnp.concatenate([selneg_t, jnp.full((n_blk - n_rank, Q_BLK), NEG, F32)], axis=0)
                for g in range(GROUP):
                    qsel_ref[0, kvh, :, g * Q_BLK:(g + 1) * Q_BLK] = jnp.concatenate(
                        [q_feats[kvh][g], selneg_t], axis=0).astype(BF16)


def _cmp_prompt(qa, kc_pad, vc_pad, m_pad, cb, gate, sz, batch, seq):
    nq = seq // Q_BLK
    n_key = kc_pad.shape[1] // 2
    d_out = N_HEADS * HEAD_DIM
    return pl.pallas_call(
        functools.partial(_cmp_prompt_body, n_key=n_key),
        grid=(batch, nq),
        in_specs=[pl.BlockSpec((1, Q_BLK, N_HEADS * LANES), lambda b, t: (b, t, 0)),
                  pl.BlockSpec((1,) + kc_pad.shape[1:], lambda b, t: (b, 0, 0)),
                  pl.BlockSpec((1,) + vc_pad.shape[1:], lambda b, t: (b, 0, 0)),
                  pl.BlockSpec(m_pad.shape, lambda b, t: (0, 0)),
                  pl.BlockSpec(cb.shape, lambda b, t: (0, 0, 0)),
                  pl.BlockSpec((Q_BLK, LANES), lambda b, t: (b * nq + t, 0)),
                  pl.BlockSpec((Q_BLK, d_out), lambda b, t: (b * nq + t, 0))],
        out_specs=[pl.BlockSpec((1, Q_BLK, d_out), lambda b, t: (b, t, 0))]
        + [pl.BlockSpec((1, KV_HEADS, LANES, GROUP * Q_BLK), lambda b, t: (b * nq + t, 0, 0, 0))] * 2,
        out_shape=[jax.ShapeDtypeStruct((batch, seq, d_out), F32)]
        + [jax.ShapeDtypeStruct((batch * nq, KV_HEADS, LANES, GROUP * Q_BLK), BF16)] * 2,
        compiler_params=_cparams("parallel", "parallel"),
        name="cmp_prompt",
    )(qa.reshape(batch, seq, -1), kc_pad, vc_pad, m_pad, cb, gate, sz)


KEY_TILE = 256
COL_CHAIN = 256
WIN_TILES = 2
V_PAD = 16
V_ROWS = HEAD_DIM + V_PAD


def _store_chain(out_ref, o_t, gate, gate_col0, lane_col, sz, row0=0):
    pair = jnp.concatenate([o_t[:, :Q_BLK], o_t[:, Q_BLK:]], axis=0).T
    g = jnp.where(_lane(pair.shape) < HEAD_DIM, _gate_col(gate, gate_col0), _gate_col(gate, gate_col0 + 1))
    out_ref[0, row0:row0 + Q_BLK, lane_col * LANES:(lane_col + 1) * LANES] = pair * g * sz


_PAIR_CHAINS = [(e, c) for e in range(2) for c in range(GROUP * Q_BLK // COL_CHAIN)]


def _slc_prompt_body(q_ref, k_ref, v_ref, sb_ref, gate_ref, sz_ref, out_ref, m_ref, acc_ref, p_ref, s_ref):
    pr = pl.program_id(1)
    qt = pl.program_id(2)
    m_ref[...] = jnp.full(m_ref.shape, M_FLOOR, F32)
    acc_ref[...] = jnp.zeros(acc_ref.shape, F32)
    p_ref[...] = jnp.zeros(p_ref.shape, BF16)
    last = (qt * Q_BLK) // KEY_TILE
    cs = lambda c: slice(c * COL_CHAIN, (c + 1) * COL_CHAIN)

    def pv_of(kt):
        base = pl.multiple_of(kt * KEY_TILE, KEY_TILE)
        return [jnp.dot(v_ref[e * V_ROWS:(e + 1) * V_ROWS, pl.ds(base, KEY_TILE)], p_ref[i],
                        preferred_element_type=F32) for i, (e, c) in enumerate(_PAIR_CHAINS)]

    def scores_of(kt):
        base = pl.multiple_of(kt * KEY_TILE, KEY_TILE)
        ks = [k_ref[0, pl.ds(base, KEY_TILE), e * LANES:(e + 1) * LANES] for e in range(2)]
        return [jnp.dot(ks[e], q_ref[0, e, :, cs(c)], preferred_element_type=F32) for e, c in _PAIR_CHAINS]

    def make_step(with_bias):
        def step(kt, carry):
            s_next = scores_of(jnp.minimum(kt + 1, last))
            pvs = pv_of(jnp.maximum(kt - 1, 0))
            ss = [s_ref[i] for i in range(len(_PAIR_CHAINS))]
            if with_bias:
                case = qt % 2 + 2 * (last - kt)
                ss = [s + sb_ref[case, 2 * pr + e, :, cs(c)] for s, (e, c) in zip(ss, _PAIR_CHAINS)]
            alphas = []
            for i, s in enumerate(ss):
                m_old = m_ref[i]
                m_new = jnp.maximum(m_old, jnp.max(s, axis=0, keepdims=True))
                alphas.append(jnp.exp(m_old - m_new))
                m_ref[i] = m_new
                p_ref[i] = jnp.exp(s - m_new).astype(BF16)
            for i in range(len(_PAIR_CHAINS)):
                acc_ref[i] = alphas[i] * (acc_ref[i] + pvs[i])
                s_ref[i] = s_next[i]
            return carry
        return step

    for i, s in enumerate(scores_of(0)):
        s_ref[i] = s
    n_far = jnp.maximum(last - 1, 0)
    lax.fori_loop(0, n_far, make_step(False), 0)
    lax.fori_loop(n_far, last + 1, make_step(True), 0)
    gate = gate_ref[...]
    pvs = pv_of(last)
    for i, (e, c) in enumerate(_PAIR_CHAINS):
        o = acc_ref[i] + pvs[i]
        o = o[:HEAD_DIM] * (1.0 / jnp.maximum(o[HEAD_DIM:HEAD_DIM + 1], 1e-30))
        _store_chain(out_ref, o, gate, N_HEADS + (2 * pr + e) * GROUP + 2 * c, i, sz_ref[:, i * LANES:(i + 1) * LANES])


def _slc_prompt(qsel_t, ska, sv_t, sb_t, gate, sz, batch, seq):
    nq = seq // Q_BLK
    cols = GROUP * Q_BLK
    d_out = N_HEADS * HEAD_DIM
    n_chain = len(_PAIR_CHAINS)
    return pl.pallas_call(
        _slc_prompt_body,
        grid=(batch, KV_HEADS // 2, nq),
        in_specs=[pl.BlockSpec((1, 2, LANES, cols), lambda b, p, t: (b * nq + t, p, 0, 0)),
                  pl.BlockSpec((1, seq, 2 * LANES), lambda b, p, t: (b, 0, p)),
                  pl.BlockSpec((2 * V_ROWS, seq), lambda b, p, t: (p, b)),
                  pl.BlockSpec(sb_t.shape, lambda b, p, t: (0, 0, 0, 0)),
                  pl.BlockSpec((Q_BLK, LANES), lambda b, p, t: (b * nq + t, 0)),
                  pl.BlockSpec((Q_BLK, 2 * GROUP * HEAD_DIM), lambda b, p, t: (b * nq + t, KV_HEADS // 2 + p))],
        out_specs=pl.BlockSpec((1, Q_BLK, 2 * GROUP * HEAD_DIM), lambda b, p, t: (b, t, p)),
        out_shape=jax.ShapeDtypeStruct((batch, seq, d_out), F32),
        scratch_shapes=[pltpu.VMEM((n_chain, 1, COL_CHAIN), F32),
                        pltpu.VMEM((n_chain, V_ROWS, COL_CHAIN), F32),
                        pltpu.VMEM((n_chain, KEY_TILE, COL_CHAIN), BF16),
                        pltpu.VMEM((n_chain, KEY_TILE, COL_CHAIN), F32)],
        compiler_params=_cparams("parallel", "parallel", "arbitrary"),
        name="slc_prompt",
    )(qsel_t, ska.reshape(batch, seq, -1), sv_t, sb_t, gate, sz)


def _win_prompt_body(q_ref, k_ref, v_ref, wb_ref, gate_ref, sz_ref, out_ref, *, n_key):
    pr = pl.program_id(1)
    t0 = pl.program_id(2) * WIN_TILES
    cs = lambda c: slice(c * COL_CHAIN, (c + 1) * COL_CHAIN)
    chains = [(u, e, c) for u in range(WIN_TILES) for e, c in _PAIR_CHAINS]
    bases = [pl.multiple_of((t0 + u) * Q_BLK, Q_BLK) for u in range(WIN_TILES)]
    ss = []
    for u in range(WIN_TILES):
        ks = [k_ref[0, pl.ds(bases[u], n_key), e * LANES:(e + 1) * LANES] for e in range(2)]
        ss += [jnp.dot(ks[e], q_ref[u, e, :, cs(c)], preferred_element_type=F32) + wb_ref[2 * pr + e, :, cs(c)]
               for e, c in _PAIR_CHAINS]
    ps = []
    for s in ss:
        m = jnp.maximum(jnp.max(s, axis=0, keepdims=True), M_FLOOR)
        ps.append(jnp.exp(s - m).astype(BF16))
    for i, (u, e, c) in enumerate(chains):
        o = jnp.dot(v_ref[e * V_ROWS:(e + 1) * V_ROWS, pl.ds(bases[u], n_key)], ps[i], preferred_element_type=F32)
        o = o[:HEAD_DIM] * (1.0 / jnp.maximum(o[HEAD_DIM:HEAD_DIM + 1], 1e-30))
        gate = gate_ref[u * Q_BLK:(u + 1) * Q_BLK, :]
        j = i % len(_PAIR_CHAINS)
        _store_chain(out_ref, o, gate, 2 * N_HEADS + (2 * pr + e) * GROUP + 2 * c, j,
                     sz_ref[u * Q_BLK:(u + 1) * Q_BLK, j * LANES:(j + 1) * LANES], u * Q_BLK)


def _win_prompt(q_t, wka_pad, wv_t_pad, wb_t, gate, sz, batch, seq):
    nq = seq // Q_BLK
    n_key = WINDOW + Q_BLK
    cols = GROUP * Q_BLK
    d_out = N_HEADS * HEAD_DIM
    padded = wka_pad.shape[1]
    return pl.pallas_call(
        functools.partial(_win_prompt_body, n_key=n_key),
        grid=(batch, KV_HEADS // 2, nq // WIN_TILES),
        in_specs=[pl.BlockSpec((WIN_TILES, 2, LANES, cols), lambda b, p, t: (b * (nq // WIN_TILES) + t, p, 0, 0)),
                  pl.BlockSpec((1, padded, 2 * LANES), lambda b, p, t: (b, 0, p)),
                  pl.BlockSpec((2 * V_ROWS, padded), lambda b, p, t: (p, b)),
                  pl.BlockSpec(wb_t.shape, lambda b, p, t: (0, 0, 0)),
                  pl.BlockSpec((WIN_TILES * Q_BLK, LANES), lambda b, p, t: (b * (nq // WIN_TILES) + t, 0)),
                  pl.BlockSpec((WIN_TILES * Q_BLK, 2 * GROUP * HEAD_DIM),
                               lambda b, p, t: (b * (nq // WIN_TILES) + t, KV_HEADS + p))],
        out_specs=pl.BlockSpec((1, WIN_TILES * Q_BLK, 2 * GROUP * HEAD_DIM), lambda b, p, t: (b, t, p)),
        out_shape=jax.ShapeDtypeStruct((batch, seq, d_out), F32),
        compiler_params=_cparams("parallel", "parallel", "parallel"),
        name="win_prompt",
    )(q_t, wka_pad, wv_t_pad, wb_t, gate, sz)


def _nsa_out_body(oc_ref, os_ref, ow_ref, *refs):
    x_ref, wo_ref, gp_ref, out_ref = refs[-4:]
    d = oc_ref.shape[-1]
    if len(refs) == 5:
        sz_ref = refs[0]
        y = oc_ref[...] * sz_ref[:, :d] + os_ref[...] * sz_ref[:, d:2 * d] + ow_ref[...] * sz_ref[:, 2 * d:]
    else:
        y = oc_ref[...] + os_ref[...] + ow_ref[...]
    o = jnp.dot(y.astype(BF16), wo_ref[...], preferred_element_type=F32)
    out_ref[...] = x_ref[...] + _rms(o, gp_ref[...])


def _nsa_out(oc, os_, ow, sz, x, wo_bf, gp, tm):
    rows, d = x.shape
    dq = oc.shape[-1]
    row = lambda n: pl.BlockSpec((tm, n), lambda i: (i, 0))
    return pl.pallas_call(
        _nsa_out_body,
        grid=(rows // tm,),
        in_specs=[row(dq), row(dq), row(dq)] + ([] if sz is None else [row(N_BRANCH * dq)])
        + [row(d), pl.BlockSpec((dq, d), lambda i: (0, 0)), pl.BlockSpec((1, d), lambda i: (0, 0))],
        out_specs=row(d),
        out_shape=jax.ShapeDtypeStruct((rows, d), F32),
        compiler_params=_cparams("parallel"),
        name="nsa_out",
    )(oc, os_, ow, *([] if sz is None else [sz]), x, wo_bf, gp.reshape(1, d))


def _decode_attn_body(pt_ref, *refs, n_pages, tq, w_keep):
    del pt_ref
    page_refs = refs[:n_pages]
    (qa_ref, kvc_ref, snew_ref, wst_ref, wnew_ref, mt_ref, e_ref, cbs_ref, sbs_ref, wbs_ref, gate_ref,
     oc_ref, os_ref, ow_ref, kt_ref, vt_ref, kwt_ref, vwt_ref) = refs[n_pages:]
    n_past = n_pages * PAGE_SIZE
    kw = KV_HEADS * HEAD_DIM
    rows = N_HEADS * tq

    def new_tile(ref):
        return jnp.concatenate([ref[...], jnp.zeros((LANES - tq, 2 * kw), F32)], axis=0).T

    def fill(k_dst, v_dst, col0, tile):
        k_dst[:, col0:col0 + tile.shape[1]] = tile[:kw].astype(BF16)
        v_dst[:, col0:col0 + tile.shape[1]] = tile[kw:].astype(BF16)

    for p in range(n_pages):
        fill(kt_ref, vt_ref, p * PAGE_SIZE, page_refs[p][0])
    fill(kt_ref, vt_ref, n_past, new_tile(snew_ref))
    fill(kwt_ref, vwt_ref, 0, wst_ref[0])
    fill(kwt_ref, vwt_ref, w_keep, new_tile(wnew_ref))

    zero = jnp.zeros((tq, LANES), F32)
    tiles = [[], []]
    for h in range(N_HEADS):
        kvh = h // GROUP
        piece = qa_ref[:, h * LANES:(h + 1) * LANES]
        piece = piece if kvh % 2 == 0 else _swap_halves(piece)
        for ct in range(2):
            tiles[ct].append(piece if kvh // 2 == ct else zero)
    qbd = jnp.concatenate([jnp.concatenate(t, axis=0) for t in tiles], axis=1).astype(BF16)
    gate = gate_ref[...]

    def branch(k_t, v_t, bias, extra=None):
        s = jnp.dot(qbd, k_t, preferred_element_type=F32) + bias
        if extra is not None:
            s = s + extra
        p, l = _softmax_rows(s)
        inv = 1.0 / jnp.maximum(l, 1e-30)
        o_t = lax.dot_general(v_t, p.astype(BF16), _NT, preferred_element_type=F32)
        return p, inv, o_t.T

    def store(out_ref, o, scale):
        o = o * scale
        for kvh in range(KV_HEADS):
            tile = o[:, (kvh // 2) * LANES:(kvh // 2 + 1) * LANES]
            for gp in range(GROUP // 2):
                h0 = kvh * GROUP + 2 * gp
                col = kvh * (GROUP // 2) + gp
                out_ref[:, col * LANES:(col + 1) * LANES] = _pair_columns(
                    tile[h0 * tq:(h0 + 1) * tq], tile[(h0 + 1) * tq:(h0 + 2) * tq], kvh % 2)

    p, inv, o = branch(kvc_ref[0, :kw, :].astype(BF16), kvc_ref[0, kw:, :].astype(BF16), cbs_ref[...])
    store(oc_ref, o, inv * _gate_col(gate, 0))
    pc = p * inv
    pc_sum = jnp.concatenate(
        [sum(pc[(kvh * GROUP + g) * tq:(kvh * GROUP + g + 1) * tq] for g in range(GROUP)) for kvh in range(KV_HEADS)],
        axis=0)
    imp_t = lax.dot_general(mt_ref[...], pc_sum, _NT, preferred_element_type=F32, precision=lax.Precision.HIGHEST)
    n_col = KV_HEADS * tq
    nb_pad = -(-(-(-(n_past + tq) // L_SLC)) // 8) * 8
    tb = (n_past + lax.broadcasted_iota(jnp.int32, (1, n_col), 1) % tq) // L_SLC
    selneg_t = _select_blocks_t(imp_t[HEAD_DIM:HEAD_DIM + nb_pad], tb, N_SEL)
    sel_t = jnp.concatenate([jnp.zeros((HEAD_DIM, n_col), F32), selneg_t,
                             jnp.zeros((LANES - HEAD_DIM - nb_pad, n_col), F32)], axis=0)
    sel = jnp.concatenate([sel_t, jnp.zeros((LANES, LANES - n_col), F32)], axis=1).T
    sel = jnp.concatenate([sel[kvh * tq:(kvh + 1) * tq] for kvh in range(KV_HEADS) for _ in range(GROUP)], axis=0)
    block_mask = jnp.dot(sel.astype(BF16), e_ref[...], preferred_element_type=F32)
    _, inv, o = branch(kt_ref[...], vt_ref[...], sbs_ref[...], block_mask)
    store(os_ref, o, inv * _gate_col(gate, 1))
    _, inv, o = branch(kwt_ref[...], vwt_ref[...], wbs_ref[...])
    store(ow_ref, o, inv * _gate_col(gate, 2))


def _decode_attn(qa, kvc_t, pages_t, page_table, slc_new, win_state_t, win_new, m_t, e_s, cbs, sbs, wbs, gate_r,
                 n_seq, tq, n_pages):
    width = pages_t.shape[1]
    n_sk = sbs.shape[-1]
    n_wk = wbs.shape[-1]
    w_keep = win_state_t.shape[2]
    d_out = N_HEADS * HEAD_DIM
    rows = N_HEADS * tq
    page_spec = lambda j: pl.BlockSpec((1, width, PAGE_SIZE), lambda i, pt: (pt[i * n_pages + j], 0, 0))
    full = lambda a: pl.BlockSpec(a.shape, lambda i, pt: (0,) * a.ndim)
    row = lambda n: pl.BlockSpec((tq, n), lambda i, pt: (i, 0))
    grid_spec = pltpu.PrefetchScalarGridSpec(
        num_scalar_prefetch=1,
        grid=(n_seq,),
        in_specs=[page_spec(j) for j in range(n_pages)] + [
            row(N_HEADS * LANES),
            pl.BlockSpec((1,) + kvc_t.shape[1:], lambda i, pt: (i, 0, 0)),
            row(width),
            pl.BlockSpec((1, width, w_keep), lambda i, pt: (i, 0, 0)),
            row(width),
            full(m_t), full(e_s), full(cbs), full(sbs), full(wbs),
            pl.BlockSpec((rows, LANES), lambda i, pt: (i, 0))],
        out_specs=[row(d_out)] * 3,
        scratch_shapes=[pltpu.VMEM((width // 2, n_sk), BF16), pltpu.VMEM((width // 2, n_sk), BF16),
                        pltpu.VMEM((width // 2, n_wk), BF16), pltpu.VMEM((width // 2, n_wk), BF16)],
    )
    return pl.pallas_call(
        functools.partial(_decode_attn_body, n_pages=n_pages, tq=tq, w_keep=w_keep),
        grid_spec=grid_spec,
        out_shape=[jax.ShapeDtypeStruct((n_seq * tq, d_out), F32)] * 3,
        compiler_params=_cparams("parallel"),
        name="decode_attn",
    )(page_table, *([pages_t] * n_pages), qa, kvc_t, slc_new, win_state_t, win_new, m_t, e_s, cbs, sbs, wbs, gate_r)


def _overlap_matrix(n_rows, row0, n_cmp, n_blk):
    import numpy as np
    m = np.zeros((n_rows, LANES), np.float32)
    cs = np.arange(n_cmp)[:, None] * CMP_STRIDE
    js = np.arange(n_blk)[None, :] * L_SLC
    m[row0:row0 + n_cmp, HEAD_DIM:HEAD_DIM + n_blk] = (cs <= js + L_SLC - 1) & (cs + L_CMP - 1 >= js)
    return jnp.asarray(m)


def kernel(x_prompt, x_sample, state_conv, cache_cmp, cache_slc, state_win, page_table, rel_bias, a_norm_pre, a_w_in, a_conv_w, a_conv_b, a_ln_g, a_ln_b, a_w_out, a_norm_post, kv_norm, w_kv, cmp_pe, cmp_w1, cmp_w2, b_norm_pre, b_w_in, b_w_out, b_norm_post):
    import numpy as np
    bp, tp, d = x_prompt.shape
    bd, tq, _ = x_sample.shape
    n_pages = page_table.shape[1]
    past = n_pages * PAGE_SIZE
    w_keep = state_win.shape[1]
    width = 2 * KV_HEADS * HEAD_DIM
    kw = KV_HEADS * HEAD_DIM
    dq = N_HEADS * HEAD_DIM
    assert b_w_in.shape[0] == 1 and tp % (2 * KEY_TILE) == 0 and tp // L_SLC <= HEAD_DIM
    tm = 256
    tm_s = min(tm, bd * tq)

    xp = x_prompt
    xs = x_sample.reshape(bd * tq, d)
    conv_p, conv_s = [], []
    for l in range(a_w_in.shape[0]):
        w_in = a_w_in[l].astype(BF16)
        w_out = a_w_out[l].astype(BF16)
        di = a_w_out.shape[1]
        tail = (a_conv_w[l], a_conv_b[l], a_ln_g[l], a_ln_b[l], w_out, a_norm_post[l])
        glu, sz = _glu_proj(xp.reshape(bp * tp, d), a_norm_pre[l], w_in, tm)
        glu = glu.reshape(bp, tp, di)
        xp = _conv_prompt(glu, sz.reshape(bp, tp, di), xp, *tail, tm)
        conv_p.append(glu[:, -(CONV_W - 1):])
        glu, sz = _glu_proj(xs, a_norm_pre[l], w_in, tm_s)
        xs = _conv_sample(glu, state_conv[l], sz, xs, *tail, 16)
        conv_s.append(jnp.concatenate([state_conv[l], glu.reshape(bd, tq, di)], axis=1)[:, -(CONV_W - 1):])

    bw = b_w_in[0]
    n_gate = N_BRANCH * N_HEADS
    wq = bw[:, :dq].astype(BF16)
    wz = bw[:, dq:dq * (1 + N_BRANCH)].astype(BF16)
    wg = jnp.pad(bw[:, dq * (1 + N_BRANCH):], ((0, 0), (0, LANES - n_gate))).astype(BF16)
    wkv = w_kv.astype(BF16)
    xp2 = xp.reshape(bp * tp, d)
    (cmp_p, slc_p, win_p, qa_p, sz_p, gate_p, ska, sv, wka, wv) = _nsa_proj(
        xp2, kv_norm, b_norm_pre[0], wkv, wq, wz, wg, tm, seq_len=tp)
    cmp_s, slc_s, win_s, qa_s, sz_s, gate_s = _nsa_proj(xs, kv_norm, b_norm_pre[0], wkv, wq, wz, wg, tm_s)
    qa_s = qa_s.astype(F32)

    w1cat = jnp.concatenate([cmp_w1[:, :CMP_STRIDE * HEAD_DIM], cmp_w1[:, CMP_STRIDE * HEAD_DIM:]], axis=2).astype(BF16)
    pe8 = jnp.pad(cmp_pe.reshape(2, L_CMP // CMP_STRIDE, CMP_STRIDE * HEAD_DIM), ((0, 0), (0, 6), (0, 0)))
    zero = jnp.zeros_like(cmp_w2)
    w2h = jnp.stack([jnp.concatenate([cmp_w2, zero], axis=2), jnp.concatenate([zero, cmp_w2], axis=2)], axis=1).astype(BF16)
    prompt_page = min(1024, tp)
    pp = tp // prompt_page
    kvc_p = _compress(cmp_p, None, w1cat, pe8, w2h, pp, prompt_page).transpose(0, 2, 1)
    pt_flat = page_table.reshape(-1).astype(jnp.int32)
    cmp_pages_t = cache_cmp.transpose(0, 2, 3, 4, 1).reshape(-1, width, PAGE_SIZE)
    kvc_s = _compress(cmp_pages_t, pt_flat, w1cat, pe8, w2h, n_pages, PAGE_SIZE)

    n_chunk_p = kvc_p.shape[1]
    nc_p = n_chunk_p - L_CMP // CMP_STRIDE + 1
    real = (jnp.arange(n_chunk_p) < nc_p)[None, :, None]
    flag = jnp.zeros((LANES - HEAD_DIM,), F32).at[0].set(1.0)
    kc4 = jnp.where(real, kvc_p[:, :, :kw], 0.0).reshape(bp, n_chunk_p, KV_HEADS, HEAD_DIM)
    aug = jnp.where(real[..., None], 0.0, flag) * jnp.ones((bp, n_chunk_p, KV_HEADS, 1), F32)
    kc_real = jnp.concatenate([kc4, aug], axis=-1).reshape(bp, n_chunk_p, KV_HEADS * LANES)
    pad_row = jnp.concatenate([jnp.zeros((HEAD_DIM,), F32), flag])
    kc_front = jnp.broadcast_to(jnp.tile(pad_row, KV_HEADS), (bp, n_chunk_p, KV_HEADS * LANES))
    kc_pad = jnp.concatenate([kc_front, kc_real], axis=1)
    vc_pad = jnp.concatenate([jnp.zeros((bp, n_chunk_p, kw), F32), jnp.where(real, kvc_p[:, :, kw:], 0.0)], axis=1)
    m_pad = _overlap_matrix(2 * n_chunk_p, n_chunk_p, nc_p, tp // L_SLC)
    n_wkey = WINDOW + Q_BLK
    cb, wb, *sb = _bias_tiles(rel_bias, [
        _tile_spec(n_chunk_p, Q_BLK, 1, CMP_STRIDE * (n_chunk_p - Q_BLK // CMP_STRIDE) - (L_CMP - 1),
                   row_step=-CMP_STRIDE),
        _tile_spec(n_wkey, Q_BLK, 1, WINDOW, WINDOW, row_step=-1)]
        + [_tile_spec(KEY_TILE, Q_BLK, 1, off, row_step=-1) for off in (0, Q_BLK, 2 * Q_BLK, 3 * Q_BLK)])
    cb = cb.reshape(KV_HEADS, GROUP, n_chunk_p, Q_BLK).transpose(0, 2, 1, 3).reshape(KV_HEADS, n_chunk_p, GROUP * Q_BLK)
    oc_p, qsel, qflag = _cmp_prompt(qa_p, kc_pad, vc_pad, m_pad, cb, gate_p, sz_p, bp, tp)
    sb = jnp.stack(sb)
    sb = sb.reshape(4, KV_HEADS, GROUP, KEY_TILE, Q_BLK).transpose(0, 1, 3, 2, 4).reshape(4, KV_HEADS, KEY_TILE, GROUP * Q_BLK)
    os_p = _slc_prompt(qsel, ska, sv, sb, gate_p, sz_p, bp, tp)
    wb = wb.reshape(KV_HEADS, GROUP, n_wkey, Q_BLK).transpose(0, 2, 1, 3).reshape(KV_HEADS, n_wkey, GROUP * Q_BLK)
    wka_front = jnp.broadcast_to(jnp.tile(pad_row, KV_HEADS).astype(BF16), (bp, WINDOW, KV_HEADS * LANES))
    wka_pad = jnp.concatenate([wka_front, wka.reshape(bp, tp, -1)], axis=1)
    v_rows = wv.shape[0]
    wv_pad = jnp.concatenate([jnp.zeros((v_rows, bp, WINDOW), BF16), wv.reshape(v_rows, bp, tp)], axis=2)
    ow_p = _win_prompt(qflag, wka_pad, wv_pad.reshape(v_rows, bp * (WINDOW + tp)), wb, gate_p, sz_p, bp, tp)
    wo = b_w_out[0].astype(BF16)
    y_p = _nsa_out(oc_p.reshape(bp * tp, dq), os_p.reshape(bp * tp, dq), ow_p.reshape(bp * tp, dq), None, xp2,
                   wo, b_norm_post[0], tm)

    n_chunk_s = kvc_s.shape[2]
    nc_s = n_chunk_s - L_CMP // CMP_STRIDE + 1
    nb_s = -(-(past + tq) // L_SLC)
    n_sk = past + LANES
    n_wk = w_keep + LANES
    m_t = _overlap_matrix(n_chunk_s, 0, nc_s, nb_s).T
    e_np = np.zeros((LANES, n_sk), np.float32)
    e_np[HEAD_DIM + np.arange(n_sk) // L_SLC, np.arange(n_sk)] = 1.0
    e_s = jnp.asarray(e_np, BF16)
    rows_s = N_HEADS * tq
    cbs, sbs, wbs = [t.reshape(rows_s, t.shape[-1]) for t in _bias_tiles(rel_bias, [
        _tile_spec(tq, n_chunk_s, -CMP_STRIDE, past - (L_CMP - 1)),
        _tile_spec(tq, n_sk, -1, past),
        _tile_spec(tq, n_wk, -1, w_keep, WINDOW)])]
    gate_r = gate_s[:, :n_gate].reshape(bd, tq, N_BRANCH, N_HEADS).transpose(0, 3, 1, 2).reshape(bd * rows_s, N_BRANCH)
    gate_r = jnp.pad(gate_r, ((0, 0), (0, LANES - N_BRANCH)))
    to_t = lambda a: a.transpose(0, 2, 3, 4, 1).reshape(a.shape[0], width, a.shape[1])
    oc_s, os_s, ow_s = _decode_attn(qa_s, kvc_s, to_t(cache_slc), pt_flat, slc_s, to_t(state_win), win_s,
                                    m_t, e_s, cbs, sbs, wbs, gate_r, bd, tq, n_pages)
    y_s = _nsa_out(oc_s, os_s, ow_s, sz_s, xs, wo, b_norm_post[0], tm_s)

    kv5 = lambda a, b, t: a.reshape(b, t, 2, KV_HEADS, HEAD_DIM)
    kv5_t = lambda a: a.reshape(bp, 2, KV_HEADS, HEAD_DIM, a.shape[2]).transpose(0, 4, 1, 2, 3)
    win_all = jnp.concatenate([state_win, kv5(win_s, bd, tq)], axis=1)
    return (y_p.reshape(bp, tp, d), y_s.reshape(bd, tq, d), jnp.stack(conv_p), jnp.stack(conv_s),
            kv5_t(cmp_p), kv5(cmp_s, bd, tq), kv5_t(slc_p), kv5(slc_s, bd, tq),
            kv5_t(win_p[:, :, -min(WINDOW, tp):]), win_all[:, -min(WINDOW, win_all.shape[1]):])
```
